```python
import math
import jax, jax.numpy as jnp
from jax import lax
import numpy as np

D_MODEL = 1024
BATCH = 4
SEQ = 8192
DEPTH = 1
DEC_BATCH = 32
DEC_SEQ = 32
PAST_LEN = 4096

CHUNK = 64
EPS = 1e-6
GLA_HEADS = 4
GLA_DK = D_MODEL // 8
GLA_DV = D_MODEL // 4
GLA_LOWRANK = 16
GLA_TAU = 16.0
GLA_BLOCK = 16
DIFF_HEADS = 8
DIFF_HD = D_MODEL // (2 * DIFF_HEADS)
DIFF_VD = 2 * DIFF_HD
ROPE_THETA = 10000.0
Q_BLOCK = 128
PEER_HEADS = 8
PEER_NKEYS = 128
PEER_NEXPERTS = PEER_NKEYS * PEER_NKEYS
PEER_DKEY = 256
PEER_TOPK = 16
PEER_BLOCK = 256
GLA_QK_W = GLA_HEADS * GLA_DK
GLA_V_W = GLA_HEADS * GLA_DV
DIFF_QK_W = DIFF_HEADS * 2 * DIFF_HD
DIFF_V_W = DIFF_HEADS * DIFF_VD
IN_SIZES = (GLA_QK_W, GLA_QK_W, GLA_V_W, GLA_V_W, GLA_LOWRANK, DIFF_QK_W, DIFF_QK_W, DIFF_V_W, 2 * D_MODEL)
IN_WIDTH = 2 * GLA_QK_W + 2 * GLA_V_W + GLA_LOWRANK + 2 * DIFF_QK_W + DIFF_V_W + 2 * D_MODEL

kernel_name = 'hybrid_gla_diffattn_peer_stream_step'

F32 = jnp.float32


def rmsnorm(x, gain):
    xf = x.astype(F32)
    y = xf * lax.rsqrt(jnp.mean(xf * xf, axis=-1, keepdims=True) + EPS)
    return (y * gain.astype(F32)).astype(x.dtype)


def modulate(x, gain, shift, scale):
    return rmsnorm(x, gain) * (1 + scale[:, None, :]) + shift[:, None, :]


def rope(x, pos):
    d = x.shape[-1]
    half = d // 2
    inv = ROPE_THETA ** (-jnp.arange(half, dtype=F32) / half)
    ang = pos.astype(F32)[:, None] * inv[None, :]
    ang = ang.reshape((1, ang.shape[0]) + (1,) * (x.ndim - 3) + (half,))
    cos, sin = jnp.cos(ang), jnp.sin(ang)
    xf = x.astype(F32)
    x1, x2 = xf[..., :half], xf[..., half:]
    return jnp.concatenate([x1 * cos - x2 * sin, x2 * cos + x1 * sin], axis=-1).astype(x.dtype)


def gla_recurrence(q, k, v, log_a, h0):
    B, L, H, _ = q.shape
    dv = v.shape[-1]
    n = L // GLA_BLOCK

    def blocks(t):
        return jnp.moveaxis(t.reshape(B, n, GLA_BLOCK, H, t.shape[-1]), 1, 0)

    causal = jnp.tril(jnp.ones((GLA_BLOCK, GLA_BLOCK), dtype=bool))

    def step(h, xs):
        qb, kb, vb, gb = xs
        b = jnp.cumsum(gb, axis=1)
        qe = qb * jnp.exp(b)
        ke = kb * jnp.exp(-b)
        a = jnp.where(causal, jnp.einsum('bthk,bshk->bhts', qe, ke), 0.0)
        o = jnp.einsum('bthk,bhkv->bthv', qe, h) + jnp.einsum('bhts,bshv->bthv', a, vb)
        b_last = b[:, -1]
        h = jnp.exp(b_last)[..., None] * h + jnp.einsum('bshk,bshv->bhkv', kb * jnp.exp(b_last[:, None] - b), vb)
        return h, o

    h, o = lax.scan(step, h0, (blocks(q), blocks(k), blocks(v), blocks(log_a)))
    o = jnp.moveaxis(o, 0, 1).reshape(B, L, H, dv)
    return o, h


def diff_attend(q1, q2, k1, k2, v, mask, lam):
    scale = DIFF_HD ** -0.5
    s1 = jnp.einsum('bqhd,bkhd->bhqk', q1, k1).astype(F32) * scale
    s2 = jnp.einsum('bqhd,bkhd->bhqk', q2, k2).astype(F32) * scale
    if mask is not None:
        s1 = jnp.where(mask, s1, -jnp.inf)
        s2 = jnp.where(mask, s2, -jnp.inf)
    p = jax.nn.softmax(s1, axis=-1) - lam * jax.nn.softmax(s2, axis=-1)
    return jnp.einsum('bhqk,bkhv->bqhv', p.astype(v.dtype), v)


def peer_ffn(x, w_pq, sub_keys1, sub_keys2, peer_u, peer_v):
    n = x.shape[0]
    pad = (-n) % PEER_BLOCK
    xb = jnp.pad(x, ((0, pad), (0, 0))).reshape(-1, PEER_BLOCK, D_MODEL)
    half = PEER_DKEY // 2

    def block(xt):
        q = (xt @ w_pq).reshape(PEER_BLOCK, PEER_HEADS, 2, half)
        s1 = jnp.einsum('thc,nc->thn', q[:, :, 0], sub_keys1)
        s2 = jnp.einsum('thc,nc->thn', q[:, :, 1], sub_keys2)
        v1, i1 = lax.top_k(s1, PEER_TOPK)
        v2, i2 = lax.top_k(s2, PEER_TOPK)
        cand = (v1[..., :, None] + v2[..., None, :]).reshape(PEER_BLOCK, PEER_HEADS, PEER_TOPK * PEER_TOPK)
        sc, ci = lax.top_k(cand, PEER_TOPK)
        e1 = jnp.take_along_axis(i1, ci // PEER_TOPK, axis=-1)
        e2 = jnp.take_along_axis(i2, ci % PEER_TOPK, axis=-1)
        idx = e1 * PEER_NKEYS + e2
        g = jax.nn.softmax(sc.astype(F32), axis=-1).astype(xt.dtype)
        act = jax.nn.gelu(jnp.einsum('td,thkd->thk', xt, peer_u[idx]), approximate=False)
        return jnp.einsum('thk,thkd->td', g * act, peer_v[idx])

    y = lax.map(block, xb)
    return y.reshape(-1, D_MODEL)[:n]


def trunk_layer(x, pos, c, past_k, past_v, gla_h0, lambda_init,
                norm1_gain, norm2_gain, w_mod, b_mod, w_in, w_alpha, b_alpha, gla_gain, w_gla_out,
                q_gain, k_gain, lam_q1, lam_k1, lam_q2, lam_k2, diff_gain, w_diff_out, w_out,
                w_pq, sub_keys1, sub_keys2, peer_u, peer_v):
    B, L, _ = x.shape
    mod = jax.nn.silu(c) @ w_mod + b_mod
    sh1, sc1, gt1, sh2, sc2, gt2 = jnp.split(mod, 6, axis=-1)

    h = modulate(x, norm1_gain, sh1, sc1)
    z = h @ w_in
    gq, gk, gv, gr, glr, dq, dk, dv, gates = jnp.split(z, np.cumsum(IN_SIZES)[:-1].tolist(), axis=-1)

    q = gq.reshape(B, L, GLA_HEADS, GLA_DK) * (GLA_DK ** -0.5)
    k = gk.reshape(B, L, GLA_HEADS, GLA_DK)
    v = gv.reshape(B, L, GLA_HEADS, GLA_DV)
    log_a = (jax.nn.log_sigmoid((glr @ w_alpha + b_alpha).astype(F32)) / GLA_TAU).reshape(B, L, GLA_HEADS, GLA_DK)
    pad = (-L) % GLA_BLOCK

    def prep(t):
        return jnp.pad(t.astype(F32), ((0, 0), (0, pad), (0, 0), (0, 0)))

    o_a, gla_state = gla_recurrence(prep(q), prep(k), prep(v), prep(log_a), gla_h0.astype(F32))
    o_a = o_a[:, :L].astype(x.dtype)
    o_a = rmsnorm(o_a, gla_gain) * jax.nn.silu(gr.reshape(B, L, GLA_HEADS, GLA_DV))
    y_a = o_a.reshape(B, L, GLA_V_W) @ w_gla_out

    qd = rope(rmsnorm(dq.reshape(B, L, DIFF_HEADS, 2, DIFF_HD), q_gain), pos)
    kd = rope(rmsnorm(dk.reshape(B, L, DIFF_HEADS, 2, DIFF_HD), k_gain), pos)
    k_rows = kd.reshape(B, L, DIFF_HEADS, 2 * DIFF_HD)
    v_rows = dv.reshape(B, L, DIFF_HEADS, DIFF_VD)
    lam = (jnp.exp(jnp.sum(lam_q1.astype(F32) * lam_k1.astype(F32)))
           - jnp.exp(jnp.sum(lam_q2.astype(F32) * lam_k2.astype(F32))) + lambda_init)
    q1, q2 = qd[..., 0, :], qd[..., 1, :]
    if past_k is None:
        k1, k2 = kd[..., 0, :], kd[..., 1, :]
        nq = L // Q_BLOCK

        def qblocks(t):
            return jnp.moveaxis(t.reshape((B, nq, Q_BLOCK) + t.shape[2:]), 1, 0)

        key_chunk = jnp.arange(L) // CHUNK

        def attend_block(args):
            q1b, q2b, qpos = args
            mask = (qpos[:, None] // CHUNK) >= key_chunk[None, :]
            return diff_attend(q1b, q2b, k1, k2, v_rows, mask, lam)

        o_b = lax.map(attend_block, (qblocks(q1), qblocks(q2), jnp.arange(L).reshape(nq, Q_BLOCK)))
        o_b = jnp.moveaxis(o_b, 0, 1).reshape(B, L, DIFF_HEADS, DIFF_VD)
    else:
        keys = jnp.concatenate([past_k.astype(k_rows.dtype), k_rows], axis=1)
        vals = jnp.concatenate([past_v.astype(v_rows.dtype), v_rows], axis=1)
        o_b = diff_attend(q1, q2, keys[..., :DIFF_HD], keys[..., DIFF_HD:], vals, None, lam)
    o_b = rmsnorm(o_b, diff_gain) * (1.0 - lambda_init)
    y_b = o_b.reshape(B, L, DIFF_V_W) @ w_diff_out

    g_a, g_b = jnp.split(gates, 2, axis=-1)
    mix = (jax.nn.sigmoid(g_a) * y_a + jax.nn.sigmoid(g_b) * y_b) @ w_out
    x = x + gt1[:, None, :] * mix

    h2 = modulate(x, norm2_gain, sh2, sc2)
    ff = peer_ffn(h2.reshape(B * L, D_MODEL), w_pq, sub_keys1, sub_keys2, peer_u, peer_v)
    x = x + gt2[:, None, :] * ff.reshape(B, L, D_MODEL)
    return x, k_rows, v_rows, gla_state.astype(x.dtype)


def setup_inputs(seed: int = 0) -> dict:
    key = jax.random.key(seed)
    ks = jax.random.split(key, 40)

    def nrm(k, shape, scale):
        return jax.random.normal(k, shape, F32) * scale

    def gain(k, shape):
        return 1.0 + 0.02 * jax.random.normal(k, shape, F32)

    D = D_MODEL
    return {
        'x_prompt': nrm(ks[0], (BATCH, SEQ, D), 1.0),
        'x_sample': nrm(ks[1], (DEC_BATCH, DEC_SEQ, D), 1.0),
        'c_prompt': nrm(ks[2], (BATCH, D), 1.0),
        'c_sample': nrm(ks[3], (DEC_BATCH, D), 1.0),
        'cache_k': nrm(ks[4], (DEPTH, DEC_BATCH, PAST_LEN, DIFF_HEADS, 2 * DIFF_HD), 1.0),
        'cache_v': nrm(ks[5], (DEPTH, DEC_BATCH, PAST_LEN, DIFF_HEADS, DIFF_VD), 1.0),
        'state_gla': nrm(ks[6], (DEPTH, DEC_BATCH, GLA_HEADS, GLA_DK, GLA_DV), GLA_DK ** -0.5),
        'norm1_gain': gain(ks[7], (DEPTH, D)),
        'norm2_gain': gain(ks[8], (DEPTH, D)),
        'w_mod': nrm(ks[9], (DEPTH, D, 6 * D), 0.5 * D ** -0.5),
        'b_mod': nrm(ks[10], (DEPTH, 6 * D), 0.01),
        'w_in': nrm(ks[11], (DEPTH, D, IN_WIDTH), D ** -0.5),
        'w_alpha': nrm(ks[12], (DEPTH, GLA_LOWRANK, GLA_QK_W), GLA_LOWRANK ** -0.5),
        'b_alpha': nrm(ks[13], (DEPTH, GLA_QK_W), 0.01),
        'gla_gain': gain(ks[14], (DEPTH, GLA_DV)),
        'w_gla_out': nrm(ks[15], (DEPTH, GLA_V_W, D), GLA_V_W ** -0.5),
        'q_gain': gain(ks[16], (DEPTH, 2, DIFF_HD)),
        'k_gain': gain(ks[17], (DEPTH, 2, DIFF_HD)),
        'lam_q1': nrm(ks[18], (DEPTH, DIFF_HD), 0.1),
        'lam_k1': nrm(ks[19], (DEPTH, DIFF_HD), 0.1),
        'lam_q2': nrm(ks[20], (DEPTH, DIFF_HD), 0.1),
        'lam_k2': nrm(ks[21], (DEPTH, DIFF_HD), 0.1),
        'diff_gain': gain(ks[22], (DEPTH, DIFF_VD)),
        'w_diff_out': nrm(ks[23], (DEPTH, DIFF_V_W, D), DIFF_V_W ** -0.5),
        'w_out': nrm(ks[24], (DEPTH, D, D), D ** -0.5),
        'w_pq': nrm(ks[25], (DEPTH, D, PEER_HEADS * PEER_DKEY), D ** -0.5),
        'sub_keys1': nrm(ks[26], (DEPTH, PEER_NKEYS, PEER_DKEY // 2), (PEER_DKEY // 2) ** -0.5),
        'sub_keys2': nrm(ks[27], (DEPTH, PEER_NKEYS, PEER_DKEY // 2), (PEER_DKEY // 2) ** -0.5),
        'peer_u': nrm(ks[28], (DEPTH, PEER_NEXPERTS, D), D ** -0.5),
        'peer_v': nrm(ks[29], (DEPTH, PEER_NEXPERTS, D), PEER_HEADS ** -0.5),
    }


def reference(x_prompt, x_sample, c_prompt, c_sample, cache_k, cache_v, state_gla,
              norm1_gain, norm2_gain, w_mod, b_mod, w_in, w_alpha, b_alpha, gla_gain, w_gla_out,
              q_gain, k_gain, lam_q1, lam_k1, lam_q2, lam_k2, diff_gain, w_diff_out, w_out,
              w_pq, sub_keys1, sub_keys2, peer_u, peer_v):
    pos_p = jnp.arange(x_prompt.shape[1])
    pos_s = cache_k.shape[2] + jnp.arange(x_sample.shape[1])
    h0_p = jnp.zeros((x_prompt.shape[0], GLA_HEADS, GLA_DK, GLA_DV), x_prompt.dtype)
    yp, ys = x_prompt, x_sample
    kp, vp, sp, ksm, vsm, ssm = [], [], [], [], [], []
    for l in range(DEPTH):
        lambda_init = 0.8 - 0.6 * math.exp(-0.3 * l)
        lw = (norm1_gain[l], norm2_gain[l], w_mod[l], b_mod[l], w_in[l], w_alpha[l], b_alpha[l],
              gla_gain[l], w_gla_out[l], q_gain[l], k_gain[l], lam_q1[l], lam_k1[l], lam_q2[l], lam_k2[l],
              diff_gain[l], w_diff_out[l], w_out[l], w_pq[l], sub_keys1[l], sub_keys2[l], peer_u[l], peer_v[l])
        yp, k_new, v_new, s_new = trunk_layer(yp, pos_p, c_prompt, None, None, h0_p, lambda_init, *lw)
        kp.append(k_new)
        vp.append(v_new)
        sp.append(s_new)
        ys, k_new, v_new, s_new = trunk_layer(ys, pos_s, c_sample, cache_k[l], cache_v[l], state_gla[l], lambda_init, *lw)
        ksm.append(k_new)
        vsm.append(v_new)
        ssm.append(s_new)
    return (yp, ys, jnp.stack(kp), jnp.stack(vp), jnp.stack(sp), jnp.stack(ksm), jnp.stack(vsm), jnp.stack(ssm))
```

```python
import math
import functools
import jax, jax.numpy as jnp
from jax import lax
import numpy as np
from jax.experimental import pallas as pl
from jax.experimental.pallas import tpu as pltpu

D_MODEL = 1024
CHUNK = 64
EPS = 1e-6
GLA_HEADS = 4
GLA_DK = 128
GLA_DV = 256
GLA_LOWRANK = 16
GLA_TAU = 16.0
GLA_BLOCK = 16
DIFF_HEADS = 8
DIFF_HD = 64
DIFF_VD = 128
ROPE_THETA = 10000.0
Q_BLOCK = 128
PEER_HEADS = 8
PEER_NKEYS = 128
PEER_DKEY = 256
PEER_TOPK = 16
PEER_BLOCK = 256
GLA_QK_W = GLA_HEADS * GLA_DK
GLA_V_W = GLA_HEADS * GLA_DV
DIFF_QK_W = DIFF_HEADS * 2 * DIFF_HD
DIFF_V_W = DIFF_HEADS * DIFF_VD
IN_SIZES = (GLA_QK_W, GLA_QK_W, GLA_V_W, GLA_V_W, GLA_LOWRANK, DIFF_QK_W, DIFF_QK_W, DIFF_V_W, 2 * D_MODEL)
F32 = jnp.float32
NSEL = PEER_HEADS * PEER_TOPK
PEER_TOK_TILE = 8


def _mod_kernel(c_ref, w_ref, b_ref, o_ref):
    c = c_ref[...]
    s = c * jax.nn.sigmoid(c)
    o_ref[...] = jnp.dot(s.astype(jnp.bfloat16), w_ref[...].astype(jnp.bfloat16),
                         preferred_element_type=F32) + b_ref[...]


def _adaln(c, w_mod, b_mod):
    n = c.shape[0]
    tn = 1536
    return pl.pallas_call(
        _mod_kernel,
        grid=(6 * D_MODEL // tn,),
        in_specs=[pl.BlockSpec((n, D_MODEL), lambda j: (0, 0)),
                  pl.BlockSpec((D_MODEL, tn), lambda j: (0, j)),
                  pl.BlockSpec((1, tn), lambda j: (0, j))],
        out_specs=pl.BlockSpec((n, tn), lambda j: (0, j)),
        out_shape=jax.ShapeDtypeStruct((n, 6 * D_MODEL), F32),
        name="adaln",
    )(c, w_mod, b_mod.reshape(1, -1))


def _peer_gather_kernel(idx_ref, h2_ref, g_ref, x1_ref, gate_ref, tab_ref, out_ref, buf, sem, *, tt):
    s = pl.program_id(0)
    n = pl.num_programs(0) - 1

    @pl.when(s < n)
    def _issue():
        slot = s % 2

        def issue_tok(t, carry):
            for j in range(NSEL):
                e = idx_ref[t, j]
                pltpu.make_async_copy(tab_ref.at[e], buf.at[slot, t, pl.ds(j, 1), :], sem.at[slot, t]).start()
            return carry

        lax.fori_loop(0, tt, issue_tok, 0)

    @pl.when(s >= 1)
    def _compute():
        slot = (s + 1) % 2
        eye = (lax.broadcasted_iota(jnp.int32, (NSEL, NSEL), 0)
               == lax.broadcasted_iota(jnp.int32, (NSEL, NSEL), 1))
        gate = gate_ref[0]

        def tok(t, carry):
            for j in range(NSEL):
                pltpu.make_async_copy(tab_ref.at[0], buf.at[slot, t, pl.ds(j, 1), :], sem.at[slot, t]).wait()
            x = h2_ref[pl.ds(t, 1), :]
            p = buf[slot, t, :, 0:128] * x[:, 0:128]
            for c in range(1, D_MODEL // 128):
                p = p + buf[slot, t, :, c * 128:(c + 1) * 128] * x[:, c * 128:(c + 1) * 128]
            a = jnp.sum(p, axis=-1, keepdims=True)
            act = 0.5 * a * (1.0 + lax.erf(a * (0.5 ** 0.5)))
            grow = g_ref[pl.ds(t, 1), :]
            gcol = jnp.sum(jnp.where(eye, grow, 0.0), axis=-1, keepdims=True)
            coef = gcol * act
            y = jnp.sum(coef * buf[slot, t, :, D_MODEL:2 * D_MODEL], axis=0, keepdims=True)
            out_ref[pl.ds(t, 1), :] = x1_ref[pl.ds(t, 1), :] + gate * y
            return carry

        lax.fori_loop(0, tt, tok, 0)


def _peer_gather(idx, g, h2, x1, gates, table, *, tt, n_prompt, l_prompt, nb_prompt, l_sample):
    n_tok = idx.shape[0]
    n_tiles = n_tok // tt
    assert n_tiles * tt == n_tok and l_prompt % tt == 0 and l_sample % tt == 0

    def cur(s):
        return jnp.maximum(s - 1, 0)

    def gate_map(s):
        tok0 = cur(s) * tt
        b = jnp.where(tok0 < n_prompt, tok0 // l_prompt, nb_prompt + (tok0 - n_prompt) // l_sample)
        return (b, 0, 0)

    return pl.pallas_call(
        functools.partial(_peer_gather_kernel, tt=tt),
        grid=(n_tiles + 1,),
        in_specs=[
            pl.BlockSpec((tt, NSEL), lambda s: (jnp.minimum(s, n_tiles - 1), 0), memory_space=pltpu.SMEM),
            pl.BlockSpec((tt, D_MODEL), lambda s: (cur(s), 0)),
            pl.BlockSpec((tt, NSEL), lambda s: (cur(s), 0)),
            pl.BlockSpec((tt, D_MODEL), lambda s: (cur(s), 0)),
            pl.BlockSpec((1, 1, D_MODEL), gate_map),
            pl.BlockSpec(memory_space=pl.ANY),
        ],
        out_specs=pl.BlockSpec((tt, D_MODEL), lambda s: (cur(s), 0)),
        out_shape=jax.ShapeDtypeStruct((n_tok, D_MODEL), F32),
        scratch_shapes=[pltpu.VMEM((2, tt, NSEL, 2 * D_MODEL), F32), pltpu.SemaphoreType.DMA((2, tt))],
        compiler_params=pltpu.CompilerParams(dimension_semantics=("arbitrary",),
                                             vmem_limit_bytes=48 * 1024 * 1024),
        name="peer_gather",
    )(idx, h2, g, x1, gates, table)


def rmsnorm(x, gain):
    xf = x.astype(F32)
    y = xf * lax.rsqrt(jnp.mean(xf * xf, axis=-1, keepdims=True) + EPS)
    return (y * gain.astype(F32)).astype(x.dtype)


def modulate(x, gain, shift, scale):
    return rmsnorm(x, gain) * (1 + scale[:, None, :]) + shift[:, None, :]


def rope(x, pos):
    d = x.shape[-1]
    half = d // 2
    inv = ROPE_THETA ** (-jnp.arange(half, dtype=F32) / half)
    ang = pos.astype(F32)[:, None] * inv[None, :]
    ang = ang.reshape((1, ang.shape[0]) + (1,) * (x.ndim - 3) + (half,))
    cos, sin = jnp.cos(ang), jnp.sin(ang)
    xf = x.astype(F32)
    x1, x2 = xf[..., :half], xf[..., half:]
    return jnp.concatenate([x1 * cos - x2 * sin, x2 * cos + x1 * sin], axis=-1).astype(x.dtype)


def gla_recurrence(q, k, v, log_a, h0):
    B, L, H, _ = q.shape
    dv = v.shape[-1]
    n = L // GLA_BLOCK

    def blocks(t):
        return jnp.moveaxis(t.reshape(B, n, GLA_BLOCK, H, t.shape[-1]), 1, 0)

    causal = jnp.tril(jnp.ones((GLA_BLOCK, GLA_BLOCK), dtype=bool))

    def step(h, xs):
        qb, kb, vb, gb = xs
        b = jnp.cumsum(gb, axis=1)
        qe = qb * jnp.exp(b)
        ke = kb * jnp.exp(-b)
        a = jnp.where(causal, jnp.einsum('bthk,bshk->bhts', qe, ke), 0.0)
        o = jnp.einsum('bthk,bhkv->bthv', qe, h) + jnp.einsum('bhts,bshv->bthv', a, vb)
        b_last = b[:, -1]
        h = jnp.exp(b_last)[..., None] * h + jnp.einsum('bshk,bshv->bhkv', kb * jnp.exp(b_last[:, None] - b), vb)
        return h, o

    h, o = lax.scan(step, h0, (blocks(q), blocks(k), blocks(v), blocks(log_a)))
    o = jnp.moveaxis(o, 0, 1).reshape(B, L, H, dv)
    return o, h


def diff_attend(q1, q2, k1, k2, v, mask, lam):
    scale = DIFF_HD ** -0.5
    s1 = jnp.einsum('bqhd,bkhd->bhqk', q1, k1).astype(F32) * scale
    s2 = jnp.einsum('bqhd,bkhd->bhqk', q2, k2).astype(F32) * scale
    if mask is not None:
        s1 = jnp.where(mask, s1, -jnp.inf)
        s2 = jnp.where(mask, s2, -jnp.inf)
    p = jax.nn.softmax(s1, axis=-1) - lam * jax.nn.softmax(s2, axis=-1)
    return jnp.einsum('bhqk,bkhv->bqhv', p.astype(v.dtype), v)


def peer_select(x, w_pq, sub_keys1, sub_keys2):
    n = x.shape[0]
    xb = x.reshape(-1, PEER_BLOCK, D_MODEL)
    half = PEER_DKEY // 2

    def block(xt):
        q = (xt @ w_pq).reshape(PEER_BLOCK, PEER_HEADS, 2, half)
        s1 = jnp.einsum('thc,nc->thn', q[:, :, 0], sub_keys1)
        s2 = jnp.einsum('thc,nc->thn', q[:, :, 1], sub_keys2)
        v1, i1 = lax.top_k(s1, PEER_TOPK)
        v2, i2 = lax.top_k(s2, PEER_TOPK)
        cand = (v1[..., :, None] + v2[..., None, :]).reshape(PEER_BLOCK, PEER_HEADS, PEER_TOPK * PEER_TOPK)
        sc, ci = lax.top_k(cand, PEER_TOPK)
        e1 = jnp.take_along_axis(i1, ci // PEER_TOPK, axis=-1)
        e2 = jnp.take_along_axis(i2, ci % PEER_TOPK, axis=-1)
        idx = e1 * PEER_NKEYS + e2
        g = jax.nn.softmax(sc.astype(F32), axis=-1)
        return idx.reshape(PEER_BLOCK, NSEL).astype(jnp.int32), g.reshape(PEER_BLOCK, NSEL)

    idx, g = lax.map(block, xb)
    return idx.reshape(n, NSEL), g.reshape(n, NSEL)


def mixer_layer(x, pos, mod, past_k, past_v, gla_h0, lambda_init,
                norm1_gain, norm2_gain, w_in, w_alpha, b_alpha, gla_gain, w_gla_out,
                q_gain, k_gain, lam_q1, lam_k1, lam_q2, lam_k2, diff_gain, w_diff_out, w_out):
    B, L, _ = x.shape
    sh1, sc1, gt1, sh2, sc2, gt2 = jnp.split(mod, 6, axis=-1)
    h = modulate(x, norm1_gain, sh1, sc1)
    z = h @ w_in
    gq, gk, gv, gr, glr, dq, dk, dv, gates = jnp.split(z, np.cumsum(IN_SIZES)[:-1].tolist(), axis=-1)
    q = gq.reshape(B, L, GLA_HEADS, GLA_DK) * (GLA_DK ** -0.5)
    k = gk.reshape(B, L, GLA_HEADS, GLA_DK)
    v = gv.reshape(B, L, GLA_HEADS, GLA_DV)
    log_a = (jax.nn.log_sigmoid((glr @ w_alpha + b_alpha).astype(F32)) / GLA_TAU).reshape(B, L, GLA_HEADS, GLA_DK)
    o_a, gla_state = gla_recurrence(q, k, v, log_a, gla_h0.astype(F32))
    o_a = rmsnorm(o_a, gla_gain) * jax.nn.silu(gr.reshape(B, L, GLA_HEADS, GLA_DV))
    y_a = o_a.reshape(B, L, GLA_V_W) @ w_gla_out
    qd = rope(rmsnorm(dq.reshape(B, L, DIFF_HEADS, 2, DIFF_HD), q_gain), pos)
    kd = rope(rmsnorm(dk.reshape(B, L, DIFF_HEADS, 2, DIFF_HD), k_gain), pos)
    k_rows = kd.reshape(B, L, DIFF_HEADS, 2 * DIFF_HD)
    v_rows = dv.reshape(B, L, DIFF_HEADS, DIFF_VD)
    lam = (jnp.exp(jnp.sum(lam_q1.astype(F32) * lam_k1.astype(F32)))
           - jnp.exp(jnp.sum(lam_q2.astype(F32) * lam_k2.astype(F32))) + lambda_init)
    q1, q2 = qd[..., 0, :], qd[..., 1, :]
    if past_k is None:
        k1, k2 = kd[..., 0, :], kd[..., 1, :]
        nq = L // Q_BLOCK

        def qblocks(t):
            return jnp.moveaxis(t.reshape((B, nq, Q_BLOCK) + t.shape[2:]), 1, 0)

        key_chunk = jnp.arange(L) // CHUNK

        def attend_block(args):
            q1b, q2b, qpos = args
            mask = (qpos[:, None] // CHUNK) >= key_chunk[None, :]
            return diff_attend(q1b, q2b, k1, k2, v_rows, mask, lam)

        o_b = lax.map(attend_block, (qblocks(q1), qblocks(q2), jnp.arange(L).reshape(nq, Q_BLOCK)))
        o_b = jnp.moveaxis(o_b, 0, 1).reshape(B, L, DIFF_HEADS, DIFF_VD)
    else:
        keys = jnp.concatenate([past_k.astype(k_rows.dtype), k_rows], axis=1)
        vals = jnp.concatenate([past_v.astype(v_rows.dtype), v_rows], axis=1)
        o_b = diff_attend(q1, q2, keys[..., :DIFF_HD], keys[..., DIFF_HD:], vals, None, lam)
    o_b = rmsnorm(o_b, diff_gain) * (1.0 - lambda_init)
    y_b = o_b.reshape(B, L, DIFF_V_W) @ w_diff_out
    g_a, g_b = jnp.split(gates, 2, axis=-1)
    mix = (jax.nn.sigmoid(g_a) * y_a + jax.nn.sigmoid(g_b) * y_b) @ w_out
    x = x + gt1[:, None, :] * mix
    h2 = modulate(x, norm2_gain, sh2, sc2)
    return x.reshape(B * L, D_MODEL), h2.reshape(B * L, D_MODEL), k_rows, v_rows, gla_state.astype(x.dtype)


def kernel(x_prompt, x_sample, c_prompt, c_sample, cache_k, cache_v, state_gla, norm1_gain, norm2_gain, w_mod, b_mod, w_in, w_alpha, b_alpha, gla_gain, w_gla_out, q_gain, k_gain, lam_q1, lam_k1, lam_q2, lam_k2, diff_gain, w_diff_out, w_out, w_pq, sub_keys1, sub_keys2, peer_u, peer_v):
    bp, lp, _ = x_prompt.shape
    bs, ls, _ = x_sample.shape
    pos_p = jnp.arange(lp)
    pos_s = cache_k.shape[2] + jnp.arange(ls)
    h0_p = jnp.zeros((bp, GLA_HEADS, GLA_DK, GLA_DV), x_prompt.dtype)
    l = 0
    lambda_init = 0.8 - 0.6 * math.exp(-0.3 * l)
    nb = bp + bs
    c_all = jnp.concatenate([c_prompt, c_sample, jnp.zeros(((-nb) % 8, D_MODEL), F32)], axis=0)
    mod_all = _adaln(c_all, w_mod[l], b_mod[l])
    lw = (norm1_gain[l], norm2_gain[l], w_in[l], w_alpha[l], b_alpha[l],
          gla_gain[l], w_gla_out[l], q_gain[l], k_gain[l], lam_q1[l], lam_k1[l], lam_q2[l], lam_k2[l],
          diff_gain[l], w_diff_out[l], w_out[l])
    x1p, h2p, kp, vp, sp = mixer_layer(x_prompt, pos_p, mod_all[:bp], None, None, h0_p, lambda_init, *lw)
    x1s, h2s, ks, vs, ss = mixer_layer(x_sample, pos_s, mod_all[bp:nb], cache_k[l], cache_v[l], state_gla[l], lambda_init, *lw)
    x1 = jnp.concatenate([x1p, x1s], axis=0)
    h2 = jnp.concatenate([h2p, h2s], axis=0)
    idx, g = peer_select(h2, w_pq[l], sub_keys1[l], sub_keys2[l])
    table = jnp.concatenate([peer_u[l], peer_v[l]], axis=1).reshape(-1, 1, 2 * D_MODEL)
    gt2 = mod_all[:nb, 5 * D_MODEL:].reshape(nb, 1, D_MODEL)
    y = _peer_gather(idx, g, h2, x1, gt2, table, tt=PEER_TOK_TILE,
                     n_prompt=bp * lp, l_prompt=lp, nb_prompt=bp, l_sample=ls)
    yp = y[:bp * lp].reshape(bp, lp, D_MODEL)
    ys = y[bp * lp:].reshape(bs, ls, D_MODEL)
    return (yp, ys, kp[None], vp[None], sp[None], ks[None], vs[None], ss[None])
```

```python
import math
import functools
import jax, jax.numpy as jnp
from jax import lax
import numpy as np
from jax.experimental import pallas as pl
from jax.experimental.pallas import tpu as pltpu

D_MODEL = 1024
CHUNK = 64
EPS = 1e-6
GLA_HEADS = 4
GLA_DK = 128
GLA_DV = 256
GLA_LOWRANK = 16
GLA_TAU = 16.0
GLA_BLOCK = 16
DIFF_HEADS = 8
DIFF_HD = 64
DIFF_VD = 128
ROPE_THETA = 10000.0
Q_BLOCK = 128
PEER_HEADS = 8
PEER_NKEYS = 128
PEER_DKEY = 256
PEER_TOPK = 16
PEER_BLOCK = 256
GLA_QK_W = GLA_HEADS * GLA_DK
GLA_V_W = GLA_HEADS * GLA_DV
DIFF_QK_W = DIFF_HEADS * 2 * DIFF_HD
DIFF_V_W = DIFF_HEADS * DIFF_VD
IN_SIZES = (GLA_QK_W, GLA_QK_W, GLA_V_W, GLA_V_W, GLA_LOWRANK, DIFF_QK_W, DIFF_QK_W, DIFF_V_W, 2 * D_MODEL)
F32 = jnp.float32
BF16 = jnp.bfloat16
NSEL = PEER_HEADS * PEER_TOPK
PEER_TOK_TILE = 8


def _mod_kernel(c_ref, w_ref, b_ref, o_ref):
    c = c_ref[...]
    s = c * jax.nn.sigmoid(c)
    o_ref[...] = jnp.dot(s.astype(jnp.bfloat16), w_ref[...].astype(jnp.bfloat16),
                         preferred_element_type=F32) + b_ref[...]


def _adaln(c, w_mod, b_mod):
    n = c.shape[0]
    tn = 1536
    return pl.pallas_call(
        _mod_kernel,
        grid=(6 * D_MODEL // tn,),
        in_specs=[pl.BlockSpec((n, D_MODEL), lambda j: (0, 0)),
                  pl.BlockSpec((D_MODEL, tn), lambda j: (0, j)),
                  pl.BlockSpec((1, tn), lambda j: (0, j))],
        out_specs=pl.BlockSpec((n, tn), lambda j: (0, j)),
        out_shape=jax.ShapeDtypeStruct((n, 6 * D_MODEL), F32),
        name="adaln",
    )(c, w_mod, b_mod.reshape(1, -1))


def _peer_gather_kernel(idx_ref, h2_ref, g_ref, x1_ref, gate_ref, tab_ref, out_ref, buf, sem, *, tt):
    s = pl.program_id(0)
    n = pl.num_programs(0) - 1

    @pl.when(s < n)
    def _issue():
        slot = s % 2

        def issue_tok(t, carry):
            for j in range(NSEL):
                e = idx_ref[t, j]
                pltpu.make_async_copy(tab_ref.at[e], buf.at[slot, t, pl.ds(j, 1), :], sem.at[slot, t]).start()
            return carry

        lax.fori_loop(0, tt, issue_tok, 0)

    @pl.when(s >= 1)
    def _compute():
        slot = (s + 1) % 2
        eye = (lax.broadcasted_iota(jnp.int32, (NSEL, NSEL), 0)
               == lax.broadcasted_iota(jnp.int32, (NSEL, NSEL), 1))
        gate = gate_ref[0]

        def tok(t, carry):
            for j in range(NSEL):
                pltpu.make_async_copy(tab_ref.at[0], buf.at[slot, t, pl.ds(j, 1), :], sem.at[slot, t]).wait()
            x = h2_ref[pl.ds(t, 1), :]
            p = buf[slot, t, :, 0:128] * x[:, 0:128]
            for c in range(1, D_MODEL // 128):
                p = p + buf[slot, t, :, c * 128:(c + 1) * 128] * x[:, c * 128:(c + 1) * 128]
            a = jnp.sum(p, axis=-1, keepdims=True)
            act = 0.5 * a * (1.0 + lax.erf(a * (0.5 ** 0.5)))
            grow = g_ref[pl.ds(t, 1), :]
            gcol = jnp.sum(jnp.where(eye, grow, 0.0), axis=-1, keepdims=True)
            coef = gcol * act
            y = jnp.sum(coef * buf[slot, t, :, D_MODEL:2 * D_MODEL], axis=0, keepdims=True)
            out_ref[pl.ds(t, 1), :] = x1_ref[pl.ds(t, 1), :] + gate * y
            return carry

        lax.fori_loop(0, tt, tok, 0)


def _peer_gather(idx, g, h2, x1, gates, table, *, tt, n_prompt, l_prompt, nb_prompt, l_sample):
    n_tok = idx.shape[0]
    n_tiles = n_tok // tt
    assert n_tiles * tt == n_tok and l_prompt % tt == 0 and l_sample % tt == 0

    def cur(s):
        return jnp.maximum(s - 1, 0)

    def gate_map(s):
        tok0 = cur(s) * tt
        b = jnp.where(tok0 < n_prompt, tok0 // l_prompt, nb_prompt + (tok0 - n_prompt) // l_sample)
        return (b, 0, 0)

    return pl.pallas_call(
        functools.partial(_peer_gather_kernel, tt=tt),
        grid=(n_tiles + 1,),
        in_specs=[
            pl.BlockSpec((tt, NSEL), lambda s: (jnp.minimum(s, n_tiles - 1), 0), memory_space=pltpu.SMEM),
            pl.BlockSpec((tt, D_MODEL), lambda s: (cur(s), 0)),
            pl.BlockSpec((tt, NSEL), lambda s: (cur(s), 0)),
            pl.BlockSpec((tt, D_MODEL), lambda s: (cur(s), 0)),
            pl.BlockSpec((1, 1, D_MODEL), gate_map),
            pl.BlockSpec(memory_space=pl.ANY),
        ],
        out_specs=pl.BlockSpec((tt, D_MODEL), lambda s: (cur(s), 0)),
        out_shape=jax.ShapeDtypeStruct((n_tok, D_MODEL), F32),
        scratch_shapes=[pltpu.VMEM((2, tt, NSEL, 2 * D_MODEL), F32), pltpu.SemaphoreType.DMA((2, tt))],
        compiler_params=pltpu.CompilerParams(dimension_semantics=("arbitrary",),
                                             vmem_limit_bytes=48 * 1024 * 1024),
        name="peer_gather",
    )(idx, h2, g, x1, gates, table)


ATTN_NEG = -1e30
ATTN_TILE = 512


def _attn_prompt_kernel(lam_ref, q_ref, k_ref, v_ref, o_ref, vt_ref, *, tq, tk):
    i = pl.program_id(2)
    n_kt = v_ref.shape[1] // tk

    @pl.when(i == 0)
    def _transpose_v():
        def body(j, c):
            vt_ref[j] = v_ref[0, pl.ds(j * tk, tk), :].astype(F32).T.astype(BF16)
            return c
        lax.fori_loop(0, n_kt, body, 0)

    q = q_ref[0]
    lane = lax.broadcasted_iota(jnp.int32, q.shape, 1)
    zero = jnp.zeros_like(q)
    qp = jnp.concatenate([jnp.where(lane < DIFF_HD, q, zero), jnp.where(lane >= DIFF_HD, q, zero)], axis=0)

    def step(j, carry, masked, koff=0):
        m, l, acc = carry
        kt = k_ref[0, pl.ds(j * tk, tk), :]
        st = lax.dot_general(kt, qp, (((1,), (1,)), ((), ())), preferred_element_type=F32)
        if masked:
            kc = (lax.broadcasted_iota(jnp.int32, st.shape, 0) + koff) // CHUNK
            qc = (lax.broadcasted_iota(jnp.int32, st.shape, 1) % tq) // CHUNK
            st = jnp.where(qc >= kc, st, ATTN_NEG)
        m_new = jnp.maximum(m, jnp.max(st, axis=0, keepdims=True))
        alpha = jnp.exp2(m - m_new)
        p = jnp.exp2(st - m_new)
        l = alpha * l + jnp.sum(p, axis=0, keepdims=True)
        acc = alpha * acc + jnp.dot(vt_ref[j], p.astype(BF16), preferred_element_type=F32)
        return m_new, l, acc

    carry = (jnp.full((1, 2 * tq), ATTN_NEG, F32), jnp.zeros((1, 2 * tq), F32), jnp.zeros((128, 2 * tq), F32))
    r = tq // tk
    carry = lax.fori_loop(0, i * r, functools.partial(step, masked=False), carry)
    for d in range(r):
        carry = step(i * r + d, carry, True, koff=d * tk)
    m, l, acc = carry
    o = acc / l
    ot = o[:, :tq] - lam_ref[0, 0] * o[:, tq:]
    o_ref[0] = ot.T


def _attn_prompt(q, k, v, lam, *, tq=ATTN_TILE, tk=ATTN_TILE):
    b, l, w = q.shape
    h = w // 128
    assert l % tq == 0 and tq % tk == 0 and tk % CHUNK == 0
    return pl.pallas_call(
        functools.partial(_attn_prompt_kernel, tq=tq, tk=tk),
        grid=(b, h, l // tq),
        in_specs=[
            pl.BlockSpec(memory_space=pltpu.SMEM),
            pl.BlockSpec((1, tq, 128), lambda bi, hi, i: (bi, i, hi)),
            pl.BlockSpec((1, l, 128), lambda bi, hi, i: (bi, 0, hi)),
            pl.BlockSpec((1, l, 128), lambda bi, hi, i: (bi, 0, hi)),
        ],
        out_specs=pl.BlockSpec((1, tq, 128), lambda bi, hi, i: (bi, i, hi)),
        out_shape=jax.ShapeDtypeStruct((b, l, w), F32),
        scratch_shapes=[pltpu.VMEM((l // tk, 128, tk), BF16)],
        compiler_params=pltpu.CompilerParams(dimension_semantics=("arbitrary", "arbitrary", "arbitrary"),
                                             vmem_limit_bytes=40 * 1024 * 1024),
        name="attn_prompt",
    )(lam, q, k, v)


def _topk_rows(s, k):
    r = s.shape[0]
    row = lax.broadcasted_iota(jnp.int32, s.shape, 0)
    vals, idxs = [], []
    for _ in range(k):
        mx = jnp.max(s, axis=0, keepdims=True)
        am = jnp.min(jnp.where(s == mx, row, r), axis=0, keepdims=True)
        vals.append(mx)
        idxs.append(am)
        s = jnp.where(row == am, -jnp.inf, s)
    return jnp.concatenate(vals, axis=0), jnp.concatenate(idxs, axis=0)


def _take_rows(tab, sel):
    out = jnp.zeros(sel.shape, tab.dtype)
    for a in range(tab.shape[0]):
        out = jnp.where(sel == a, tab[a:a + 1, :], out)
    return out


def _peer_select_kernel(h2_ref, w_ref, k1_ref, k2_ref, idx_ref, g_ref):
    q = jnp.dot(h2_ref[...].astype(BF16), w_ref[...], preferred_element_type=F32)
    nt = (((1,), (1,)), ((), ()))
    s1 = lax.dot_general(k1_ref[...], q[:, :PEER_NKEYS].astype(BF16), nt, preferred_element_type=F32)
    s2 = lax.dot_general(k2_ref[...], q[:, PEER_NKEYS:].astype(BF16), nt, preferred_element_type=F32)
    v1, i1 = _topk_rows(s1, PEER_TOPK)
    v2, i2 = _topk_rows(s2, PEER_TOPK)
    cand = jnp.concatenate([v1[a:a + 1, :] + v2 for a in range(PEER_TOPK)], axis=0)
    sc, ci = _topk_rows(cand, PEER_TOPK)
    e1 = _take_rows(i1, ci // PEER_TOPK)
    e2 = _take_rows(i2, ci % PEER_TOPK)
    idx_ref[...] = e1 * PEER_NKEYS + e2
    p = jnp.exp(sc - sc[0:1, :])
    g_ref[...] = p / jnp.sum(p, axis=0, keepdims=True)


def _peer_select(h2, w_pq_bf, k1_bf, k2_bf, *, tt=256):
    n, d = h2.shape
    assert n % tt == 0
    return pl.pallas_call(
        _peer_select_kernel,
        grid=(n // tt, PEER_HEADS),
        in_specs=[
            pl.BlockSpec((tt, d), lambda i, h: (i, 0)),
            pl.BlockSpec((d, PEER_DKEY), lambda i, h: (0, h)),
            pl.BlockSpec((PEER_NKEYS, PEER_DKEY // 2), lambda i, h: (0, 0)),
            pl.BlockSpec((PEER_NKEYS, PEER_DKEY // 2), lambda i, h: (0, 0)),
        ],
        out_specs=[pl.BlockSpec((PEER_TOPK, tt), lambda i, h: (h, i)),
                   pl.BlockSpec((PEER_TOPK, tt), lambda i, h: (h, i))],
        out_shape=[jax.ShapeDtypeStruct((NSEL, n), jnp.int32),
                   jax.ShapeDtypeStruct((NSEL, n), F32)],
        compiler_params=pltpu.CompilerParams(dimension_semantics=("arbitrary", "arbitrary")),
        name="peer_select",
    )(h2, w_pq_bf, k1_bf, k2_bf)


def rmsnorm(x, gain):
    xf = x.astype(F32)
    y = xf * lax.rsqrt(jnp.mean(xf * xf, axis=-1, keepdims=True) + EPS)
    return (y * gain.astype(F32)).astype(x.dtype)


def modulate(x, gain, shift, scale):
    return rmsnorm(x, gain) * (1 + scale[:, None, :]) + shift[:, None, :]


def rope(x, pos):
    d = x.shape[-1]
    half = d // 2
    inv = ROPE_THETA ** (-jnp.arange(half, dtype=F32) / half)
    ang = pos.astype(F32)[:, None] * inv[None, :]
    ang = ang.reshape((1, ang.shape[0]) + (1,) * (x.ndim - 3) + (half,))
    cos, sin = jnp.cos(ang), jnp.sin(ang)
    xf = x.astype(F32)
    x1, x2 = xf[..., :half], xf[..., half:]
    return jnp.concatenate([x1 * cos - x2 * sin, x2 * cos + x1 * sin], axis=-1).astype(x.dtype)


def gla_recurrence(q, k, v, log_a, h0):
    B, L, H, _ = q.shape
    dv = v.shape[-1]
    n = L // GLA_BLOCK

    def blocks(t):
        return jnp.moveaxis(t.reshape(B, n, GLA_BLOCK, H, t.shape[-1]), 1, 0)

    causal = jnp.tril(jnp.ones((GLA_BLOCK, GLA_BLOCK), dtype=bool))

    def step(h, xs):
        qb, kb, vb, gb = xs
        b = jnp.cumsum(gb, axis=1)
        qe = qb * jnp.exp(b)
        ke = kb * jnp.exp(-b)
        a = jnp.where(causal, jnp.einsum('bthk,bshk->bhts', qe, ke), 0.0)
        o = jnp.einsum('bthk,bhkv->bthv', qe, h) + jnp.einsum('bhts,bshv->bthv', a, vb)
        b_last = b[:, -1]
        h = jnp.exp(b_last)[..., None] * h + jnp.einsum('bshk,bshv->bhkv', kb * jnp.exp(b_last[:, None] - b), vb)
        return h, o

    h, o = lax.scan(step, h0, (blocks(q), blocks(k), blocks(v), blocks(log_a)))
    o = jnp.moveaxis(o, 0, 1).reshape(B, L, H, dv)
    return o, h


def diff_attend(q1, q2, k1, k2, v, mask, lam):
    scale = DIFF_HD ** -0.5
    s1 = jnp.einsum('bqhd,bkhd->bhqk', q1, k1).astype(F32) * scale
    s2 = jnp.einsum('bqhd,bkhd->bhqk', q2, k2).astype(F32) * scale
    if mask is not None:
        s1 = jnp.where(mask, s1, -jnp.inf)
        s2 = jnp.where(mask, s2, -jnp.inf)
    p = jax.nn.softmax(s1, axis=-1) - lam * jax.nn.softmax(s2, axis=-1)
    return jnp.einsum('bhqk,bkhv->bqhv', p.astype(v.dtype), v)


def mixer_layer(x, pos, mod, past_k, past_v, gla_h0, lambda_init,
                norm1_gain, norm2_gain, w_in, w_alpha, b_alpha, gla_gain, w_gla_out,
                q_gain, k_gain, lam_q1, lam_k1, lam_q2, lam_k2, diff_gain, w_diff_out, w_out):
    B, L, _ = x.shape
    sh1, sc1, gt1, sh2, sc2, gt2 = jnp.split(mod, 6, axis=-1)
    h = modulate(x, norm1_gain, sh1, sc1)
    z = h @ w_in
    gq, gk, gv, gr, glr, dq, dk, dv, gates = jnp.split(z, np.cumsum(IN_SIZES)[:-1].tolist(), axis=-1)
    q = gq.reshape(B, L, GLA_HEADS, GLA_DK) * (GLA_DK ** -0.5)
    k = gk.reshape(B, L, GLA_HEADS, GLA_DK)
    v = gv.reshape(B, L, GLA_HEADS, GLA_DV)
    log_a = (jax.nn.log_sigmoid((glr @ w_alpha + b_alpha).astype(F32)) / GLA_TAU).reshape(B, L, GLA_HEADS, GLA_DK)
    o_a, gla_state = gla_recurrence(q, k, v, log_a, gla_h0.astype(F32))
    o_a = rmsnorm(o_a, gla_gain) * jax.nn.silu(gr.reshape(B, L, GLA_HEADS, GLA_DV))
    y_a = o_a.reshape(B, L, GLA_V_W) @ w_gla_out
    qd = rope(rmsnorm(dq.reshape(B, L, DIFF_HEADS, 2, DIFF_HD), q_gain), pos)
    kd = rope(rmsnorm(dk.reshape(B, L, DIFF_HEADS, 2, DIFF_HD), k_gain), pos)
    k_rows = kd.reshape(B, L, DIFF_HEADS, 2 * DIFF_HD)
    v_rows = dv.reshape(B, L, DIFF_HEADS, DIFF_VD)
    lam = (jnp.exp(jnp.sum(lam_q1.astype(F32) * lam_k1.astype(F32)))
           - jnp.exp(jnp.sum(lam_q2.astype(F32) * lam_k2.astype(F32))) + lambda_init)
    q1, q2 = qd[..., 0, :], qd[..., 1, :]
    if past_k is None:
        q_bf = (qd.reshape(B, L, DIFF_QK_W) * (math.log2(math.e) * DIFF_HD ** -0.5)).astype(BF16)
        o_b = _attn_prompt(q_bf, kd.reshape(B, L, DIFF_QK_W).astype(BF16), dv.astype(BF16),
                           lam.reshape(1, 1)).reshape(B, L, DIFF_HEADS, DIFF_VD)
    else:
        keys = jnp.concatenate([past_k.astype(k_rows.dtype), k_rows], axis=1)
        vals = jnp.concatenate([past_v.astype(v_rows.dtype), v_rows], axis=1)
        o_b = diff_attend(q1, q2, keys[..., :DIFF_HD], keys[..., DIFF_HD:], vals, None, lam)
    o_b = rmsnorm(o_b, diff_gain) * (1.0 - lambda_init)
    y_b = o_b.reshape(B, L, DIFF_V_W) @ w_diff_out
    g_a, g_b = jnp.split(gates, 2, axis=-1)
    mix = (jax.nn.sigmoid(g_a) * y_a + jax.nn.sigmoid(g_b) * y_b) @ w_out
    x = x + gt1[:, None, :] * mix
    h2 = modulate(x, norm2_gain, sh2, sc2)
    return x.reshape(B * L, D_MODEL), h2.reshape(B * L, D_MODEL), k_rows, v_rows, gla_state.astype(x.dtype)


def kernel(x_prompt, x_sample, c_prompt, c_sample, cache_k, cache_v, state_gla, norm1_gain, norm2_gain, w_mod, b_mod, w_in, w_alpha, b_alpha, gla_gain, w_gla_out, q_gain, k_gain, lam_q1, lam_k1, lam_q2, lam_k2, diff_gain, w_diff_out, w_out, w_pq, sub_keys1, sub_keys2, peer_u, peer_v):
    bp, lp, _ = x_prompt.shape
    bs, ls, _ = x_sample.shape
    pos_p = jnp.arange(lp)
    pos_s = cache_k.shape[2] + jnp.arange(ls)
    h0_p = jnp.zeros((bp, GLA_HEADS, GLA_DK, GLA_DV), x_prompt.dtype)
    l = 0
    lambda_init = 0.8 - 0.6 * math.exp(-0.3 * l)
    nb = bp + bs
    c_all = jnp.concatenate([c_prompt, c_sample, jnp.zeros(((-nb) % 8, D_MODEL), F32)], axis=0)
    mod_all = _adaln(c_all, w_mod[l], b_mod[l])
    lw = (norm1_gain[l], norm2_gain[l], w_in[l], w_alpha[l], b_alpha[l],
          gla_gain[l], w_gla_out[l], q_gain[l], k_gain[l], lam_q1[l], lam_k1[l], lam_q2[l], lam_k2[l],
          diff_gain[l], w_diff_out[l], w_out[l])
    x1p, h2p, kp, vp, sp = mixer_layer(x_prompt, pos_p, mod_all[:bp], None, None, h0_p, lambda_init, *lw)
    x1s, h2s, ks, vs, ss = mixer_layer(x_sample, pos_s, mod_all[bp:nb], cache_k[l], cache_v[l], state_gla[l], lambda_init, *lw)
    x1 = jnp.concatenate([x1p, x1s], axis=0)
    h2 = jnp.concatenate([h2p, h2s], axis=0)
    idx_t, g_t = _peer_select(h2, w_pq[l].astype(BF16), sub_keys1[l].astype(BF16), sub_keys2[l].astype(BF16))
    idx, g = idx_t.T, g_t.T
    table = jnp.concatenate([peer_u[l], peer_v[l]], axis=1).reshape(-1, 1, 2 * D_MODEL)
    gt2 = mod_all[:nb, 5 * D_MODEL:].reshape(nb, 1, D_MODEL)
    y = _peer_gather(idx, g, h2, x1, gt2, table, tt=PEER_TOK_TILE,
                     n_prompt=bp * lp, l_prompt=lp, nb_prompt=bp, l_sample=ls)
    yp = y[:bp * lp].reshape(bp, lp, D_MODEL)
    ys = y[bp * lp:].reshape(bs, ls, D_MODEL)
    return (yp, ys, kp[None], vp[None], sp[None], ks[None], vs[None], ss[None])
```

```python
import math
import functools
import jax, jax.numpy as jnp
from jax import lax
from jax.experimental import pallas as pl
from jax.experimental.pallas import tpu as pltpu

D_MODEL = 1024
CHUNK = 64
EPS = 1e-6
GLA_HEADS = 4
GLA_DK = 128
GLA_DV = 256
GLA_LOWRANK = 16
GLA_TAU = 16.0
GLA_BLOCK = 16
DIFF_HEADS = 8
DIFF_HD = 64
DIFF_VD = 128
ROPE_THETA = 10000.0
PEER_HEADS = 8
PEER_NKEYS = 128
PEER_DKEY = 256
PEER_TOPK = 16
GLA_QK_W = GLA_HEADS * GLA_DK
GLA_V_W = GLA_HEADS * GLA_DV
DIFF_QK_W = DIFF_HEADS * 2 * DIFF_HD
DIFF_V_W = DIFF_HEADS * DIFF_VD
F32 = jnp.float32
BF16 = jnp.bfloat16
NSEL = PEER_HEADS * PEER_TOPK
LANES = 128

PROJ_TILE = 512
POST_TILE = 256
GLA_TILE = 256
SAMPLE_TILE = 256
PEER_TOK_TILE = 8
ATTN_NEG = -1e30
ATTN_TILE = 512


def _mod_kernel(c_ref, w_ref, b_ref, o_ref):
    c = c_ref[...]
    s = c * jax.nn.sigmoid(c)
    o_ref[...] = jnp.dot(s.astype(BF16), w_ref[...].astype(BF16), preferred_element_type=F32) + b_ref[...]


def _adaln(c, w_mod, b_mod):
    n = c.shape[0]
    tn = 1536
    return pl.pallas_call(
        _mod_kernel,
        grid=(6 * D_MODEL // tn,),
        in_specs=[pl.BlockSpec((n, D_MODEL), lambda j: (0, 0)),
                  pl.BlockSpec((D_MODEL, tn), lambda j: (0, j)),
                  pl.BlockSpec((1, tn), lambda j: (0, j))],
        out_specs=pl.BlockSpec((n, tn), lambda j: (0, j)),
        out_shape=jax.ShapeDtypeStruct((n, 6 * D_MODEL), F32),
        name="adaln",
    )(c, w_mod, b_mod.reshape(1, -1))


def _modulated_norm(x, gain, scale, shift):
    y = x * lax.rsqrt(jnp.mean(x * x, axis=-1, keepdims=True) + EPS)
    return (y * gain) * (1.0 + scale) + shift


def _proj_gla_kernel(x_ref, sh_ref, sc_ref, gain_ref, w_ref, wa_ref, ba_ref,
                     q_ref, k_ref, v_ref, r_ref, la_ref):
    h = _modulated_norm(x_ref[...], gain_ref[...], sc_ref[0], sh_ref[0]).astype(BF16)
    z = jnp.dot(h, w_ref[...], preferred_element_type=F32)
    q_ref[...] = z[:, 0:GLA_QK_W] * (GLA_DK ** -0.5)
    k_ref[...] = z[:, GLA_QK_W:2 * GLA_QK_W]
    v_ref[...] = z[:, 2 * GLA_QK_W:2 * GLA_QK_W + GLA_V_W]
    r_ref[...] = z[:, 2 * GLA_QK_W + GLA_V_W:2 * GLA_QK_W + 2 * GLA_V_W]
    glr = z[:, 2 * GLA_QK_W + 2 * GLA_V_W:]
    a = jnp.dot(glr.astype(BF16), wa_ref[...], preferred_element_type=F32) + ba_ref[...]
    la_ref[...] = jax.nn.log_sigmoid(a) * (1.0 / GLA_TAU)


def _group_mean_square(x, ones_ref):
    s = x * x
    hi = s.astype(BF16)
    lo = (s - hi.astype(F32)).astype(BF16)
    ss = (jnp.dot(hi, ones_ref[...], preferred_element_type=F32)
          + jnp.dot(lo, ones_ref[...], preferred_element_type=F32))
    return ss * (1.0 / DIFF_HD)


def _norm_rope(x, gain, cos, sin_signed, ones_ref):
    outs = []
    lane = lax.broadcasted_iota(jnp.int32, (x.shape[0], LANES), 1)
    first_half = (lane % DIFF_HD) < (DIFF_HD // 2)
    for b in range(DIFF_QK_W // LANES):
        xb = x[:, b * LANES:(b + 1) * LANES]
        y = xb * lax.rsqrt(_group_mean_square(xb, ones_ref) + EPS) * gain
        rot = jnp.where(first_half, pltpu.roll(y, LANES - DIFF_HD // 2, axis=1), pltpu.roll(y, DIFF_HD // 2, axis=1))
        outs.append(y * cos + rot * sin_signed)
    return jnp.concatenate(outs, axis=1)


def _proj_diff_kernel(x_ref, sh_ref, sc_ref, gain_ref, w_ref, qg_ref, kg_ref, cos_ref, sin_ref, ones_ref,
                      qb_ref, k_ref, kb_ref, v_ref, vb_ref, *, q_scale):
    h = _modulated_norm(x_ref[...], gain_ref[...], sc_ref[0], sh_ref[0]).astype(BF16)
    z = jnp.dot(h, w_ref[...], preferred_element_type=F32)
    cos, sin = cos_ref[...], sin_ref[...]
    q = _norm_rope(z[:, 0:DIFF_QK_W], qg_ref[...], cos, sin, ones_ref)
    qb_ref[...] = (q * q_scale).astype(BF16)
    k = _norm_rope(z[:, DIFF_QK_W:2 * DIFF_QK_W], kg_ref[...], cos, sin, ones_ref)
    k_ref[...] = k
    kb_ref[...] = k.astype(BF16)
    v = z[:, 2 * DIFF_QK_W:2 * DIFF_QK_W + DIFF_V_W]
    v_ref[...] = v
    vb_ref[...] = v.astype(BF16)


def _proj_gla(x, shift, scale, gain, w_a, w_alpha_pad, b_alpha, *, tt, mod_tiles):
    n = x.shape[0]
    assert n % tt == 0
    bmap = lambda i: (i // mod_tiles, 0, 0)
    mod = pl.BlockSpec((1,) + shift.shape[1:], bmap)
    row = lambda w: pl.BlockSpec((tt, w), lambda i: (i, 0))
    const = lambda a: pl.BlockSpec(a.shape, lambda i: (0,) * a.ndim)
    return pl.pallas_call(
        _proj_gla_kernel,
        grid=(n // tt,),
        in_specs=[row(D_MODEL), mod, mod,
                  const(gain), const(w_a), const(w_alpha_pad), const(b_alpha)],
        out_specs=[row(GLA_QK_W), row(GLA_QK_W), row(GLA_V_W), row(GLA_V_W), row(GLA_QK_W)],
        out_shape=[jax.ShapeDtypeStruct((n, GLA_QK_W), F32), jax.ShapeDtypeStruct((n, GLA_QK_W), F32),
                   jax.ShapeDtypeStruct((n, GLA_V_W), F32), jax.ShapeDtypeStruct((n, GLA_V_W), F32),
                   jax.ShapeDtypeStruct((n, GLA_QK_W), F32)],
        compiler_params=pltpu.CompilerParams(dimension_semantics=("arbitrary",), vmem_limit_bytes=48 * 1024 * 1024),
        name="proj_gla",
    )(x, shift, scale, gain, w_a, w_alpha_pad, b_alpha)


def _proj_diff(x, shift, scale, gain, w_b, q_gain128, k_gain128, cos128, sin128, ones128, *, tt, mod_tiles):
    n = x.shape[0]
    assert n % tt == 0 and cos128.shape[0] % tt == 0
    bmap = lambda i: (i // mod_tiles, 0, 0)
    mod = pl.BlockSpec((1,) + shift.shape[1:], bmap)
    pos_tiles = cos128.shape[0] // tt
    row = lambda w: pl.BlockSpec((tt, w), lambda i: (i, 0))
    const = lambda a: pl.BlockSpec(a.shape, lambda i: (0,) * a.ndim)
    pos = pl.BlockSpec((tt, LANES), lambda i: (i % pos_tiles, 0))
    q_scale = math.log2(math.e) * DIFF_HD ** -0.5
    return pl.pallas_call(
        functools.partial(_proj_diff_kernel, q_scale=q_scale),
        grid=(n // tt,),
        in_specs=[row(D_MODEL), mod, mod,
                  const(gain), const(w_b), const(q_gain128), const(k_gain128), pos, pos, const(ones128)],
        out_specs=[row(DIFF_QK_W)] * 5,
        out_shape=[jax.ShapeDtypeStruct((n, DIFF_QK_W), BF16), jax.ShapeDtypeStruct((n, DIFF_QK_W), F32),
                   jax.ShapeDtypeStruct((n, DIFF_QK_W), BF16), jax.ShapeDtypeStruct((n, DIFF_V_W), F32),
                   jax.ShapeDtypeStruct((n, DIFF_V_W), BF16)],
        compiler_params=pltpu.CompilerParams(dimension_semantics=("arbitrary",), vmem_limit_bytes=48 * 1024 * 1024),
        name="proj_diff",
    )(x, shift, scale, gain, w_b, q_gain128, k_gain128, cos128, sin128, ones128)


def _gla_kernel(q_ref, k_ref, la_ref, v_ref, h0_ref, o_ref, hT_ref, st_ref, *, tb):
    i = pl.program_id(2)
    nb = tb // GLA_BLOCK

    @pl.when(i == 0)
    def _init():
        st_ref[...] = h0_ref[0, 0].T

    row = lax.broadcasted_iota(jnp.int32, (tb, tb), 0)
    col = lax.broadcasted_iota(jnp.int32, (tb, tb), 1)
    same = (row // GLA_BLOCK) == (col // GLA_BLOCK)
    causal = same & (col <= row)
    la = la_ref[...]
    hp = lax.Precision.HIGHEST
    b = jnp.dot(causal.astype(F32), la, precision=hp, preferred_element_type=F32)
    b_last = jnp.dot(same.astype(F32), la, precision=hp, preferred_element_type=F32)
    q, k, v = q_ref[...], k_ref[...], v_ref[...]
    qe = (q * jnp.exp(b)).astype(BF16)
    ke = (k * jnp.exp(-b)).astype(BF16)
    kd = (k * jnp.exp(b_last - b)).astype(BF16)
    vb = v.astype(BF16)
    a = lax.dot_general(qe, ke, (((1,), (1,)), ((), ())), preferred_element_type=F32)
    a = jnp.where(causal, a, 0.0).astype(BF16)
    o_intra = jnp.dot(a, vb, preferred_element_type=F32)

    st = st_ref[...]
    outs = []
    for n in range(nb):
        rs = slice(n * GLA_BLOCK, (n + 1) * GLA_BLOCK)
        o_inter = lax.dot_general(qe[rs], st.astype(BF16), (((1,), (1,)), ((), ())), preferred_element_type=F32)
        outs.append(o_inter + o_intra[rs])
        u = lax.dot_general(vb[rs], kd[rs], (((0,), (0,)), ((), ())), preferred_element_type=F32)
        st = jnp.exp(b_last[n * GLA_BLOCK:n * GLA_BLOCK + 1, :]) * st + u
    o_ref[...] = jnp.concatenate(outs, axis=0)
    st_ref[...] = st

    @pl.when(i == pl.num_programs(2) - 1)
    def _fin():
        hT_ref[0, 0] = st.T


def _gla(q, k, la, v, h0, *, l_seq, tb):
    n = q.shape[0]
    bsz = n // l_seq
    nt = l_seq // tb
    assert l_seq % tb == 0 and tb % GLA_BLOCK == 0
    tok = lambda w: pl.BlockSpec((tb, w), lambda b, h, i: (b * nt + i, h))
    st = pl.BlockSpec((1, 1, GLA_DK, GLA_DV), lambda b, h, i: (b, h, 0, 0))
    return pl.pallas_call(
        functools.partial(_gla_kernel, tb=tb),
        grid=(bsz, GLA_HEADS, nt),
        in_specs=[tok(GLA_DK), tok(GLA_DK), tok(GLA_DK), tok(GLA_DV), st],
        out_specs=[tok(GLA_DV), st],
        out_shape=[jax.ShapeDtypeStruct((n, GLA_V_W), F32),
                   jax.ShapeDtypeStruct((bsz, GLA_HEADS, GLA_DK, GLA_DV), F32)],
        scratch_shapes=[pltpu.VMEM((GLA_DV, GLA_DK), F32)],
        compiler_params=pltpu.CompilerParams(dimension_semantics=("arbitrary", "arbitrary", "arbitrary")),
        name="gla",
    )(q, k, la, v, h0)


def _split_maps(q):
    lane = lax.broadcasted_iota(jnp.int32, q.shape, 1)
    zero = jnp.zeros_like(q)
    return jnp.concatenate([jnp.where(lane < DIFF_HD, q, zero), jnp.where(lane >= DIFF_HD, q, zero)], axis=0)


def _attn_prompt_kernel(lam_ref, q_ref, k_ref, v_ref, o_ref, vt_ref, *, tq, tk):
    i = pl.program_id(2)
    n_kt = v_ref.shape[1] // tk

    @pl.when(i == 0)
    def _transpose_v():
        def body(j, c):
            vt_ref[j] = v_ref[0, pl.ds(j * tk, tk), :].astype(F32).T.astype(BF16)
            return c
        lax.fori_loop(0, n_kt, body, 0)

    qp = _split_maps(q_ref[0])

    def step(j, carry, masked, koff=0):
        m, l, acc = carry
        kt = k_ref[0, pl.ds(j * tk, tk), :]
        st = lax.dot_general(kt, qp, (((1,), (1,)), ((), ())), preferred_element_type=F32)
        if masked:
            kc = (lax.broadcasted_iota(jnp.int32, st.shape, 0) + koff) // CHUNK
            qc = (lax.broadcasted_iota(jnp.int32, st.shape, 1) % tq) // CHUNK
            st = jnp.where(qc >= kc, st, ATTN_NEG)
        m_new = jnp.maximum(m, jnp.max(st, axis=0, keepdims=True))
        alpha = jnp.exp2(m - m_new)
        p = jnp.exp2(st - m_new)
        l = alpha * l + jnp.sum(p, axis=0, keepdims=True)
        acc = alpha * acc + jnp.dot(vt_ref[j], p.astype(BF16), preferred_element_type=F32)
        return m_new, l, acc

    carry = (jnp.full((1, 2 * tq), ATTN_NEG, F32), jnp.zeros((1, 2 * tq), F32), jnp.zeros((LANES, 2 * tq), F32))
    r = tq // tk
    carry = lax.fori_loop(0, i * r, functools.partial(step, masked=False), carry)
    for d in range(r):
        carry = step(i * r + d, carry, True, koff=d * tk)
    m, l, acc = carry
    o = acc / l
    ot = o[:, :tq] - lam_ref[0, 0] * o[:, tq:]
    o_ref[0] = ot.T


def _attn_prompt(q, k, v, lam, *, tq=ATTN_TILE, tk=ATTN_TILE):
    b, l, w = q.shape
    h = w // LANES
    assert l % tq == 0 and tq % tk == 0 and tk % CHUNK == 0
    return pl.pallas_call(
        functools.partial(_attn_prompt_kernel, tq=tq, tk=tk),
        grid=(b, h, l // tq),
        in_specs=[
            pl.BlockSpec(memory_space=pltpu.SMEM),
            pl.BlockSpec((1, tq, LANES), lambda bi, hi, i: (bi, i, hi)),
            pl.BlockSpec((1, l, LANES), lambda bi, hi, i: (bi, 0, hi)),
            pl.BlockSpec((1, l, LANES), lambda bi, hi, i: (bi, 0, hi)),
        ],
        out_specs=pl.BlockSpec((1, tq, LANES), lambda bi, hi, i: (bi, i, hi)),
        out_shape=jax.ShapeDtypeStruct((b, l, w), F32),
        scratch_shapes=[pltpu.VMEM((l // tk, LANES, tk), BF16)],
        compiler_params=pltpu.CompilerParams(dimension_semantics=("arbitrary", "arbitrary", "arbitrary"),
                                             vmem_limit_bytes=40 * 1024 * 1024),
        name="attn_prompt",
    )(lam, q, k, v)


def _attn_sample_kernel(lam_ref, q_ref, kn_ref, vn_ref, kc_ref, vc_ref, o_ref, m_ref, l_ref, acc_ref, *, heads):
    j = pl.program_id(1)
    lq = q_ref.shape[0]
    nt = (((1,), (1,)), ((), ()))

    @pl.when(j == 0)
    def _init():
        m_ref[...] = jnp.full(m_ref.shape, ATTN_NEG, F32)
        l_ref[...] = jnp.zeros(l_ref.shape, F32)
        acc_ref[...] = jnp.zeros(acc_ref.shape, F32)

    def update(h, k, v):
        qp = _split_maps(q_ref[:, h * LANES:(h + 1) * LANES])
        s = lax.dot_general(qp, k, nt, preferred_element_type=F32)
        m_old = m_ref[h]
        m_new = jnp.maximum(m_old, jnp.max(s, axis=-1, keepdims=True))
        alpha = jnp.exp2(m_old - m_new)
        p = jnp.exp2(s - m_new)
        l_ref[h] = alpha * l_ref[h] + jnp.sum(p, axis=-1, keepdims=True)
        acc_ref[h] = alpha * acc_ref[h] + jnp.dot(p.astype(BF16), v, preferred_element_type=F32)
        m_ref[h] = m_new

    for h in range(heads):
        update(h, kc_ref[:, h, :].astype(BF16), vc_ref[:, h, :].astype(BF16))

    @pl.when(j == pl.num_programs(1) - 1)
    def _fin():
        for h in range(heads):
            update(h, kn_ref[:, h * LANES:(h + 1) * LANES], vn_ref[:, h * LANES:(h + 1) * LANES])
            o = acc_ref[h] / l_ref[h]
            o_ref[:, h * LANES:(h + 1) * LANES] = o[:lq] - lam_ref[0, 0] * o[lq:]


def _attn_sample(q, k_new, v_new, cache_k, cache_v, lam, *, l_seq, tk=1024):
    n, w = q.shape
    bsz, past, heads, _ = cache_k.shape
    tk = min(tk, past)
    assert n == bsz * l_seq and w == heads * LANES and past % tk == 0
    tok = pl.BlockSpec((l_seq, w), lambda b, j: (b, 0))
    cache = pl.BlockSpec((None, tk, heads, LANES), lambda b, j: (b, j, 0, 0))
    return pl.pallas_call(
        functools.partial(_attn_sample_kernel, heads=heads),
        grid=(bsz, past // tk),
        in_specs=[pl.BlockSpec(memory_space=pltpu.SMEM), tok, tok, tok, cache, cache],
        out_specs=tok,
        out_shape=jax.ShapeDtypeStruct((n, w), F32),
        scratch_shapes=[pltpu.VMEM((heads, 2 * l_seq, 1), F32), pltpu.VMEM((heads, 2 * l_seq, 1), F32),
                        pltpu.VMEM((heads, 2 * l_seq, LANES), F32)],
        compiler_params=pltpu.CompilerParams(dimension_semantics=("arbitrary", "arbitrary"),
                                             vmem_limit_bytes=40 * 1024 * 1024),
        name="attn_sample",
    )(lam, q, k_new, v_new, cache_k, cache_v)


def _head_norm(x, width):
    outs = []
    for g in range(x.shape[1] // width):
        xg = x[:, g * width:(g + 1) * width]
        outs.append(xg * lax.rsqrt(jnp.mean(xg * xg, axis=-1, keepdims=True) + EPS))
    return jnp.concatenate(outs, axis=1)


def _post_kernel(x_ref, oa_ref, gr_ref, ob_ref, sh1_ref, sc1_ref, gt1_ref, sh2_ref, sc2_ref,
                 n1_ref, n2_ref, ga_ref, gb_ref, wa_ref, wb_ref, wg_ref, wo_ref, x1_ref, h2_ref, *, diff_out_scale):
    x = x_ref[...]
    h = _modulated_norm(x, n1_ref[...], sc1_ref[0], sh1_ref[0]).astype(BF16)
    gates = jnp.dot(h, wg_ref[...], preferred_element_type=F32)
    gr = gr_ref[...]
    oa = _head_norm(oa_ref[...], GLA_DV) * ga_ref[...] * (gr * jax.nn.sigmoid(gr))
    ya = jnp.dot(oa.astype(BF16), wa_ref[...], preferred_element_type=F32)
    ob = _head_norm(ob_ref[...], DIFF_VD) * gb_ref[...] * diff_out_scale
    yb = jnp.dot(ob.astype(BF16), wb_ref[...], preferred_element_type=F32)
    merged = jax.nn.sigmoid(gates[:, :D_MODEL]) * ya + jax.nn.sigmoid(gates[:, D_MODEL:]) * yb
    mix = jnp.dot(merged.astype(BF16), wo_ref[...], preferred_element_type=F32)
    x1 = x + gt1_ref[0] * mix
    x1_ref[...] = x1
    h2_ref[...] = _modulated_norm(x1, n2_ref[...], sc2_ref[0], sh2_ref[0])


def _post_mixer(x, oa, gr, ob, mods, n1, n2, ga, gb, wa, wb, wg, wo, *, tt, mod_tiles, diff_out_scale):
    n = x.shape[0]
    assert n % tt == 0
    bmap = lambda i: (i // mod_tiles, 0, 0)
    mod = pl.BlockSpec((1,) + mods[0].shape[1:], bmap)
    row = pl.BlockSpec((tt, D_MODEL), lambda i: (i, 0))
    const = lambda a: pl.BlockSpec(a.shape, lambda i: (0,) * a.ndim)
    return pl.pallas_call(
        functools.partial(_post_kernel, diff_out_scale=diff_out_scale),
        grid=(n // tt,),
        in_specs=[row, row, row, row] + [mod] * 5 + [const(a) for a in (n1, n2, ga, gb, wa, wb, wg, wo)],
        out_specs=[row, row],
        out_shape=[jax.ShapeDtypeStruct((n, D_MODEL), F32), jax.ShapeDtypeStruct((n, D_MODEL), F32)],
        compiler_params=pltpu.CompilerParams(dimension_semantics=("arbitrary",), vmem_limit_bytes=48 * 1024 * 1024),
        name="post_mixer",
    )(x, oa, gr, ob, *mods, n1, n2, ga, gb, wa, wb, wg, wo)


def _topk_rows(s, k):
    r = s.shape[0]
    row = lax.broadcasted_iota(jnp.int32, s.shape, 0)
    vals, idxs = [], []
    for _ in range(k):
        mx = jnp.max(s, axis=0, keepdims=True)
        am = jnp.min(jnp.where(s == mx, row, r), axis=0, keepdims=True)
        vals.append(mx)
        idxs.append(am)
        s = jnp.where(row == am, -jnp.inf, s)
    return jnp.concatenate(vals, axis=0), jnp.concatenate(idxs, axis=0)


def _take_rows(tab, sel):
    out = jnp.zeros(sel.shape, tab.dtype)
    for a in range(tab.shape[0]):
        out = jnp.where(sel == a, tab[a:a + 1, :], out)
    return out


def _peer_select_kernel(h2_ref, w_ref, k1_ref, k2_ref, idx_ref, g_ref):
    q = jnp.dot(h2_ref[...].astype(BF16), w_ref[...], preferred_element_type=F32)
    nt = (((1,), (1,)), ((), ()))
    s1 = lax.dot_general(k1_ref[...], q[:, :PEER_NKEYS].astype(BF16), nt, preferred_element_type=F32)
    s2 = lax.dot_general(k2_ref[...], q[:, PEER_NKEYS:].astype(BF16), nt, preferred_element_type=F32)
    v1, i1 = _topk_rows(s1, PEER_TOPK)
    v2, i2 = _topk_rows(s2, PEER_TOPK)
    cand = jnp.concatenate([v1[a:a + 1, :] + v2 for a in range(PEER_TOPK)], axis=0)
    sc, ci = _topk_rows(cand, PEER_TOPK)
    e1 = _take_rows(i1, ci // PEER_TOPK)
    e2 = _take_rows(i2, ci % PEER_TOPK)
    idx_ref[...] = e1 * PEER_NKEYS + e2
    p = jnp.exp(sc - sc[0:1, :])
    g_ref[...] = p / jnp.sum(p, axis=0, keepdims=True)


def _peer_select(h2, w_pq_bf, k1_bf, k2_bf, *, tt=256):
    n, d = h2.shape
    assert n % tt == 0
    return pl.pallas_call(
        _peer_select_kernel,
        grid=(n // tt, PEER_HEADS),
        in_specs=[
            pl.BlockSpec((tt, d), lambda i, h: (i, 0)),
            pl.BlockSpec((d, PEER_DKEY), lambda i, h: (0, h)),
            pl.BlockSpec((PEER_NKEYS, PEER_DKEY // 2), lambda i, h: (0, 0)),
            pl.BlockSpec((PEER_NKEYS, PEER_DKEY // 2), lambda i, h: (0, 0)),
        ],
        out_specs=[pl.BlockSpec((PEER_TOPK, tt), lambda i, h: (h, i)),
                   pl.BlockSpec((PEER_TOPK, tt), lambda i, h: (h, i))],
        out_shape=[jax.ShapeDtypeStruct((NSEL, n), jnp.int32),
                   jax.ShapeDtypeStruct((NSEL, n), F32)],
        compiler_params=pltpu.CompilerParams(dimension_semantics=("arbitrary", "arbitrary")),
        name="peer_select",
    )(h2, w_pq_bf, k1_bf, k2_bf)


def _peer_gather_kernel(idx_ref, h2_ref, g_ref, x1_ref, gate_ref, tab_ref, out_ref, buf, sem, *, tt):
    s = pl.program_id(0)
    n = pl.num_programs(0) - 1

    @pl.when(s < n)
    def _issue():
        slot = s % 2

        def issue_tok(t, carry):
            for j in range(NSEL):
                e = idx_ref[t, j]
                pltpu.make_async_copy(tab_ref.at[e], buf.at[slot, t, pl.ds(j, 1), :], sem.at[slot, t]).start()
            return carry

        lax.fori_loop(0, tt, issue_tok, 0)

    @pl.when(s >= 1)
    def _compute():
        slot = (s + 1) % 2
        eye = (lax.broadcasted_iota(jnp.int32, (NSEL, NSEL), 0)
               == lax.broadcasted_iota(jnp.int32, (NSEL, NSEL), 1))
        gate = gate_ref[0]

        def tok(t, carry):
            for j in range(NSEL):
                pltpu.make_async_copy(tab_ref.at[0], buf.at[slot, t, pl.ds(j, 1), :], sem.at[slot, t]).wait()
            x = h2_ref[pl.ds(t, 1), :]
            p = buf[slot, t, :, 0:LANES] * x[:, 0:LANES]
            for c in range(1, D_MODEL // LANES):
                p = p + buf[slot, t, :, c * LANES:(c + 1) * LANES] * x[:, c * LANES:(c + 1) * LANES]
            a = jnp.sum(p, axis=-1, keepdims=True)
            act = 0.5 * a * (1.0 + lax.erf(a * (0.5 ** 0.5)))
            grow = g_ref[pl.ds(t, 1), :]
            gcol = jnp.sum(jnp.where(eye, grow, 0.0), axis=-1, keepdims=True)
            coef = gcol * act
            y = jnp.sum(coef * buf[slot, t, :, D_MODEL:2 * D_MODEL], axis=0, keepdims=True)
            out_ref[pl.ds(t, 1), :] = x1_ref[pl.ds(t, 1), :] + gate * y
            return carry

        lax.fori_loop(0, tt, tok, 0)


def _peer_gather(idx, g, h2, x1, gates, table, *, tt, l_seq):
    n_tok = idx.shape[0]
    n_tiles = n_tok // tt
    assert n_tiles * tt == n_tok and l_seq % tt == 0

    def cur(s):
        return jnp.maximum(s - 1, 0)

    return pl.pallas_call(
        functools.partial(_peer_gather_kernel, tt=tt),
        grid=(n_tiles + 1,),
        in_specs=[
            pl.BlockSpec((tt, NSEL), lambda s: (jnp.minimum(s, n_tiles - 1), 0), memory_space=pltpu.SMEM),
            pl.BlockSpec((tt, D_MODEL), lambda s: (cur(s), 0)),
            pl.BlockSpec((tt, NSEL), lambda s: (cur(s), 0)),
            pl.BlockSpec((tt, D_MODEL), lambda s: (cur(s), 0)),
            pl.BlockSpec((1, 1, D_MODEL), lambda s: (cur(s) * tt // l_seq, 0, 0)),
            pl.BlockSpec(memory_space=pl.ANY),
        ],
        out_specs=pl.BlockSpec((tt, D_MODEL), lambda s: (cur(s), 0)),
        out_shape=jax.ShapeDtypeStruct((n_tok, D_MODEL), F32),
        scratch_shapes=[pltpu.VMEM((2, tt, NSEL, 2 * D_MODEL), F32), pltpu.SemaphoreType.DMA((2, tt))],
        compiler_params=pltpu.CompilerParams(dimension_semantics=("arbitrary",),
                                             vmem_limit_bytes=48 * 1024 * 1024),
        name="peer_gather",
    )(idx, h2, g, x1, gates, table)


def _rope_tables(pos):
    half = DIFF_HD // 2
    inv = ROPE_THETA ** (-jnp.arange(half, dtype=F32) / half)
    ang = pos.astype(F32)[:, None] * inv[None, :]
    cos, sin = jnp.cos(ang), jnp.sin(ang)
    return jnp.tile(cos, (1, 4)), jnp.tile(jnp.concatenate([-sin, sin], axis=1), (1, 2))


def _stream(x, mods, mod_tiles, tile, cos, sin, l_seq, h0, wts, attend, lambda_init):
    sh1, sc1, gt1, sh2, sc2, gate2 = mods
    q_g, k_g, v_g, r_g, la = _proj_gla(x, sh1, sc1, wts["n1"], wts["w_a"], wts["w_alpha"], wts["b_alpha"],
                                       tt=tile, mod_tiles=mod_tiles)
    q_bf, k_rows, k_bf, v_rows, v_bf = _proj_diff(x, sh1, sc1, wts["n1"], wts["w_b"], wts["q_gain"], wts["k_gain"],
                                                  cos, sin, wts["ones"], tt=tile, mod_tiles=mod_tiles)
    o_a, state = _gla(q_g, k_g, la, v_g, h0, l_seq=l_seq, tb=min(GLA_TILE, l_seq))
    o_b = attend(q_bf, k_bf, v_bf)
    post_tile = min(POST_TILE, tile)
    x1, h2 = _post_mixer(x, o_a, r_g, o_b, (sh1, sc1, gt1, sh2, sc2), wts["n1"], wts["n2"], wts["gla_gain"],
                         wts["diff_gain"], wts["w_gla_out"], wts["w_diff_out"], wts["w_g"], wts["w_out"],
                         tt=post_tile, mod_tiles=mod_tiles * (tile // post_tile), diff_out_scale=1.0 - lambda_init)
    idx_t, g_t = _peer_select(h2, wts["w_pq"], wts["k1"], wts["k2"])
    y = _peer_gather(idx_t.T, g_t.T, h2, x1, gate2, wts["table"], tt=PEER_TOK_TILE, l_seq=l_seq)
    return y, k_rows, v_rows, state


def kernel(x_prompt, x_sample, c_prompt, c_sample, cache_k, cache_v, state_gla, norm1_gain, norm2_gain, w_mod, b_mod, w_in, w_alpha, b_alpha, gla_gain, w_gla_out, q_gain, k_gain, lam_q1, lam_k1, lam_q2, lam_k2, diff_gain, w_diff_out, w_out, w_pq, sub_keys1, sub_keys2, peer_u, peer_v):
    bp, lp, _ = x_prompt.shape
    bs, ls, _ = x_sample.shape
    past = cache_k.shape[2]
    l = 0
    lambda_init = 0.8 - 0.6 * math.exp(-0.3 * l)
    nb = bp + bs
    c_all = jnp.concatenate([c_prompt, c_sample, jnp.zeros(((-nb) % 8, D_MODEL), F32)], axis=0)
    mod_all = _adaln(c_all, w_mod[l], b_mod[l])
    mods = [mod_all[:nb, i * D_MODEL:(i + 1) * D_MODEL] for i in range(6)]

    wi = w_in[l]
    o_glr = 2 * GLA_QK_W + 2 * GLA_V_W
    o_diff = o_glr + GLA_LOWRANK
    o_gate = o_diff + 2 * DIFF_QK_W + DIFF_V_W
    pad_lr = LANES - GLA_LOWRANK
    lane = jnp.arange(LANES)
    wts = dict(
        n1=norm1_gain[l][None], n2=norm2_gain[l][None],
        w_a=jnp.concatenate([wi[:, :o_diff], jnp.zeros((D_MODEL, pad_lr), F32)], axis=1).astype(BF16),
        w_alpha=jnp.concatenate([w_alpha[l], jnp.zeros((pad_lr, GLA_QK_W), F32)], axis=0).astype(BF16),
        b_alpha=b_alpha[l][None],
        w_b=wi[:, o_diff:o_gate].astype(BF16), w_g=wi[:, o_gate:].astype(BF16),
        q_gain=q_gain[l].reshape(1, LANES), k_gain=k_gain[l].reshape(1, LANES),
        ones=(lane[:, None] // DIFF_HD == lane[None, :] // DIFF_HD).astype(BF16),
        gla_gain=jnp.tile(gla_gain[l], GLA_HEADS)[None], diff_gain=jnp.tile(diff_gain[l], DIFF_HEADS)[None],
        w_gla_out=w_gla_out[l].astype(BF16), w_diff_out=w_diff_out[l].astype(BF16), w_out=w_out[l].astype(BF16),
        w_pq=w_pq[l].astype(BF16), k1=sub_keys1[l].astype(BF16), k2=sub_keys2[l].astype(BF16),
        table=jnp.concatenate([peer_u[l], peer_v[l]], axis=1).reshape(-1, 1, 2 * D_MODEL),
    )
    lam = (jnp.exp(jnp.sum(lam_q1[l] * lam_k1[l])) - jnp.exp(jnp.sum(lam_q2[l] * lam_k2[l])) + lambda_init).reshape(1, 1)

    cos_p, sin_p = _rope_tables(jnp.arange(lp))
    mods_p = [m[:bp, None, :] for m in mods]
    attend_p = lambda q, k, v: _attn_prompt(q.reshape(bp, lp, -1), k.reshape(bp, lp, -1), v.reshape(bp, lp, -1),
                                            lam).reshape(bp * lp, -1)
    h0_p = jnp.zeros((bp, GLA_HEADS, GLA_DK, GLA_DV), F32)
    yp, kp, vp, sp = _stream(x_prompt.reshape(bp * lp, D_MODEL), mods_p, lp // PROJ_TILE, PROJ_TILE, cos_p, sin_p,
                             lp, h0_p, wts, attend_p, lambda_init)

    seq_per_tile = SAMPLE_TILE // ls
    cos_s, sin_s = _rope_tables(jnp.tile(past + jnp.arange(ls), seq_per_tile))
    mods_s = [jnp.repeat(m[bp:nb], ls, axis=0).reshape(-1, SAMPLE_TILE, D_MODEL) for m in mods[:5]]
    mods_s.append(mods[5][bp:nb, None, :])
    attend_s = lambda q, k, v: _attn_sample(q, k, v, cache_k[l], cache_v[l], lam, l_seq=ls)
    ys, ks, vs, ss = _stream(x_sample.reshape(bs * ls, D_MODEL), mods_s, 1, SAMPLE_TILE, cos_s, sin_s,
                             ls, state_gla[l], wts, attend_s, lambda_init)

    return (yp.reshape(bp, lp, D_MODEL), ys.reshape(bs, ls, D_MODEL),
            kp.reshape(1, bp, lp, DIFF_HEADS, 2 * DIFF_HD), vp.reshape(1, bp, lp, DIFF_HEADS, DIFF_VD), sp[None],
            ks.reshape(1, bs, ls, DIFF_HEADS, 2 * DIFF_HD), vs.reshape(1, bs, ls, DIFF_HEADS, DIFF_VD), ss[None])
```

```python
import math
import functools
import jax, jax.numpy as jnp
from jax import lax
from jax.experimental import pallas as pl
from jax.experimental.pallas import tpu as pltpu

D_MODEL = 1024
CHUNK = 64
EPS = 1e-6
GLA_HEADS = 4
GLA_DK = 128
GLA_DV = 256
GLA_LOWRANK = 16
GLA_TAU = 16.0
GLA_BLOCK = 16
DIFF_HEADS = 8
DIFF_HD = 64
DIFF_VD = 128
ROPE_THETA = 10000.0
PEER_HEADS = 8
PEER_NKEYS = 128
PEER_DKEY = 256
PEER_TOPK = 16
GLA_QK_W = GLA_HEADS * GLA_DK
GLA_V_W = GLA_HEADS * GLA_DV
DIFF_QK_W = DIFF_HEADS * 2 * DIFF_HD
DIFF_V_W = DIFF_HEADS * DIFF_VD
F32 = jnp.float32
BF16 = jnp.bfloat16
NSEL = PEER_HEADS * PEER_TOPK
LANES = 128
SUBLANES = 8

PROJ_TILE = 512
POST_TILE = 256
GLA_TILE = 256
SAMPLE_TILE = 256
PEER_TOK_TILE = 8
ATTN_NEG = -1e30
ATTN_TILE = 512


def _mod_kernel(c_ref, w_ref, b_ref, o_ref):
    c = c_ref[...]
    s = c * jax.nn.sigmoid(c)
    o_ref[...] = jnp.dot(s.astype(BF16), w_ref[...].astype(BF16), preferred_element_type=F32) + b_ref[...]


def _adaln(c, w_mod, b_mod):
    n = c.shape[0]
    tn = 1536
    return pl.pallas_call(
        _mod_kernel,
        grid=(6 * D_MODEL // tn,),
        in_specs=[pl.BlockSpec((n, D_MODEL), lambda j: (0, 0)),
                  pl.BlockSpec((D_MODEL, tn), lambda j: (0, j)),
                  pl.BlockSpec((1, tn), lambda j: (0, j))],
        out_specs=pl.BlockSpec((n, tn), lambda j: (0, j)),
        out_shape=jax.ShapeDtypeStruct((n, 6 * D_MODEL), F32),
        name="adaln",
    )(c, w_mod, b_mod.reshape(1, -1))


def _modulated_norm(x, gain, scale, shift):
    y = x * lax.rsqrt(jnp.mean(x * x, axis=-1, keepdims=True) + EPS)
    return (y * gain) * (1.0 + scale) + shift


def _proj_gla_kernel(x_ref, sh_ref, sc_ref, gain_ref, w_ref, wa_ref, ba_ref,
                     q_ref, k_ref, v_ref, r_ref, la_ref):
    h = _modulated_norm(x_ref[...], gain_ref[...], sc_ref[0], sh_ref[0]).astype(BF16)
    z = jnp.dot(h, w_ref[...], preferred_element_type=F32)
    q_ref[...] = z[:, 0:GLA_QK_W] * (GLA_DK ** -0.5)
    k_ref[...] = z[:, GLA_QK_W:2 * GLA_QK_W]
    v_ref[...] = z[:, 2 * GLA_QK_W:2 * GLA_QK_W + GLA_V_W]
    r_ref[...] = z[:, 2 * GLA_QK_W + GLA_V_W:2 * GLA_QK_W + 2 * GLA_V_W]
    glr = z[:, 2 * GLA_QK_W + 2 * GLA_V_W:]
    a = jnp.dot(glr.astype(BF16), wa_ref[...], preferred_element_type=F32) + ba_ref[...]
    la_ref[...] = jax.nn.log_sigmoid(a) * (1.0 / GLA_TAU)


def _group_mean_square(x, ones_ref):
    s = x * x
    hi = s.astype(BF16)
    lo = (s - hi.astype(F32)).astype(BF16)
    ss = (jnp.dot(hi, ones_ref[...], preferred_element_type=F32)
          + jnp.dot(lo, ones_ref[...], preferred_element_type=F32))
    return ss * (1.0 / DIFF_HD)


def _norm_rope(x, gain, cos, sin_signed, ones_ref):
    outs = []
    lane = lax.broadcasted_iota(jnp.int32, (x.shape[0], LANES), 1)
    first_half = (lane % DIFF_HD) < (DIFF_HD // 2)
    for b in range(DIFF_QK_W // LANES):
        xb = x[:, b * LANES:(b + 1) * LANES]
        y = xb * lax.rsqrt(_group_mean_square(xb, ones_ref) + EPS) * gain
        rot = jnp.where(first_half, pltpu.roll(y, LANES - DIFF_HD // 2, axis=1), pltpu.roll(y, DIFF_HD // 2, axis=1))
        outs.append(y * cos + rot * sin_signed)
    return jnp.concatenate(outs, axis=1)


def _proj_diff_kernel(x_ref, sh_ref, sc_ref, gain_ref, w_ref, qg_ref, kg_ref, cos_ref, sin_ref, ones_ref,
                      qb_ref, k_ref, kb_ref, v_ref, vb_ref, *, q_scale):
    h = _modulated_norm(x_ref[...], gain_ref[...], sc_ref[0], sh_ref[0]).astype(BF16)
    z = jnp.dot(h, w_ref[...], preferred_element_type=F32)
    cos, sin = cos_ref[...], sin_ref[...]
    q = _norm_rope(z[:, 0:DIFF_QK_W], qg_ref[...], cos, sin, ones_ref)
    qb_ref[...] = (q * q_scale).astype(BF16)
    k = _norm_rope(z[:, DIFF_QK_W:2 * DIFF_QK_W], kg_ref[...], cos, sin, ones_ref)
    k_ref[...] = k
    kb_ref[...] = k.astype(BF16)
    v = z[:, 2 * DIFF_QK_W:2 * DIFF_QK_W + DIFF_V_W]
    v_ref[...] = v
    vb_ref[...] = v.astype(BF16)


def _proj_gla(x, shift, scale, gain, w_a, w_alpha_pad, b_alpha, *, tt, mod_tiles):
    n = x.shape[0]
    assert n % tt == 0
    bmap = lambda i: (i // mod_tiles, 0, 0)
    mod = pl.BlockSpec((1,) + shift.shape[1:], bmap)
    row = lambda w: pl.BlockSpec((tt, w), lambda i: (i, 0))
    const = lambda a: pl.BlockSpec(a.shape, lambda i: (0,) * a.ndim)
    return pl.pallas_call(
        _proj_gla_kernel,
        grid=(n // tt,),
        in_specs=[row(D_MODEL), mod, mod,
                  const(gain), const(w_a), const(w_alpha_pad), const(b_alpha)],
        out_specs=[row(GLA_QK_W), row(GLA_QK_W), row(GLA_V_W), row(GLA_V_W), row(GLA_QK_W)],
        out_shape=[jax.ShapeDtypeStruct((n, GLA_QK_W), F32), jax.ShapeDtypeStruct((n, GLA_QK_W), F32),
                   jax.ShapeDtypeStruct((n, GLA_V_W), F32), jax.ShapeDtypeStruct((n, GLA_V_W), F32),
                   jax.ShapeDtypeStruct((n, GLA_QK_W), F32)],
        compiler_params=pltpu.CompilerParams(dimension_semantics=("arbitrary",), vmem_limit_bytes=48 * 1024 * 1024),
        name="proj_gla",
    )(x, shift, scale, gain, w_a, w_alpha_pad, b_alpha)


def _proj_diff(x, shift, scale, gain, w_b, q_gain128, k_gain128, cos128, sin128, ones128, *, tt, mod_tiles):
    n = x.shape[0]
    assert n % tt == 0 and cos128.shape[0] % tt == 0
    bmap = lambda i: (i // mod_tiles, 0, 0)
    mod = pl.BlockSpec((1,) + shift.shape[1:], bmap)
    pos_tiles = cos128.shape[0] // tt
    row = lambda w: pl.BlockSpec((tt, w), lambda i: (i, 0))
    const = lambda a: pl.BlockSpec(a.shape, lambda i: (0,) * a.ndim)
    pos = pl.BlockSpec((tt, LANES), lambda i: (i % pos_tiles, 0))
    q_scale = math.log2(math.e) * DIFF_HD ** -0.5
    return pl.pallas_call(
        functools.partial(_proj_diff_kernel, q_scale=q_scale),
        grid=(n // tt,),
        in_specs=[row(D_MODEL), mod, mod,
                  const(gain), const(w_b), const(q_gain128), const(k_gain128), pos, pos, const(ones128)],
        out_specs=[row(DIFF_QK_W)] * 5,
        out_shape=[jax.ShapeDtypeStruct((n, DIFF_QK_W), BF16), jax.ShapeDtypeStruct((n, DIFF_QK_W), F32),
                   jax.ShapeDtypeStruct((n, DIFF_QK_W), BF16), jax.ShapeDtypeStruct((n, DIFF_V_W), F32),
                   jax.ShapeDtypeStruct((n, DIFF_V_W), BF16)],
        compiler_params=pltpu.CompilerParams(dimension_semantics=("arbitrary",), vmem_limit_bytes=48 * 1024 * 1024),
        name="proj_diff",
    )(x, shift, scale, gain, w_b, q_gain128, k_gain128, cos128, sin128, ones128)


def _gla_kernel(q_ref, k_ref, la_ref, v_ref, h0_ref, o_ref, hT_ref, st_ref, *, tb):
    i = pl.program_id(2)
    nb = tb // GLA_BLOCK

    @pl.when(i == 0)
    def _init():
        st_ref[...] = h0_ref[0, 0].T

    row = lax.broadcasted_iota(jnp.int32, (tb, tb), 0)
    col = lax.broadcasted_iota(jnp.int32, (tb, tb), 1)
    same = (row // GLA_BLOCK) == (col // GLA_BLOCK)
    causal = same & (col <= row)
    la = la_ref[...]
    hp = lax.Precision.HIGHEST
    b = jnp.dot(causal.astype(F32), la, precision=hp, preferred_element_type=F32)
    b_last = jnp.dot(same.astype(F32), la, precision=hp, preferred_element_type=F32)
    q, k, v = q_ref[...], k_ref[...], v_ref[...]
    qe = (q * jnp.exp(b)).astype(BF16)
    ke = (k * jnp.exp(-b)).astype(BF16)
    kd = (k * jnp.exp(b_last - b)).astype(BF16)
    vb = v.astype(BF16)
    a = lax.dot_general(qe, ke, (((1,), (1,)), ((), ())), preferred_element_type=F32)
    a = jnp.where(causal, a, 0.0).astype(BF16)
    o_intra = jnp.dot(a, vb, preferred_element_type=F32)

    st = st_ref[...]
    outs = []
    for n in range(nb):
        rs = slice(n * GLA_BLOCK, (n + 1) * GLA_BLOCK)
        o_inter = lax.dot_general(qe[rs], st.astype(BF16), (((1,), (1,)), ((), ())), preferred_element_type=F32)
        outs.append(o_inter + o_intra[rs])
        u = lax.dot_general(vb[rs], kd[rs], (((0,), (0,)), ((), ())), preferred_element_type=F32)
        st = jnp.exp(b_last[n * GLA_BLOCK:n * GLA_BLOCK + 1, :]) * st + u
    o_ref[...] = jnp.concatenate(outs, axis=0)
    st_ref[...] = st

    @pl.when(i == pl.num_programs(2) - 1)
    def _fin():
        hT_ref[0, 0] = st.T


def _gla(q, k, la, v, h0, *, l_seq, tb):
    n = q.shape[0]
    bsz = n // l_seq
    nt = l_seq // tb
    assert l_seq % tb == 0 and tb % GLA_BLOCK == 0
    tok = lambda w: pl.BlockSpec((tb, w), lambda b, h, i: (b * nt + i, h))
    st = pl.BlockSpec((1, 1, GLA_DK, GLA_DV), lambda b, h, i: (b, h, 0, 0))
    return pl.pallas_call(
        functools.partial(_gla_kernel, tb=tb),
        grid=(bsz, GLA_HEADS, nt),
        in_specs=[tok(GLA_DK), tok(GLA_DK), tok(GLA_DK), tok(GLA_DV), st],
        out_specs=[tok(GLA_DV), st],
        out_shape=[jax.ShapeDtypeStruct((n, GLA_V_W), F32),
                   jax.ShapeDtypeStruct((bsz, GLA_HEADS, GLA_DK, GLA_DV), F32)],
        scratch_shapes=[pltpu.VMEM((GLA_DV, GLA_DK), F32)],
        compiler_params=pltpu.CompilerParams(dimension_semantics=("arbitrary", "arbitrary", "arbitrary")),
        name="gla",
    )(q, k, la, v, h0)


def _split_maps(q):
    lane = lax.broadcasted_iota(jnp.int32, q.shape, 1)
    zero = jnp.zeros_like(q)
    return jnp.concatenate([jnp.where(lane < DIFF_HD, q, zero), jnp.where(lane >= DIFF_HD, q, zero)], axis=0)


def _attn_prompt_kernel(lam_ref, q_ref, k_ref, v_ref, o_ref, vt_ref, *, tq, tk):
    i = pl.program_id(2)
    n_kt = v_ref.shape[1] // tk

    @pl.when(i == 0)
    def _transpose_v():
        def body(j, c):
            vt_ref[j] = v_ref[0, pl.ds(j * tk, tk), :].astype(F32).T.astype(BF16)
            return c
        lax.fori_loop(0, n_kt, body, 0)

    qp = _split_maps(q_ref[0])

    def step(j, carry, masked, koff=0):
        m, l, acc = carry
        kt = k_ref[0, pl.ds(j * tk, tk), :]
        st = lax.dot_general(kt, qp, (((1,), (1,)), ((), ())), preferred_element_type=F32)
        if masked:
            kc = (lax.broadcasted_iota(jnp.int32, st.shape, 0) + koff) // CHUNK
            qc = (lax.broadcasted_iota(jnp.int32, st.shape, 1) % tq) // CHUNK
            st = jnp.where(qc >= kc, st, ATTN_NEG)
        m_new = jnp.maximum(m, jnp.max(st, axis=0, keepdims=True))
        alpha = jnp.exp2(m - m_new)
        p = jnp.exp2(st - m_new)
        l = alpha * l + jnp.sum(p, axis=0, keepdims=True)
        acc = alpha * acc + jnp.dot(vt_ref[j], p.astype(BF16), preferred_element_type=F32)
        return m_new, l, acc

    carry = (jnp.full((1, 2 * tq), ATTN_NEG, F32), jnp.zeros((1, 2 * tq), F32), jnp.zeros((LANES, 2 * tq), F32))
    r = tq // tk
    carry = lax.fori_loop(0, i * r, functools.partial(step, masked=False), carry)
    for d in range(r):
        carry = step(i * r + d, carry, True, koff=d * tk)
    m, l, acc = carry
    o = acc / l
    ot = o[:, :tq] - lam_ref[0, 0] * o[:, tq:]
    o_ref[0] = ot.T


def _attn_prompt(q, k, v, lam, *, tq=ATTN_TILE, tk=ATTN_TILE):
    b, l, w = q.shape
    h = w // LANES
    assert l % tq == 0 and tq % tk == 0 and tk % CHUNK == 0
    return pl.pallas_call(
        functools.partial(_attn_prompt_kernel, tq=tq, tk=tk),
        grid=(b, h, l // tq),
        in_specs=[
            pl.BlockSpec(memory_space=pltpu.SMEM),
            pl.BlockSpec((1, tq, LANES), lambda bi, hi, i: (bi, i, hi)),
            pl.BlockSpec((1, l, LANES), lambda bi, hi, i: (bi, 0, hi)),
            pl.BlockSpec((1, l, LANES), lambda bi, hi, i: (bi, 0, hi)),
        ],
        out_specs=pl.BlockSpec((1, tq, LANES), lambda bi, hi, i: (bi, i, hi)),
        out_shape=jax.ShapeDtypeStruct((b, l, w), F32),
        scratch_shapes=[pltpu.VMEM((l // tk, LANES, tk), BF16)],
        compiler_params=pltpu.CompilerParams(dimension_semantics=("arbitrary", "arbitrary", "arbitrary"),
                                             vmem_limit_bytes=40 * 1024 * 1024),
        name="attn_prompt",
    )(lam, q, k, v)


def _attn_sample_kernel(lam_ref, q_ref, kn_ref, vn_ref, kc_ref, vc_ref, o_ref, m_ref, l_ref, acc_ref, *, heads):
    j = pl.program_id(1)
    lq = q_ref.shape[0]
    nt = (((1,), (1,)), ((), ()))

    @pl.when(j == 0)
    def _init():
        m_ref[...] = jnp.full(m_ref.shape, ATTN_NEG, F32)
        l_ref[...] = jnp.zeros(l_ref.shape, F32)
        acc_ref[...] = jnp.zeros(acc_ref.shape, F32)

    def update(h, k, v):
        qp = _split_maps(q_ref[:, h * LANES:(h + 1) * LANES])
        s = lax.dot_general(qp, k, nt, preferred_element_type=F32)
        m_old = m_ref[h]
        m_new = jnp.maximum(m_old, jnp.max(s, axis=-1, keepdims=True))
        alpha = jnp.exp2(m_old - m_new)
        p = jnp.exp2(s - m_new)
        l_ref[h] = alpha * l_ref[h] + jnp.sum(p, axis=-1, keepdims=True)
        acc_ref[h] = alpha * acc_ref[h] + jnp.dot(p.astype(BF16), v, preferred_element_type=F32)
        m_ref[h] = m_new

    for h in range(heads):
        update(h, kc_ref[:, h, :].astype(BF16), vc_ref[:, h, :].astype(BF16))

    @pl.when(j == pl.num_programs(1) - 1)
    def _fin():
        for h in range(heads):
            update(h, kn_ref[:, h * LANES:(h + 1) * LANES], vn_ref[:, h * LANES:(h + 1) * LANES])
            o = acc_ref[h] / l_ref[h]
            o_ref[:, h * LANES:(h + 1) * LANES] = o[:lq] - lam_ref[0, 0] * o[lq:]


def _attn_sample(q, k_new, v_new, cache_k, cache_v, lam, *, l_seq, tk=1024):
    n, w = q.shape
    bsz, past, heads, _ = cache_k.shape
    tk = min(tk, past)
    assert n == bsz * l_seq and w == heads * LANES and past % tk == 0
    tok = pl.BlockSpec((l_seq, w), lambda b, j: (b, 0))
    cache = pl.BlockSpec((None, tk, heads, LANES), lambda b, j: (b, j, 0, 0))
    return pl.pallas_call(
        functools.partial(_attn_sample_kernel, heads=heads),
        grid=(bsz, past // tk),
        in_specs=[pl.BlockSpec(memory_space=pltpu.SMEM), tok, tok, tok, cache, cache],
        out_specs=tok,
        out_shape=jax.ShapeDtypeStruct((n, w), F32),
        scratch_shapes=[pltpu.VMEM((heads, 2 * l_seq, 1), F32), pltpu.VMEM((heads, 2 * l_seq, 1), F32),
                        pltpu.VMEM((heads, 2 * l_seq, LANES), F32)],
        compiler_params=pltpu.CompilerParams(dimension_semantics=("arbitrary", "arbitrary"),
                                             vmem_limit_bytes=40 * 1024 * 1024),
        name="attn_sample",
    )(lam, q, k_new, v_new, cache_k, cache_v)


def _head_norm(x, width):
    outs = []
    for g in range(x.shape[1] // width):
        xg = x[:, g * width:(g + 1) * width]
        outs.append(xg * lax.rsqrt(jnp.mean(xg * xg, axis=-1, keepdims=True) + EPS))
    return jnp.concatenate(outs, axis=1)


def _post_kernel(x_ref, oa_ref, gr_ref, ob_ref, sh1_ref, sc1_ref, gt1_ref, sh2_ref, sc2_ref,
                 n1_ref, n2_ref, ga_ref, gb_ref, wa_ref, wb_ref, wg_ref, wo_ref, x1_ref, h2_ref, *, diff_out_scale):
    x = x_ref[...]
    h = _modulated_norm(x, n1_ref[...], sc1_ref[0], sh1_ref[0]).astype(BF16)
    gates = jnp.dot(h, wg_ref[...], preferred_element_type=F32)
    gr = gr_ref[...]
    oa = _head_norm(oa_ref[...], GLA_DV) * ga_ref[...] * (gr * jax.nn.sigmoid(gr))
    ya = jnp.dot(oa.astype(BF16), wa_ref[...], preferred_element_type=F32)
    ob = _head_norm(ob_ref[...], DIFF_VD) * gb_ref[...] * diff_out_scale
    yb = jnp.dot(ob.astype(BF16), wb_ref[...], preferred_element_type=F32)
    merged = jax.nn.sigmoid(gates[:, :D_MODEL]) * ya + jax.nn.sigmoid(gates[:, D_MODEL:]) * yb
    mix = jnp.dot(merged.astype(BF16), wo_ref[...], preferred_element_type=F32)
    x1 = x + gt1_ref[0] * mix
    x1_ref[...] = x1
    h2_ref[...] = _modulated_norm(x1, n2_ref[...], sc2_ref[0], sh2_ref[0])


def _post_mixer(x, oa, gr, ob, mods, n1, n2, ga, gb, wa, wb, wg, wo, *, tt, mod_tiles, diff_out_scale):
    n = x.shape[0]
    assert n % tt == 0
    bmap = lambda i: (i // mod_tiles, 0, 0)
    mod = pl.BlockSpec((1,) + mods[0].shape[1:], bmap)
    row = pl.BlockSpec((tt, D_MODEL), lambda i: (i, 0))
    const = lambda a: pl.BlockSpec(a.shape, lambda i: (0,) * a.ndim)
    return pl.pallas_call(
        functools.partial(_post_kernel, diff_out_scale=diff_out_scale),
        grid=(n // tt,),
        in_specs=[row, row, row, row] + [mod] * 5 + [const(a) for a in (n1, n2, ga, gb, wa, wb, wg, wo)],
        out_specs=[row, row],
        out_shape=[jax.ShapeDtypeStruct((n, D_MODEL), F32), jax.ShapeDtypeStruct((n, D_MODEL), F32)],
        compiler_params=pltpu.CompilerParams(dimension_semantics=("arbitrary",), vmem_limit_bytes=48 * 1024 * 1024),
        name="post_mixer",
    )(x, oa, gr, ob, *mods, n1, n2, ga, gb, wa, wb, wg, wo)


def _topk_rows(s, k):
    r = s.shape[0]
    row = lax.broadcasted_iota(jnp.int32, s.shape, 0)
    vals, idxs = [], []
    for _ in range(k):
        mx = jnp.max(s, axis=0, keepdims=True)
        am = jnp.min(jnp.where(s == mx, row, r), axis=0, keepdims=True)
        vals.append(mx)
        idxs.append(am)
        s = jnp.where(row == am, -jnp.inf, s)
    return jnp.concatenate(vals, axis=0), jnp.concatenate(idxs, axis=0)


def _take_rows(tab, sel):
    out = jnp.zeros(sel.shape, tab.dtype)
    for a in range(tab.shape[0]):
        out = jnp.where(sel == a, tab[a:a + 1, :], out)
    return out


def _peer_select_kernel(h2_ref, w_ref, k1_ref, k2_ref, idx_ref, g_ref):
    q = jnp.dot(h2_ref[...].astype(BF16), w_ref[...], preferred_element_type=F32)
    nt = (((1,), (1,)), ((), ()))
    s1 = lax.dot_general(k1_ref[...], q[:, :PEER_NKEYS].astype(BF16), nt, preferred_element_type=F32)
    s2 = lax.dot_general(k2_ref[...], q[:, PEER_NKEYS:].astype(BF16), nt, preferred_element_type=F32)
    v1, i1 = _topk_rows(s1, PEER_TOPK)
    v2, i2 = _topk_rows(s2, PEER_TOPK)
    cand = jnp.concatenate([v1[a:a + 1, :] + v2 for a in range(PEER_TOPK)], axis=0)
    sc, ci = _topk_rows(cand, PEER_TOPK)
    e1 = _take_rows(i1, ci // PEER_TOPK)
    e2 = _take_rows(i2, ci % PEER_TOPK)
    idx_ref[...] = e1 * PEER_NKEYS + e2
    p = jnp.exp(sc - sc[0:1, :])
    g_ref[...] = p / jnp.sum(p, axis=0, keepdims=True)


def _peer_select(h2, w_pq_bf, k1_bf, k2_bf, *, tt=256):
    n, d = h2.shape
    assert n % tt == 0
    return pl.pallas_call(
        _peer_select_kernel,
        grid=(n // tt, PEER_HEADS),
        in_specs=[
            pl.BlockSpec((tt, d), lambda i, h: (i, 0)),
            pl.BlockSpec((d, PEER_DKEY), lambda i, h: (0, h)),
            pl.BlockSpec((PEER_NKEYS, PEER_DKEY // 2), lambda i, h: (0, 0)),
            pl.BlockSpec((PEER_NKEYS, PEER_DKEY // 2), lambda i, h: (0, 0)),
        ],
        out_specs=[pl.BlockSpec((PEER_TOPK, tt), lambda i, h: (h, i)),
                   pl.BlockSpec((PEER_TOPK, tt), lambda i, h: (h, i))],
        out_shape=[jax.ShapeDtypeStruct((NSEL, n), jnp.int32),
                   jax.ShapeDtypeStruct((NSEL, n), F32)],
        compiler_params=pltpu.CompilerParams(dimension_semantics=("arbitrary", "arbitrary")),
        name="peer_select",
    )(h2, w_pq_bf, k1_bf, k2_bf)


def _peer_gather_kernel(idx_ref, h2_ref, g_ref, x1_ref, gate_ref, tab_ref, out_ref, buf, sem, *, tt):
    s = pl.program_id(0)
    n = pl.num_programs(0) - 1
    islot = s % 2
    cslot = 1 - islot
    ngrp = NSEL // SUBLANES
    nch = D_MODEL // LANES

    def start_row(t, j):
        e = idx_ref[t, j]
        pltpu.make_async_copy(tab_ref.at[e], buf.at[islot, t, pl.ds(j, 1), :],
                              sem.at[islot, t]).start(priority=j % 2)

    def wait_tile():
        for t in range(tt):
            for j in range(NSEL):
                pltpu.make_async_copy(tab_ref.at[0], buf.at[cslot, t, pl.ds(j, 1), :], sem.at[cslot, t]).wait()

    def u_piece(t, x, acc, j):
        r, c = divmod(j, nch)
        term = (buf[cslot, t, r * SUBLANES:(r + 1) * SUBLANES, c * LANES:(c + 1) * LANES]
                * x[:, c * LANES:(c + 1) * LANES])
        acc[r] = term if c == 0 else acc[r] + term

    def v_piece(t, coef, yacc, j):
        r, c = divmod(j, nch)
        term = coef[r] * buf[cslot, t, r * SUBLANES:(r + 1) * SUBLANES,
                             D_MODEL + c * LANES:D_MODEL + (c + 1) * LANES]
        yacc[c] = term if r == 0 else yacc[c] + term

    def gate_coef(t, acc):
        eye = (lax.broadcasted_iota(jnp.int32, (NSEL, NSEL), 0)
               == lax.broadcasted_iota(jnp.int32, (NSEL, NSEL), 1))
        a = jnp.sum(jnp.concatenate(acc, axis=0), axis=-1, keepdims=True)
        act = 0.5 * a * (1.0 + lax.erf(a * (0.5 ** 0.5)))
        grow = g_ref[pl.ds(t, 1), :]
        gcol = jnp.sum(jnp.where(eye, grow, 0.0), axis=-1, keepdims=True)
        coef = gcol * act
        return [coef[r * SUBLANES:(r + 1) * SUBLANES, :] for r in range(ngrp)]

    def store_row(t, yacc):
        y = jnp.concatenate([jnp.sum(yc, axis=0, keepdims=True) for yc in yacc], axis=1)
        out_ref[pl.ds(t, 1), :] = x1_ref[pl.ds(t, 1), :] + gate_ref[0] * y

    def first_coef():
        x = h2_ref[pl.ds(0, 1), :]
        acc = [None] * ngrp
        for j in range(NSEL):
            u_piece(0, x, acc, j)
        return gate_coef(0, acc)

    def token(t, coef, do_issue):
        tn = jnp.minimum(t + 1, tt - 1)
        xn = h2_ref[pl.ds(tn, 1), :]
        acc = [None] * ngrp
        yacc = [None] * nch
        for j in range(NSEL):
            if do_issue:
                start_row(t, j)
            v_piece(t, coef, yacc, j)
            u_piece(tn, xn, acc, j)
        store_row(t, yacc)
        return gate_coef(tn, acc)

    @pl.when(s == 0)
    def _first():
        def body(t, c):
            for j in range(NSEL):
                start_row(t, j)
            return c
        lax.fori_loop(0, tt, body, 0)

    @pl.when((s > 0) & (s < n))
    def _steady():
        wait_tile()
        lax.fori_loop(0, tt, lambda t, coef: token(t, coef, True), first_coef())

    @pl.when(s == n)
    def _last():
        wait_tile()
        lax.fori_loop(0, tt, lambda t, coef: token(t, coef, False), first_coef())


def _peer_gather(idx, g, h2, x1, gates, table, *, tt, l_seq):
    n_tok = idx.shape[0]
    n_tiles = n_tok // tt
    assert n_tiles * tt == n_tok and l_seq % tt == 0

    def cur(s):
        return jnp.maximum(s - 1, 0)

    return pl.pallas_call(
        functools.partial(_peer_gather_kernel, tt=tt),
        grid=(n_tiles + 1,),
        in_specs=[
            pl.BlockSpec((tt, NSEL), lambda s: (jnp.minimum(s, n_tiles - 1), 0), memory_space=pltpu.SMEM),
            pl.BlockSpec((tt, D_MODEL), lambda s: (cur(s), 0)),
            pl.BlockSpec((tt, NSEL), lambda s: (cur(s), 0)),
            pl.BlockSpec((tt, D_MODEL), lambda s: (cur(s), 0)),
            pl.BlockSpec((1, 1, D_MODEL), lambda s: (cur(s) * tt // l_seq, 0, 0)),
            pl.BlockSpec(memory_space=pl.ANY),
        ],
        out_specs=pl.BlockSpec((tt, D_MODEL), lambda s: (cur(s), 0)),
        out_shape=jax.ShapeDtypeStruct((n_tok, D_MODEL), F32),
        scratch_shapes=[pltpu.VMEM((2, tt, NSEL, 2 * D_MODEL), F32), pltpu.SemaphoreType.DMA((2, tt))],
        compiler_params=pltpu.CompilerParams(dimension_semantics=("arbitrary",),
                                             vmem_limit_bytes=48 * 1024 * 1024),
        name="peer_gather",
    )(idx, h2, g, x1, gates, table)


def _rope_tables(pos):
    half = DIFF_HD // 2
    inv = ROPE_THETA ** (-jnp.arange(half, dtype=F32) / half)
    ang = pos.astype(F32)[:, None] * inv[None, :]
    cos, sin = jnp.cos(ang), jnp.sin(ang)
    return jnp.tile(cos, (1, 4)), jnp.tile(jnp.concatenate([-sin, sin], axis=1), (1, 2))


def _stream(x, mods, mod_tiles, tile, cos, sin, l_seq, h0, wts, attend, lambda_init):
    sh1, sc1, gt1, sh2, sc2, gate2 = mods
    q_g, k_g, v_g, r_g, la = _proj_gla(x, sh1, sc1, wts["n1"], wts["w_a"], wts["w_alpha"], wts["b_alpha"],
                                       tt=tile, mod_tiles=mod_tiles)
    q_bf, k_rows, k_bf, v_rows, v_bf = _proj_diff(x, sh1, sc1, wts["n1"], wts["w_b"], wts["q_gain"], wts["k_gain"],
                                                  cos, sin, wts["ones"], tt=tile, mod_tiles=mod_tiles)
    o_a, state = _gla(q_g, k_g, la, v_g, h0, l_seq=l_seq, tb=min(GLA_TILE, l_seq))
    o_b = attend(q_bf, k_bf, v_bf)
    post_tile = min(POST_TILE, tile)
    x1, h2 = _post_mixer(x, o_a, r_g, o_b, (sh1, sc1, gt1, sh2, sc2), wts["n1"], wts["n2"], wts["gla_gain"],
                         wts["diff_gain"], wts["w_gla_out"], wts["w_diff_out"], wts["w_g"], wts["w_out"],
                         tt=post_tile, mod_tiles=mod_tiles * (tile // post_tile), diff_out_scale=1.0 - lambda_init)
    idx_t, g_t = _peer_select(h2, wts["w_pq"], wts["k1"], wts["k2"])
    y = _peer_gather(idx_t.T, g_t.T, h2, x1, gate2, wts["table"], tt=PEER_TOK_TILE, l_seq=l_seq)
    return y, k_rows, v_rows, state


def kernel(x_prompt, x_sample, c_prompt, c_sample, cache_k, cache_v, state_gla, norm1_gain, norm2_gain, w_mod, b_mod, w_in, w_alpha, b_alpha, gla_gain, w_gla_out, q_gain, k_gain, lam_q1, lam_k1, lam_q2, lam_k2, diff_gain, w_diff_out, w_out, w_pq, sub_keys1, sub_keys2, peer_u, peer_v):
    bp, lp, _ = x_prompt.shape
    bs, ls, _ = x_sample.shape
    past = cache_k.shape[2]
    l = 0
    lambda_init = 0.8 - 0.6 * math.exp(-0.3 * l)
    nb = bp + bs
    c_all = jnp.concatenate([c_prompt, c_sample, jnp.zeros(((-nb) % 8, D_MODEL), F32)], axis=0)
    mod_all = _adaln(c_all, w_mod[l], b_mod[l])
    mods = [mod_all[:nb, i * D_MODEL:(i + 1) * D_MODEL] for i in range(6)]

    wi = w_in[l]
    o_glr = 2 * GLA_QK_W + 2 * GLA_V_W
    o_diff = o_glr + GLA_LOWRANK
    o_gate = o_diff + 2 * DIFF_QK_W + DIFF_V_W
    pad_lr = LANES - GLA_LOWRANK
    lane = jnp.arange(LANES)
    wts = dict(
        n1=norm1_gain[l][None], n2=norm2_gain[l][None],
        w_a=jnp.concatenate([wi[:, :o_diff], jnp.zeros((D_MODEL, pad_lr), F32)], axis=1).astype(BF16),
        w_alpha=jnp.concatenate([w_alpha[l], jnp.zeros((pad_lr, GLA_QK_W), F32)], axis=0).astype(BF16),
        b_alpha=b_alpha[l][None],
        w_b=wi[:, o_diff:o_gate].astype(BF16), w_g=wi[:, o_gate:].astype(BF16),
        q_gain=q_gain[l].reshape(1, LANES), k_gain=k_gain[l].reshape(1, LANES),
        ones=(lane[:, None] // DIFF_HD == lane[None, :] // DIFF_HD).astype(BF16),
        gla_gain=jnp.tile(gla_gain[l], GLA_HEADS)[None], diff_gain=jnp.tile(diff_gain[l], DIFF_HEADS)[None],
        w_gla_out=w_gla_out[l].astype(BF16), w_diff_out=w_diff_out[l].astype(BF16), w_out=w_out[l].astype(BF16),
        w_pq=w_pq[l].astype(BF16), k1=sub_keys1[l].astype(BF16), k2=sub_keys2[l].astype(BF16),
        table=jnp.concatenate([peer_u[l], peer_v[l]], axis=1).reshape(-1, 1, 2 * D_MODEL),
    )
    lam = (jnp.exp(jnp.sum(lam_q1[l] * lam_k1[l])) - jnp.exp(jnp.sum(lam_q2[l] * lam_k2[l])) + lambda_init).reshape(1, 1)

    cos_p, sin_p = _rope_tables(jnp.arange(lp))
    mods_p = [m[:bp, None, :] for m in mods]
    attend_p = lambda q, k, v: _attn_prompt(q.reshape(bp, lp, -1), k.reshape(bp, lp, -1), v.reshape(bp, lp, -1),
                                            lam).reshape(bp * lp, -1)
    h0_p = jnp.zeros((bp, GLA_HEADS, GLA_DK, GLA_DV), F32)
    yp, kp, vp, sp = _stream(x_prompt.reshape(bp * lp, D_MODEL), mods_p, lp // PROJ_TILE, PROJ_TILE, cos_p, sin_p,
                             lp, h0_p, wts, attend_p, lambda_init)

    seq_per_tile = SAMPLE_TILE // ls
    cos_s, sin_s = _rope_tables(jnp.tile(past + jnp.arange(ls), seq_per_tile))
    mods_s = [jnp.repeat(m[bp:nb], ls, axis=0).reshape(-1, SAMPLE_TILE, D_MODEL) for m in mods[:5]]
    mods_s.append(mods[5][bp:nb, None, :])
    attend_s = lambda q, k, v: _attn_sample(q, k, v, cache_k[l], cache_v[l], lam, l_seq=ls)
    ys, ks, vs, ss = _stream(x_sample.reshape(bs * ls, D_MODEL), mods_s, 1, SAMPLE_TILE, cos_s, sin_s,
                             ls, state_gla[l], wts, attend_s, lambda_init)

    return (yp.reshape(bp, lp, D_MODEL), ys.reshape(bs, ls, D_MODEL),
            kp.reshape(1, bp, lp, DIFF_HEADS, 2 * DIFF_HD), vp.reshape(1, bp, lp, DIFF_HEADS, DIFF_VD), sp[None],
            ks.reshape(1, bs, ls, DIFF_HEADS, 2 * DIFF_HD), vs.reshape(1, bs, ls, DIFF_HEADS, DIFF_VD), ss[None])
```

```python
import math
import functools
import jax, jax.numpy as jnp
from jax import lax
from jax.experimental import pallas as pl
from jax.experimental.pallas import tpu as pltpu

D_MODEL = 1024
CHUNK = 64
EPS = 1e-6
GLA_HEADS = 4
GLA_DK = 128
GLA_DV = 256
GLA_LOWRANK = 16
GLA_TAU = 16.0
GLA_BLOCK = 16
DIFF_HEADS = 8
DIFF_HD = 64
DIFF_VD = 128
ROPE_THETA = 10000.0
PEER_HEADS = 8
PEER_NKEYS = 128
PEER_DKEY = 256
PEER_TOPK = 16
GLA_QK_W = GLA_HEADS * GLA_DK
GLA_V_W = GLA_HEADS * GLA_DV
DIFF_QK_W = DIFF_HEADS * 2 * DIFF_HD
DIFF_V_W = DIFF_HEADS * DIFF_VD
F32 = jnp.float32
BF16 = jnp.bfloat16
NSEL = PEER_HEADS * PEER_TOPK
LANES = 128
SUBLANES = 8

PROJ_TILE = 512
POST_TILE = 256
GLA_TILE = 256
SAMPLE_TILE = 256
PEER_TOK_TILE = 8
ATTN_NEG = -1e30
ATTN_TILE = 1024
SELECT_TILE = 1024


def _mod_kernel(c_ref, w_ref, b_ref, o_ref):
    c = c_ref[...]
    s = c * jax.nn.sigmoid(c)
    o_ref[...] = jnp.dot(s.astype(BF16), w_ref[...].astype(BF16), preferred_element_type=F32) + b_ref[...]


def _adaln(c, w_mod, b_mod):
    n = c.shape[0]
    tn = 1536
    return pl.pallas_call(
        _mod_kernel,
        grid=(6 * D_MODEL // tn,),
        in_specs=[pl.BlockSpec((n, D_MODEL), lambda j: (0, 0)),
                  pl.BlockSpec((D_MODEL, tn), lambda j: (0, j)),
                  pl.BlockSpec((1, tn), lambda j: (0, j))],
        out_specs=pl.BlockSpec((n, tn), lambda j: (0, j)),
        out_shape=jax.ShapeDtypeStruct((n, 6 * D_MODEL), F32),
        name="adaln",
    )(c, w_mod, b_mod.reshape(1, -1))


def _modulated_norm(x, gain, scale, shift):
    y = x * lax.rsqrt(jnp.mean(x * x, axis=-1, keepdims=True) + EPS)
    return (y * gain) * (1.0 + scale) + shift


def _proj_gla_kernel(x_ref, sh_ref, sc_ref, gain_ref, w_ref, wa_ref, ba_ref,
                     q_ref, k_ref, v_ref, r_ref, la_ref):
    h = _modulated_norm(x_ref[...], gain_ref[...], sc_ref[0], sh_ref[0]).astype(BF16)
    z = jnp.dot(h, w_ref[...], preferred_element_type=F32)
    q_ref[...] = z[:, 0:GLA_QK_W] * (GLA_DK ** -0.5)
    k_ref[...] = z[:, GLA_QK_W:2 * GLA_QK_W]
    v_ref[...] = z[:, 2 * GLA_QK_W:2 * GLA_QK_W + GLA_V_W]
    r_ref[...] = z[:, 2 * GLA_QK_W + GLA_V_W:2 * GLA_QK_W + 2 * GLA_V_W]
    glr = z[:, 2 * GLA_QK_W + 2 * GLA_V_W:]
    a = jnp.dot(glr.astype(BF16), wa_ref[...], preferred_element_type=F32) + ba_ref[...]
    la_ref[...] = jax.nn.log_sigmoid(a) * (1.0 / GLA_TAU)


def _group_mean_square(x, ones_ref):
    s = x * x
    hi = s.astype(BF16)
    lo = (s - hi.astype(F32)).astype(BF16)
    ss = (jnp.dot(hi, ones_ref[...], preferred_element_type=F32)
          + jnp.dot(lo, ones_ref[...], preferred_element_type=F32))
    return ss * (1.0 / DIFF_HD)


def _norm_rope(x, gain, cos, sin_signed, ones_ref):
    outs = []
    lane = lax.broadcasted_iota(jnp.int32, (x.shape[0], LANES), 1)
    first_half = (lane % DIFF_HD) < (DIFF_HD // 2)
    for b in range(DIFF_QK_W // LANES):
        xb = x[:, b * LANES:(b + 1) * LANES]
        y = xb * lax.rsqrt(_group_mean_square(xb, ones_ref) + EPS) * gain
        rot = jnp.where(first_half, pltpu.roll(y, LANES - DIFF_HD // 2, axis=1), pltpu.roll(y, DIFF_HD // 2, axis=1))
        outs.append(y * cos + rot * sin_signed)
    return jnp.concatenate(outs, axis=1)


def _proj_diff_kernel(x_ref, sh_ref, sc_ref, gain_ref, w_ref, qg_ref, kg_ref, cos_ref, sin_ref, ones_ref,
                      qb_ref, k_ref, kb_ref, v_ref, vb_ref, *, q_scale):
    h = _modulated_norm(x_ref[...], gain_ref[...], sc_ref[0], sh_ref[0]).astype(BF16)
    z = jnp.dot(h, w_ref[...], preferred_element_type=F32)
    cos, sin = cos_ref[...], sin_ref[...]
    q = _norm_rope(z[:, 0:DIFF_QK_W], qg_ref[...], cos, sin, ones_ref)
    qb_ref[...] = (q * q_scale).astype(BF16)
    k = _norm_rope(z[:, DIFF_QK_W:2 * DIFF_QK_W], kg_ref[...], cos, sin, ones_ref)
    k_ref[...] = k
    kb_ref[...] = k.astype(BF16)
    v = z[:, 2 * DIFF_QK_W:2 * DIFF_QK_W + DIFF_V_W]
    v_ref[...] = v
    vb_ref[...] = v.astype(BF16)


def _proj_gla(x, shift, scale, gain, w_a, w_alpha_pad, b_alpha, *, tt, mod_tiles):
    n = x.shape[0]
    assert n % tt == 0
    bmap = lambda i: (i // mod_tiles, 0, 0)
    mod = pl.BlockSpec((1,) + shift.shape[1:], bmap)
    row = lambda w: pl.BlockSpec((tt, w), lambda i: (i, 0))
    const = lambda a: pl.BlockSpec(a.shape, lambda i: (0,) * a.ndim)
    return pl.pallas_call(
        _proj_gla_kernel,
        grid=(n // tt,),
        in_specs=[row(D_MODEL), mod, mod,
                  const(gain), const(w_a), const(w_alpha_pad), const(b_alpha)],
        out_specs=[row(GLA_QK_W), row(GLA_QK_W), row(GLA_V_W), row(GLA_V_W), row(GLA_QK_W)],
        out_shape=[jax.ShapeDtypeStruct((n, GLA_QK_W), F32), jax.ShapeDtypeStruct((n, GLA_QK_W), F32),
                   jax.ShapeDtypeStruct((n, GLA_V_W), F32), jax.ShapeDtypeStruct((n, GLA_V_W), F32),
                   jax.ShapeDtypeStruct((n, GLA_QK_W), F32)],
        compiler_params=pltpu.CompilerParams(dimension_semantics=("arbitrary",), vmem_limit_bytes=48 * 1024 * 1024),
        name="proj_gla",
    )(x, shift, scale, gain, w_a, w_alpha_pad, b_alpha)


def _proj_diff(x, shift, scale, gain, w_b, q_gain128, k_gain128, cos128, sin128, ones128, *, tt, mod_tiles):
    n = x.shape[0]
    assert n % tt == 0 and cos128.shape[0] % tt == 0
    bmap = lambda i: (i // mod_tiles, 0, 0)
    mod = pl.BlockSpec((1,) + shift.shape[1:], bmap)
    pos_tiles = cos128.shape[0] // tt
    row = lambda w: pl.BlockSpec((tt, w), lambda i: (i, 0))
    const = lambda a: pl.BlockSpec(a.shape, lambda i: (0,) * a.ndim)
    pos = pl.BlockSpec((tt, LANES), lambda i: (i % pos_tiles, 0))
    q_scale = math.log2(math.e) * DIFF_HD ** -0.5
    return pl.pallas_call(
        functools.partial(_proj_diff_kernel, q_scale=q_scale),
        grid=(n // tt,),
        in_specs=[row(D_MODEL), mod, mod,
                  const(gain), const(w_b), const(q_gain128), const(k_gain128), pos, pos, const(ones128)],
        out_specs=[row(DIFF_QK_W)] * 5,
        out_shape=[jax.ShapeDtypeStruct((n, DIFF_QK_W), BF16), jax.ShapeDtypeStruct((n, DIFF_QK_W), F32),
                   jax.ShapeDtypeStruct((n, DIFF_QK_W), BF16), jax.ShapeDtypeStruct((n, DIFF_V_W), F32),
                   jax.ShapeDtypeStruct((n, DIFF_V_W), BF16)],
        compiler_params=pltpu.CompilerParams(dimension_semantics=("arbitrary",), vmem_limit_bytes=48 * 1024 * 1024),
        name="proj_diff",
    )(x, shift, scale, gain, w_b, q_gain128, k_gain128, cos128, sin128, ones128)


def _gla_kernel(q_ref, k_ref, la_ref, v_ref, h0_ref, o_ref, hT_ref, st_ref, *, tb):
    i = pl.program_id(2)
    nb = tb // GLA_BLOCK

    @pl.when(i == 0)
    def _init():
        st_ref[...] = h0_ref[0, 0].T

    row = lax.broadcasted_iota(jnp.int32, (tb, tb), 0)
    col = lax.broadcasted_iota(jnp.int32, (tb, tb), 1)
    same = (row // GLA_BLOCK) == (col // GLA_BLOCK)
    causal = same & (col <= row)
    la = la_ref[...]
    hp = lax.Precision.HIGHEST
    b = jnp.dot(causal.astype(F32), la, precision=hp, preferred_element_type=F32)
    b_last = jnp.dot(same.astype(F32), la, precision=hp, preferred_element_type=F32)
    q, k, v = q_ref[...], k_ref[...], v_ref[...]
    qe = (q * jnp.exp(b)).astype(BF16)
    ke = (k * jnp.exp(-b)).astype(BF16)
    kd = (k * jnp.exp(b_last - b)).astype(BF16)
    vb = v.astype(BF16)
    a = lax.dot_general(qe, ke, (((1,), (1,)), ((), ())), preferred_element_type=F32)
    a = jnp.where(causal, a, 0.0).astype(BF16)
    o_intra = jnp.dot(a, vb, preferred_element_type=F32)

    st = st_ref[...]
    outs = []
    for n in range(nb):
        rs = slice(n * GLA_BLOCK, (n + 1) * GLA_BLOCK)
        o_inter = lax.dot_general(qe[rs], st.astype(BF16), (((1,), (1,)), ((), ())), preferred_element_type=F32)
        outs.append(o_inter + o_intra[rs])
        u = lax.dot_general(vb[rs], kd[rs], (((0,), (0,)), ((), ())), preferred_element_type=F32)
        st = jnp.exp(b_last[n * GLA_BLOCK:n * GLA_BLOCK + 1, :]) * st + u
    o_ref[...] = jnp.concatenate(outs, axis=0)
    st_ref[...] = st

    @pl.when(i == pl.num_programs(2) - 1)
    def _fin():
        hT_ref[0, 0] = st.T


def _gla(q, k, la, v, h0, *, l_seq, tb):
    n = q.shape[0]
    bsz = n // l_seq
    nt = l_seq // tb
    assert l_seq % tb == 0 and tb % GLA_BLOCK == 0
    tok = lambda w: pl.BlockSpec((tb, w), lambda b, h, i: (b * nt + i, h))
    st = pl.BlockSpec((1, 1, GLA_DK, GLA_DV), lambda b, h, i: (b, h, 0, 0))
    return pl.pallas_call(
        functools.partial(_gla_kernel, tb=tb),
        grid=(bsz, GLA_HEADS, nt),
        in_specs=[tok(GLA_DK), tok(GLA_DK), tok(GLA_DK), tok(GLA_DV), st],
        out_specs=[tok(GLA_DV), st],
        out_shape=[jax.ShapeDtypeStruct((n, GLA_V_W), F32),
                   jax.ShapeDtypeStruct((bsz, GLA_HEADS, GLA_DK, GLA_DV), F32)],
        scratch_shapes=[pltpu.VMEM((GLA_DV, GLA_DK), F32)],
        compiler_params=pltpu.CompilerParams(dimension_semantics=("arbitrary", "arbitrary", "arbitrary")),
        name="gla",
    )(q, k, la, v, h0)


def _split_maps(q):
    lane = lax.broadcasted_iota(jnp.int32, q.shape, 1)
    zero = jnp.zeros_like(q)
    return jnp.concatenate([jnp.where(lane < DIFF_HD, q, zero), jnp.where(lane >= DIFF_HD, q, zero)], axis=0)


def _attn_prompt_kernel(lam_ref, q_ref, k_ref, v_ref, o_ref, vt_ref, *, tq, tk):
    i = pl.program_id(2)
    n_kt = v_ref.shape[1] // tk

    @pl.when(i == 0)
    def _transpose_v():
        def body(j, c):
            vt_ref[j] = v_ref[0, pl.ds(j * tk, tk), :].astype(F32).T.astype(BF16)
            return c
        lax.fori_loop(0, n_kt, body, 0)

    qp = _split_maps(q_ref[0])

    def step(j, carry, masked, koff=0):
        m, l, acc = carry
        kt = k_ref[0, pl.ds(j * tk, tk), :]
        st = lax.dot_general(kt, qp, (((1,), (1,)), ((), ())), preferred_element_type=F32)
        if masked:
            kc = (lax.broadcasted_iota(jnp.int32, st.shape, 0) + koff) // CHUNK
            qc = (lax.broadcasted_iota(jnp.int32, st.shape, 1) % tq) // CHUNK
            st = jnp.where(qc >= kc, st, ATTN_NEG)
        m_new = jnp.maximum(m, jnp.max(st, axis=0, keepdims=True))
        alpha = jnp.exp2(m - m_new)
        p = jnp.exp2(st - m_new)
        l = alpha * l + jnp.sum(p, axis=0, keepdims=True)
        acc = alpha * acc + jnp.dot(vt_ref[j], p.astype(BF16), preferred_element_type=F32)
        return m_new, l, acc

    carry = (jnp.full((1, 2 * tq), ATTN_NEG, F32), jnp.zeros((1, 2 * tq), F32), jnp.zeros((LANES, 2 * tq), F32))
    r = tq // tk
    carry = lax.fori_loop(0, i * r, functools.partial(step, masked=False), carry)
    for d in range(r):
        carry = step(i * r + d, carry, True, koff=d * tk)
    m, l, acc = carry
    o = acc / l
    ot = o[:, :tq] - lam_ref[0, 0] * o[:, tq:]
    o_ref[0] = ot.T


def _attn_prompt(q, k, v, lam, *, tq=ATTN_TILE, tk=ATTN_TILE):
    b, l, w = q.shape
    h = w // LANES
    assert l % tq == 0 and tq % tk == 0 and tk % CHUNK == 0
    return pl.pallas_call(
        functools.partial(_attn_prompt_kernel, tq=tq, tk=tk),
        grid=(b, h, l // tq),
        in_specs=[
            pl.BlockSpec(memory_space=pltpu.SMEM),
            pl.BlockSpec((1, tq, LANES), lambda bi, hi, i: (bi, i, hi)),
            pl.BlockSpec((1, l, LANES), lambda bi, hi, i: (bi, 0, hi)),
            pl.BlockSpec((1, l, LANES), lambda bi, hi, i: (bi, 0, hi)),
        ],
        out_specs=pl.BlockSpec((1, tq, LANES), lambda bi, hi, i: (bi, i, hi)),
        out_shape=jax.ShapeDtypeStruct((b, l, w), F32),
        scratch_shapes=[pltpu.VMEM((l // tk, LANES, tk), BF16)],
        compiler_params=pltpu.CompilerParams(dimension_semantics=("arbitrary", "arbitrary", "arbitrary"),
                                             vmem_limit_bytes=40 * 1024 * 1024),
        name="attn_prompt",
    )(lam, q, k, v)


def _attn_sample_kernel(lam_ref, q_ref, kn_ref, vn_ref, kc_ref, vc_ref, o_ref, m_ref, l_ref, acc_ref, *, heads):
    j = pl.program_id(1)
    lq = q_ref.shape[0]
    nt = (((1,), (1,)), ((), ()))

    @pl.when(j == 0)
    def _init():
        m_ref[...] = jnp.full(m_ref.shape, ATTN_NEG, F32)
        l_ref[...] = jnp.zeros(l_ref.shape, F32)
        acc_ref[...] = jnp.zeros(acc_ref.shape, F32)

    def update(h, k, v):
        qp = _split_maps(q_ref[:, h * LANES:(h + 1) * LANES])
        s = lax.dot_general(qp, k, nt, preferred_element_type=F32)
        m_old = m_ref[h]
        m_new = jnp.maximum(m_old, jnp.max(s, axis=-1, keepdims=True))
        alpha = jnp.exp2(m_old - m_new)
        p = jnp.exp2(s - m_new)
        l_ref[h] = alpha * l_ref[h] + jnp.sum(p, axis=-1, keepdims=True)
        acc_ref[h] = alpha * acc_ref[h] + jnp.dot(p.astype(BF16), v, preferred_element_type=F32)
        m_ref[h] = m_new

    for h in range(heads):
        update(h, kc_ref[:, h, :].astype(BF16), vc_ref[:, h, :].astype(BF16))

    @pl.when(j == pl.num_programs(1) - 1)
    def _fin():
        for h in range(heads):
            update(h, kn_ref[:, h * LANES:(h + 1) * LANES], vn_ref[:, h * LANES:(h + 1) * LANES])
            o = acc_ref[h] / l_ref[h]
            o_ref[:, h * LANES:(h + 1) * LANES] = o[:lq] - lam_ref[0, 0] * o[lq:]


def _attn_sample(q, k_new, v_new, cache_k, cache_v, lam, *, l_seq, tk=1024):
    n, w = q.shape
    bsz, past, heads, _ = cache_k.shape
    tk = min(tk, past)
    assert n == bsz * l_seq and w == heads * LANES and past % tk == 0
    tok = pl.BlockSpec((l_seq, w), lambda b, j: (b, 0))
    cache = pl.BlockSpec((None, tk, heads, LANES), lambda b, j: (b, j, 0, 0))
    return pl.pallas_call(
        functools.partial(_attn_sample_kernel, heads=heads),
        grid=(bsz, past // tk),
        in_specs=[pl.BlockSpec(memory_space=pltpu.SMEM), tok, tok, tok, cache, cache],
        out_specs=tok,
        out_shape=jax.ShapeDtypeStruct((n, w), F32),
        scratch_shapes=[pltpu.VMEM((heads, 2 * l_seq, 1), F32), pltpu.VMEM((heads, 2 * l_seq, 1), F32),
                        pltpu.VMEM((heads, 2 * l_seq, LANES), F32)],
        compiler_params=pltpu.CompilerParams(dimension_semantics=("arbitrary", "arbitrary"),
                                             vmem_limit_bytes=40 * 1024 * 1024),
        name="attn_sample",
    )(lam, q, k_new, v_new, cache_k, cache_v)


def _head_norm(x, width):
    outs = []
    for g in range(x.shape[1] // width):
        xg = x[:, g * width:(g + 1) * width]
        outs.append(xg * lax.rsqrt(jnp.mean(xg * xg, axis=-1, keepdims=True) + EPS))
    return jnp.concatenate(outs, axis=1)


def _post_kernel(x_ref, oa_ref, gr_ref, ob_ref, sh1_ref, sc1_ref, gt1_ref, sh2_ref, sc2_ref,
                 n1_ref, n2_ref, ga_ref, gb_ref, wa_ref, wb_ref, wg_ref, wo_ref, x1_ref, h2_ref, *, diff_out_scale):
    x = x_ref[...]
    h = _modulated_norm(x, n1_ref[...], sc1_ref[0], sh1_ref[0]).astype(BF16)
    gates = jnp.dot(h, wg_ref[...], preferred_element_type=F32)
    gr = gr_ref[...]
    oa = _head_norm(oa_ref[...], GLA_DV) * ga_ref[...] * (gr * jax.nn.sigmoid(gr))
    ya = jnp.dot(oa.astype(BF16), wa_ref[...], preferred_element_type=F32)
    ob = _head_norm(ob_ref[...], DIFF_VD) * gb_ref[...] * diff_out_scale
    yb = jnp.dot(ob.astype(BF16), wb_ref[...], preferred_element_type=F32)
    merged = jax.nn.sigmoid(gates[:, :D_MODEL]) * ya + jax.nn.sigmoid(gates[:, D_MODEL:]) * yb
    mix = jnp.dot(merged.astype(BF16), wo_ref[...], preferred_element_type=F32)
    x1 = x + gt1_ref[0] * mix
    x1_ref[...] = x1
    h2_ref[...] = _modulated_norm(x1, n2_ref[...], sc2_ref[0], sh2_ref[0])


def _post_mixer(x, oa, gr, ob, mods, n1, n2, ga, gb, wa, wb, wg, wo, *, tt, mod_tiles, diff_out_scale):
    n = x.shape[0]
    assert n % tt == 0
    bmap = lambda i: (i // mod_tiles, 0, 0)
    mod = pl.BlockSpec((1,) + mods[0].shape[1:], bmap)
    row = pl.BlockSpec((tt, D_MODEL), lambda i: (i, 0))
    const = lambda a: pl.BlockSpec(a.shape, lambda i: (0,) * a.ndim)
    return pl.pallas_call(
        functools.partial(_post_kernel, diff_out_scale=diff_out_scale),
        grid=(n // tt,),
        in_specs=[row, row, row, row] + [mod] * 5 + [const(a) for a in (n1, n2, ga, gb, wa, wb, wg, wo)],
        out_specs=[row, row],
        out_shape=[jax.ShapeDtypeStruct((n, D_MODEL), F32), jax.ShapeDtypeStruct((n, D_MODEL), F32)],
        compiler_params=pltpu.CompilerParams(dimension_semantics=("arbitrary",), vmem_limit_bytes=48 * 1024 * 1024),
        name="post_mixer",
    )(x, oa, gr, ob, *mods, n1, n2, ga, gb, wa, wb, wg, wo)


def _topk_rows(s, k):
    r = s.shape[0]
    row = lax.broadcasted_iota(jnp.int32, s.shape, 0)
    vals, idxs = [], []
    for _ in range(k):
        mx = jnp.max(s, axis=0, keepdims=True)
        am = jnp.min(jnp.where(s == mx, row, r), axis=0, keepdims=True)
        vals.append(mx)
        idxs.append(am)
        s = jnp.where(row == am, -jnp.inf, s)
    return jnp.concatenate(vals, axis=0), jnp.concatenate(idxs, axis=0)


def _take_rows(tab, sel):
    out = jnp.zeros(sel.shape, tab.dtype)
    for a in range(tab.shape[0]):
        out = jnp.where(sel == a, tab[a:a + 1, :], out)
    return out


def _staircase():
    return [(a, b) for a in range(PEER_TOPK) for b in range(PEER_TOPK) if (a + 1) * (b + 1) <= PEER_TOPK]


def _peer_select_kernel(h2_ref, w_ref, k1_ref, k2_ref, idx_ref, g_ref):
    q = jnp.dot(h2_ref[...].astype(BF16), w_ref[...], preferred_element_type=F32)
    nt = (((1,), (1,)), ((), ()))
    s1 = lax.dot_general(k1_ref[...], q[:, :PEER_NKEYS].astype(BF16), nt, preferred_element_type=F32)
    s2 = lax.dot_general(k2_ref[...], q[:, PEER_NKEYS:].astype(BF16), nt, preferred_element_type=F32)
    v1, i1 = _topk_rows(s1, PEER_TOPK)
    v2, i2 = _topk_rows(s2, PEER_TOPK)
    pairs = _staircase()
    neg = jnp.full(((-len(pairs)) % SUBLANES, v1.shape[1]), -jnp.inf, F32)
    cand = jnp.concatenate([v1[a:a + 1, :] + v2[b:b + 1, :] for a, b in pairs] + [neg], axis=0)
    sc, cr = _topk_rows(cand, PEER_TOPK)
    ra = jnp.zeros(cr.shape, jnp.int32)
    rb = jnp.zeros(cr.shape, jnp.int32)
    for r, (a, b) in enumerate(pairs):
        hit = cr == r
        ra = jnp.where(hit, a, ra)
        rb = jnp.where(hit, b, rb)
    e1 = _take_rows(i1, ra)
    e2 = _take_rows(i2, rb)
    idx_ref[...] = e1 * PEER_NKEYS + e2
    p = jnp.exp(sc - sc[0:1, :])
    g_ref[...] = p / jnp.sum(p, axis=0, keepdims=True)


def _peer_select(h2, w_pq_bf, k1_bf, k2_bf, *, tt=SELECT_TILE):
    n, d = h2.shape
    assert n % tt == 0
    return pl.pallas_call(
        _peer_select_kernel,
        grid=(n // tt, PEER_HEADS),
        in_specs=[
            pl.BlockSpec((tt, d), lambda i, h: (i, 0)),
            pl.BlockSpec((d, PEER_DKEY), lambda i, h: (0, h)),
            pl.BlockSpec((PEER_NKEYS, PEER_DKEY // 2), lambda i, h: (0, 0)),
            pl.BlockSpec((PEER_NKEYS, PEER_DKEY // 2), lambda i, h: (0, 0)),
        ],
        out_specs=[pl.BlockSpec((PEER_TOPK, tt), lambda i, h: (h, i)),
                   pl.BlockSpec((PEER_TOPK, tt), lambda i, h: (h, i))],
        out_shape=[jax.ShapeDtypeStruct((NSEL, n), jnp.int32),
                   jax.ShapeDtypeStruct((NSEL, n), F32)],
        compiler_params=pltpu.CompilerParams(dimension_semantics=("arbitrary", "arbitrary")),
        name="peer_select",
    )(h2, w_pq_bf, k1_bf, k2_bf)


def _peer_gather_kernel(idx_ref, h2_ref, g_ref, x1_ref, gate_ref, tab_ref, out_ref, buf, sem, *, tt):
    s = pl.program_id(0)
    n = pl.num_programs(0) - 1

    @pl.when(s < n)
    def _issue():
        slot = s % 2

        def issue_tok(t, carry):
            for j in range(NSEL):
                e = idx_ref[t, j]
                pltpu.make_async_copy(tab_ref.at[e], buf.at[slot, t, pl.ds(j, 1), :], sem.at[slot, t]).start()
            return carry

        lax.fori_loop(0, tt, issue_tok, 0)

    @pl.when(s >= 1)
    def _compute():
        slot = (s + 1) % 2
        eye = (lax.broadcasted_iota(jnp.int32, (NSEL, NSEL), 0)
               == lax.broadcasted_iota(jnp.int32, (NSEL, NSEL), 1))
        gate = gate_ref[0]

        def tok(t, carry):
            for j in range(NSEL):
                pltpu.make_async_copy(tab_ref.at[0], buf.at[slot, t, pl.ds(j, 1), :], sem.at[slot, t]).wait()
            x = h2_ref[pl.ds(t, 1), :]
            p = buf[slot, t, :, 0:LANES] * x[:, 0:LANES]
            for c in range(1, D_MODEL // LANES):
                p = p + buf[slot, t, :, c * LANES:(c + 1) * LANES] * x[:, c * LANES:(c + 1) * LANES]
            a = jnp.sum(p, axis=-1, keepdims=True)
            act = 0.5 * a * (1.0 + lax.erf(a * (0.5 ** 0.5)))
            grow = g_ref[pl.ds(t, 1), :]
            gcol = jnp.sum(jnp.where(eye, grow, 0.0), axis=-1, keepdims=True)
            coef = gcol * act
            y = jnp.sum(coef * buf[slot, t, :, D_MODEL:2 * D_MODEL], axis=0, keepdims=True)
            out_ref[pl.ds(t, 1), :] = x1_ref[pl.ds(t, 1), :] + gate * y
            return carry

        lax.fori_loop(0, tt, tok, 0)


def _peer_gather(idx, g, h2, x1, gates, table, *, tt, l_seq):
    n_tok = idx.shape[0]
    n_tiles = n_tok // tt
    assert n_tiles * tt == n_tok and l_seq % tt == 0

    def cur(s):
        return jnp.maximum(s - 1, 0)

    return pl.pallas_call(
        functools.partial(_peer_gather_kernel, tt=tt),
        grid=(n_tiles + 1,),
        in_specs=[
            pl.BlockSpec((tt, NSEL), lambda s: (jnp.minimum(s, n_tiles - 1), 0), memory_space=pltpu.SMEM),
            pl.BlockSpec((tt, D_MODEL), lambda s: (cur(s), 0)),
            pl.BlockSpec((tt, NSEL), lambda s: (cur(s), 0)),
            pl.BlockSpec((tt, D_MODEL), lambda s: (cur(s), 0)),
            pl.BlockSpec((1, 1, D_MODEL), lambda s: (cur(s) * tt // l_seq, 0, 0)),
            pl.BlockSpec(memory_space=pl.ANY),
        ],
        out_specs=pl.BlockSpec((tt, D_MODEL), lambda s: (cur(s), 0)),
        out_shape=jax.ShapeDtypeStruct((n_tok, D_MODEL), F32),
        scratch_shapes=[pltpu.VMEM((2, tt, NSEL, 2 * D_MODEL), F32), pltpu.SemaphoreType.DMA((2, tt))],
        compiler_params=pltpu.CompilerParams(dimension_semantics=("arbitrary",),
                                             vmem_limit_bytes=48 * 1024 * 1024),
        name="peer_gather",
    )(idx, h2, g, x1, gates, table)


def _rope_tables(pos):
    half = DIFF_HD // 2
    inv = ROPE_THETA ** (-jnp.arange(half, dtype=F32) / half)
    ang = pos.astype(F32)[:, None] * inv[None, :]
    cos, sin = jnp.cos(ang), jnp.sin(ang)
    return jnp.tile(cos, (1, 4)), jnp.tile(jnp.concatenate([-sin, sin], axis=1), (1, 2))


def _stream(x, mods, mod_tiles, tile, cos, sin, l_seq, h0, wts, attend, lambda_init):
    sh1, sc1, gt1, sh2, sc2, gate2 = mods
    q_g, k_g, v_g, r_g, la = _proj_gla(x, sh1, sc1, wts["n1"], wts["w_a"], wts["w_alpha"], wts["b_alpha"],
                                       tt=tile, mod_tiles=mod_tiles)
    q_bf, k_rows, k_bf, v_rows, v_bf = _proj_diff(x, sh1, sc1, wts["n1"], wts["w_b"], wts["q_gain"], wts["k_gain"],
                                                  cos, sin, wts["ones"], tt=tile, mod_tiles=mod_tiles)
    o_a, state = _gla(q_g, k_g, la, v_g, h0, l_seq=l_seq, tb=min(GLA_TILE, l_seq))
    o_b = attend(q_bf, k_bf, v_bf)
    post_tile = min(POST_TILE, tile)
    x1, h2 = _post_mixer(x, o_a, r_g, o_b, (sh1, sc1, gt1, sh2, sc2), wts["n1"], wts["n2"], wts["gla_gain"],
                         wts["diff_gain"], wts["w_gla_out"], wts["w_diff_out"], wts["w_g"], wts["w_out"],
                         tt=post_tile, mod_tiles=mod_tiles * (tile // post_tile), diff_out_scale=1.0 - lambda_init)
    idx_t, g_t = _peer_select(h2, wts["w_pq"], wts["k1"], wts["k2"])
    y = _peer_gather(idx_t.T, g_t.T, h2, x1, gate2, wts["table"], tt=PEER_TOK_TILE, l_seq=l_seq)
    return y, k_rows, v_rows, state


def kernel(x_prompt, x_sample, c_prompt, c_sample, cache_k, cache_v, state_gla, norm1_gain, norm2_gain, w_mod, b_mod, w_in, w_alpha, b_alpha, gla_gain, w_gla_out, q_gain, k_gain, lam_q1, lam_k1, lam_q2, lam_k2, diff_gain, w_diff_out, w_out, w_pq, sub_keys1, sub_keys2, peer_u, peer_v):
    bp, lp, _ = x_prompt.shape
    bs, ls, _ = x_sample.shape
    past = cache_k.shape[2]
    l = 0
    lambda_init = 0.8 - 0.6 * math.exp(-0.3 * l)
    nb = bp + bs
    c_all = jnp.concatenate([c_prompt, c_sample, jnp.zeros(((-nb) % 8, D_MODEL), F32)], axis=0)
    mod_all = _adaln(c_all, w_mod[l], b_mod[l])
    mods = [mod_all[:nb, i * D_MODEL:(i + 1) * D_MODEL] for i in range(6)]

    wi = w_in[l]
    o_glr = 2 * GLA_QK_W + 2 * GLA_V_W
    o_diff = o_glr + GLA_LOWRANK
    o_gate = o_diff + 2 * DIFF_QK_W + DIFF_V_W
    pad_lr = LANES - GLA_LOWRANK
    lane = jnp.arange(LANES)
    wts = dict(
        n1=norm1_gain[l][None], n2=norm2_gain[l][None],
        w_a=jnp.concatenate([wi[:, :o_diff], jnp.zeros((D_MODEL, pad_lr), F32)], axis=1).astype(BF16),
        w_alpha=jnp.concatenate([w_alpha[l], jnp.zeros((pad_lr, GLA_QK_W), F32)], axis=0).astype(BF16),
        b_alpha=b_alpha[l][None],
        w_b=wi[:, o_diff:o_gate].astype(BF16), w_g=wi[:, o_gate:].astype(BF16),
        q_gain=q_gain[l].reshape(1, LANES), k_gain=k_gain[l].reshape(1, LANES),
        ones=(lane[:, None] // DIFF_HD == lane[None, :] // DIFF_HD).astype(BF16),
        gla_gain=jnp.tile(gla_gain[l], GLA_HEADS)[None], diff_gain=jnp.tile(diff_gain[l], DIFF_HEADS)[None],
        w_gla_out=w_gla_out[l].astype(BF16), w_diff_out=w_diff_out[l].astype(BF16), w_out=w_out[l].astype(BF16),
        w_pq=w_pq[l].astype(BF16), k1=sub_keys1[l].astype(BF16), k2=sub_keys2[l].astype(BF16),
        table=jnp.concatenate([peer_u[l], peer_v[l]], axis=1).reshape(-1, 1, 2 * D_MODEL),
    )
    lam = (jnp.exp(jnp.sum(lam_q1[l] * lam_k1[l])) - jnp.exp(jnp.sum(lam_q2[l] * lam_k2[l])) + lambda_init).reshape(1, 1)

    cos_p, sin_p = _rope_tables(jnp.arange(lp))
    mods_p = [m[:bp, None, :] for m in mods]
    attend_p = lambda q, k, v: _attn_prompt(q.reshape(bp, lp, -1), k.reshape(bp, lp, -1), v.reshape(bp, lp, -1),
                                            lam).reshape(bp * lp, -1)
    h0_p = jnp.zeros((bp, GLA_HEADS, GLA_DK, GLA_DV), F32)
    yp, kp, vp, sp = _stream(x_prompt.reshape(bp * lp, D_MODEL), mods_p, lp // PROJ_TILE, PROJ_TILE, cos_p, sin_p,
                             lp, h0_p, wts, attend_p, lambda_init)

    seq_per_tile = SAMPLE_TILE // ls
    cos_s, sin_s = _rope_tables(jnp.tile(past + jnp.arange(ls), seq_per_tile))
    mods_s = [jnp.repeat(m[bp:nb], ls, axis=0).reshape(-1, SAMPLE_TILE, D_MODEL) for m in mods[:5]]
    mods_s.append(mods[5][bp:nb, None, :])
    attend_s = lambda q, k, v: _attn_sample(q, k, v, cache_k[l], cache_v[l], lam, l_seq=ls)
    ys, ks, vs, ss = _stream(x_sample.reshape(bs * ls, D_MODEL), mods_s, 1, SAMPLE_TILE, cos_s, sin_s,
                             ls, state_gla[l], wts, attend_s, lambda_init)

    return (yp.reshape(bp, lp, D_MODEL), ys.reshape(bs, ls, D_MODEL),
            kp.reshape(1, bp, lp, DIFF_HEADS, 2 * DIFF_HD), vp.reshape(1, bp, lp, DIFF_HEADS, DIFF_VD), sp[None],
            ks.reshape(1, bs, ls, DIFF_HEADS, 2 * DIFF_HD), vs.reshape(1, bs, ls, DIFF_HEADS, DIFF_VD), ss[None])
```

```python
import math
import functools
import jax, jax.numpy as jnp
from jax import lax
from jax.experimental import pallas as pl
from jax.experimental.pallas import tpu as pltpu
from jax.experimental.pallas import tpu_sc as plsc

D_MODEL = 1024
CHUNK = 64
EPS = 1e-6
GLA_HEADS = 4
GLA_DK = 128
GLA_DV = 256
GLA_LOWRANK = 16
GLA_TAU = 16.0
GLA_BLOCK = 16
DIFF_HEADS = 8
DIFF_HD = 64
DIFF_VD = 128
ROPE_THETA = 10000.0
PEER_HEADS = 8
PEER_NKEYS = 128
PEER_DKEY = 256
PEER_TOPK = 16
GLA_QK_W = GLA_HEADS * GLA_DK
GLA_V_W = GLA_HEADS * GLA_DV
DIFF_QK_W = DIFF_HEADS * 2 * DIFF_HD
DIFF_V_W = DIFF_HEADS * DIFF_VD
F32 = jnp.float32
BF16 = jnp.bfloat16
NSEL = PEER_HEADS * PEER_TOPK
LANES = 128
SUBLANES = 8

PROJ_TILE = 512
POST_TILE = 256
GLA_TILE = 256
SAMPLE_TILE = 256
PEER_TOK_TILE = 8
ATTN_NEG = -1e30
ATTN_TILE = 1024
SELECT_TILE = 1024
SC_LANES = 16
SC_WORKERS = 32
SC_ROWS = 32
SC_STRIP = 256
SC_RGROUP = 8


def _mod_kernel(c_ref, w_ref, b_ref, o_ref):
    c = c_ref[...]
    s = c * jax.nn.sigmoid(c)
    o_ref[...] = jnp.dot(s.astype(BF16), w_ref[...].astype(BF16), preferred_element_type=F32) + b_ref[...]


def _adaln(c, w_mod, b_mod):
    n = c.shape[0]
    tn = 1536
    return pl.pallas_call(
        _mod_kernel,
        grid=(6 * D_MODEL // tn,),
        in_specs=[pl.BlockSpec((n, D_MODEL), lambda j: (0, 0)),
                  pl.BlockSpec((D_MODEL, tn), lambda j: (0, j)),
                  pl.BlockSpec((1, tn), lambda j: (0, j))],
        out_specs=pl.BlockSpec((n, tn), lambda j: (0, j)),
        out_shape=jax.ShapeDtypeStruct((n, 6 * D_MODEL), F32),
        name="adaln",
    )(c, w_mod, b_mod.reshape(1, -1))


def _modulated_norm(x, gain, scale, shift):
    y = x * lax.rsqrt(jnp.mean(x * x, axis=-1, keepdims=True) + EPS)
    return (y * gain) * (1.0 + scale) + shift


def _proj_gla_kernel(x_ref, sh_ref, sc_ref, gain_ref, w_ref, wa_ref, ba_ref,
                     q_ref, k_ref, v_ref, r_ref, la_ref):
    h = _modulated_norm(x_ref[...], gain_ref[...], sc_ref[0], sh_ref[0]).astype(BF16)
    z = jnp.dot(h, w_ref[...], preferred_element_type=F32)
    q_ref[...] = z[:, 0:GLA_QK_W] * (GLA_DK ** -0.5)
    k_ref[...] = z[:, GLA_QK_W:2 * GLA_QK_W]
    v_ref[...] = z[:, 2 * GLA_QK_W:2 * GLA_QK_W + GLA_V_W]
    r_ref[...] = z[:, 2 * GLA_QK_W + GLA_V_W:2 * GLA_QK_W + 2 * GLA_V_W]
    glr = z[:, 2 * GLA_QK_W + 2 * GLA_V_W:]
    a = jnp.dot(glr.astype(BF16), wa_ref[...], preferred_element_type=F32) + ba_ref[...]
    la_ref[...] = jax.nn.log_sigmoid(a) * (1.0 / GLA_TAU)


def _group_mean_square(x, ones_ref):
    s = x * x
    hi = s.astype(BF16)
    lo = (s - hi.astype(F32)).astype(BF16)
    ss = (jnp.dot(hi, ones_ref[...], preferred_element_type=F32)
          + jnp.dot(lo, ones_ref[...], preferred_element_type=F32))
    return ss * (1.0 / DIFF_HD)


def _norm_rope(x, gain, cos, sin_signed, ones_ref):
    outs = []
    lane = lax.broadcasted_iota(jnp.int32, (x.shape[0], LANES), 1)
    first_half = (lane % DIFF_HD) < (DIFF_HD // 2)
    for b in range(DIFF_QK_W // LANES):
        xb = x[:, b * LANES:(b + 1) * LANES]
        y = xb * lax.rsqrt(_group_mean_square(xb, ones_ref) + EPS) * gain
        rot = jnp.where(first_half, pltpu.roll(y, LANES - DIFF_HD // 2, axis=1), pltpu.roll(y, DIFF_HD // 2, axis=1))
        outs.append(y * cos + rot * sin_signed)
    return jnp.concatenate(outs, axis=1)


def _proj_diff_kernel(x_ref, sh_ref, sc_ref, gain_ref, w_ref, qg_ref, kg_ref, cos_ref, sin_ref, ones_ref,
                      qb_ref, k_ref, kb_ref, v_ref, vb_ref, *, q_scale):
    h = _modulated_norm(x_ref[...], gain_ref[...], sc_ref[0], sh_ref[0]).astype(BF16)
    z = jnp.dot(h, w_ref[...], preferred_element_type=F32)
    cos, sin = cos_ref[...], sin_ref[...]
    q = _norm_rope(z[:, 0:DIFF_QK_W], qg_ref[...], cos, sin, ones_ref)
    qb_ref[...] = (q * q_scale).astype(BF16)
    k = _norm_rope(z[:, DIFF_QK_W:2 * DIFF_QK_W], kg_ref[...], cos, sin, ones_ref)
    k_ref[...] = k
    kb_ref[...] = k.astype(BF16)
    v = z[:, 2 * DIFF_QK_W:2 * DIFF_QK_W + DIFF_V_W]
    v_ref[...] = v
    vb_ref[...] = v.astype(BF16)


def _proj_gla(x, shift, scale, gain, w_a, w_alpha_pad, b_alpha, *, tt, mod_tiles):
    n = x.shape[0]
    assert n % tt == 0
    bmap = lambda i: (i // mod_tiles, 0, 0)
    mod = pl.BlockSpec((1,) + shift.shape[1:], bmap)
    row = lambda w: pl.BlockSpec((tt, w), lambda i: (i, 0))
    const = lambda a: pl.BlockSpec(a.shape, lambda i: (0,) * a.ndim)
    return pl.pallas_call(
        _proj_gla_kernel,
        grid=(n // tt,),
        in_specs=[row(D_MODEL), mod, mod,
                  const(gain), const(w_a), const(w_alpha_pad), const(b_alpha)],
        out_specs=[row(GLA_QK_W), row(GLA_QK_W), row(GLA_V_W), row(GLA_V_W), row(GLA_QK_W)],
        out_shape=[jax.ShapeDtypeStruct((n, GLA_QK_W), F32), jax.ShapeDtypeStruct((n, GLA_QK_W), F32),
                   jax.ShapeDtypeStruct((n, GLA_V_W), F32), jax.ShapeDtypeStruct((n, GLA_V_W), F32),
                   jax.ShapeDtypeStruct((n, GLA_QK_W), F32)],
        compiler_params=pltpu.CompilerParams(dimension_semantics=("arbitrary",), vmem_limit_bytes=48 * 1024 * 1024),
        name="proj_gla",
    )(x, shift, scale, gain, w_a, w_alpha_pad, b_alpha)


def _proj_diff(x, shift, scale, gain, w_b, q_gain128, k_gain128, cos128, sin128, ones128, *, tt, mod_tiles):
    n = x.shape[0]
    assert n % tt == 0 and cos128.shape[0] % tt == 0
    bmap = lambda i: (i // mod_tiles, 0, 0)
    mod = pl.BlockSpec((1,) + shift.shape[1:], bmap)
    pos_tiles = cos128.shape[0] // tt
    row = lambda w: pl.BlockSpec((tt, w), lambda i: (i, 0))
    const = lambda a: pl.BlockSpec(a.shape, lambda i: (0,) * a.ndim)
    pos = pl.BlockSpec((tt, LANES), lambda i: (i % pos_tiles, 0))
    q_scale = math.log2(math.e) * DIFF_HD ** -0.5
    return pl.pallas_call(
        functools.partial(_proj_diff_kernel, q_scale=q_scale),
        grid=(n // tt,),
        in_specs=[row(D_MODEL), mod, mod,
                  const(gain), const(w_b), const(q_gain128), const(k_gain128), pos, pos, const(ones128)],
        out_specs=[row(DIFF_QK_W)] * 5,
        out_shape=[jax.ShapeDtypeStruct((n, DIFF_QK_W), BF16), jax.ShapeDtypeStruct((n, DIFF_QK_W), F32),
                   jax.ShapeDtypeStruct((n, DIFF_QK_W), BF16), jax.ShapeDtypeStruct((n, DIFF_V_W), F32),
                   jax.ShapeDtypeStruct((n, DIFF_V_W), BF16)],
        compiler_params=pltpu.CompilerParams(dimension_semantics=("arbitrary",), vmem_limit_bytes=48 * 1024 * 1024),
        name="proj_diff",
    )(x, shift, scale, gain, w_b, q_gain128, k_gain128, cos128, sin128, ones128)


def _gla_kernel(q_ref, k_ref, la_ref, v_ref, h0_ref, o_ref, hT_ref, st_ref, *, tb):
    i = pl.program_id(2)
    nb = tb // GLA_BLOCK

    @pl.when(i == 0)
    def _init():
        st_ref[...] = h0_ref[0, 0].T

    row = lax.broadcasted_iota(jnp.int32, (tb, tb), 0)
    col = lax.broadcasted_iota(jnp.int32, (tb, tb), 1)
    same = (row // GLA_BLOCK) == (col // GLA_BLOCK)
    causal = same & (col <= row)
    la = la_ref[...]
    hp = lax.Precision.HIGHEST
    b = jnp.dot(causal.astype(F32), la, precision=hp, preferred_element_type=F32)
    b_last = jnp.dot(same.astype(F32), la, precision=hp, preferred_element_type=F32)
    q, k, v = q_ref[...], k_ref[...], v_ref[...]
    qe = (q * jnp.exp(b)).astype(BF16)
    ke = (k * jnp.exp(-b)).astype(BF16)
    kd = (k * jnp.exp(b_last - b)).astype(BF16)
    vb = v.astype(BF16)
    a = lax.dot_general(qe, ke, (((1,), (1,)), ((), ())), preferred_element_type=F32)
    a = jnp.where(causal, a, 0.0).astype(BF16)
    o_intra = jnp.dot(a, vb, preferred_element_type=F32)

    st = st_ref[...]
    outs = []
    for n in range(nb):
        rs = slice(n * GLA_BLOCK, (n + 1) * GLA_BLOCK)
        o_inter = lax.dot_general(qe[rs], st.astype(BF16), (((1,), (1,)), ((), ())), preferred_element_type=F32)
        outs.append(o_inter + o_intra[rs])
        u = lax.dot_general(vb[rs], kd[rs], (((0,), (0,)), ((), ())), preferred_element_type=F32)
        st = jnp.exp(b_last[n * GLA_BLOCK:n * GLA_BLOCK + 1, :]) * st + u
    o_ref[...] = jnp.concatenate(outs, axis=0)
    st_ref[...] = st

    @pl.when(i == pl.num_programs(2) - 1)
    def _fin():
        hT_ref[0, 0] = st.T


def _gla(q, k, la, v, h0, *, l_seq, tb):
    n = q.shape[0]
    bsz = n // l_seq
    nt = l_seq // tb
    assert l_seq % tb == 0 and tb % GLA_BLOCK == 0
    tok = lambda w: pl.BlockSpec((tb, w), lambda b, h, i: (b * nt + i, h))
    st = pl.BlockSpec((1, 1, GLA_DK, GLA_DV), lambda b, h, i: (b, h, 0, 0))
    return pl.pallas_call(
        functools.partial(_gla_kernel, tb=tb),
        grid=(bsz, GLA_HEADS, nt),
        in_specs=[tok(GLA_DK), tok(GLA_DK), tok(GLA_DK), tok(GLA_DV), st],
        out_specs=[tok(GLA_DV), st],
        out_shape=[jax.ShapeDtypeStruct((n, GLA_V_W), F32),
                   jax.ShapeDtypeStruct((bsz, GLA_HEADS, GLA_DK, GLA_DV), F32)],
        scratch_shapes=[pltpu.VMEM((GLA_DV, GLA_DK), F32)],
        compiler_params=pltpu.CompilerParams(dimension_semantics=("arbitrary", "arbitrary", "arbitrary")),
        name="gla",
    )(q, k, la, v, h0)


def _split_maps(q):
    lane = lax.broadcasted_iota(jnp.int32, q.shape, 1)
    zero = jnp.zeros_like(q)
    return jnp.concatenate([jnp.where(lane < DIFF_HD, q, zero), jnp.where(lane >= DIFF_HD, q, zero)], axis=0)


def _attn_prompt_kernel(lam_ref, q_ref, k_ref, v_ref, o_ref, vt_ref, *, tq, tk):
    i = pl.program_id(2)
    n_kt = v_ref.shape[1] // tk

    @pl.when(i == 0)
    def _transpose_v():
        def body(j, c):
            vt_ref[j] = v_ref[0, pl.ds(j * tk, tk), :].astype(F32).T.astype(BF16)
            return c
        lax.fori_loop(0, n_kt, body, 0)

    qp = _split_maps(q_ref[0])

    def step(j, carry, masked, koff=0):
        m, l, acc = carry
        kt = k_ref[0, pl.ds(j * tk, tk), :]
        st = lax.dot_general(kt, qp, (((1,), (1,)), ((), ())), preferred_element_type=F32)
        if masked:
            kc = (lax.broadcasted_iota(jnp.int32, st.shape, 0) + koff) // CHUNK
            qc = (lax.broadcasted_iota(jnp.int32, st.shape, 1) % tq) // CHUNK
            st = jnp.where(qc >= kc, st, ATTN_NEG)
        m_new = jnp.maximum(m, jnp.max(st, axis=0, keepdims=True))
        alpha = jnp.exp2(m - m_new)
        p = jnp.exp2(st - m_new)
        l = alpha * l + jnp.sum(p, axis=0, keepdims=True)
        acc = alpha * acc + jnp.dot(vt_ref[j], p.astype(BF16), preferred_element_type=F32)
        return m_new, l, acc

    carry = (jnp.full((1, 2 * tq), ATTN_NEG, F32), jnp.zeros((1, 2 * tq), F32), jnp.zeros((LANES, 2 * tq), F32))
    r = tq // tk
    carry = lax.fori_loop(0, i * r, functools.partial(step, masked=False), carry)
    for d in range(r):
        carry = step(i * r + d, carry, True, koff=d * tk)
    m, l, acc = carry
    o = acc / l
    ot = o[:, :tq] - lam_ref[0, 0] * o[:, tq:]
    o_ref[0] = ot.T


def _attn_prompt(q, k, v, lam, *, tq=ATTN_TILE, tk=ATTN_TILE):
    b, l, w = q.shape
    h = w // LANES
    assert l % tq == 0 and tq % tk == 0 and tk % CHUNK == 0
    return pl.pallas_call(
        functools.partial(_attn_prompt_kernel, tq=tq, tk=tk),
        grid=(b, h, l // tq),
        in_specs=[
            pl.BlockSpec(memory_space=pltpu.SMEM),
            pl.BlockSpec((1, tq, LANES), lambda bi, hi, i: (bi, i, hi)),
            pl.BlockSpec((1, l, LANES), lambda bi, hi, i: (bi, 0, hi)),
            pl.BlockSpec((1, l, LANES), lambda bi, hi, i: (bi, 0, hi)),
        ],
        out_specs=pl.BlockSpec((1, tq, LANES), lambda bi, hi, i: (bi, i, hi)),
        out_shape=jax.ShapeDtypeStruct((b, l, w), F32),
        scratch_shapes=[pltpu.VMEM((l // tk, LANES, tk), BF16)],
        compiler_params=pltpu.CompilerParams(dimension_semantics=("arbitrary", "arbitrary", "arbitrary"),
                                             vmem_limit_bytes=40 * 1024 * 1024),
        name="attn_prompt",
    )(lam, q, k, v)


def _attn_sample_kernel(lam_ref, q_ref, kn_ref, vn_ref, kc_ref, vc_ref, o_ref, m_ref, l_ref, acc_ref, *, heads):
    j = pl.program_id(1)
    lq = q_ref.shape[0]
    nt = (((1,), (1,)), ((), ()))

    @pl.when(j == 0)
    def _init():
        m_ref[...] = jnp.full(m_ref.shape, ATTN_NEG, F32)
        l_ref[...] = jnp.zeros(l_ref.shape, F32)
        acc_ref[...] = jnp.zeros(acc_ref.shape, F32)

    def update(h, k, v):
        qp = _split_maps(q_ref[:, h * LANES:(h + 1) * LANES])
        s = lax.dot_general(qp, k, nt, preferred_element_type=F32)
        m_old = m_ref[h]
        m_new = jnp.maximum(m_old, jnp.max(s, axis=-1, keepdims=True))
        alpha = jnp.exp2(m_old - m_new)
        p = jnp.exp2(s - m_new)
        l_ref[h] = alpha * l_ref[h] + jnp.sum(p, axis=-1, keepdims=True)
        acc_ref[h] = alpha * acc_ref[h] + jnp.dot(p.astype(BF16), v, preferred_element_type=F32)
        m_ref[h] = m_new

    for h in range(heads):
        update(h, kc_ref[:, h, :].astype(BF16), vc_ref[:, h, :].astype(BF16))

    @pl.when(j == pl.num_programs(1) - 1)
    def _fin():
        for h in range(heads):
            update(h, kn_ref[:, h * LANES:(h + 1) * LANES], vn_ref[:, h * LANES:(h + 1) * LANES])
            o = acc_ref[h] / l_ref[h]
            o_ref[:, h * LANES:(h + 1) * LANES] = o[:lq] - lam_ref[0, 0] * o[lq:]


def _attn_sample(q, k_new, v_new, cache_k, cache_v, lam, *, l_seq, tk=1024):
    n, w = q.shape
    bsz, past, heads, _ = cache_k.shape
    tk = min(tk, past)
    assert n == bsz * l_seq and w == heads * LANES and past % tk == 0
    tok = pl.BlockSpec((l_seq, w), lambda b, j: (b, 0))
    cache = pl.BlockSpec((None, tk, heads, LANES), lambda b, j: (b, j, 0, 0))
    return pl.pallas_call(
        functools.partial(_attn_sample_kernel, heads=heads),
        grid=(bsz, past // tk),
        in_specs=[pl.BlockSpec(memory_space=pltpu.SMEM), tok, tok, tok, cache, cache],
        out_specs=tok,
        out_shape=jax.ShapeDtypeStruct((n, w), F32),
        scratch_shapes=[pltpu.VMEM((heads, 2 * l_seq, 1), F32), pltpu.VMEM((heads, 2 * l_seq, 1), F32),
                        pltpu.VMEM((heads, 2 * l_seq, LANES), F32)],
        compiler_params=pltpu.CompilerParams(dimension_semantics=("arbitrary", "arbitrary"),
                                             vmem_limit_bytes=40 * 1024 * 1024),
        name="attn_sample",
    )(lam, q, k_new, v_new, cache_k, cache_v)


def _head_norm(x, width):
    outs = []
    for g in range(x.shape[1] // width):
        xg = x[:, g * width:(g + 1) * width]
        outs.append(xg * lax.rsqrt(jnp.mean(xg * xg, axis=-1, keepdims=True) + EPS))
    return jnp.concatenate(outs, axis=1)


def _post_kernel(x_ref, oa_ref, gr_ref, ob_ref, sh1_ref, sc1_ref, gt1_ref, sh2_ref, sc2_ref,
                 n1_ref, n2_ref, ga_ref, gb_ref, wa_ref, wb_ref, wg_ref, wo_ref, x1_ref, h2_ref, *, diff_out_scale):
    x = x_ref[...]
    h = _modulated_norm(x, n1_ref[...], sc1_ref[0], sh1_ref[0]).astype(BF16)
    gates = jnp.dot(h, wg_ref[...], preferred_element_type=F32)
    gr = gr_ref[...]
    oa = _head_norm(oa_ref[...], GLA_DV) * ga_ref[...] * (gr * jax.nn.sigmoid(gr))
    ya = jnp.dot(oa.astype(BF16), wa_ref[...], preferred_element_type=F32)
    ob = _head_norm(ob_ref[...], DIFF_VD) * gb_ref[...] * diff_out_scale
    yb = jnp.dot(ob.astype(BF16), wb_ref[...], preferred_element_type=F32)
    merged = jax.nn.sigmoid(gates[:, :D_MODEL]) * ya + jax.nn.sigmoid(gates[:, D_MODEL:]) * yb
    mix = jnp.dot(merged.astype(BF16), wo_ref[...], preferred_element_type=F32)
    x1 = x + gt1_ref[0] * mix
    x1_ref[...] = x1
    h2_ref[...] = _modulated_norm(x1, n2_ref[...], sc2_ref[0], sh2_ref[0])


def _post_mixer(x, oa, gr, ob, mods, n1, n2, ga, gb, wa, wb, wg, wo, *, tt, mod_tiles, diff_out_scale):
    n = x.shape[0]
    assert n % tt == 0
    bmap = lambda i: (i // mod_tiles, 0, 0)
    mod = pl.BlockSpec((1,) + mods[0].shape[1:], bmap)
    row = pl.BlockSpec((tt, D_MODEL), lambda i: (i, 0))
    const = lambda a: pl.BlockSpec(a.shape, lambda i: (0,) * a.ndim)
    return pl.pallas_call(
        functools.partial(_post_kernel, diff_out_scale=diff_out_scale),
        grid=(n // tt,),
        in_specs=[row, row, row, row] + [mod] * 5 + [const(a) for a in (n1, n2, ga, gb, wa, wb, wg, wo)],
        out_specs=[row, row],
        out_shape=[jax.ShapeDtypeStruct((n, D_MODEL), F32), jax.ShapeDtypeStruct((n, D_MODEL), F32)],
        compiler_params=pltpu.CompilerParams(dimension_semantics=("arbitrary",), vmem_limit_bytes=48 * 1024 * 1024),
        name="post_mixer",
    )(x, oa, gr, ob, *mods, n1, n2, ga, gb, wa, wb, wg, wo)


def _topk_rows(s, k):
    r = s.shape[0]
    row = lax.broadcasted_iota(jnp.int32, s.shape, 0)
    vals, idxs = [], []
    for _ in range(k):
        mx = jnp.max(s, axis=0, keepdims=True)
        am = jnp.min(jnp.where(s == mx, row, r), axis=0, keepdims=True)
        vals.append(mx)
        idxs.append(am)
        s = jnp.where(row == am, -jnp.inf, s)
    return jnp.concatenate(vals, axis=0), jnp.concatenate(idxs, axis=0)


def _take_rows(tab, sel):
    out = jnp.zeros(sel.shape, tab.dtype)
    for a in range(tab.shape[0]):
        out = jnp.where(sel == a, tab[a:a + 1, :], out)
    return out


def _staircase():
    return [(a, b) for a in range(PEER_TOPK) for b in range(PEER_TOPK) if (a + 1) * (b + 1) <= PEER_TOPK]


def _peer_select_kernel(h2_ref, w_ref, k1_ref, k2_ref, idx_ref, g_ref):
    q = jnp.dot(h2_ref[...].astype(BF16), w_ref[...], preferred_element_type=F32)
    nt = (((1,), (1,)), ((), ()))
    s1 = lax.dot_general(k1_ref[...], q[:, :PEER_NKEYS].astype(BF16), nt, preferred_element_type=F32)
    s2 = lax.dot_general(k2_ref[...], q[:, PEER_NKEYS:].astype(BF16), nt, preferred_element_type=F32)
    v1, i1 = _topk_rows(s1, PEER_TOPK)
    v2, i2 = _topk_rows(s2, PEER_TOPK)
    pairs = _staircase()
    neg = jnp.full(((-len(pairs)) % SUBLANES, v1.shape[1]), -jnp.inf, F32)
    cand = jnp.concatenate([v1[a:a + 1, :] + v2[b:b + 1, :] for a, b in pairs] + [neg], axis=0)
    sc, cr = _topk_rows(cand, PEER_TOPK)
    ra = jnp.zeros(cr.shape, jnp.int32)
    rb = jnp.zeros(cr.shape, jnp.int32)
    for r, (a, b) in enumerate(pairs):
        hit = cr == r
        ra = jnp.where(hit, a, ra)
        rb = jnp.where(hit, b, rb)
    e1 = _take_rows(i1, ra)
    e2 = _take_rows(i2, rb)
    idx_ref[...] = e1 * PEER_NKEYS + e2
    p = jnp.exp(sc - sc[0:1, :])
    g_ref[...] = p / jnp.sum(p, axis=0, keepdims=True)


def _peer_select(h2, w_pq_bf, k1_bf, k2_bf, *, tt=SELECT_TILE):
    n, d = h2.shape
    assert n % tt == 0
    return pl.pallas_call(
        _peer_select_kernel,
        grid=(n // tt, PEER_HEADS),
        in_specs=[
            pl.BlockSpec((tt, d), lambda i, h: (i, 0)),
            pl.BlockSpec((d, PEER_DKEY), lambda i, h: (0, h)),
            pl.BlockSpec((PEER_NKEYS, PEER_DKEY // 2), lambda i, h: (0, 0)),
            pl.BlockSpec((PEER_NKEYS, PEER_DKEY // 2), lambda i, h: (0, 0)),
        ],
        out_specs=[pl.BlockSpec((PEER_TOPK, tt), lambda i, h: (h, i)),
                   pl.BlockSpec((PEER_TOPK, tt), lambda i, h: (h, i))],
        out_shape=[jax.ShapeDtypeStruct((NSEL, n), jnp.int32),
                   jax.ShapeDtypeStruct((NSEL, n), F32)],
        compiler_params=pltpu.CompilerParams(dimension_semantics=("arbitrary", "arbitrary")),
        name="peer_select",
    )(h2, w_pq_bf, k1_bf, k2_bf)


def _peer_gather_kernel(idx_ref, h2_ref, g_ref, x1_ref, gate_ref, tab_ref, out_ref, buf, sem, *, tt):
    s = pl.program_id(0)
    n = pl.num_programs(0) - 1

    @pl.when(s < n)
    def _issue():
        slot = s % 2

        def issue_tok(t, carry):
            for j in range(NSEL):
                e = idx_ref[t, j]
                pltpu.make_async_copy(tab_ref.at[e], buf.at[slot, t, pl.ds(j, 1), :], sem.at[slot, t]).start()
            return carry

        lax.fori_loop(0, tt, issue_tok, 0)

    @pl.when(s >= 1)
    def _compute():
        slot = (s + 1) % 2
        eye = (lax.broadcasted_iota(jnp.int32, (NSEL, NSEL), 0)
               == lax.broadcasted_iota(jnp.int32, (NSEL, NSEL), 1))
        gate = gate_ref[0]

        def tok(t, carry):
            for j in range(NSEL):
                pltpu.make_async_copy(tab_ref.at[0], buf.at[slot, t, pl.ds(j, 1), :], sem.at[slot, t]).wait()
            x = h2_ref[pl.ds(t, 1), :]
            p = buf[slot, t, :, 0:LANES] * x[:, 0:LANES]
            for c in range(1, D_MODEL // LANES):
                p = p + buf[slot, t, :, c * LANES:(c + 1) * LANES] * x[:, c * LANES:(c + 1) * LANES]
            a = jnp.sum(p, axis=-1, keepdims=True)
            act = 0.5 * a * (1.0 + lax.erf(a * (0.5 ** 0.5)))
            grow = g_ref[pl.ds(t, 1), :]
            gcol = jnp.sum(jnp.where(eye, grow, 0.0), axis=-1, keepdims=True)
            coef = gcol * act
            y = jnp.sum(coef * buf[slot, t, :, D_MODEL:2 * D_MODEL], axis=0, keepdims=True)
            out_ref[pl.ds(t, 1), :] = x1_ref[pl.ds(t, 1), :] + gate * y
            return carry

        lax.fori_loop(0, tt, tok, 0)


def _peer_gather(idx, g, h2, x1, gates, table, *, tt, l_seq):
    n_tok = idx.shape[0]
    n_tiles = n_tok // tt
    assert n_tiles * tt == n_tok and l_seq % tt == 0

    def cur(s):
        return jnp.maximum(s - 1, 0)

    return pl.pallas_call(
        functools.partial(_peer_gather_kernel, tt=tt),
        grid=(n_tiles + 1,),
        in_specs=[
            pl.BlockSpec((tt, NSEL), lambda s: (jnp.minimum(s, n_tiles - 1), 0), memory_space=pltpu.SMEM),
            pl.BlockSpec((tt, D_MODEL), lambda s: (cur(s), 0)),
            pl.BlockSpec((tt, NSEL), lambda s: (cur(s), 0)),
            pl.BlockSpec((tt, D_MODEL), lambda s: (cur(s), 0)),
            pl.BlockSpec((1, 1, D_MODEL), lambda s: (cur(s) * tt // l_seq, 0, 0)),
            pl.BlockSpec(memory_space=pl.ANY),
        ],
        out_specs=pl.BlockSpec((tt, D_MODEL), lambda s: (cur(s), 0)),
        out_shape=jax.ShapeDtypeStruct((n_tok, D_MODEL), F32),
        scratch_shapes=[pltpu.VMEM((2, tt, NSEL, 2 * D_MODEL), F32), pltpu.SemaphoreType.DMA((2, tt))],
        compiler_params=pltpu.CompilerParams(dimension_semantics=("arbitrary",),
                                             vmem_limit_bytes=48 * 1024 * 1024),
        name="peer_gather",
    )(idx, h2, g, x1, gates, table)


def _sc_chunk_copy(tab_hbm, idx_v, rows_v, sem, c):
    return pltpu.make_async_copy(tab_hbm.at[idx_v.at[pl.ds(c * SC_ROWS, SC_ROWS)]], rows_v.at[c % 2], sem.at[c % 2])


def _sc_udot(idx, x, table):
    ns = idx.shape[0]
    per_w = ns // SC_WORKERS
    assert per_w * SC_WORKERS == ns
    nchunk, nstrip, nk = NSEL // SC_ROWS, D_MODEL // SC_STRIP, SC_STRIP // SC_LANES

    @functools.partial(
        pl.kernel, mesh=plsc.VectorSubcoreMesh(core_axis_name="c", subcore_axis_name="s"),
        out_type=jax.ShapeDtypeStruct((ns, NSEL, SC_LANES), F32),
        scratch_types=[pltpu.VMEM((NSEL,), jnp.int32), pltpu.VMEM((D_MODEL,), F32),
                       pltpu.VMEM((2, SC_ROWS, D_MODEL), F32), pltpu.VMEM((NSEL, SC_LANES), F32),
                       pltpu.SemaphoreType.DMA((2,))],
    )
    def k(idx_hbm, x_hbm, tab_hbm, out_hbm, idx_v, x_v, rows_v, acc_v, sem):
        base = (lax.axis_index("s") * 2 + lax.axis_index("c")) * per_w

        def token(i, carry):
            tok = base + i
            pltpu.sync_copy(idx_hbm.at[tok], idx_v)
            pltpu.sync_copy(x_hbm.at[tok], x_v)
            _sc_chunk_copy(tab_hbm, idx_v, rows_v, sem, 0).start()
            for c in range(nchunk):
                if c + 1 < nchunk:
                    _sc_chunk_copy(tab_hbm, idx_v, rows_v, sem, c + 1).start()
                _sc_chunk_copy(tab_hbm, idx_v, rows_v, sem, c).wait()
                for kc in range(nstrip):
                    xr = [x_v[pl.ds(kc * SC_STRIP + kk * SC_LANES, SC_LANES)] for kk in range(nk)]

                    def row(r, cr, c=c, kc=kc, xr=xr):
                        s = rows_v[c % 2, r, pl.ds(kc * SC_STRIP, SC_LANES)] * xr[0]
                        for kk in range(1, nk):
                            s = s + rows_v[c % 2, r, pl.ds(kc * SC_STRIP + kk * SC_LANES, SC_LANES)] * xr[kk]
                        if kc == 0:
                            acc_v[c * SC_ROWS + r, :] = s
                        else:
                            acc_v[c * SC_ROWS + r, :] = acc_v[c * SC_ROWS + r, :] + s
                        return cr

                    lax.fori_loop(0, SC_ROWS, row, 0)
            pltpu.sync_copy(acc_v, out_hbm.at[tok])
            return carry

        lax.fori_loop(0, per_w, token, 0)

    return k(idx, x, table)


def _sc_vsum(idx, coefb, table):
    ns = idx.shape[0]
    per_w = ns // SC_WORKERS
    assert per_w * SC_WORKERS == ns
    nchunk, nstrip, nk = NSEL // SC_ROWS, D_MODEL // SC_STRIP, SC_STRIP // SC_LANES

    @functools.partial(
        pl.kernel, mesh=plsc.VectorSubcoreMesh(core_axis_name="c", subcore_axis_name="s"),
        out_type=jax.ShapeDtypeStruct((ns, D_MODEL), F32),
        scratch_types=[pltpu.VMEM((NSEL,), jnp.int32), pltpu.VMEM((NSEL, SC_LANES), F32),
                       pltpu.VMEM((2, SC_ROWS, D_MODEL), F32), pltpu.VMEM((D_MODEL,), F32),
                       pltpu.SemaphoreType.DMA((2,))],
    )
    def k(idx_hbm, coef_hbm, tab_hbm, out_hbm, idx_v, coef_v, rows_v, y_v, sem):
        base = (lax.axis_index("s") * 2 + lax.axis_index("c")) * per_w

        def token(i, carry):
            tok = base + i
            pltpu.sync_copy(idx_hbm.at[tok], idx_v)
            pltpu.sync_copy(coef_hbm.at[tok], coef_v)
            _sc_chunk_copy(tab_hbm, idx_v, rows_v, sem, 0).start()
            for kk in range(D_MODEL // SC_LANES):
                y_v[pl.ds(kk * SC_LANES, SC_LANES)] = jnp.zeros((SC_LANES,), F32)
            for c in range(nchunk):
                if c + 1 < nchunk:
                    _sc_chunk_copy(tab_hbm, idx_v, rows_v, sem, c + 1).start()
                _sc_chunk_copy(tab_hbm, idx_v, rows_v, sem, c).wait()
                for kc in range(nstrip):
                    def rows8(rg, cr, c=c, kc=kc):
                        acc = [y_v[pl.ds(kc * SC_STRIP + kk * SC_LANES, SC_LANES)] for kk in range(nk)]
                        for rr in range(SC_RGROUP):
                            r = rg * SC_RGROUP + rr
                            cv = coef_v[c * SC_ROWS + r, :]
                            for kk in range(nk):
                                acc[kk] = acc[kk] + cv * rows_v[c % 2, r, pl.ds(kc * SC_STRIP + kk * SC_LANES, SC_LANES)]
                        for kk in range(nk):
                            y_v[pl.ds(kc * SC_STRIP + kk * SC_LANES, SC_LANES)] = acc[kk]
                        return cr

                    lax.fori_loop(0, SC_ROWS // SC_RGROUP, rows8, 0)
            pltpu.sync_copy(y_v, out_hbm.at[tok])
            return carry

        lax.fori_loop(0, per_w, token, 0)

    return k(idx, coefb, table)


def _sc_coef_kernel(a_ref, g_ref, o_ref):
    hp = lax.Precision.HIGHEST
    grp = lax.broadcasted_iota(jnp.int32, (NSEL * SC_LANES, NSEL), 0) // SC_LANES
    fold = (grp == lax.broadcasted_iota(jnp.int32, (NSEL * SC_LANES, NSEL), 1)).astype(F32)
    a = jnp.dot(a_ref[...], fold, precision=hp, preferred_element_type=F32)
    coef = g_ref[...] * (0.5 * a * (1.0 + lax.erf(a * (0.5 ** 0.5))))
    o_ref[...] = lax.dot_general(coef, fold, (((1,), (1,)), ((), ())), precision=hp, preferred_element_type=F32)


def _sc_coef(a_part, g, *, tt=512):
    n = a_part.shape[0]
    wide = pl.BlockSpec((tt, NSEL * SC_LANES), lambda i: (i, 0))
    return pl.pallas_call(
        _sc_coef_kernel, grid=(n // tt,),
        in_specs=[wide, pl.BlockSpec((tt, NSEL), lambda i: (i, 0))], out_specs=wide,
        out_shape=jax.ShapeDtypeStruct((n, NSEL * SC_LANES), F32), name="sc_coef",
    )(a_part, g)


def _sc_residual_kernel(x1_ref, ff_ref, gate_ref, o_ref):
    o_ref[...] = x1_ref[...] + gate_ref[0] * ff_ref[...]


def _sc_residual(x1, ff, gate, *, tt=1024):
    n = x1.shape[0]
    row = pl.BlockSpec((tt, D_MODEL), lambda i: (i, 0))
    return pl.pallas_call(
        _sc_residual_kernel, grid=(n // tt,),
        in_specs=[row, row, pl.BlockSpec((1, 1, D_MODEL), lambda i: (0, 0, 0))], out_specs=row,
        out_shape=jax.ShapeDtypeStruct((n, D_MODEL), F32), name="sc_residual",
    )(x1, ff, gate)


def _rope_tables(pos):
    half = DIFF_HD // 2
    inv = ROPE_THETA ** (-jnp.arange(half, dtype=F32) / half)
    ang = pos.astype(F32)[:, None] * inv[None, :]
    cos, sin = jnp.cos(ang), jnp.sin(ang)
    return jnp.tile(cos, (1, 4)), jnp.tile(jnp.concatenate([-sin, sin], axis=1), (1, 2))


def _stream(x, mods, mod_tiles, tile, cos, sin, l_seq, h0, wts, attend, lambda_init, sc_share):
    sh1, sc1, gt1, sh2, sc2, gate2 = mods
    q_g, k_g, v_g, r_g, la = _proj_gla(x, sh1, sc1, wts["n1"], wts["w_a"], wts["w_alpha"], wts["b_alpha"],
                                       tt=tile, mod_tiles=mod_tiles)
    q_bf, k_rows, k_bf, v_rows, v_bf = _proj_diff(x, sh1, sc1, wts["n1"], wts["w_b"], wts["q_gain"], wts["k_gain"],
                                                  cos, sin, wts["ones"], tt=tile, mod_tiles=mod_tiles)
    o_a, state = _gla(q_g, k_g, la, v_g, h0, l_seq=l_seq, tb=min(GLA_TILE, l_seq))
    o_b = attend(q_bf, k_bf, v_bf)
    post_tile = min(POST_TILE, tile)
    x1, h2 = _post_mixer(x, o_a, r_g, o_b, (sh1, sc1, gt1, sh2, sc2), wts["n1"], wts["n2"], wts["gla_gain"],
                         wts["diff_gain"], wts["w_gla_out"], wts["w_diff_out"], wts["w_g"], wts["w_out"],
                         tt=post_tile, mod_tiles=mod_tiles * (tile // post_tile), diff_out_scale=1.0 - lambda_init)
    idx_t, g_t = _peer_select(h2, wts["w_pq"], wts["k1"], wts["k2"])
    idx, g = idx_t.T, g_t.T
    n_seq = x.shape[0] // l_seq
    if not sc_share:
        y = _peer_gather(idx, g, h2, x1, gate2, wts["table"], tt=PEER_TOK_TILE, l_seq=l_seq)
        return y, k_rows, v_rows, state

    def part(a, lo, hi):
        return a[lo * l_seq:hi * l_seq]

    n_a = (n_seq - 1) // 3
    last = (n_seq - 1, n_seq)
    a_part = _sc_udot(part(idx, *last), part(h2, *last), wts["u"])
    y_a = _peer_gather(part(idx, 0, n_a), part(g, 0, n_a), part(h2, 0, n_a), part(x1, 0, n_a), gate2[:n_a],
                       wts["table"], tt=PEER_TOK_TILE, l_seq=l_seq)
    coefb = _sc_coef(a_part.reshape(l_seq, NSEL * SC_LANES), part(g, *last))
    ff = _sc_vsum(part(idx, *last), coefb.reshape(l_seq, NSEL, SC_LANES), wts["v"])
    y_b = _peer_gather(part(idx, n_a, n_seq - 1), part(g, n_a, n_seq - 1), part(h2, n_a, n_seq - 1),
                       part(x1, n_a, n_seq - 1), gate2[n_a:n_seq - 1], wts["table"], tt=PEER_TOK_TILE, l_seq=l_seq)
    y_c = _sc_residual(part(x1, *last), ff, gate2[n_seq - 1:])
    return jnp.concatenate([y_a, y_b, y_c], axis=0), k_rows, v_rows, state


def kernel(x_prompt, x_sample, c_prompt, c_sample, cache_k, cache_v, state_gla, norm1_gain, norm2_gain, w_mod, b_mod, w_in, w_alpha, b_alpha, gla_gain, w_gla_out, q_gain, k_gain, lam_q1, lam_k1, lam_q2, lam_k2, diff_gain, w_diff_out, w_out, w_pq, sub_keys1, sub_keys2, peer_u, peer_v):
    bp, lp, _ = x_prompt.shape
    bs, ls, _ = x_sample.shape
    past = cache_k.shape[2]
    l = 0
    lambda_init = 0.8 - 0.6 * math.exp(-0.3 * l)
    nb = bp + bs
    c_all = jnp.concatenate([c_prompt, c_sample, jnp.zeros(((-nb) % 8, D_MODEL), F32)], axis=0)
    mod_all = _adaln(c_all, w_mod[l], b_mod[l])
    mods = [mod_all[:nb, i * D_MODEL:(i + 1) * D_MODEL] for i in range(6)]

    wi = w_in[l]
    o_glr = 2 * GLA_QK_W + 2 * GLA_V_W
    o_diff = o_glr + GLA_LOWRANK
    o_gate = o_diff + 2 * DIFF_QK_W + DIFF_V_W
    pad_lr = LANES - GLA_LOWRANK
    lane = jnp.arange(LANES)
    wts = dict(
        n1=norm1_gain[l][None], n2=norm2_gain[l][None],
        w_a=jnp.concatenate([wi[:, :o_diff], jnp.zeros((D_MODEL, pad_lr), F32)], axis=1).astype(BF16),
        w_alpha=jnp.concatenate([w_alpha[l], jnp.zeros((pad_lr, GLA_QK_W), F32)], axis=0).astype(BF16),
        b_alpha=b_alpha[l][None],
        w_b=wi[:, o_diff:o_gate].astype(BF16), w_g=wi[:, o_gate:].astype(BF16),
        q_gain=q_gain[l].reshape(1, LANES), k_gain=k_gain[l].reshape(1, LANES),
        ones=(lane[:, None] // DIFF_HD == lane[None, :] // DIFF_HD).astype(BF16),
        gla_gain=jnp.tile(gla_gain[l], GLA_HEADS)[None], diff_gain=jnp.tile(diff_gain[l], DIFF_HEADS)[None],
        w_gla_out=w_gla_out[l].astype(BF16), w_diff_out=w_diff_out[l].astype(BF16), w_out=w_out[l].astype(BF16),
        w_pq=w_pq[l].astype(BF16), k1=sub_keys1[l].astype(BF16), k2=sub_keys2[l].astype(BF16),
        table=jnp.concatenate([peer_u[l], peer_v[l]], axis=1).reshape(-1, 1, 2 * D_MODEL),
        u=peer_u[l], v=peer_v[l],
    )
    lam = (jnp.exp(jnp.sum(lam_q1[l] * lam_k1[l])) - jnp.exp(jnp.sum(lam_q2[l] * lam_k2[l])) + lambda_init).reshape(1, 1)

    cos_p, sin_p = _rope_tables(jnp.arange(lp))
    mods_p = [m[:bp, None, :] for m in mods]
    attend_p = lambda q, k, v: _attn_prompt(q.reshape(bp, lp, -1), k.reshape(bp, lp, -1), v.reshape(bp, lp, -1),
                                            lam).reshape(bp * lp, -1)
    h0_p = jnp.zeros((bp, GLA_HEADS, GLA_DK, GLA_DV), F32)
    yp, kp, vp, sp = _stream(x_prompt.reshape(bp * lp, D_MODEL), mods_p, lp // PROJ_TILE, PROJ_TILE, cos_p, sin_p,
                             lp, h0_p, wts, attend_p, lambda_init, sc_share=bp >= 4)

    seq_per_tile = SAMPLE_TILE // ls
    cos_s, sin_s = _rope_tables(jnp.tile(past + jnp.arange(ls), seq_per_tile))
    mods_s = [jnp.repeat(m[bp:nb], ls, axis=0).reshape(-1, SAMPLE_TILE, D_MODEL) for m in mods[:5]]
    mods_s.append(mods[5][bp:nb, None, :])
    attend_s = lambda q, k, v: _attn_sample(q, k, v, cache_k[l], cache_v[l], lam, l_seq=ls)
    ys, ks, vs, ss = _stream(x_sample.reshape(bs * ls, D_MODEL), mods_s, 1, SAMPLE_TILE, cos_s, sin_s,
                             ls, state_gla[l], wts, attend_s, lambda_init, sc_share=False)

    return (yp.reshape(bp, lp, D_MODEL), ys.reshape(bs, ls, D_MODEL),
            kp.reshape(1, bp, lp, DIFF_HEADS, 2 * DIFF_HD), vp.reshape(1, bp, lp, DIFF_HEADS, DIFF_VD), sp[None],
            ks.reshape(1, bs, ls, DIFF_HEADS, 2 * DIFF_HD), vs.reshape(1, bs, ls, DIFF_HEADS, DIFF_VD), ss[None])
```

```python
import math
import functools
import jax, jax.numpy as jnp
from jax import lax
from jax.experimental import pallas as pl
from jax.experimental.pallas import tpu as pltpu
from jax.experimental.pallas import tpu_sc as plsc

D_MODEL = 1024
CHUNK = 64
EPS = 1e-6
GLA_HEADS = 4
GLA_DK = 128
GLA_DV = 256
GLA_LOWRANK = 16
GLA_TAU = 16.0
GLA_BLOCK = 16
DIFF_HEADS = 8
DIFF_HD = 64
DIFF_VD = 128
ROPE_THETA = 10000.0
PEER_HEADS = 8
PEER_NKEYS = 128
PEER_DKEY = 256
PEER_TOPK = 16
GLA_QK_W = GLA_HEADS * GLA_DK
GLA_V_W = GLA_HEADS * GLA_DV
DIFF_QK_W = DIFF_HEADS * 2 * DIFF_HD
DIFF_V_W = DIFF_HEADS * DIFF_VD
F32 = jnp.float32
BF16 = jnp.bfloat16
NSEL = PEER_HEADS * PEER_TOPK
LANES = 128
SUBLANES = 8

PROJ_TILE = 512
POST_TILE = 256
GLA_TILE = 256
SAMPLE_TILE = 256
PEER_TOK_TILE = 8
ATTN_NEG = -1e30
ATTN_TILE = 1024
SELECT_TILE = 1024
SC_LANES = 16
SC_WORKERS = 32
SC_ROWS = 32
SC_STRIP = 256
SC_RGROUP = 8
SC_SPLIT_BLOCK = 512
SC_SHARE = 21 / 64
SC_FIRST_SHARE = 20 / 64


def _mod_kernel(c_ref, w_ref, b_ref, o_ref):
    c = c_ref[...]
    s = c * jax.nn.sigmoid(c)
    o_ref[...] = jnp.dot(s.astype(BF16), w_ref[...].astype(BF16), preferred_element_type=F32) + b_ref[...]


def _adaln(c, w_mod, b_mod):
    n = c.shape[0]
    tn = 1536
    return pl.pallas_call(
        _mod_kernel,
        grid=(6 * D_MODEL // tn,),
        in_specs=[pl.BlockSpec((n, D_MODEL), lambda j: (0, 0)),
                  pl.BlockSpec((D_MODEL, tn), lambda j: (0, j)),
                  pl.BlockSpec((1, tn), lambda j: (0, j))],
        out_specs=pl.BlockSpec((n, tn), lambda j: (0, j)),
        out_shape=jax.ShapeDtypeStruct((n, 6 * D_MODEL), F32),
        name="adaln",
    )(c, w_mod, b_mod.reshape(1, -1))


def _modulated_norm(x, gain, scale, shift):
    y = x * lax.rsqrt(jnp.mean(x * x, axis=-1, keepdims=True) + EPS)
    return (y * gain) * (1.0 + scale) + shift


def _proj_gla_kernel(x_ref, sh_ref, sc_ref, gain_ref, w_ref, wa_ref, ba_ref,
                     q_ref, k_ref, v_ref, r_ref, la_ref):
    h = _modulated_norm(x_ref[...], gain_ref[...], sc_ref[0], sh_ref[0]).astype(BF16)
    z = jnp.dot(h, w_ref[...], preferred_element_type=F32)
    q_ref[...] = z[:, 0:GLA_QK_W] * (GLA_DK ** -0.5)
    k_ref[...] = z[:, GLA_QK_W:2 * GLA_QK_W]
    v_ref[...] = z[:, 2 * GLA_QK_W:2 * GLA_QK_W + GLA_V_W]
    r_ref[...] = z[:, 2 * GLA_QK_W + GLA_V_W:2 * GLA_QK_W + 2 * GLA_V_W]
    glr = z[:, 2 * GLA_QK_W + 2 * GLA_V_W:]
    a = jnp.dot(glr.astype(BF16), wa_ref[...], preferred_element_type=F32) + ba_ref[...]
    la_ref[...] = jax.nn.log_sigmoid(a) * (1.0 / GLA_TAU)


def _group_mean_square(x, ones_ref):
    s = x * x
    hi = s.astype(BF16)
    lo = (s - hi.astype(F32)).astype(BF16)
    ss = (jnp.dot(hi, ones_ref[...], preferred_element_type=F32)
          + jnp.dot(lo, ones_ref[...], preferred_element_type=F32))
    return ss * (1.0 / DIFF_HD)


def _norm_rope(x, gain, cos, sin_signed, ones_ref):
    outs = []
    lane = lax.broadcasted_iota(jnp.int32, (x.shape[0], LANES), 1)
    first_half = (lane % DIFF_HD) < (DIFF_HD // 2)
    for b in range(DIFF_QK_W // LANES):
        xb = x[:, b * LANES:(b + 1) * LANES]
        y = xb * lax.rsqrt(_group_mean_square(xb, ones_ref) + EPS) * gain
        rot = jnp.where(first_half, pltpu.roll(y, LANES - DIFF_HD // 2, axis=1), pltpu.roll(y, DIFF_HD // 2, axis=1))
        outs.append(y * cos + rot * sin_signed)
    return jnp.concatenate(outs, axis=1)


def _proj_diff_kernel(x_ref, sh_ref, sc_ref, gain_ref, w_ref, qg_ref, kg_ref, cos_ref, sin_ref, ones_ref,
                      qb_ref, k_ref, kb_ref, v_ref, vb_ref, *, q_scale):
    h = _modulated_norm(x_ref[...], gain_ref[...], sc_ref[0], sh_ref[0]).astype(BF16)
    z = jnp.dot(h, w_ref[...], preferred_element_type=F32)
    cos, sin = cos_ref[...], sin_ref[...]
    q = _norm_rope(z[:, 0:DIFF_QK_W], qg_ref[...], cos, sin, ones_ref)
    qb_ref[...] = (q * q_scale).astype(BF16)
    k = _norm_rope(z[:, DIFF_QK_W:2 * DIFF_QK_W], kg_ref[...], cos, sin, ones_ref)
    k_ref[...] = k
    kb_ref[...] = k.astype(BF16)
    v = z[:, 2 * DIFF_QK_W:2 * DIFF_QK_W + DIFF_V_W]
    v_ref[...] = v
    vb_ref[...] = v.astype(BF16)


def _proj_gla(x, shift, scale, gain, w_a, w_alpha_pad, b_alpha, *, tt, mod_tiles):
    n = x.shape[0]
    assert n % tt == 0
    bmap = lambda i: (i // mod_tiles, 0, 0)
    mod = pl.BlockSpec((1,) + shift.shape[1:], bmap)
    row = lambda w: pl.BlockSpec((tt, w), lambda i: (i, 0))
    const = lambda a: pl.BlockSpec(a.shape, lambda i: (0,) * a.ndim)
    return pl.pallas_call(
        _proj_gla_kernel,
        grid=(n // tt,),
        in_specs=[row(D_MODEL), mod, mod,
                  const(gain), const(w_a), const(w_alpha_pad), const(b_alpha)],
        out_specs=[row(GLA_QK_W), row(GLA_QK_W), row(GLA_V_W), row(GLA_V_W), row(GLA_QK_W)],
        out_shape=[jax.ShapeDtypeStruct((n, GLA_QK_W), F32), jax.ShapeDtypeStruct((n, GLA_QK_W), F32),
                   jax.ShapeDtypeStruct((n, GLA_V_W), F32), jax.ShapeDtypeStruct((n, GLA_V_W), F32),
                   jax.ShapeDtypeStruct((n, GLA_QK_W), F32)],
        compiler_params=pltpu.CompilerParams(dimension_semantics=("arbitrary",), vmem_limit_bytes=48 * 1024 * 1024),
        name="proj_gla",
    )(x, shift, scale, gain, w_a, w_alpha_pad, b_alpha)


def _proj_diff(x, shift, scale, gain, w_b, q_gain128, k_gain128, cos128, sin128, ones128, *, tt, mod_tiles):
    n = x.shape[0]
    assert n % tt == 0 and cos128.shape[0] % tt == 0
    bmap = lambda i: (i // mod_tiles, 0, 0)
    mod = pl.BlockSpec((1,) + shift.shape[1:], bmap)
    pos_tiles = cos128.shape[0] // tt
    row = lambda w: pl.BlockSpec((tt, w), lambda i: (i, 0))
    const = lambda a: pl.BlockSpec(a.shape, lambda i: (0,) * a.ndim)
    pos = pl.BlockSpec((tt, LANES), lambda i: (i % pos_tiles, 0))
    q_scale = math.log2(math.e) * DIFF_HD ** -0.5
    return pl.pallas_call(
        functools.partial(_proj_diff_kernel, q_scale=q_scale),
        grid=(n // tt,),
        in_specs=[row(D_MODEL), mod, mod,
                  const(gain), const(w_b), const(q_gain128), const(k_gain128), pos, pos, const(ones128)],
        out_specs=[row(DIFF_QK_W)] * 5,
        out_shape=[jax.ShapeDtypeStruct((n, DIFF_QK_W), BF16), jax.ShapeDtypeStruct((n, DIFF_QK_W), F32),
                   jax.ShapeDtypeStruct((n, DIFF_QK_W), BF16), jax.ShapeDtypeStruct((n, DIFF_V_W), F32),
                   jax.ShapeDtypeStruct((n, DIFF_V_W), BF16)],
        compiler_params=pltpu.CompilerParams(dimension_semantics=("arbitrary",), vmem_limit_bytes=48 * 1024 * 1024),
        name="proj_diff",
    )(x, shift, scale, gain, w_b, q_gain128, k_gain128, cos128, sin128, ones128)


def _gla_kernel(q_ref, k_ref, la_ref, v_ref, h0_ref, o_ref, hT_ref, st_ref, *, tb):
    i = pl.program_id(2)
    nb = tb // GLA_BLOCK

    @pl.when(i == 0)
    def _init():
        st_ref[...] = h0_ref[0, 0].T

    row = lax.broadcasted_iota(jnp.int32, (tb, tb), 0)
    col = lax.broadcasted_iota(jnp.int32, (tb, tb), 1)
    same = (row // GLA_BLOCK) == (col // GLA_BLOCK)
    causal = same & (col <= row)
    la = la_ref[...]
    hp = lax.Precision.HIGHEST
    b = jnp.dot(causal.astype(F32), la, precision=hp, preferred_element_type=F32)
    b_last = jnp.dot(same.astype(F32), la, precision=hp, preferred_element_type=F32)
    q, k, v = q_ref[...], k_ref[...], v_ref[...]
    qe = (q * jnp.exp(b)).astype(BF16)
    ke = (k * jnp.exp(-b)).astype(BF16)
    kd = (k * jnp.exp(b_last - b)).astype(BF16)
    vb = v.astype(BF16)
    a = lax.dot_general(qe, ke, (((1,), (1,)), ((), ())), preferred_element_type=F32)
    a = jnp.where(causal, a, 0.0).astype(BF16)
    o_intra = jnp.dot(a, vb, preferred_element_type=F32)

    st = st_ref[...]
    outs = []
    for n in range(nb):
        rs = slice(n * GLA_BLOCK, (n + 1) * GLA_BLOCK)
        o_inter = lax.dot_general(qe[rs], st.astype(BF16), (((1,), (1,)), ((), ())), preferred_element_type=F32)
        outs.append(o_inter + o_intra[rs])
        u = lax.dot_general(vb[rs], kd[rs], (((0,), (0,)), ((), ())), preferred_element_type=F32)
        st = jnp.exp(b_last[n * GLA_BLOCK:n * GLA_BLOCK + 1, :]) * st + u
    o_ref[...] = jnp.concatenate(outs, axis=0)
    st_ref[...] = st

    @pl.when(i == pl.num_programs(2) - 1)
    def _fin():
        hT_ref[0, 0] = st.T


def _gla(q, k, la, v, h0, *, l_seq, tb):
    n = q.shape[0]
    bsz = n // l_seq
    nt = l_seq // tb
    assert l_seq % tb == 0 and tb % GLA_BLOCK == 0
    tok = lambda w: pl.BlockSpec((tb, w), lambda b, h, i: (b * nt + i, h))
    st = pl.BlockSpec((1, 1, GLA_DK, GLA_DV), lambda b, h, i: (b, h, 0, 0))
    return pl.pallas_call(
        functools.partial(_gla_kernel, tb=tb),
        grid=(bsz, GLA_HEADS, nt),
        in_specs=[tok(GLA_DK), tok(GLA_DK), tok(GLA_DK), tok(GLA_DV), st],
        out_specs=[tok(GLA_DV), st],
        out_shape=[jax.ShapeDtypeStruct((n, GLA_V_W), F32),
                   jax.ShapeDtypeStruct((bsz, GLA_HEADS, GLA_DK, GLA_DV), F32)],
        scratch_shapes=[pltpu.VMEM((GLA_DV, GLA_DK), F32)],
        compiler_params=pltpu.CompilerParams(dimension_semantics=("arbitrary", "arbitrary", "arbitrary")),
        name="gla",
    )(q, k, la, v, h0)


def _split_maps(q):
    lane = lax.broadcasted_iota(jnp.int32, q.shape, 1)
    zero = jnp.zeros_like(q)
    return jnp.concatenate([jnp.where(lane < DIFF_HD, q, zero), jnp.where(lane >= DIFF_HD, q, zero)], axis=0)


def _attn_prompt_kernel(lam_ref, q_ref, k_ref, v_ref, o_ref, vt_ref, *, tq, tk):
    i = pl.program_id(2)
    n_kt = v_ref.shape[1] // tk

    @pl.when(i == 0)
    def _transpose_v():
        def body(j, c):
            vt_ref[j] = v_ref[0, pl.ds(j * tk, tk), :].astype(F32).T.astype(BF16)
            return c
        lax.fori_loop(0, n_kt, body, 0)

    qp = _split_maps(q_ref[0])

    def step(j, carry, masked, koff=0):
        m, l, acc = carry
        kt = k_ref[0, pl.ds(j * tk, tk), :]
        st = lax.dot_general(kt, qp, (((1,), (1,)), ((), ())), preferred_element_type=F32)
        if masked:
            kc = (lax.broadcasted_iota(jnp.int32, st.shape, 0) + koff) // CHUNK
            qc = (lax.broadcasted_iota(jnp.int32, st.shape, 1) % tq) // CHUNK
            st = jnp.where(qc >= kc, st, ATTN_NEG)
        m_new = jnp.maximum(m, jnp.max(st, axis=0, keepdims=True))
        alpha = jnp.exp2(m - m_new)
        p = jnp.exp2(st - m_new)
        l = alpha * l + jnp.sum(p, axis=0, keepdims=True)
        acc = alpha * acc + jnp.dot(vt_ref[j], p.astype(BF16), preferred_element_type=F32)
        return m_new, l, acc

    carry = (jnp.full((1, 2 * tq), ATTN_NEG, F32), jnp.zeros((1, 2 * tq), F32), jnp.zeros((LANES, 2 * tq), F32))
    r = tq // tk
    carry = lax.fori_loop(0, i * r, functools.partial(step, masked=False), carry)
    for d in range(r):
        carry = step(i * r + d, carry, True, koff=d * tk)
    m, l, acc = carry
    o = acc / l
    ot = o[:, :tq] - lam_ref[0, 0] * o[:, tq:]
    o_ref[0] = ot.T


def _attn_prompt(q, k, v, lam, *, tq=ATTN_TILE, tk=ATTN_TILE):
    b, l, w = q.shape
    h = w // LANES
    assert l % tq == 0 and tq % tk == 0 and tk % CHUNK == 0
    return pl.pallas_call(
        functools.partial(_attn_prompt_kernel, tq=tq, tk=tk),
        grid=(b, h, l // tq),
        in_specs=[
            pl.BlockSpec(memory_space=pltpu.SMEM),
            pl.BlockSpec((1, tq, LANES), lambda bi, hi, i: (bi, i, hi)),
            pl.BlockSpec((1, l, LANES), lambda bi, hi, i: (bi, 0, hi)),
            pl.BlockSpec((1, l, LANES), lambda bi, hi, i: (bi, 0, hi)),
        ],
        out_specs=pl.BlockSpec((1, tq, LANES), lambda bi, hi, i: (bi, i, hi)),
        out_shape=jax.ShapeDtypeStruct((b, l, w), F32),
        scratch_shapes=[pltpu.VMEM((l // tk, LANES, tk), BF16)],
        compiler_params=pltpu.CompilerParams(dimension_semantics=("arbitrary", "arbitrary", "arbitrary"),
                                             vmem_limit_bytes=40 * 1024 * 1024),
        name="attn_prompt",
    )(lam, q, k, v)


def _attn_sample_kernel(lam_ref, q_ref, kn_ref, vn_ref, kc_ref, vc_ref, o_ref, m_ref, l_ref, acc_ref, *, heads):
    j = pl.program_id(1)
    lq = q_ref.shape[0]
    nt = (((1,), (1,)), ((), ()))

    @pl.when(j == 0)
    def _init():
        m_ref[...] = jnp.full(m_ref.shape, ATTN_NEG, F32)
        l_ref[...] = jnp.zeros(l_ref.shape, F32)
        acc_ref[...] = jnp.zeros(acc_ref.shape, F32)

    def update(h, k, v):
        qp = _split_maps(q_ref[:, h * LANES:(h + 1) * LANES])
        s = lax.dot_general(qp, k, nt, preferred_element_type=F32)
        m_old = m_ref[h]
        m_new = jnp.maximum(m_old, jnp.max(s, axis=-1, keepdims=True))
        alpha = jnp.exp2(m_old - m_new)
        p = jnp.exp2(s - m_new)
        l_ref[h] = alpha * l_ref[h] + jnp.sum(p, axis=-1, keepdims=True)
        acc_ref[h] = alpha * acc_ref[h] + jnp.dot(p.astype(BF16), v, preferred_element_type=F32)
        m_ref[h] = m_new

    for h in range(heads):
        update(h, kc_ref[:, h, :].astype(BF16), vc_ref[:, h, :].astype(BF16))

    @pl.when(j == pl.num_programs(1) - 1)
    def _fin():
        for h in range(heads):
            update(h, kn_ref[:, h * LANES:(h + 1) * LANES], vn_ref[:, h * LANES:(h + 1) * LANES])
            o = acc_ref[h] / l_ref[h]
            o_ref[:, h * LANES:(h + 1) * LANES] = o[:lq] - lam_ref[0, 0] * o[lq:]


def _attn_sample(q, k_new, v_new, cache_k, cache_v, lam, *, l_seq, tk=1024):
    n, w = q.shape
    bsz, past, heads, _ = cache_k.shape
    tk = min(tk, past)
    assert n == bsz * l_seq and w == heads * LANES and past % tk == 0
    tok = pl.BlockSpec((l_seq, w), lambda b, j: (b, 0))
    cache = pl.BlockSpec((None, tk, heads, LANES), lambda b, j: (b, j, 0, 0))
    return pl.pallas_call(
        functools.partial(_attn_sample_kernel, heads=heads),
        grid=(bsz, past // tk),
        in_specs=[pl.BlockSpec(memory_space=pltpu.SMEM), tok, tok, tok, cache, cache],
        out_specs=tok,
        out_shape=jax.ShapeDtypeStruct((n, w), F32),
        scratch_shapes=[pltpu.VMEM((heads, 2 * l_seq, 1), F32), pltpu.VMEM((heads, 2 * l_seq, 1), F32),
                        pltpu.VMEM((heads, 2 * l_seq, LANES), F32)],
        compiler_params=pltpu.CompilerParams(dimension_semantics=("arbitrary", "arbitrary"),
                                             vmem_limit_bytes=40 * 1024 * 1024),
        name="attn_sample",
    )(lam, q, k_new, v_new, cache_k, cache_v)


def _head_norm(x, width):
    outs = []
    for g in range(x.shape[1] // width):
        xg = x[:, g * width:(g + 1) * width]
        outs.append(xg * lax.rsqrt(jnp.mean(xg * xg, axis=-1, keepdims=True) + EPS))
    return jnp.concatenate(outs, axis=1)


def _post_kernel(x_ref, oa_ref, gr_ref, ob_ref, sh1_ref, sc1_ref, gt1_ref, sh2_ref, sc2_ref,
                 n1_ref, n2_ref, ga_ref, gb_ref, wa_ref, wb_ref, wg_ref, wo_ref, x1_ref, h2_ref, *, diff_out_scale):
    x = x_ref[...]
    h = _modulated_norm(x, n1_ref[...], sc1_ref[0], sh1_ref[0]).astype(BF16)
    gates = jnp.dot(h, wg_ref[...], preferred_element_type=F32)
    gr = gr_ref[...]
    oa = _head_norm(oa_ref[...], GLA_DV) * ga_ref[...] * (gr * jax.nn.sigmoid(gr))
    ya = jnp.dot(oa.astype(BF16), wa_ref[...], preferred_element_type=F32)
    ob = _head_norm(ob_ref[...], DIFF_VD) * gb_ref[...] * diff_out_scale
    yb = jnp.dot(ob.astype(BF16), wb_ref[...], preferred_element_type=F32)
    merged = jax.nn.sigmoid(gates[:, :D_MODEL]) * ya + jax.nn.sigmoid(gates[:, D_MODEL:]) * yb
    mix = jnp.dot(merged.astype(BF16), wo_ref[...], preferred_element_type=F32)
    x1 = x + gt1_ref[0] * mix
    x1_ref[...] = x1
    h2_ref[...] = _modulated_norm(x1, n2_ref[...], sc2_ref[0], sh2_ref[0])


def _post_mixer(x, oa, gr, ob, mods, n1, n2, ga, gb, wa, wb, wg, wo, *, tt, mod_tiles, diff_out_scale):
    n = x.shape[0]
    assert n % tt == 0
    bmap = lambda i: (i // mod_tiles, 0, 0)
    mod = pl.BlockSpec((1,) + mods[0].shape[1:], bmap)
    row = pl.BlockSpec((tt, D_MODEL), lambda i: (i, 0))
    const = lambda a: pl.BlockSpec(a.shape, lambda i: (0,) * a.ndim)
    return pl.pallas_call(
        functools.partial(_post_kernel, diff_out_scale=diff_out_scale),
        grid=(n // tt,),
        in_specs=[row, row, row, row] + [mod] * 5 + [const(a) for a in (n1, n2, ga, gb, wa, wb, wg, wo)],
        out_specs=[row, row],
        out_shape=[jax.ShapeDtypeStruct((n, D_MODEL), F32), jax.ShapeDtypeStruct((n, D_MODEL), F32)],
        compiler_params=pltpu.CompilerParams(dimension_semantics=("arbitrary",), vmem_limit_bytes=48 * 1024 * 1024),
        name="post_mixer",
    )(x, oa, gr, ob, *mods, n1, n2, ga, gb, wa, wb, wg, wo)


def _topk_rows(s, k):
    r = s.shape[0]
    row = lax.broadcasted_iota(jnp.int32, s.shape, 0)
    vals, idxs = [], []
    for _ in range(k):
        mx = jnp.max(s, axis=0, keepdims=True)
        am = jnp.min(jnp.where(s == mx, row, r), axis=0, keepdims=True)
        vals.append(mx)
        idxs.append(am)
        s = jnp.where(row == am, -jnp.inf, s)
    return jnp.concatenate(vals, axis=0), jnp.concatenate(idxs, axis=0)


def _take_rows(tab, sel):
    out = jnp.zeros(sel.shape, tab.dtype)
    for a in range(tab.shape[0]):
        out = jnp.where(sel == a, tab[a:a + 1, :], out)
    return out


def _staircase():
    return [(a, b) for a in range(PEER_TOPK) for b in range(PEER_TOPK) if (a + 1) * (b + 1) <= PEER_TOPK]


def _peer_select_kernel(h2_ref, w_ref, k1_ref, k2_ref, idx_ref, g_ref):
    q = jnp.dot(h2_ref[...].astype(BF16), w_ref[...], preferred_element_type=F32)
    nt = (((1,), (1,)), ((), ()))
    s1 = lax.dot_general(k1_ref[...], q[:, :PEER_NKEYS].astype(BF16), nt, preferred_element_type=F32)
    s2 = lax.dot_general(k2_ref[...], q[:, PEER_NKEYS:].astype(BF16), nt, preferred_element_type=F32)
    v1, i1 = _topk_rows(s1, PEER_TOPK)
    v2, i2 = _topk_rows(s2, PEER_TOPK)
    pairs = _staircase()
    neg = jnp.full(((-len(pairs)) % SUBLANES, v1.shape[1]), -jnp.inf, F32)
    cand = jnp.concatenate([v1[a:a + 1, :] + v2[b:b + 1, :] for a, b in pairs] + [neg], axis=0)
    sc, cr = _topk_rows(cand, PEER_TOPK)
    ra = jnp.zeros(cr.shape, jnp.int32)
    rb = jnp.zeros(cr.shape, jnp.int32)
    for r, (a, b) in enumerate(pairs):
        hit = cr == r
        ra = jnp.where(hit, a, ra)
        rb = jnp.where(hit, b, rb)
    e1 = _take_rows(i1, ra)
    e2 = _take_rows(i2, rb)
    idx_ref[...] = e1 * PEER_NKEYS + e2
    p = jnp.exp(sc - sc[0:1, :])
    g_ref[...] = p / jnp.sum(p, axis=0, keepdims=True)


def _peer_select(h2, w_pq_bf, k1_bf, k2_bf, *, tt=SELECT_TILE):
    n, d = h2.shape
    assert n % tt == 0
    return pl.pallas_call(
        _peer_select_kernel,
        grid=(n // tt, PEER_HEADS),
        in_specs=[
            pl.BlockSpec((tt, d), lambda i, h: (i, 0)),
            pl.BlockSpec((d, PEER_DKEY), lambda i, h: (0, h)),
            pl.BlockSpec((PEER_NKEYS, PEER_DKEY // 2), lambda i, h: (0, 0)),
            pl.BlockSpec((PEER_NKEYS, PEER_DKEY // 2), lambda i, h: (0, 0)),
        ],
        out_specs=[pl.BlockSpec((PEER_TOPK, tt), lambda i, h: (h, i)),
                   pl.BlockSpec((PEER_TOPK, tt), lambda i, h: (h, i))],
        out_shape=[jax.ShapeDtypeStruct((NSEL, n), jnp.int32),
                   jax.ShapeDtypeStruct((NSEL, n), F32)],
        compiler_params=pltpu.CompilerParams(dimension_semantics=("arbitrary", "arbitrary")),
        name="peer_select",
    )(h2, w_pq_bf, k1_bf, k2_bf)


def _peer_gather_kernel(idx_ref, h2_ref, g_ref, x1_ref, gate_ref, tab_ref, out_ref, buf, sem, *, tt):
    s = pl.program_id(0)
    n = pl.num_programs(0) - 1

    @pl.when(s < n)
    def _issue():
        slot = s % 2

        def issue_tok(t, carry):
            for j in range(NSEL):
                e = idx_ref[t, j]
                pltpu.make_async_copy(tab_ref.at[e], buf.at[slot, t, pl.ds(j, 1), :], sem.at[slot, t]).start()
            return carry

        lax.fori_loop(0, tt, issue_tok, 0)

    @pl.when(s >= 1)
    def _compute():
        slot = (s + 1) % 2
        eye = (lax.broadcasted_iota(jnp.int32, (NSEL, NSEL), 0)
               == lax.broadcasted_iota(jnp.int32, (NSEL, NSEL), 1))
        gate = gate_ref[0]

        def tok(t, carry):
            for j in range(NSEL):
                pltpu.make_async_copy(tab_ref.at[0], buf.at[slot, t, pl.ds(j, 1), :], sem.at[slot, t]).wait()
            x = h2_ref[pl.ds(t, 1), :]
            p = buf[slot, t, :, 0:LANES] * x[:, 0:LANES]
            for c in range(1, D_MODEL // LANES):
                p = p + buf[slot, t, :, c * LANES:(c + 1) * LANES] * x[:, c * LANES:(c + 1) * LANES]
            a = jnp.sum(p, axis=-1, keepdims=True)
            act = 0.5 * a * (1.0 + lax.erf(a * (0.5 ** 0.5)))
            grow = g_ref[pl.ds(t, 1), :]
            gcol = jnp.sum(jnp.where(eye, grow, 0.0), axis=-1, keepdims=True)
            coef = gcol * act
            y = jnp.sum(coef * buf[slot, t, :, D_MODEL:2 * D_MODEL], axis=0, keepdims=True)
            out_ref[pl.ds(t, 1), :] = x1_ref[pl.ds(t, 1), :] + gate * y
            return carry

        lax.fori_loop(0, tt, tok, 0)


def _peer_gather(idx, g, h2, x1, *, gates, table, tt, l_seq, tok_offset=0):
    n_tok = idx.shape[0]
    n_tiles = n_tok // tt
    assert n_tiles * tt == n_tok and l_seq % tt == 0

    def cur(s):
        return jnp.maximum(s - 1, 0)

    return pl.pallas_call(
        functools.partial(_peer_gather_kernel, tt=tt),
        grid=(n_tiles + 1,),
        in_specs=[
            pl.BlockSpec((tt, NSEL), lambda s: (jnp.minimum(s, n_tiles - 1), 0), memory_space=pltpu.SMEM),
            pl.BlockSpec((tt, D_MODEL), lambda s: (cur(s), 0)),
            pl.BlockSpec((tt, NSEL), lambda s: (cur(s), 0)),
            pl.BlockSpec((tt, D_MODEL), lambda s: (cur(s), 0)),
            pl.BlockSpec((1, 1, D_MODEL), lambda s: ((tok_offset + cur(s) * tt) // l_seq, 0, 0)),
            pl.BlockSpec(memory_space=pl.ANY),
        ],
        out_specs=pl.BlockSpec((tt, D_MODEL), lambda s: (cur(s), 0)),
        out_shape=jax.ShapeDtypeStruct((n_tok, D_MODEL), F32),
        scratch_shapes=[pltpu.VMEM((2, tt, NSEL, 2 * D_MODEL), F32), pltpu.SemaphoreType.DMA((2, tt))],
        compiler_params=pltpu.CompilerParams(dimension_semantics=("arbitrary",),
                                             vmem_limit_bytes=48 * 1024 * 1024),
        name="peer_gather",
    )(idx, h2, g, x1, gates, table)


def _sc_chunk_copy(tab_hbm, idx_v, rows_v, sem, c):
    return pltpu.make_async_copy(tab_hbm.at[idx_v.at[pl.ds(c * SC_ROWS, SC_ROWS)]], rows_v.at[c % 2], sem.at[c % 2])


def _sc_udot(idx, x, table):
    ns = idx.shape[0]
    per_w = ns // SC_WORKERS
    assert per_w * SC_WORKERS == ns
    nchunk, nstrip, nk = NSEL // SC_ROWS, D_MODEL // SC_STRIP, SC_STRIP // SC_LANES

    @functools.partial(
        pl.kernel, mesh=plsc.VectorSubcoreMesh(core_axis_name="c", subcore_axis_name="s"),
        out_type=jax.ShapeDtypeStruct((ns, NSEL, SC_LANES), F32),
        scratch_types=[pltpu.VMEM((NSEL,), jnp.int32), pltpu.VMEM((D_MODEL,), F32),
                       pltpu.VMEM((2, SC_ROWS, D_MODEL), F32), pltpu.VMEM((NSEL, SC_LANES), F32),
                       pltpu.SemaphoreType.DMA((2,))],
    )
    def k(idx_hbm, x_hbm, tab_hbm, out_hbm, idx_v, x_v, rows_v, acc_v, sem):
        base = (lax.axis_index("s") * 2 + lax.axis_index("c")) * per_w

        def token(i, carry):
            tok = base + i
            pltpu.sync_copy(idx_hbm.at[tok], idx_v)
            pltpu.sync_copy(x_hbm.at[tok], x_v)
            _sc_chunk_copy(tab_hbm, idx_v, rows_v, sem, 0).start()
            for c in range(nchunk):
                if c + 1 < nchunk:
                    _sc_chunk_copy(tab_hbm, idx_v, rows_v, sem, c + 1).start()
                _sc_chunk_copy(tab_hbm, idx_v, rows_v, sem, c).wait()
                for kc in range(nstrip):
                    xr = [x_v[pl.ds(kc * SC_STRIP + kk * SC_LANES, SC_LANES)] for kk in range(nk)]

                    def row(r, cr, c=c, kc=kc, xr=xr):
                        s = rows_v[c % 2, r, pl.ds(kc * SC_STRIP, SC_LANES)] * xr[0]
                        for kk in range(1, nk):
                            s = s + rows_v[c % 2, r, pl.ds(kc * SC_STRIP + kk * SC_LANES, SC_LANES)] * xr[kk]
                        if kc == 0:
                            acc_v[c * SC_ROWS + r, :] = s
                        else:
                            acc_v[c * SC_ROWS + r, :] = acc_v[c * SC_ROWS + r, :] + s
                        return cr

                    lax.fori_loop(0, SC_ROWS, row, 0)
            pltpu.sync_copy(acc_v, out_hbm.at[tok])
            return carry

        lax.fori_loop(0, per_w, token, 0)

    return k(idx, x, table)


def _sc_vsum(idx, coefb, table):
    ns = idx.shape[0]
    per_w = ns // SC_WORKERS
    assert per_w * SC_WORKERS == ns
    nchunk, nstrip, nk = NSEL // SC_ROWS, D_MODEL // SC_STRIP, SC_STRIP // SC_LANES

    @functools.partial(
        pl.kernel, mesh=plsc.VectorSubcoreMesh(core_axis_name="c", subcore_axis_name="s"),
        out_type=jax.ShapeDtypeStruct((ns, D_MODEL), F32),
        scratch_types=[pltpu.VMEM((NSEL,), jnp.int32), pltpu.VMEM((NSEL, SC_LANES), F32),
                       pltpu.VMEM((2, SC_ROWS, D_MODEL), F32), pltpu.VMEM((D_MODEL,), F32),
                       pltpu.SemaphoreType.DMA((2,))],
    )
    def k(idx_hbm, coef_hbm, tab_hbm, out_hbm, idx_v, coef_v, rows_v, y_v, sem):
        base = (lax.axis_index("s") * 2 + lax.axis_index("c")) * per_w

        def token(i, carry):
            tok = base + i
            pltpu.sync_copy(idx_hbm.at[tok], idx_v)
            pltpu.sync_copy(coef_hbm.at[tok], coef_v)
            _sc_chunk_copy(tab_hbm, idx_v, rows_v, sem, 0).start()
            for kk in range(D_MODEL // SC_LANES):
                y_v[pl.ds(kk * SC_LANES, SC_LANES)] = jnp.zeros((SC_LANES,), F32)
            for c in range(nchunk):
                if c + 1 < nchunk:
                    _sc_chunk_copy(tab_hbm, idx_v, rows_v, sem, c + 1).start()
                _sc_chunk_copy(tab_hbm, idx_v, rows_v, sem, c).wait()
                for kc in range(nstrip):
                    def rows8(rg, cr, c=c, kc=kc):
                        acc = [y_v[pl.ds(kc * SC_STRIP + kk * SC_LANES, SC_LANES)] for kk in range(nk)]
                        for rr in range(SC_RGROUP):
                            r = rg * SC_RGROUP + rr
                            cv = coef_v[c * SC_ROWS + r, :]
                            for kk in range(nk):
                                acc[kk] = acc[kk] + cv * rows_v[c % 2, r, pl.ds(kc * SC_STRIP + kk * SC_LANES, SC_LANES)]
                        for kk in range(nk):
                            y_v[pl.ds(kc * SC_STRIP + kk * SC_LANES, SC_LANES)] = acc[kk]
                        return cr

                    lax.fori_loop(0, SC_ROWS // SC_RGROUP, rows8, 0)
            pltpu.sync_copy(y_v, out_hbm.at[tok])
            return carry

        lax.fori_loop(0, per_w, token, 0)

    return k(idx, coefb, table)


def _sc_coef_kernel(a_ref, g_ref, after_ref, o_ref):
    hp = lax.Precision.HIGHEST
    grp = lax.broadcasted_iota(jnp.int32, (NSEL * SC_LANES, NSEL), 0) // SC_LANES
    fold = (grp == lax.broadcasted_iota(jnp.int32, (NSEL * SC_LANES, NSEL), 1)).astype(F32)
    a = jnp.dot(a_ref[...], fold, precision=hp, preferred_element_type=F32)
    coef = g_ref[...] * (0.5 * a * (1.0 + lax.erf(a * (0.5 ** 0.5))))
    o_ref[...] = lax.dot_general(coef, fold, (((1,), (1,)), ((), ())), precision=hp, preferred_element_type=F32)


def _sc_coef(a_part, g, after, *, tt=SC_SPLIT_BLOCK):
    n = a_part.shape[0]
    wide = pl.BlockSpec((tt, NSEL * SC_LANES), lambda i: (i, 0))
    return pl.pallas_call(
        _sc_coef_kernel, grid=(n // tt,),
        in_specs=[wide, pl.BlockSpec((tt, NSEL), lambda i: (i, 0)),
                  pl.BlockSpec((SUBLANES, D_MODEL), lambda i: (0, 0))], out_specs=wide,
        out_shape=jax.ShapeDtypeStruct((n, NSEL * SC_LANES), F32), name="sc_coef",
    )(a_part, g, after)


def _sc_residual_kernel(x1_ref, ff_ref, gate_ref, after_ref, o_ref):
    o_ref[...] = x1_ref[...] + gate_ref[0] * ff_ref[...]


def _sc_residual(x1, ff, gate, after, *, tok_offset, l_seq, tt=SC_SPLIT_BLOCK):
    n = x1.shape[0]
    row = pl.BlockSpec((tt, D_MODEL), lambda i: (i, 0))
    return pl.pallas_call(
        _sc_residual_kernel, grid=(n // tt,),
        in_specs=[row, row, pl.BlockSpec((1, 1, D_MODEL), lambda i: ((tok_offset + i * tt) // l_seq, 0, 0)),
                  pl.BlockSpec((SUBLANES, D_MODEL), lambda i: (0, 0))], out_specs=row,
        out_shape=jax.ShapeDtypeStruct((n, D_MODEL), F32), name="sc_residual",
    )(x1, ff, gate, after)


def _rope_tables(pos):
    half = DIFF_HD // 2
    inv = ROPE_THETA ** (-jnp.arange(half, dtype=F32) / half)
    ang = pos.astype(F32)[:, None] * inv[None, :]
    cos, sin = jnp.cos(ang), jnp.sin(ang)
    return jnp.tile(cos, (1, 4)), jnp.tile(jnp.concatenate([-sin, sin], axis=1), (1, 2))


def _stream(x, mods, mod_tiles, tile, cos, sin, l_seq, h0, wts, attend, lambda_init, sc_share):
    sh1, sc1, gt1, sh2, sc2, gate2 = mods
    q_g, k_g, v_g, r_g, la = _proj_gla(x, sh1, sc1, wts["n1"], wts["w_a"], wts["w_alpha"], wts["b_alpha"],
                                       tt=tile, mod_tiles=mod_tiles)
    q_bf, k_rows, k_bf, v_rows, v_bf = _proj_diff(x, sh1, sc1, wts["n1"], wts["w_b"], wts["q_gain"], wts["k_gain"],
                                                  cos, sin, wts["ones"], tt=tile, mod_tiles=mod_tiles)
    o_a, state = _gla(q_g, k_g, la, v_g, h0, l_seq=l_seq, tb=min(GLA_TILE, l_seq))
    o_b = attend(q_bf, k_bf, v_bf)
    post_tile = min(POST_TILE, tile)
    x1, h2 = _post_mixer(x, o_a, r_g, o_b, (sh1, sc1, gt1, sh2, sc2), wts["n1"], wts["n2"], wts["gla_gain"],
                         wts["diff_gain"], wts["w_gla_out"], wts["w_diff_out"], wts["w_g"], wts["w_out"],
                         tt=post_tile, mod_tiles=mod_tiles * (tile // post_tile), diff_out_scale=1.0 - lambda_init)
    idx_t, g_t = _peer_select(h2, wts["w_pq"], wts["k1"], wts["k2"])
    idx, g = idx_t.T, g_t.T
    n_seq = x.shape[0] // l_seq
    if not sc_share:
        y = _peer_gather(idx, g, h2, x1, gates=gate2, table=wts["table"], tt=PEER_TOK_TILE, l_seq=l_seq)
        return y, k_rows, v_rows, state

    nblk = x.shape[0] // SC_SPLIT_BLOCK
    n_sc = round(nblk * SC_SHARE) * SC_SPLIT_BLOCK
    n_tc = x.shape[0] - n_sc
    n_a = round(nblk * SC_FIRST_SHARE) * SC_SPLIT_BLOCK
    gather = functools.partial(_peer_gather, gates=gate2, table=wts["table"], tt=PEER_TOK_TILE, l_seq=l_seq)
    a_part = _sc_udot(idx[n_tc:], h2[n_tc:], wts["u"])
    y_a = gather(idx[:n_a], g[:n_a], h2[:n_a], x1[:n_a], tok_offset=0)
    coefb = _sc_coef(a_part.reshape(n_sc, NSEL * SC_LANES), g[n_tc:], y_a[:SUBLANES])
    ff = _sc_vsum(idx[n_tc:], coefb.reshape(n_sc, NSEL, SC_LANES), wts["v"])
    y_b = gather(idx[n_a:n_tc], g[n_a:n_tc], h2[n_a:n_tc], x1[n_a:n_tc], tok_offset=n_a)
    y_c = _sc_residual(x1[n_tc:], ff, gate2, y_b[:SUBLANES], tok_offset=n_tc, l_seq=l_seq)
    return jnp.concatenate([y_a, y_b, y_c], axis=0), k_rows, v_rows, state


def kernel(x_prompt, x_sample, c_prompt, c_sample, cache_k, cache_v, state_gla, norm1_gain, norm2_gain, w_mod, b_mod, w_in, w_alpha, b_alpha, gla_gain, w_gla_out, q_gain, k_gain, lam_q1, lam_k1, lam_q2, lam_k2, diff_gain, w_diff_out, w_out, w_pq, sub_keys1, sub_keys2, peer_u, peer_v):
    bp, lp, _ = x_prompt.shape
    bs, ls, _ = x_sample.shape
    past = cache_k.shape[2]
    l = 0
    lambda_init = 0.8 - 0.6 * math.exp(-0.3 * l)
    nb = bp + bs
    c_all = jnp.concatenate([c_prompt, c_sample, jnp.zeros(((-nb) % 8, D_MODEL), F32)], axis=0)
    mod_all = _adaln(c_all, w_mod[l], b_mod[l])
    mods = [mod_all[:nb, i * D_MODEL:(i + 1) * D_MODEL] for i in range(6)]

    wi = w_in[l]
    o_glr = 2 * GLA_QK_W + 2 * GLA_V_W
    o_diff = o_glr + GLA_LOWRANK
    o_gate = o_diff + 2 * DIFF_QK_W + DIFF_V_W
    pad_lr = LANES - GLA_LOWRANK
    lane = jnp.arange(LANES)
    wts = dict(
        n1=norm1_gain[l][None], n2=norm2_gain[l][None],
        w_a=jnp.concatenate([wi[:, :o_diff], jnp.zeros((D_MODEL, pad_lr), F32)], axis=1).astype(BF16),
        w_alpha=jnp.concatenate([w_alpha[l], jnp.zeros((pad_lr, GLA_QK_W), F32)], axis=0).astype(BF16),
        b_alpha=b_alpha[l][None],
        w_b=wi[:, o_diff:o_gate].astype(BF16), w_g=wi[:, o_gate:].astype(BF16),
        q_gain=q_gain[l].reshape(1, LANES), k_gain=k_gain[l].reshape(1, LANES),
        ones=(lane[:, None] // DIFF_HD == lane[None, :] // DIFF_HD).astype(BF16),
        gla_gain=jnp.tile(gla_gain[l], GLA_HEADS)[None], diff_gain=jnp.tile(diff_gain[l], DIFF_HEADS)[None],
        w_gla_out=w_gla_out[l].astype(BF16), w_diff_out=w_diff_out[l].astype(BF16), w_out=w_out[l].astype(BF16),
        w_pq=w_pq[l].astype(BF16), k1=sub_keys1[l].astype(BF16), k2=sub_keys2[l].astype(BF16),
        table=jnp.concatenate([peer_u[l], peer_v[l]], axis=1).reshape(-1, 1, 2 * D_MODEL),
        u=peer_u[l], v=peer_v[l],
    )
    lam = (jnp.exp(jnp.sum(lam_q1[l] * lam_k1[l])) - jnp.exp(jnp.sum(lam_q2[l] * lam_k2[l])) + lambda_init).reshape(1, 1)

    cos_p, sin_p = _rope_tables(jnp.arange(lp))
    mods_p = [m[:bp, None, :] for m in mods]
    attend_p = lambda q, k, v: _attn_prompt(q.reshape(bp, lp, -1), k.reshape(bp, lp, -1), v.reshape(bp, lp, -1),
                                            lam).reshape(bp * lp, -1)
    h0_p = jnp.zeros((bp, GLA_HEADS, GLA_DK, GLA_DV), F32)
    yp, kp, vp, sp = _stream(x_prompt.reshape(bp * lp, D_MODEL), mods_p, lp // PROJ_TILE, PROJ_TILE, cos_p, sin_p,
                             lp, h0_p, wts, attend_p, lambda_init, sc_share=bp >= 4)

    seq_per_tile = SAMPLE_TILE // ls
    cos_s, sin_s = _rope_tables(jnp.tile(past + jnp.arange(ls), seq_per_tile))
    mods_s = [jnp.repeat(m[bp:nb], ls, axis=0).reshape(-1, SAMPLE_TILE, D_MODEL) for m in mods[:5]]
    mods_s.append(mods[5][bp:nb, None, :])
    attend_s = lambda q, k, v: _attn_sample(q, k, v, cache_k[l], cache_v[l], lam, l_seq=ls)
    ys, ks, vs, ss = _stream(x_sample.reshape(bs * ls, D_MODEL), mods_s, 1, SAMPLE_TILE, cos_s, sin_s,
                             ls, state_gla[l], wts, attend_s, lambda_init, sc_share=False)

    return (yp.reshape(bp, lp, D_MODEL), ys.reshape(bs, ls, D_MODEL),
            kp.reshape(1, bp, lp, DIFF_HEADS, 2 * DIFF_HD), vp.reshape(1, bp, lp, DIFF_HEADS, DIFF_VD), sp[None],
            ks.reshape(1, bs, ls, DIFF_HEADS, 2 * DIFF_HD), vs.reshape(1, bs, ls, DIFF_HEADS, DIFF_VD), ss[None])
```

```python
import math
import functools
import jax, jax.numpy as jnp
from jax import lax
from jax.experimental import pallas as pl
from jax.experimental.pallas import tpu as pltpu
from jax.experimental.pallas import tpu_sc as plsc

D_MODEL = 1024
CHUNK = 64
EPS = 1e-6
GLA_HEADS = 4
GLA_DK = 128
GLA_DV = 256
GLA_LOWRANK = 16
GLA_TAU = 16.0
GLA_BLOCK = 16
DIFF_HEADS = 8
DIFF_HD = 64
DIFF_VD = 128
ROPE_THETA = 10000.0
PEER_HEADS = 8
PEER_NKEYS = 128
PEER_DKEY = 256
PEER_TOPK = 16
GLA_QK_W = GLA_HEADS * GLA_DK
GLA_V_W = GLA_HEADS * GLA_DV
DIFF_QK_W = DIFF_HEADS * 2 * DIFF_HD
DIFF_V_W = DIFF_HEADS * DIFF_VD
F32 = jnp.float32
BF16 = jnp.bfloat16
NSEL = PEER_HEADS * PEER_TOPK
LANES = 128
SUBLANES = 8

PROJ_TILE = 512
POST_TILE = 256
GLA_TILE = 256
SAMPLE_TILE = 256
PEER_TOK_TILE = 8
ATTN_NEG = -1e30
ATTN_TILE = 1024
SELECT_TILE = 1024
SC_LANES = 16
SC_WORKERS = 32
SC_ROWS = 32
SC_STRIP = 256
SC_RGROUP = 8
SC_SPLIT_BLOCK = 512
SC_SHARE = 25 / 64
SC_FIRST_SHARE = 25 / 64


def _mod_kernel(c_ref, w_ref, b_ref, o_ref):
    c = c_ref[...]
    s = c * jax.nn.sigmoid(c)
    o_ref[...] = jnp.dot(s.astype(BF16), w_ref[...].astype(BF16), preferred_element_type=F32) + b_ref[...]


def _adaln(c, w_mod, b_mod):
    n = c.shape[0]
    tn = 1536
    return pl.pallas_call(
        _mod_kernel,
        grid=(6 * D_MODEL // tn,),
        in_specs=[pl.BlockSpec((n, D_MODEL), lambda j: (0, 0)),
                  pl.BlockSpec((D_MODEL, tn), lambda j: (0, j)),
                  pl.BlockSpec((1, tn), lambda j: (0, j))],
        out_specs=pl.BlockSpec((n, tn), lambda j: (0, j)),
        out_shape=jax.ShapeDtypeStruct((n, 6 * D_MODEL), F32),
        name="adaln",
    )(c, w_mod, b_mod.reshape(1, -1))


def _modulated_norm(x, gain, scale, shift):
    y = x * lax.rsqrt(jnp.mean(x * x, axis=-1, keepdims=True) + EPS)
    return (y * gain) * (1.0 + scale) + shift


def _proj_gla_kernel(x_ref, sh_ref, sc_ref, gain_ref, w_ref, wa_ref, ba_ref,
                     q_ref, k_ref, v_ref, r_ref, la_ref):
    h = _modulated_norm(x_ref[...], gain_ref[...], sc_ref[0], sh_ref[0]).astype(BF16)
    z = jnp.dot(h, w_ref[...], preferred_element_type=F32)
    q_ref[...] = z[:, 0:GLA_QK_W] * (GLA_DK ** -0.5)
    k_ref[...] = z[:, GLA_QK_W:2 * GLA_QK_W]
    v_ref[...] = z[:, 2 * GLA_QK_W:2 * GLA_QK_W + GLA_V_W]
    r_ref[...] = z[:, 2 * GLA_QK_W + GLA_V_W:2 * GLA_QK_W + 2 * GLA_V_W]
    glr = z[:, 2 * GLA_QK_W + 2 * GLA_V_W:]
    a = jnp.dot(glr.astype(BF16), wa_ref[...], preferred_element_type=F32) + ba_ref[...]
    la_ref[...] = jax.nn.log_sigmoid(a) * (1.0 / GLA_TAU)


def _group_mean_square(x, ones_ref):
    s = x * x
    hi = s.astype(BF16)
    lo = (s - hi.astype(F32)).astype(BF16)
    ss = (jnp.dot(hi, ones_ref[...], preferred_element_type=F32)
          + jnp.dot(lo, ones_ref[...], preferred_element_type=F32))
    return ss * (1.0 / DIFF_HD)


def _norm_rope(x, gain, cos, sin_signed, ones_ref):
    outs = []
    lane = lax.broadcasted_iota(jnp.int32, (x.shape[0], LANES), 1)
    first_half = (lane % DIFF_HD) < (DIFF_HD // 2)
    for b in range(DIFF_QK_W // LANES):
        xb = x[:, b * LANES:(b + 1) * LANES]
        y = xb * lax.rsqrt(_group_mean_square(xb, ones_ref) + EPS) * gain
        rot = jnp.where(first_half, pltpu.roll(y, LANES - DIFF_HD // 2, axis=1), pltpu.roll(y, DIFF_HD // 2, axis=1))
        outs.append(y * cos + rot * sin_signed)
    return jnp.concatenate(outs, axis=1)


def _proj_diff_kernel(x_ref, sh_ref, sc_ref, gain_ref, w_ref, qg_ref, kg_ref, cos_ref, sin_ref, ones_ref,
                      qb_ref, k_ref, kb_ref, v_ref, vb_ref, *, q_scale):
    h = _modulated_norm(x_ref[...], gain_ref[...], sc_ref[0], sh_ref[0]).astype(BF16)
    z = jnp.dot(h, w_ref[...], preferred_element_type=F32)
    cos, sin = cos_ref[...], sin_ref[...]
    q = _norm_rope(z[:, 0:DIFF_QK_W], qg_ref[...], cos, sin, ones_ref)
    qb_ref[...] = (q * q_scale).astype(BF16)
    k = _norm_rope(z[:, DIFF_QK_W:2 * DIFF_QK_W], kg_ref[...], cos, sin, ones_ref)
    k_ref[...] = k
    kb_ref[...] = k.astype(BF16)
    v = z[:, 2 * DIFF_QK_W:2 * DIFF_QK_W + DIFF_V_W]
    v_ref[...] = v
    vb_ref[...] = v.astype(BF16)


def _proj_gla(x, shift, scale, gain, w_a, w_alpha_pad, b_alpha, *, tt, mod_tiles):
    n = x.shape[0]
    assert n % tt == 0
    bmap = lambda i: (i // mod_tiles, 0, 0)
    mod = pl.BlockSpec((1,) + shift.shape[1:], bmap)
    row = lambda w: pl.BlockSpec((tt, w), lambda i: (i, 0))
    const = lambda a: pl.BlockSpec(a.shape, lambda i: (0,) * a.ndim)
    return pl.pallas_call(
        _proj_gla_kernel,
        grid=(n // tt,),
        in_specs=[row(D_MODEL), mod, mod,
                  const(gain), const(w_a), const(w_alpha_pad), const(b_alpha)],
        out_specs=[row(GLA_QK_W), row(GLA_QK_W), row(GLA_V_W), row(GLA_V_W), row(GLA_QK_W)],
        out_shape=[jax.ShapeDtypeStruct((n, GLA_QK_W), F32), jax.ShapeDtypeStruct((n, GLA_QK_W), F32),
                   jax.ShapeDtypeStruct((n, GLA_V_W), F32), jax.ShapeDtypeStruct((n, GLA_V_W), F32),
                   jax.ShapeDtypeStruct((n, GLA_QK_W), F32)],
        compiler_params=pltpu.CompilerParams(dimension_semantics=("arbitrary",), vmem_limit_bytes=48 * 1024 * 1024),
        name="proj_gla",
    )(x, shift, scale, gain, w_a, w_alpha_pad, b_alpha)


def _proj_diff(x, shift, scale, gain, w_b, q_gain128, k_gain128, cos128, sin128, ones128, *, tt, mod_tiles):
    n = x.shape[0]
    assert n % tt == 0 and cos128.shape[0] % tt == 0
    bmap = lambda i: (i // mod_tiles, 0, 0)
    mod = pl.BlockSpec((1,) + shift.shape[1:], bmap)
    pos_tiles = cos128.shape[0] // tt
    row = lambda w: pl.BlockSpec((tt, w), lambda i: (i, 0))
    const = lambda a: pl.BlockSpec(a.shape, lambda i: (0,) * a.ndim)
    pos = pl.BlockSpec((tt, LANES), lambda i: (i % pos_tiles, 0))
    q_scale = math.log2(math.e) * DIFF_HD ** -0.5
    return pl.pallas_call(
        functools.partial(_proj_diff_kernel, q_scale=q_scale),
        grid=(n // tt,),
        in_specs=[row(D_MODEL), mod, mod,
                  const(gain), const(w_b), const(q_gain128), const(k_gain128), pos, pos, const(ones128)],
        out_specs=[row(DIFF_QK_W)] * 5,
        out_shape=[jax.ShapeDtypeStruct((n, DIFF_QK_W), BF16), jax.ShapeDtypeStruct((n, DIFF_QK_W), F32),
                   jax.ShapeDtypeStruct((n, DIFF_QK_W), BF16), jax.ShapeDtypeStruct((n, DIFF_V_W), F32),
                   jax.ShapeDtypeStruct((n, DIFF_V_W), BF16)],
        compiler_params=pltpu.CompilerParams(dimension_semantics=("arbitrary",), vmem_limit_bytes=48 * 1024 * 1024),
        name="proj_diff",
    )(x, shift, scale, gain, w_b, q_gain128, k_gain128, cos128, sin128, ones128)


def _gla_kernel(q_ref, k_ref, la_ref, v_ref, h0_ref, o_ref, hT_ref, st_ref, *, tb):
    i = pl.program_id(2)
    nb = tb // GLA_BLOCK

    @pl.when(i == 0)
    def _init():
        st_ref[...] = h0_ref[0, 0].T

    row = lax.broadcasted_iota(jnp.int32, (tb, tb), 0)
    col = lax.broadcasted_iota(jnp.int32, (tb, tb), 1)
    same = (row // GLA_BLOCK) == (col // GLA_BLOCK)
    causal = same & (col <= row)
    la = la_ref[...]
    hp = lax.Precision.HIGHEST
    b = jnp.dot(causal.astype(F32), la, precision=hp, preferred_element_type=F32)
    b_last = jnp.dot(same.astype(F32), la, precision=hp, preferred_element_type=F32)
    q, k, v = q_ref[...], k_ref[...], v_ref[...]
    qe = (q * jnp.exp(b)).astype(BF16)
    ke = (k * jnp.exp(-b)).astype(BF16)
    kd = (k * jnp.exp(b_last - b)).astype(BF16)
    vb = v.astype(BF16)
    a = lax.dot_general(qe, ke, (((1,), (1,)), ((), ())), preferred_element_type=F32)
    a = jnp.where(causal, a, 0.0).astype(BF16)
    o_intra = jnp.dot(a, vb, preferred_element_type=F32)

    st = st_ref[...]
    outs = []
    for n in range(nb):
        rs = slice(n * GLA_BLOCK, (n + 1) * GLA_BLOCK)
        o_inter = lax.dot_general(qe[rs], st.astype(BF16), (((1,), (1,)), ((), ())), preferred_element_type=F32)
        outs.append(o_inter + o_intra[rs])
        u = lax.dot_general(vb[rs], kd[rs], (((0,), (0,)), ((), ())), preferred_element_type=F32)
        st = jnp.exp(b_last[n * GLA_BLOCK:n * GLA_BLOCK + 1, :]) * st + u
    o_ref[...] = jnp.concatenate(outs, axis=0)
    st_ref[...] = st

    @pl.when(i == pl.num_programs(2) - 1)
    def _fin():
        hT_ref[0, 0] = st.T


def _gla(q, k, la, v, h0, *, l_seq, tb):
    n = q.shape[0]
    bsz = n // l_seq
    nt = l_seq // tb
    assert l_seq % tb == 0 and tb % GLA_BLOCK == 0
    tok = lambda w: pl.BlockSpec((tb, w), lambda b, h, i: (b * nt + i, h))
    st = pl.BlockSpec((1, 1, GLA_DK, GLA_DV), lambda b, h, i: (b, h, 0, 0))
    return pl.pallas_call(
        functools.partial(_gla_kernel, tb=tb),
        grid=(bsz, GLA_HEADS, nt),
        in_specs=[tok(GLA_DK), tok(GLA_DK), tok(GLA_DK), tok(GLA_DV), st],
        out_specs=[tok(GLA_DV), st],
        out_shape=[jax.ShapeDtypeStruct((n, GLA_V_W), F32),
                   jax.ShapeDtypeStruct((bsz, GLA_HEADS, GLA_DK, GLA_DV), F32)],
        scratch_shapes=[pltpu.VMEM((GLA_DV, GLA_DK), F32)],
        compiler_params=pltpu.CompilerParams(dimension_semantics=("arbitrary", "arbitrary", "arbitrary")),
        name="gla",
    )(q, k, la, v, h0)


def _split_maps(q):
    lane = lax.broadcasted_iota(jnp.int32, q.shape, 1)
    zero = jnp.zeros_like(q)
    return jnp.concatenate([jnp.where(lane < DIFF_HD, q, zero), jnp.where(lane >= DIFF_HD, q, zero)], axis=0)


def _attn_prompt_kernel(lam_ref, q_ref, k_ref, v_ref, o_ref, vt_ref, *, tq, tk):
    i = pl.program_id(2)
    n_kt = v_ref.shape[1] // tk

    @pl.when(i == 0)
    def _transpose_v():
        def body(j, c):
            vt_ref[j] = v_ref[0, pl.ds(j * tk, tk), :].astype(F32).T.astype(BF16)
            return c
        lax.fori_loop(0, n_kt, body, 0)

    qp = _split_maps(q_ref[0])

    def step(j, carry, masked, koff=0):
        m, l, acc = carry
        kt = k_ref[0, pl.ds(j * tk, tk), :]
        st = lax.dot_general(kt, qp, (((1,), (1,)), ((), ())), preferred_element_type=F32)
        if masked:
            kc = (lax.broadcasted_iota(jnp.int32, st.shape, 0) + koff) // CHUNK
            qc = (lax.broadcasted_iota(jnp.int32, st.shape, 1) % tq) // CHUNK
            st = jnp.where(qc >= kc, st, ATTN_NEG)
        m_new = jnp.maximum(m, jnp.max(st, axis=0, keepdims=True))
        alpha = jnp.exp2(m - m_new)
        p = jnp.exp2(st - m_new)
        l = alpha * l + jnp.sum(p, axis=0, keepdims=True)
        acc = alpha * acc + jnp.dot(vt_ref[j], p.astype(BF16), preferred_element_type=F32)
        return m_new, l, acc

    carry = (jnp.full((1, 2 * tq), ATTN_NEG, F32), jnp.zeros((1, 2 * tq), F32), jnp.zeros((LANES, 2 * tq), F32))
    r = tq // tk
    carry = lax.fori_loop(0, i * r, functools.partial(step, masked=False), carry)
    for d in range(r):
        carry = step(i * r + d, carry, True, koff=d * tk)
    m, l, acc = carry
    o = acc / l
    ot = o[:, :tq] - lam_ref[0, 0] * o[:, tq:]
    o_ref[0] = ot.T


def _attn_prompt(q, k, v, lam, *, tq=ATTN_TILE, tk=ATTN_TILE):
    b, l, w = q.shape
    h = w // LANES
    assert l % tq == 0 and tq % tk == 0 and tk % CHUNK == 0
    return pl.pallas_call(
        functools.partial(_attn_prompt_kernel, tq=tq, tk=tk),
        grid=(b, h, l // tq),
        in_specs=[
            pl.BlockSpec(memory_space=pltpu.SMEM),
            pl.BlockSpec((1, tq, LANES), lambda bi, hi, i: (bi, i, hi)),
            pl.BlockSpec((1, l, LANES), lambda bi, hi, i: (bi, 0, hi)),
            pl.BlockSpec((1, l, LANES), lambda bi, hi, i: (bi, 0, hi)),
        ],
        out_specs=pl.BlockSpec((1, tq, LANES), lambda bi, hi, i: (bi, i, hi)),
        out_shape=jax.ShapeDtypeStruct((b, l, w), F32),
        scratch_shapes=[pltpu.VMEM((l // tk, LANES, tk), BF16)],
        compiler_params=pltpu.CompilerParams(dimension_semantics=("arbitrary", "arbitrary", "arbitrary"),
                                             vmem_limit_bytes=40 * 1024 * 1024),
        name="attn_prompt",
    )(lam, q, k, v)


def _attn_sample_kernel(lam_ref, q_ref, kn_ref, vn_ref, kc_ref, vc_ref, o_ref, m_ref, l_ref, acc_ref, *, heads):
    j = pl.program_id(1)
    lq = q_ref.shape[0]
    nt = (((1,), (1,)), ((), ()))

    @pl.when(j == 0)
    def _init():
        m_ref[...] = jnp.full(m_ref.shape, ATTN_NEG, F32)
        l_ref[...] = jnp.zeros(l_ref.shape, F32)
        acc_ref[...] = jnp.zeros(acc_ref.shape, F32)

    def update(h, k, v):
        qp = _split_maps(q_ref[:, h * LANES:(h + 1) * LANES])
        s = lax.dot_general(qp, k, nt, preferred_element_type=F32)
        m_old = m_ref[h]
        m_new = jnp.maximum(m_old, jnp.max(s, axis=-1, keepdims=True))
        alpha = jnp.exp2(m_old - m_new)
        p = jnp.exp2(s - m_new)
        l_ref[h] = alpha * l_ref[h] + jnp.sum(p, axis=-1, keepdims=True)
        acc_ref[h] = alpha * acc_ref[h] + jnp.dot(p.astype(BF16), v, preferred_element_type=F32)
        m_ref[h] = m_new

    for h in range(heads):
        update(h, kc_ref[:, h, :].astype(BF16), vc_ref[:, h, :].astype(BF16))

    @pl.when(j == pl.num_programs(1) - 1)
    def _fin():
        for h in range(heads):
            update(h, kn_ref[:, h * LANES:(h + 1) * LANES], vn_ref[:, h * LANES:(h + 1) * LANES])
            o = acc_ref[h] / l_ref[h]
            o_ref[:, h * LANES:(h + 1) * LANES] = o[:lq] - lam_ref[0, 0] * o[lq:]


def _attn_sample(q, k_new, v_new, cache_k, cache_v, lam, *, l_seq, tk=1024):
    n, w = q.shape
    bsz, past, heads, _ = cache_k.shape
    tk = min(tk, past)
    assert n == bsz * l_seq and w == heads * LANES and past % tk == 0
    tok = pl.BlockSpec((l_seq, w), lambda b, j: (b, 0))
    cache = pl.BlockSpec((None, tk, heads, LANES), lambda b, j: (b, j, 0, 0))
    return pl.pallas_call(
        functools.partial(_attn_sample_kernel, heads=heads),
        grid=(bsz, past // tk),
        in_specs=[pl.BlockSpec(memory_space=pltpu.SMEM), tok, tok, tok, cache, cache],
        out_specs=tok,
        out_shape=jax.ShapeDtypeStruct((n, w), F32),
        scratch_shapes=[pltpu.VMEM((heads, 2 * l_seq, 1), F32), pltpu.VMEM((heads, 2 * l_seq, 1), F32),
                        pltpu.VMEM((heads, 2 * l_seq, LANES), F32)],
        compiler_params=pltpu.CompilerParams(dimension_semantics=("arbitrary", "arbitrary"),
                                             vmem_limit_bytes=40 * 1024 * 1024),
        name="attn_sample",
    )(lam, q, k_new, v_new, cache_k, cache_v)


def _head_norm(x, width):
    outs = []
    for g in range(x.shape[1] // width):
        xg = x[:, g * width:(g + 1) * width]
        outs.append(xg * lax.rsqrt(jnp.mean(xg * xg, axis=-1, keepdims=True) + EPS))
    return jnp.concatenate(outs, axis=1)


def _post_kernel(x_ref, oa_ref, gr_ref, ob_ref, sh1_ref, sc1_ref, gt1_ref, sh2_ref, sc2_ref,
                 n1_ref, n2_ref, ga_ref, gb_ref, wa_ref, wb_ref, wg_ref, wo_ref, x1_ref, h2_ref, *, diff_out_scale):
    x = x_ref[...]
    h = _modulated_norm(x, n1_ref[...], sc1_ref[0], sh1_ref[0]).astype(BF16)
    gates = jnp.dot(h, wg_ref[...], preferred_element_type=F32)
    gr = gr_ref[...]
    oa = _head_norm(oa_ref[...], GLA_DV) * ga_ref[...] * (gr * jax.nn.sigmoid(gr))
    ya = jnp.dot(oa.astype(BF16), wa_ref[...], preferred_element_type=F32)
    ob = _head_norm(ob_ref[...], DIFF_VD) * gb_ref[...] * diff_out_scale
    yb = jnp.dot(ob.astype(BF16), wb_ref[...], preferred_element_type=F32)
    merged = jax.nn.sigmoid(gates[:, :D_MODEL]) * ya + jax.nn.sigmoid(gates[:, D_MODEL:]) * yb
    mix = jnp.dot(merged.astype(BF16), wo_ref[...], preferred_element_type=F32)
    x1 = x + gt1_ref[0] * mix
    x1_ref[...] = x1
    h2_ref[...] = _modulated_norm(x1, n2_ref[...], sc2_ref[0], sh2_ref[0])


def _post_mixer(x, oa, gr, ob, mods, n1, n2, ga, gb, wa, wb, wg, wo, *, tt, mod_tiles, diff_out_scale):
    n = x.shape[0]
    assert n % tt == 0
    bmap = lambda i: (i // mod_tiles, 0, 0)
    mod = pl.BlockSpec((1,) + mods[0].shape[1:], bmap)
    row = pl.BlockSpec((tt, D_MODEL), lambda i: (i, 0))
    const = lambda a: pl.BlockSpec(a.shape, lambda i: (0,) * a.ndim)
    return pl.pallas_call(
        functools.partial(_post_kernel, diff_out_scale=diff_out_scale),
        grid=(n // tt,),
        in_specs=[row, row, row, row] + [mod] * 5 + [const(a) for a in (n1, n2, ga, gb, wa, wb, wg, wo)],
        out_specs=[row, row],
        out_shape=[jax.ShapeDtypeStruct((n, D_MODEL), F32), jax.ShapeDtypeStruct((n, D_MODEL), F32)],
        compiler_params=pltpu.CompilerParams(dimension_semantics=("arbitrary",), vmem_limit_bytes=48 * 1024 * 1024),
        name="post_mixer",
    )(x, oa, gr, ob, *mods, n1, n2, ga, gb, wa, wb, wg, wo)


def _topk_rows(s, k):
    r = s.shape[0]
    row = lax.broadcasted_iota(jnp.int32, s.shape, 0)
    vals, idxs = [], []
    for _ in range(k):
        mx = jnp.max(s, axis=0, keepdims=True)
        am = jnp.min(jnp.where(s == mx, row, r), axis=0, keepdims=True)
        vals.append(mx)
        idxs.append(am)
        s = jnp.where(row == am, -jnp.inf, s)
    return jnp.concatenate(vals, axis=0), jnp.concatenate(idxs, axis=0)


def _take_rows(tab, sel):
    out = jnp.zeros(sel.shape, tab.dtype)
    for a in range(tab.shape[0]):
        out = jnp.where(sel == a, tab[a:a + 1, :], out)
    return out


def _staircase():
    return [(a, b) for a in range(PEER_TOPK) for b in range(PEER_TOPK) if (a + 1) * (b + 1) <= PEER_TOPK]


def _peer_select_kernel(h2_ref, w_ref, k1_ref, k2_ref, idx_ref, g_ref):
    q = jnp.dot(h2_ref[...].astype(BF16), w_ref[...], preferred_element_type=F32)
    nt = (((1,), (1,)), ((), ()))
    s1 = lax.dot_general(k1_ref[...], q[:, :PEER_NKEYS].astype(BF16), nt, preferred_element_type=F32)
    s2 = lax.dot_general(k2_ref[...], q[:, PEER_NKEYS:].astype(BF16), nt, preferred_element_type=F32)
    v1, i1 = _topk_rows(s1, PEER_TOPK)
    v2, i2 = _topk_rows(s2, PEER_TOPK)
    pairs = _staircase()
    neg = jnp.full(((-len(pairs)) % SUBLANES, v1.shape[1]), -jnp.inf, F32)
    cand = jnp.concatenate([v1[a:a + 1, :] + v2[b:b + 1, :] for a, b in pairs] + [neg], axis=0)
    sc, cr = _topk_rows(cand, PEER_TOPK)
    ra = jnp.zeros(cr.shape, jnp.int32)
    rb = jnp.zeros(cr.shape, jnp.int32)
    for r, (a, b) in enumerate(pairs):
        hit = cr == r
        ra = jnp.where(hit, a, ra)
        rb = jnp.where(hit, b, rb)
    e1 = _take_rows(i1, ra)
    e2 = _take_rows(i2, rb)
    idx_ref[...] = e1 * PEER_NKEYS + e2
    p = jnp.exp(sc - sc[0:1, :])
    g_ref[...] = p / jnp.sum(p, axis=0, keepdims=True)


def _peer_select(h2, w_pq_bf, k1_bf, k2_bf, *, tt=SELECT_TILE):
    n, d = h2.shape
    assert n % tt == 0
    return pl.pallas_call(
        _peer_select_kernel,
        grid=(n // tt, PEER_HEADS),
        in_specs=[
            pl.BlockSpec((tt, d), lambda i, h: (i, 0)),
            pl.BlockSpec((d, PEER_DKEY), lambda i, h: (0, h)),
            pl.BlockSpec((PEER_NKEYS, PEER_DKEY // 2), lambda i, h: (0, 0)),
            pl.BlockSpec((PEER_NKEYS, PEER_DKEY // 2), lambda i, h: (0, 0)),
        ],
        out_specs=[pl.BlockSpec((PEER_TOPK, tt), lambda i, h: (h, i)),
                   pl.BlockSpec((PEER_TOPK, tt), lambda i, h: (h, i))],
        out_shape=[jax.ShapeDtypeStruct((NSEL, n), jnp.int32),
                   jax.ShapeDtypeStruct((NSEL, n), F32)],
        compiler_params=pltpu.CompilerParams(dimension_semantics=("arbitrary", "arbitrary")),
        name="peer_select",
    )(h2, w_pq_bf, k1_bf, k2_bf)


def _peer_gather_kernel(idx_ref, h2_ref, g_ref, x1_ref, gate_ref, tab_ref, out_ref, buf, sem, *, tt):
    s = pl.program_id(0)
    n = pl.num_programs(0) - 1

    @pl.when(s < n)
    def _issue():
        slot = s % 2

        def issue_tok(t, carry):
            for j in range(NSEL):
                e = idx_ref[t, j]
                pltpu.make_async_copy(tab_ref.at[e], buf.at[slot, t, pl.ds(j, 1), :], sem.at[slot, t]).start()
            return carry

        lax.fori_loop(0, tt, issue_tok, 0)

    @pl.when(s >= 1)
    def _compute():
        slot = (s + 1) % 2
        eye = (lax.broadcasted_iota(jnp.int32, (NSEL, NSEL), 0)
               == lax.broadcasted_iota(jnp.int32, (NSEL, NSEL), 1))
        gate = gate_ref[0]

        def tok(t, carry):
            for j in range(NSEL):
                pltpu.make_async_copy(tab_ref.at[0], buf.at[slot, t, pl.ds(j, 1), :], sem.at[slot, t]).wait()
            x = h2_ref[pl.ds(t, 1), :]
            p = buf[slot, t, :, 0:LANES] * x[:, 0:LANES]
            for c in range(1, D_MODEL // LANES):
                p = p + buf[slot, t, :, c * LANES:(c + 1) * LANES] * x[:, c * LANES:(c + 1) * LANES]
            a = jnp.sum(p, axis=-1, keepdims=True)
            act = 0.5 * a * (1.0 + lax.erf(a * (0.5 ** 0.5)))
            grow = g_ref[pl.ds(t, 1), :]
            gcol = jnp.sum(jnp.where(eye, grow, 0.0), axis=-1, keepdims=True)
            coef = gcol * act
            y = jnp.sum(coef * buf[slot, t, :, D_MODEL:2 * D_MODEL], axis=0, keepdims=True)
            out_ref[pl.ds(t, 1), :] = x1_ref[pl.ds(t, 1), :] + gate * y
            return carry

        lax.fori_loop(0, tt, tok, 0)


def _peer_gather(idx, g, h2, x1, *, gates, table, tt, l_seq, tok_offset=0):
    n_tok = idx.shape[0]
    n_tiles = n_tok // tt
    assert n_tiles * tt == n_tok and l_seq % tt == 0

    def cur(s):
        return jnp.maximum(s - 1, 0)

    return pl.pallas_call(
        functools.partial(_peer_gather_kernel, tt=tt),
        grid=(n_tiles + 1,),
        in_specs=[
            pl.BlockSpec((tt, NSEL), lambda s: (jnp.minimum(s, n_tiles - 1), 0), memory_space=pltpu.SMEM),
            pl.BlockSpec((tt, D_MODEL), lambda s: (cur(s), 0)),
            pl.BlockSpec((tt, NSEL), lambda s: (cur(s), 0)),
            pl.BlockSpec((tt, D_MODEL), lambda s: (cur(s), 0)),
            pl.BlockSpec((1, 1, D_MODEL), lambda s: ((tok_offset + cur(s) * tt) // l_seq, 0, 0)),
            pl.BlockSpec(memory_space=pl.ANY),
        ],
        out_specs=pl.BlockSpec((tt, D_MODEL), lambda s: (cur(s), 0)),
        out_shape=jax.ShapeDtypeStruct((n_tok, D_MODEL), F32),
        scratch_shapes=[pltpu.VMEM((2, tt, NSEL, 2 * D_MODEL), F32), pltpu.SemaphoreType.DMA((2, tt))],
        compiler_params=pltpu.CompilerParams(dimension_semantics=("arbitrary",),
                                             vmem_limit_bytes=48 * 1024 * 1024),
        name="peer_gather",
    )(idx, h2, g, x1, gates, table)


def _sc_chunk_copy(tab_hbm, idx_v, rows_v, sem, c):
    return pltpu.make_async_copy(tab_hbm.at[idx_v.at[pl.ds(c * SC_ROWS, SC_ROWS)]], rows_v.at[c % 2], sem.at[c % 2])


def _sc_udot(idx, x, table):
    ns = idx.shape[0]
    per_w = ns // SC_WORKERS
    assert per_w * SC_WORKERS == ns
    nchunk, nstrip, nk = NSEL // SC_ROWS, D_MODEL // SC_STRIP, SC_STRIP // SC_LANES

    @functools.partial(
        pl.kernel, mesh=plsc.VectorSubcoreMesh(core_axis_name="c", subcore_axis_name="s"),
        out_type=jax.ShapeDtypeStruct((ns, NSEL * SC_LANES), F32),
        scratch_types=[pltpu.VMEM((NSEL,), jnp.int32), pltpu.VMEM((D_MODEL,), F32),
                       pltpu.VMEM((2, SC_ROWS, D_MODEL), F32), pltpu.VMEM((NSEL * SC_LANES,), F32),
                       pltpu.SemaphoreType.DMA((2,))],
    )
    def k(idx_hbm, x_hbm, tab_hbm, out_hbm, idx_v, x_v, rows_v, acc_v, sem):
        base = (lax.axis_index("s") * 2 + lax.axis_index("c")) * per_w

        def token(i, carry):
            tok = base + i
            pltpu.sync_copy(idx_hbm.at[tok], idx_v)
            pltpu.sync_copy(x_hbm.at[tok], x_v)
            _sc_chunk_copy(tab_hbm, idx_v, rows_v, sem, 0).start()
            for c in range(nchunk):
                if c + 1 < nchunk:
                    _sc_chunk_copy(tab_hbm, idx_v, rows_v, sem, c + 1).start()
                _sc_chunk_copy(tab_hbm, idx_v, rows_v, sem, c).wait()
                for kc in range(nstrip):
                    xr = [x_v[pl.ds(kc * SC_STRIP + kk * SC_LANES, SC_LANES)] for kk in range(nk)]

                    def row(r, cr, c=c, kc=kc, xr=xr):
                        s = rows_v[c % 2, r, pl.ds(kc * SC_STRIP, SC_LANES)] * xr[0]
                        for kk in range(1, nk):
                            s = s + rows_v[c % 2, r, pl.ds(kc * SC_STRIP + kk * SC_LANES, SC_LANES)] * xr[kk]
                        dst = pl.ds((c * SC_ROWS + r) * SC_LANES, SC_LANES)
                        if kc == 0:
                            acc_v[dst] = s
                        else:
                            acc_v[dst] = acc_v[dst] + s
                        return cr

                    lax.fori_loop(0, SC_ROWS, row, 0)
            pltpu.sync_copy(acc_v, out_hbm.at[tok])
            return carry

        lax.fori_loop(0, per_w, token, 0)

    return k(idx, x, table)


def _sc_vsum(idx, coefb, table):
    ns = idx.shape[0]
    per_w = ns // SC_WORKERS
    assert per_w * SC_WORKERS == ns
    nchunk, nstrip, nk = NSEL // SC_ROWS, D_MODEL // SC_STRIP, SC_STRIP // SC_LANES

    @functools.partial(
        pl.kernel, mesh=plsc.VectorSubcoreMesh(core_axis_name="c", subcore_axis_name="s"),
        out_type=jax.ShapeDtypeStruct((ns, D_MODEL), F32),
        scratch_types=[pltpu.VMEM((NSEL,), jnp.int32), pltpu.VMEM((NSEL * SC_LANES,), F32),
                       pltpu.VMEM((2, SC_ROWS, D_MODEL), F32), pltpu.VMEM((D_MODEL,), F32),
                       pltpu.SemaphoreType.DMA((2,))],
    )
    def k(idx_hbm, coef_hbm, tab_hbm, out_hbm, idx_v, coef_v, rows_v, y_v, sem):
        base = (lax.axis_index("s") * 2 + lax.axis_index("c")) * per_w

        def token(i, carry):
            tok = base + i
            pltpu.sync_copy(idx_hbm.at[tok], idx_v)
            pltpu.sync_copy(coef_hbm.at[tok], coef_v)
            _sc_chunk_copy(tab_hbm, idx_v, rows_v, sem, 0).start()
            for kk in range(D_MODEL // SC_LANES):
                y_v[pl.ds(kk * SC_LANES, SC_LANES)] = jnp.zeros((SC_LANES,), F32)
            for c in range(nchunk):
                if c + 1 < nchunk:
                    _sc_chunk_copy(tab_hbm, idx_v, rows_v, sem, c + 1).start()
                _sc_chunk_copy(tab_hbm, idx_v, rows_v, sem, c).wait()
                for kc in range(nstrip):
                    def rows8(rg, cr, c=c, kc=kc):
                        acc = [y_v[pl.ds(kc * SC_STRIP + kk * SC_LANES, SC_LANES)] for kk in range(nk)]
                        for rr in range(SC_RGROUP):
                            r = rg * SC_RGROUP + rr
                            cv = coef_v[pl.ds((c * SC_ROWS + r) * SC_LANES, SC_LANES)]
                            for kk in range(nk):
                                acc[kk] = acc[kk] + cv * rows_v[c % 2, r, pl.ds(kc * SC_STRIP + kk * SC_LANES, SC_LANES)]
                        for kk in range(nk):
                            y_v[pl.ds(kc * SC_STRIP + kk * SC_LANES, SC_LANES)] = acc[kk]
                        return cr

                    lax.fori_loop(0, SC_ROWS // SC_RGROUP, rows8, 0)
            pltpu.sync_copy(y_v, out_hbm.at[tok])
            return carry

        lax.fori_loop(0, per_w, token, 0)

    return k(idx, coefb, table)


def _sc_coef_kernel(a_ref, g_ref, after_ref, o_ref):
    hp = lax.Precision.HIGHEST
    grp = lax.broadcasted_iota(jnp.int32, (NSEL * SC_LANES, NSEL), 0) // SC_LANES
    fold = (grp == lax.broadcasted_iota(jnp.int32, (NSEL * SC_LANES, NSEL), 1)).astype(F32)
    a = jnp.dot(a_ref[...], fold, precision=hp, preferred_element_type=F32)
    coef = g_ref[...] * (0.5 * a * (1.0 + lax.erf(a * (0.5 ** 0.5))))
    o_ref[...] = lax.dot_general(coef, fold, (((1,), (1,)), ((), ())), precision=hp, preferred_element_type=F32)


def _sc_coef(a_part, g, after, *, tt=SC_SPLIT_BLOCK):
    n = a_part.shape[0]
    wide = pl.BlockSpec((tt, NSEL * SC_LANES), lambda i: (i, 0))
    return pl.pallas_call(
        _sc_coef_kernel, grid=(n // tt,),
        in_specs=[wide, pl.BlockSpec((tt, NSEL), lambda i: (i, 0)),
                  pl.BlockSpec((SUBLANES, D_MODEL), lambda i: (0, 0))], out_specs=wide,
        out_shape=jax.ShapeDtypeStruct((n, NSEL * SC_LANES), F32), name="sc_coef",
    )(a_part, g, after)


def _sc_residual_kernel(x1_ref, ff_ref, gate_ref, after_ref, o_ref):
    o_ref[...] = x1_ref[...] + gate_ref[0] * ff_ref[...]


def _sc_residual(x1, ff, gate, after, *, tok_offset, l_seq, tt=SC_SPLIT_BLOCK):
    n = x1.shape[0]
    row = pl.BlockSpec((tt, D_MODEL), lambda i: (i, 0))
    return pl.pallas_call(
        _sc_residual_kernel, grid=(n // tt,),
        in_specs=[row, row, pl.BlockSpec((1, 1, D_MODEL), lambda i: ((tok_offset + i * tt) // l_seq, 0, 0)),
                  pl.BlockSpec((SUBLANES, D_MODEL), lambda i: (0, 0))], out_specs=row,
        out_shape=jax.ShapeDtypeStruct((n, D_MODEL), F32), name="sc_residual",
    )(x1, ff, gate, after)


def _rope_tables(pos):
    half = DIFF_HD // 2
    inv = ROPE_THETA ** (-jnp.arange(half, dtype=F32) / half)
    ang = pos.astype(F32)[:, None] * inv[None, :]
    cos, sin = jnp.cos(ang), jnp.sin(ang)
    return jnp.tile(cos, (1, 4)), jnp.tile(jnp.concatenate([-sin, sin], axis=1), (1, 2))


def _stream(x, mods, mod_tiles, tile, cos, sin, l_seq, h0, wts, attend, lambda_init, sc_share):
    sh1, sc1, gt1, sh2, sc2, gate2 = mods
    q_g, k_g, v_g, r_g, la = _proj_gla(x, sh1, sc1, wts["n1"], wts["w_a"], wts["w_alpha"], wts["b_alpha"],
                                       tt=tile, mod_tiles=mod_tiles)
    q_bf, k_rows, k_bf, v_rows, v_bf = _proj_diff(x, sh1, sc1, wts["n1"], wts["w_b"], wts["q_gain"], wts["k_gain"],
                                                  cos, sin, wts["ones"], tt=tile, mod_tiles=mod_tiles)
    o_a, state = _gla(q_g, k_g, la, v_g, h0, l_seq=l_seq, tb=min(GLA_TILE, l_seq))
    o_b = attend(q_bf, k_bf, v_bf)
    post_tile = min(POST_TILE, tile)
    x1, h2 = _post_mixer(x, o_a, r_g, o_b, (sh1, sc1, gt1, sh2, sc2), wts["n1"], wts["n2"], wts["gla_gain"],
                         wts["diff_gain"], wts["w_gla_out"], wts["w_diff_out"], wts["w_g"], wts["w_out"],
                         tt=post_tile, mod_tiles=mod_tiles * (tile // post_tile), diff_out_scale=1.0 - lambda_init)
    idx_t, g_t = _peer_select(h2, wts["w_pq"], wts["k1"], wts["k2"])
    idx, g = idx_t.T, g_t.T
    n_seq = x.shape[0] // l_seq
    if not sc_share:
        y = _peer_gather(idx, g, h2, x1, gates=gate2, table=wts["table"], tt=PEER_TOK_TILE, l_seq=l_seq)
        return y, k_rows, v_rows, state

    nblk = x.shape[0] // SC_SPLIT_BLOCK
    n_sc = round(nblk * SC_SHARE) * SC_SPLIT_BLOCK
    n_tc = x.shape[0] - n_sc
    n_a = round(nblk * SC_FIRST_SHARE) * SC_SPLIT_BLOCK
    gather = functools.partial(_peer_gather, gates=gate2, table=wts["table"], tt=PEER_TOK_TILE, l_seq=l_seq)
    a_part = _sc_udot(idx[n_tc:], h2[n_tc:], wts["u"])
    y_a = gather(idx[:n_a], g[:n_a], h2[:n_a], x1[:n_a], tok_offset=0)
    coefb = _sc_coef(a_part, g[n_tc:], y_a[:SUBLANES])
    ff = _sc_vsum(idx[n_tc:], coefb, wts["v"])
    y_b = gather(idx[n_a:n_tc], g[n_a:n_tc], h2[n_a:n_tc], x1[n_a:n_tc], tok_offset=n_a)
    y_c = _sc_residual(x1[n_tc:], ff, gate2, y_b[:SUBLANES], tok_offset=n_tc, l_seq=l_seq)
    return jnp.concatenate([y_a, y_b, y_c], axis=0), k_rows, v_rows, state


def kernel(x_prompt, x_sample, c_prompt, c_sample, cache_k, cache_v, state_gla, norm1_gain, norm2_gain, w_mod, b_mod, w_in, w_alpha, b_alpha, gla_gain, w_gla_out, q_gain, k_gain, lam_q1, lam_k1, lam_q2, lam_k2, diff_gain, w_diff_out, w_out, w_pq, sub_keys1, sub_keys2, peer_u, peer_v):
    bp, lp, _ = x_prompt.shape
    bs, ls, _ = x_sample.shape
    past = cache_k.shape[2]
    l = 0
    lambda_init = 0.8 - 0.6 * math.exp(-0.3 * l)
    nb = bp + bs
    c_all = jnp.concatenate([c_prompt, c_sample, jnp.zeros(((-nb) % 8, D_MODEL), F32)], axis=0)
    mod_all = _adaln(c_all, w_mod[l], b_mod[l])
    mods = [mod_all[:nb, i * D_MODEL:(i + 1) * D_MODEL] for i in range(6)]

    wi = w_in[l]
    o_glr = 2 * GLA_QK_W + 2 * GLA_V_W
    o_diff = o_glr + GLA_LOWRANK
    o_gate = o_diff + 2 * DIFF_QK_W + DIFF_V_W
    pad_lr = LANES - GLA_LOWRANK
    lane = jnp.arange(LANES)
    wts = dict(
        n1=norm1_gain[l][None], n2=norm2_gain[l][None],
        w_a=jnp.concatenate([wi[:, :o_diff], jnp.zeros((D_MODEL, pad_lr), F32)], axis=1).astype(BF16),
        w_alpha=jnp.concatenate([w_alpha[l], jnp.zeros((pad_lr, GLA_QK_W), F32)], axis=0).astype(BF16),
        b_alpha=b_alpha[l][None],
        w_b=wi[:, o_diff:o_gate].astype(BF16), w_g=wi[:, o_gate:].astype(BF16),
        q_gain=q_gain[l].reshape(1, LANES), k_gain=k_gain[l].reshape(1, LANES),
        ones=(lane[:, None] // DIFF_HD == lane[None, :] // DIFF_HD).astype(BF16),
        gla_gain=jnp.tile(gla_gain[l], GLA_HEADS)[None], diff_gain=jnp.tile(diff_gain[l], DIFF_HEADS)[None],
        w_gla_out=w_gla_out[l].astype(BF16), w_diff_out=w_diff_out[l].astype(BF16), w_out=w_out[l].astype(BF16),
        w_pq=w_pq[l].astype(BF16), k1=sub_keys1[l].astype(BF16), k2=sub_keys2[l].astype(BF16),
        table=jnp.concatenate([peer_u[l], peer_v[l]], axis=1).reshape(-1, 1, 2 * D_MODEL),
        u=peer_u[l], v=peer_v[l],
    )
    lam = (jnp.exp(jnp.sum(lam_q1[l] * lam_k1[l])) - jnp.exp(jnp.sum(lam_q2[l] * lam_k2[l])) + lambda_init).reshape(1, 1)

    cos_p, sin_p = _rope_tables(jnp.arange(lp))
    mods_p = [m[:bp, None, :] for m in mods]
    attend_p = lambda q, k, v: _attn_prompt(q.reshape(bp, lp, -1), k.reshape(bp, lp, -1), v.reshape(bp, lp, -1),
                                            lam).reshape(bp * lp, -1)
    h0_p = jnp.zeros((bp, GLA_HEADS, GLA_DK, GLA_DV), F32)
    yp, kp, vp, sp = _stream(x_prompt.reshape(bp * lp, D_MODEL), mods_p, lp // PROJ_TILE, PROJ_TILE, cos_p, sin_p,
                             lp, h0_p, wts, attend_p, lambda_init, sc_share=bp >= 4)

    seq_per_tile = SAMPLE_TILE // ls
    cos_s, sin_s = _rope_tables(jnp.tile(past + jnp.arange(ls), seq_per_tile))
    mods_s = [jnp.repeat(m[bp:nb], ls, axis=0).reshape(-1, SAMPLE_TILE, D_MODEL) for m in mods[:5]]
    mods_s.append(mods[5][bp:nb, None, :])
    attend_s = lambda q, k, v: _attn_sample(q, k, v, cache_k[l], cache_v[l], lam, l_seq=ls)
    ys, ks, vs, ss = _stream(x_sample.reshape(bs * ls, D_MODEL), mods_s, 1, SAMPLE_TILE, cos_s, sin_s,
                             ls, state_gla[l], wts, attend_s, lambda_init, sc_share=False)

    return (yp.reshape(bp, lp, D_MODEL), ys.reshape(bs, ls, D_MODEL),
            kp.reshape(1, bp, lp, DIFF_HEADS, 2 * DIFF_HD), vp.reshape(1, bp, lp, DIFF_HEADS, DIFF_VD), sp[None],
            ks.reshape(1, bs, ls, DIFF_HEADS, 2 * DIFF_HD), vs.reshape(1, bs, ls, DIFF_HEADS, DIFF_VD), ss[None])
```

```python
import math
import functools
import jax, jax.numpy as jnp
from jax import lax
from jax.experimental import pallas as pl
from jax.experimental.pallas import tpu as pltpu
from jax.experimental.pallas import tpu_sc as plsc

D_MODEL = 1024
CHUNK = 64
EPS = 1e-6
GLA_HEADS = 4
GLA_DK = 128
GLA_DV = 256
GLA_LOWRANK = 16
GLA_TAU = 16.0
GLA_BLOCK = 16
DIFF_HEADS = 8
DIFF_HD = 64
DIFF_VD = 128
ROPE_THETA = 10000.0
PEER_HEADS = 8
PEER_NKEYS = 128
PEER_DKEY = 256
PEER_TOPK = 16
GLA_QK_W = GLA_HEADS * GLA_DK
GLA_V_W = GLA_HEADS * GLA_DV
DIFF_QK_W = DIFF_HEADS * 2 * DIFF_HD
DIFF_V_W = DIFF_HEADS * DIFF_VD
F32 = jnp.float32
BF16 = jnp.bfloat16
NSEL = PEER_HEADS * PEER_TOPK
LANES = 128
SUBLANES = 8

PROJ_TILE = 512
POST_TILE = 256
GLA_TILE = 256
SAMPLE_TILE = 256
PEER_TOK_TILE = 8
ATTN_NEG = -1e30
ATTN_TILE = 1024
SELECT_TILE = 1024
SC_LANES = 16
SC_WORKERS = 32
SC_ROWS = 32
SC_STRIP = 256
SC_RGROUP = 8
SC_UGROUP = 4
SC_SPLIT_BLOCK = 512
SC_SHARE = 29 / 64
SC_FIRST_SHARE = 23 / 64


def _mod_kernel(c_ref, w_ref, b_ref, o_ref):
    c = c_ref[...]
    s = c * jax.nn.sigmoid(c)
    o_ref[...] = jnp.dot(s.astype(BF16), w_ref[...].astype(BF16), preferred_element_type=F32) + b_ref[...]


def _adaln(c, w_mod, b_mod):
    n = c.shape[0]
    tn = 1536
    return pl.pallas_call(
        _mod_kernel,
        grid=(6 * D_MODEL // tn,),
        in_specs=[pl.BlockSpec((n, D_MODEL), lambda j: (0, 0)),
                  pl.BlockSpec((D_MODEL, tn), lambda j: (0, j)),
                  pl.BlockSpec((1, tn), lambda j: (0, j))],
        out_specs=pl.BlockSpec((n, tn), lambda j: (0, j)),
        out_shape=jax.ShapeDtypeStruct((n, 6 * D_MODEL), F32),
        name="adaln",
    )(c, w_mod, b_mod.reshape(1, -1))


def _modulated_norm(x, gain, scale, shift):
    y = x * lax.rsqrt(jnp.mean(x * x, axis=-1, keepdims=True) + EPS)
    return (y * gain) * (1.0 + scale) + shift


def _proj_gla_kernel(x_ref, sh_ref, sc_ref, gain_ref, w_ref, wa_ref, ba_ref,
                     q_ref, k_ref, v_ref, r_ref, la_ref):
    h = _modulated_norm(x_ref[...], gain_ref[...], sc_ref[0], sh_ref[0]).astype(BF16)
    z = jnp.dot(h, w_ref[...], preferred_element_type=F32)
    q_ref[...] = z[:, 0:GLA_QK_W] * (GLA_DK ** -0.5)
    k_ref[...] = z[:, GLA_QK_W:2 * GLA_QK_W]
    v_ref[...] = z[:, 2 * GLA_QK_W:2 * GLA_QK_W + GLA_V_W]
    r_ref[...] = z[:, 2 * GLA_QK_W + GLA_V_W:2 * GLA_QK_W + 2 * GLA_V_W]
    glr = z[:, 2 * GLA_QK_W + 2 * GLA_V_W:]
    a = jnp.dot(glr.astype(BF16), wa_ref[...], preferred_element_type=F32) + ba_ref[...]
    la_ref[...] = jax.nn.log_sigmoid(a) * (1.0 / GLA_TAU)


def _group_mean_square(x, ones_ref):
    s = x * x
    hi = s.astype(BF16)
    lo = (s - hi.astype(F32)).astype(BF16)
    ss = (jnp.dot(hi, ones_ref[...], preferred_element_type=F32)
          + jnp.dot(lo, ones_ref[...], preferred_element_type=F32))
    return ss * (1.0 / DIFF_HD)


def _norm_rope(x, gain, cos, sin_signed, ones_ref):
    outs = []
    lane = lax.broadcasted_iota(jnp.int32, (x.shape[0], LANES), 1)
    first_half = (lane % DIFF_HD) < (DIFF_HD // 2)
    for b in range(DIFF_QK_W // LANES):
        xb = x[:, b * LANES:(b + 1) * LANES]
        y = xb * lax.rsqrt(_group_mean_square(xb, ones_ref) + EPS) * gain
        rot = jnp.where(first_half, pltpu.roll(y, LANES - DIFF_HD // 2, axis=1), pltpu.roll(y, DIFF_HD // 2, axis=1))
        outs.append(y * cos + rot * sin_signed)
    return jnp.concatenate(outs, axis=1)


def _proj_diff_kernel(x_ref, sh_ref, sc_ref, gain_ref, w_ref, qg_ref, kg_ref, cos_ref, sin_ref, ones_ref,
                      qb_ref, k_ref, kb_ref, v_ref, vb_ref, *, q_scale):
    h = _modulated_norm(x_ref[...], gain_ref[...], sc_ref[0], sh_ref[0]).astype(BF16)
    z = jnp.dot(h, w_ref[...], preferred_element_type=F32)
    cos, sin = cos_ref[...], sin_ref[...]
    q = _norm_rope(z[:, 0:DIFF_QK_W], qg_ref[...], cos, sin, ones_ref)
    qb_ref[...] = (q * q_scale).astype(BF16)
    k = _norm_rope(z[:, DIFF_QK_W:2 * DIFF_QK_W], kg_ref[...], cos, sin, ones_ref)
    k_ref[...] = k
    kb_ref[...] = k.astype(BF16)
    v = z[:, 2 * DIFF_QK_W:2 * DIFF_QK_W + DIFF_V_W]
    v_ref[...] = v
    vb_ref[...] = v.astype(BF16)


def _proj_gla(x, shift, scale, gain, w_a, w_alpha_pad, b_alpha, *, tt, mod_tiles):
    n = x.shape[0]
    assert n % tt == 0
    bmap = lambda i: (i // mod_tiles, 0, 0)
    mod = pl.BlockSpec((1,) + shift.shape[1:], bmap)
    row = lambda w: pl.BlockSpec((tt, w), lambda i: (i, 0))
    const = lambda a: pl.BlockSpec(a.shape, lambda i: (0,) * a.ndim)
    return pl.pallas_call(
        _proj_gla_kernel,
        grid=(n // tt,),
        in_specs=[row(D_MODEL), mod, mod,
                  const(gain), const(w_a), const(w_alpha_pad), const(b_alpha)],
        out_specs=[row(GLA_QK_W), row(GLA_QK_W), row(GLA_V_W), row(GLA_V_W), row(GLA_QK_W)],
        out_shape=[jax.ShapeDtypeStruct((n, GLA_QK_W), F32), jax.ShapeDtypeStruct((n, GLA_QK_W), F32),
                   jax.ShapeDtypeStruct((n, GLA_V_W), F32), jax.ShapeDtypeStruct((n, GLA_V_W), F32),
                   jax.ShapeDtypeStruct((n, GLA_QK_W), F32)],
        compiler_params=pltpu.CompilerParams(dimension_semantics=("arbitrary",), vmem_limit_bytes=48 * 1024 * 1024),
        name="proj_gla",
    )(x, shift, scale, gain, w_a, w_alpha_pad, b_alpha)


def _proj_diff(x, shift, scale, gain, w_b, q_gain128, k_gain128, cos128, sin128, ones128, *, tt, mod_tiles):
    n = x.shape[0]
    assert n % tt == 0 and cos128.shape[0] % tt == 0
    bmap = lambda i: (i // mod_tiles, 0, 0)
    mod = pl.BlockSpec((1,) + shift.shape[1:], bmap)
    pos_tiles = cos128.shape[0] // tt
    row = lambda w: pl.BlockSpec((tt, w), lambda i: (i, 0))
    const = lambda a: pl.BlockSpec(a.shape, lambda i: (0,) * a.ndim)
    pos = pl.BlockSpec((tt, LANES), lambda i: (i % pos_tiles, 0))
    q_scale = math.log2(math.e) * DIFF_HD ** -0.5
    return pl.pallas_call(
        functools.partial(_proj_diff_kernel, q_scale=q_scale),
        grid=(n // tt,),
        in_specs=[row(D_MODEL), mod, mod,
                  const(gain), const(w_b), const(q_gain128), const(k_gain128), pos, pos, const(ones128)],
        out_specs=[row(DIFF_QK_W)] * 5,
        out_shape=[jax.ShapeDtypeStruct((n, DIFF_QK_W), BF16), jax.ShapeDtypeStruct((n, DIFF_QK_W), F32),
                   jax.ShapeDtypeStruct((n, DIFF_QK_W), BF16), jax.ShapeDtypeStruct((n, DIFF_V_W), F32),
                   jax.ShapeDtypeStruct((n, DIFF_V_W), BF16)],
        compiler_params=pltpu.CompilerParams(dimension_semantics=("arbitrary",), vmem_limit_bytes=48 * 1024 * 1024),
        name="proj_diff",
    )(x, shift, scale, gain, w_b, q_gain128, k_gain128, cos128, sin128, ones128)


def _gla_kernel(q_ref, k_ref, la_ref, v_ref, h0_ref, o_ref, hT_ref, st_ref, *, tb):
    i = pl.program_id(2)
    nb = tb // GLA_BLOCK

    @pl.when(i == 0)
    def _init():
        st_ref[...] = h0_ref[0, 0].T

    row = lax.broadcasted_iota(jnp.int32, (tb, tb), 0)
    col = lax.broadcasted_iota(jnp.int32, (tb, tb), 1)
    same = (row // GLA_BLOCK) == (col // GLA_BLOCK)
    causal = same & (col <= row)
    la = la_ref[...]
    hp = lax.Precision.HIGHEST
    b = jnp.dot(causal.astype(F32), la, precision=hp, preferred_element_type=F32)
    b_last = jnp.dot(same.astype(F32), la, precision=hp, preferred_element_type=F32)
    q, k, v = q_ref[...], k_ref[...], v_ref[...]
    qe = (q * jnp.exp(b)).astype(BF16)
    ke = (k * jnp.exp(-b)).astype(BF16)
    kd = (k * jnp.exp(b_last - b)).astype(BF16)
    vb = v.astype(BF16)
    a = lax.dot_general(qe, ke, (((1,), (1,)), ((), ())), preferred_element_type=F32)
    a = jnp.where(causal, a, 0.0).astype(BF16)
    o_intra = jnp.dot(a, vb, preferred_element_type=F32)

    st = st_ref[...]
    outs = []
    for n in range(nb):
        rs = slice(n * GLA_BLOCK, (n + 1) * GLA_BLOCK)
        o_inter = lax.dot_general(qe[rs], st.astype(BF16), (((1,), (1,)), ((), ())), preferred_element_type=F32)
        outs.append(o_inter + o_intra[rs])
        u = lax.dot_general(vb[rs], kd[rs], (((0,), (0,)), ((), ())), preferred_element_type=F32)
        st = jnp.exp(b_last[n * GLA_BLOCK:n * GLA_BLOCK + 1, :]) * st + u
    o_ref[...] = jnp.concatenate(outs, axis=0)
    st_ref[...] = st

    @pl.when(i == pl.num_programs(2) - 1)
    def _fin():
        hT_ref[0, 0] = st.T


def _gla(q, k, la, v, h0, *, l_seq, tb):
    n = q.shape[0]
    bsz = n // l_seq
    nt = l_seq // tb
    assert l_seq % tb == 0 and tb % GLA_BLOCK == 0
    tok = lambda w: pl.BlockSpec((tb, w), lambda b, h, i: (b * nt + i, h))
    st = pl.BlockSpec((1, 1, GLA_DK, GLA_DV), lambda b, h, i: (b, h, 0, 0))
    return pl.pallas_call(
        functools.partial(_gla_kernel, tb=tb),
        grid=(bsz, GLA_HEADS, nt),
        in_specs=[tok(GLA_DK), tok(GLA_DK), tok(GLA_DK), tok(GLA_DV), st],
        out_specs=[tok(GLA_DV), st],
        out_shape=[jax.ShapeDtypeStruct((n, GLA_V_W), F32),
                   jax.ShapeDtypeStruct((bsz, GLA_HEADS, GLA_DK, GLA_DV), F32)],
        scratch_shapes=[pltpu.VMEM((GLA_DV, GLA_DK), F32)],
        compiler_params=pltpu.CompilerParams(dimension_semantics=("arbitrary", "arbitrary", "arbitrary")),
        name="gla",
    )(q, k, la, v, h0)


def _split_maps(q):
    lane = lax.broadcasted_iota(jnp.int32, q.shape, 1)
    zero = jnp.zeros_like(q)
    return jnp.concatenate([jnp.where(lane < DIFF_HD, q, zero), jnp.where(lane >= DIFF_HD, q, zero)], axis=0)


def _attn_prompt_kernel(lam_ref, q_ref, k_ref, v_ref, o_ref, vt_ref, *, tq, tk):
    i = pl.program_id(2)
    n_kt = v_ref.shape[1] // tk

    @pl.when(i == 0)
    def _transpose_v():
        def body(j, c):
            vt_ref[j] = v_ref[0, pl.ds(j * tk, tk), :].astype(F32).T.astype(BF16)
            return c
        lax.fori_loop(0, n_kt, body, 0)

    qp = _split_maps(q_ref[0])

    def step(j, carry, masked, koff=0):
        m, l, acc = carry
        kt = k_ref[0, pl.ds(j * tk, tk), :]
        st = lax.dot_general(kt, qp, (((1,), (1,)), ((), ())), preferred_element_type=F32)
        if masked:
            kc = (lax.broadcasted_iota(jnp.int32, st.shape, 0) + koff) // CHUNK
            qc = (lax.broadcasted_iota(jnp.int32, st.shape, 1) % tq) // CHUNK
            st = jnp.where(qc >= kc, st, ATTN_NEG)
        m_new = jnp.maximum(m, jnp.max(st, axis=0, keepdims=True))
        alpha = jnp.exp2(m - m_new)
        p = jnp.exp2(st - m_new)
        l = alpha * l + jnp.sum(p, axis=0, keepdims=True)
        acc = alpha * acc + jnp.dot(vt_ref[j], p.astype(BF16), preferred_element_type=F32)
        return m_new, l, acc

    carry = (jnp.full((1, 2 * tq), ATTN_NEG, F32), jnp.zeros((1, 2 * tq), F32), jnp.zeros((LANES, 2 * tq), F32))
    r = tq // tk
    carry = lax.fori_loop(0, i * r, functools.partial(step, masked=False), carry)
    for d in range(r):
        carry = step(i * r + d, carry, True, koff=d * tk)
    m, l, acc = carry
    o = acc / l
    ot = o[:, :tq] - lam_ref[0, 0] * o[:, tq:]
    o_ref[0] = ot.T


def _attn_prompt(q, k, v, lam, *, tq=ATTN_TILE, tk=ATTN_TILE):
    b, l, w = q.shape
    h = w // LANES
    assert l % tq == 0 and tq % tk == 0 and tk % CHUNK == 0
    return pl.pallas_call(
        functools.partial(_attn_prompt_kernel, tq=tq, tk=tk),
        grid=(b, h, l // tq),
        in_specs=[
            pl.BlockSpec(memory_space=pltpu.SMEM),
            pl.BlockSpec((1, tq, LANES), lambda bi, hi, i: (bi, i, hi)),
            pl.BlockSpec((1, l, LANES), lambda bi, hi, i: (bi, 0, hi)),
            pl.BlockSpec((1, l, LANES), lambda bi, hi, i: (bi, 0, hi)),
        ],
        out_specs=pl.BlockSpec((1, tq, LANES), lambda bi, hi, i: (bi, i, hi)),
        out_shape=jax.ShapeDtypeStruct((b, l, w), F32),
        scratch_shapes=[pltpu.VMEM((l // tk, LANES, tk), BF16)],
        compiler_params=pltpu.CompilerParams(dimension_semantics=("arbitrary", "arbitrary", "arbitrary"),
                                             vmem_limit_bytes=40 * 1024 * 1024),
        name="attn_prompt",
    )(lam, q, k, v)


def _attn_sample_kernel(lam_ref, q_ref, kn_ref, vn_ref, kc_ref, vc_ref, o_ref, m_ref, l_ref, acc_ref, *, heads):
    j = pl.program_id(1)
    lq = q_ref.shape[0]
    nt = (((1,), (1,)), ((), ()))

    @pl.when(j == 0)
    def _init():
        m_ref[...] = jnp.full(m_ref.shape, ATTN_NEG, F32)
        l_ref[...] = jnp.zeros(l_ref.shape, F32)
        acc_ref[...] = jnp.zeros(acc_ref.shape, F32)

    def update(h, k, v):
        qp = _split_maps(q_ref[:, h * LANES:(h + 1) * LANES])
        s = lax.dot_general(qp, k, nt, preferred_element_type=F32)
        m_old = m_ref[h]
        m_new = jnp.maximum(m_old, jnp.max(s, axis=-1, keepdims=True))
        alpha = jnp.exp2(m_old - m_new)
        p = jnp.exp2(s - m_new)
        l_ref[h] = alpha * l_ref[h] + jnp.sum(p, axis=-1, keepdims=True)
        acc_ref[h] = alpha * acc_ref[h] + jnp.dot(p.astype(BF16), v, preferred_element_type=F32)
        m_ref[h] = m_new

    for h in range(heads):
        update(h, kc_ref[:, h, :].astype(BF16), vc_ref[:, h, :].astype(BF16))

    @pl.when(j == pl.num_programs(1) - 1)
    def _fin():
        for h in range(heads):
            update(h, kn_ref[:, h * LANES:(h + 1) * LANES], vn_ref[:, h * LANES:(h + 1) * LANES])
            o = acc_ref[h] / l_ref[h]
            o_ref[:, h * LANES:(h + 1) * LANES] = o[:lq] - lam_ref[0, 0] * o[lq:]


def _attn_sample(q, k_new, v_new, cache_k, cache_v, lam, *, l_seq, tk=1024):
    n, w = q.shape
    bsz, past, heads, _ = cache_k.shape
    tk = min(tk, past)
    assert n == bsz * l_seq and w == heads * LANES and past % tk == 0
    tok = pl.BlockSpec((l_seq, w), lambda b, j: (b, 0))
    cache = pl.BlockSpec((None, tk, heads, LANES), lambda b, j: (b, j, 0, 0))
    return pl.pallas_call(
        functools.partial(_attn_sample_kernel, heads=heads),
        grid=(bsz, past // tk),
        in_specs=[pl.BlockSpec(memory_space=pltpu.SMEM), tok, tok, tok, cache, cache],
        out_specs=tok,
        out_shape=jax.ShapeDtypeStruct((n, w), F32),
        scratch_shapes=[pltpu.VMEM((heads, 2 * l_seq, 1), F32), pltpu.VMEM((heads, 2 * l_seq, 1), F32),
                        pltpu.VMEM((heads, 2 * l_seq, LANES), F32)],
        compiler_params=pltpu.CompilerParams(dimension_semantics=("arbitrary", "arbitrary"),
                                             vmem_limit_bytes=40 * 1024 * 1024),
        name="attn_sample",
    )(lam, q, k_new, v_new, cache_k, cache_v)


def _head_norm(x, width):
    outs = []
    for g in range(x.shape[1] // width):
        xg = x[:, g * width:(g + 1) * width]
        outs.append(xg * lax.rsqrt(jnp.mean(xg * xg, axis=-1, keepdims=True) + EPS))
    return jnp.concatenate(outs, axis=1)


def _post_kernel(x_ref, oa_ref, gr_ref, ob_ref, sh1_ref, sc1_ref, gt1_ref, sh2_ref, sc2_ref,
                 n1_ref, n2_ref, ga_ref, gb_ref, wa_ref, wb_ref, wg_ref, wo_ref, x1_ref, h2_ref, *, diff_out_scale):
    x = x_ref[...]
    h = _modulated_norm(x, n1_ref[...], sc1_ref[0], sh1_ref[0]).astype(BF16)
    gates = jnp.dot(h, wg_ref[...], preferred_element_type=F32)
    gr = gr_ref[...]
    oa = _head_norm(oa_ref[...], GLA_DV) * ga_ref[...] * (gr * jax.nn.sigmoid(gr))
    ya = jnp.dot(oa.astype(BF16), wa_ref[...], preferred_element_type=F32)
    ob = _head_norm(ob_ref[...], DIFF_VD) * gb_ref[...] * diff_out_scale
    yb = jnp.dot(ob.astype(BF16), wb_ref[...], preferred_element_type=F32)
    merged = jax.nn.sigmoid(gates[:, :D_MODEL]) * ya + jax.nn.sigmoid(gates[:, D_MODEL:]) * yb
    mix = jnp.dot(merged.astype(BF16), wo_ref[...], preferred_element_type=F32)
    x1 = x + gt1_ref[0] * mix
    x1_ref[...] = x1
    h2_ref[...] = _modulated_norm(x1, n2_ref[...], sc2_ref[0], sh2_ref[0])


def _post_mixer(x, oa, gr, ob, mods, n1, n2, ga, gb, wa, wb, wg, wo, *, tt, mod_tiles, diff_out_scale):
    n = x.shape[0]
    assert n % tt == 0
    bmap = lambda i: (i // mod_tiles, 0, 0)
    mod = pl.BlockSpec((1,) + mods[0].shape[1:], bmap)
    row = pl.BlockSpec((tt, D_MODEL), lambda i: (i, 0))
    const = lambda a: pl.BlockSpec(a.shape, lambda i: (0,) * a.ndim)
    return pl.pallas_call(
        functools.partial(_post_kernel, diff_out_scale=diff_out_scale),
        grid=(n // tt,),
        in_specs=[row, row, row, row] + [mod] * 5 + [const(a) for a in (n1, n2, ga, gb, wa, wb, wg, wo)],
        out_specs=[row, row],
        out_shape=[jax.ShapeDtypeStruct((n, D_MODEL), F32), jax.ShapeDtypeStruct((n, D_MODEL), F32)],
        compiler_params=pltpu.CompilerParams(dimension_semantics=("arbitrary",), vmem_limit_bytes=48 * 1024 * 1024),
        name="post_mixer",
    )(x, oa, gr, ob, *mods, n1, n2, ga, gb, wa, wb, wg, wo)


def _topk_rows(s, k):
    r = s.shape[0]
    row = lax.broadcasted_iota(jnp.int32, s.shape, 0)
    vals, idxs = [], []
    for _ in range(k):
        mx = jnp.max(s, axis=0, keepdims=True)
        am = jnp.min(jnp.where(s == mx, row, r), axis=0, keepdims=True)
        vals.append(mx)
        idxs.append(am)
        s = jnp.where(row == am, -jnp.inf, s)
    return jnp.concatenate(vals, axis=0), jnp.concatenate(idxs, axis=0)


def _take_rows(tab, sel):
    out = jnp.zeros(sel.shape, tab.dtype)
    for a in range(tab.shape[0]):
        out = jnp.where(sel == a, tab[a:a + 1, :], out)
    return out


def _staircase():
    return [(a, b) for a in range(PEER_TOPK) for b in range(PEER_TOPK) if (a + 1) * (b + 1) <= PEER_TOPK]


def _peer_select_kernel(h2_ref, w_ref, k1_ref, k2_ref, idx_ref, g_ref):
    q = jnp.dot(h2_ref[...].astype(BF16), w_ref[...], preferred_element_type=F32)
    nt = (((1,), (1,)), ((), ()))
    s1 = lax.dot_general(k1_ref[...], q[:, :PEER_NKEYS].astype(BF16), nt, preferred_element_type=F32)
    s2 = lax.dot_general(k2_ref[...], q[:, PEER_NKEYS:].astype(BF16), nt, preferred_element_type=F32)
    v1, i1 = _topk_rows(s1, PEER_TOPK)
    v2, i2 = _topk_rows(s2, PEER_TOPK)
    pairs = _staircase()
    neg = jnp.full(((-len(pairs)) % SUBLANES, v1.shape[1]), -jnp.inf, F32)
    cand = jnp.concatenate([v1[a:a + 1, :] + v2[b:b + 1, :] for a, b in pairs] + [neg], axis=0)
    sc, cr = _topk_rows(cand, PEER_TOPK)
    ra = jnp.zeros(cr.shape, jnp.int32)
    rb = jnp.zeros(cr.shape, jnp.int32)
    for r, (a, b) in enumerate(pairs):
        hit = cr == r
        ra = jnp.where(hit, a, ra)
        rb = jnp.where(hit, b, rb)
    e1 = _take_rows(i1, ra)
    e2 = _take_rows(i2, rb)
    idx_ref[...] = e1 * PEER_NKEYS + e2
    p = jnp.exp(sc - sc[0:1, :])
    g_ref[...] = p / jnp.sum(p, axis=0, keepdims=True)


def _peer_select(h2, w_pq_bf, k1_bf, k2_bf, *, tt=SELECT_TILE):
    n, d = h2.shape
    assert n % tt == 0
    return pl.pallas_call(
        _peer_select_kernel,
        grid=(n // tt, PEER_HEADS),
        in_specs=[
            pl.BlockSpec((tt, d), lambda i, h: (i, 0)),
            pl.BlockSpec((d, PEER_DKEY), lambda i, h: (0, h)),
            pl.BlockSpec((PEER_NKEYS, PEER_DKEY // 2), lambda i, h: (0, 0)),
            pl.BlockSpec((PEER_NKEYS, PEER_DKEY // 2), lambda i, h: (0, 0)),
        ],
        out_specs=[pl.BlockSpec((PEER_TOPK, tt), lambda i, h: (h, i)),
                   pl.BlockSpec((PEER_TOPK, tt), lambda i, h: (h, i))],
        out_shape=[jax.ShapeDtypeStruct((NSEL, n), jnp.int32),
                   jax.ShapeDtypeStruct((NSEL, n), F32)],
        compiler_params=pltpu.CompilerParams(dimension_semantics=("arbitrary", "arbitrary")),
        name="peer_select",
    )(h2, w_pq_bf, k1_bf, k2_bf)


def _peer_gather_kernel(idx_ref, h2_ref, g_ref, x1_ref, gate_ref, tab_ref, out_ref, buf, sem, *, tt):
    s = pl.program_id(0)
    n = pl.num_programs(0) - 1

    @pl.when(s < n)
    def _issue():
        slot = s % 2

        def issue_tok(t, carry):
            for j in range(NSEL):
                e = idx_ref[t, j]
                pltpu.make_async_copy(tab_ref.at[e], buf.at[slot, t, pl.ds(j, 1), :], sem.at[slot, t]).start()
            return carry

        lax.fori_loop(0, tt, issue_tok, 0)

    @pl.when(s >= 1)
    def _compute():
        slot = (s + 1) % 2
        eye = (lax.broadcasted_iota(jnp.int32, (NSEL, NSEL), 0)
               == lax.broadcasted_iota(jnp.int32, (NSEL, NSEL), 1))
        gate = gate_ref[0]

        def tok(t, carry):
            for j in range(NSEL):
                pltpu.make_async_copy(tab_ref.at[0], buf.at[slot, t, pl.ds(j, 1), :], sem.at[slot, t]).wait()
            x = h2_ref[pl.ds(t, 1), :]
            p = buf[slot, t, :, 0:LANES] * x[:, 0:LANES]
            for c in range(1, D_MODEL // LANES):
                p = p + buf[slot, t, :, c * LANES:(c + 1) * LANES] * x[:, c * LANES:(c + 1) * LANES]
            a = jnp.sum(p, axis=-1, keepdims=True)
            act = 0.5 * a * (1.0 + lax.erf(a * (0.5 ** 0.5)))
            grow = g_ref[pl.ds(t, 1), :]
            gcol = jnp.sum(jnp.where(eye, grow, 0.0), axis=-1, keepdims=True)
            coef = gcol * act
            y = jnp.sum(coef * buf[slot, t, :, D_MODEL:2 * D_MODEL], axis=0, keepdims=True)
            out_ref[pl.ds(t, 1), :] = x1_ref[pl.ds(t, 1), :] + gate * y
            return carry

        lax.fori_loop(0, tt, tok, 0)


def _peer_gather(idx, g, h2, x1, *, gates, table, tt, l_seq, tok_offset=0):
    n_tok = idx.shape[0]
    n_tiles = n_tok // tt
    assert n_tiles * tt == n_tok and l_seq % tt == 0

    def cur(s):
        return jnp.maximum(s - 1, 0)

    return pl.pallas_call(
        functools.partial(_peer_gather_kernel, tt=tt),
        grid=(n_tiles + 1,),
        in_specs=[
            pl.BlockSpec((tt, NSEL), lambda s: (jnp.minimum(s, n_tiles - 1), 0), memory_space=pltpu.SMEM),
            pl.BlockSpec((tt, D_MODEL), lambda s: (cur(s), 0)),
            pl.BlockSpec((tt, NSEL), lambda s: (cur(s), 0)),
            pl.BlockSpec((tt, D_MODEL), lambda s: (cur(s), 0)),
            pl.BlockSpec((1, 1, D_MODEL), lambda s: ((tok_offset + cur(s) * tt) // l_seq, 0, 0)),
            pl.BlockSpec(memory_space=pl.ANY),
        ],
        out_specs=pl.BlockSpec((tt, D_MODEL), lambda s: (cur(s), 0)),
        out_shape=jax.ShapeDtypeStruct((n_tok, D_MODEL), F32),
        scratch_shapes=[pltpu.VMEM((2, tt, NSEL, 2 * D_MODEL), F32), pltpu.SemaphoreType.DMA((2, tt))],
        compiler_params=pltpu.CompilerParams(dimension_semantics=("arbitrary",),
                                             vmem_limit_bytes=48 * 1024 * 1024),
        name="peer_gather",
    )(idx, h2, g, x1, gates, table)


def _sc_chunk_copy(tab_hbm, idx_v, rows_v, sem, c):
    return pltpu.make_async_copy(tab_hbm.at[idx_v.at[pl.ds(c * SC_ROWS, SC_ROWS)]], rows_v.at[c % 2], sem.at[c % 2])


def _sc_udot(idx, x, table):
    ns = idx.shape[0]
    per_w = ns // SC_WORKERS
    assert per_w * SC_WORKERS == ns
    nchunk, nstrip, nk = NSEL // SC_ROWS, D_MODEL // SC_STRIP, SC_STRIP // SC_LANES

    @functools.partial(
        pl.kernel, mesh=plsc.VectorSubcoreMesh(core_axis_name="c", subcore_axis_name="s"),
        out_type=jax.ShapeDtypeStruct((ns, NSEL * SC_LANES), F32),
        scratch_types=[pltpu.VMEM((NSEL,), jnp.int32), pltpu.VMEM((D_MODEL,), F32),
                       pltpu.VMEM((2, SC_ROWS, D_MODEL), F32), pltpu.VMEM((NSEL * SC_LANES,), F32),
                       pltpu.SemaphoreType.DMA((2,))],
    )
    def k(idx_hbm, x_hbm, tab_hbm, out_hbm, idx_v, x_v, rows_v, acc_v, sem):
        base = (lax.axis_index("s") * 2 + lax.axis_index("c")) * per_w

        def token(i, carry):
            tok = base + i
            pltpu.sync_copy(idx_hbm.at[tok], idx_v)
            pltpu.sync_copy(x_hbm.at[tok], x_v)
            _sc_chunk_copy(tab_hbm, idx_v, rows_v, sem, 0).start()
            for c in range(nchunk):
                if c + 1 < nchunk:
                    _sc_chunk_copy(tab_hbm, idx_v, rows_v, sem, c + 1).start()
                _sc_chunk_copy(tab_hbm, idx_v, rows_v, sem, c).wait()
                for kc in range(nstrip):
                    xr = [x_v[pl.ds(kc * SC_STRIP + kk * SC_LANES, SC_LANES)] for kk in range(nk)]

                    def rows8(rg, cr, c=c, kc=kc, xr=xr):
                        for rr in range(SC_UGROUP):
                            r = rg * SC_UGROUP + rr
                            s = rows_v[c % 2, r, pl.ds(kc * SC_STRIP, SC_LANES)] * xr[0]
                            for kk in range(1, nk):
                                s = s + rows_v[c % 2, r, pl.ds(kc * SC_STRIP + kk * SC_LANES, SC_LANES)] * xr[kk]
                            dst = pl.ds((c * SC_ROWS + r) * SC_LANES, SC_LANES)
                            if kc == 0:
                                acc_v[dst] = s
                            else:
                                acc_v[dst] = acc_v[dst] + s
                        return cr

                    lax.fori_loop(0, SC_ROWS // SC_UGROUP, rows8, 0)
            pltpu.sync_copy(acc_v, out_hbm.at[tok])
            return carry

        lax.fori_loop(0, per_w, token, 0)

    return k(idx, x, table)


def _sc_vsum(idx, coefb, table):
    ns = idx.shape[0]
    per_w = ns // SC_WORKERS
    assert per_w * SC_WORKERS == ns
    nchunk, nstrip, nk = NSEL // SC_ROWS, D_MODEL // SC_STRIP, SC_STRIP // SC_LANES

    @functools.partial(
        pl.kernel, mesh=plsc.VectorSubcoreMesh(core_axis_name="c", subcore_axis_name="s"),
        out_type=jax.ShapeDtypeStruct((ns, D_MODEL), F32),
        scratch_types=[pltpu.VMEM((NSEL,), jnp.int32), pltpu.VMEM((NSEL * SC_LANES,), F32),
                       pltpu.VMEM((2, SC_ROWS, D_MODEL), F32), pltpu.VMEM((D_MODEL,), F32),
                       pltpu.SemaphoreType.DMA((2,))],
    )
    def k(idx_hbm, coef_hbm, tab_hbm, out_hbm, idx_v, coef_v, rows_v, y_v, sem):
        base = (lax.axis_index("s") * 2 + lax.axis_index("c")) * per_w

        def token(i, carry):
            tok = base + i
            pltpu.sync_copy(idx_hbm.at[tok], idx_v)
            pltpu.sync_copy(coef_hbm.at[tok], coef_v)
            _sc_chunk_copy(tab_hbm, idx_v, rows_v, sem, 0).start()
            for kk in range(D_MODEL // SC_LANES):
                y_v[pl.ds(kk * SC_LANES, SC_LANES)] = jnp.zeros((SC_LANES,), F32)
            for c in range(nchunk):
                if c + 1 < nchunk:
                    _sc_chunk_copy(tab_hbm, idx_v, rows_v, sem, c + 1).start()
                _sc_chunk_copy(tab_hbm, idx_v, rows_v, sem, c).wait()
                for kc in range(nstrip):
                    def rows8(rg, cr, c=c, kc=kc):
                        acc = [y_v[pl.ds(kc * SC_STRIP + kk * SC_LANES, SC_LANES)] for kk in range(nk)]
                        for rr in range(SC_RGROUP):
                            r = rg * SC_RGROUP + rr
                            cv = coef_v[pl.ds((c * SC_ROWS + r) * SC_LANES, SC_LANES)]
                            for kk in range(nk):
                                acc[kk] = acc[kk] + cv * rows_v[c % 2, r, pl.ds(kc * SC_STRIP + kk * SC_LANES, SC_LANES)]
                        for kk in range(nk):
                            y_v[pl.ds(kc * SC_STRIP + kk * SC_LANES, SC_LANES)] = acc[kk]
                        return cr

                    lax.fori_loop(0, SC_ROWS // SC_RGROUP, rows8, 0)
            pltpu.sync_copy(y_v, out_hbm.at[tok])
            return carry

        lax.fori_loop(0, per_w, token, 0)

    return k(idx, coefb, table)


def _sc_coef_kernel(a_ref, g_ref, after_ref, o_ref):
    hp = lax.Precision.HIGHEST
    grp = lax.broadcasted_iota(jnp.int32, (NSEL * SC_LANES, NSEL), 0) // SC_LANES
    fold = (grp == lax.broadcasted_iota(jnp.int32, (NSEL * SC_LANES, NSEL), 1)).astype(F32)
    a = jnp.dot(a_ref[...], fold, precision=hp, preferred_element_type=F32)
    coef = g_ref[...] * (0.5 * a * (1.0 + lax.erf(a * (0.5 ** 0.5))))
    o_ref[...] = lax.dot_general(coef, fold, (((1,), (1,)), ((), ())), precision=hp, preferred_element_type=F32)


def _sc_coef(a_part, g, after, *, tt=SC_SPLIT_BLOCK):
    n = a_part.shape[0]
    wide = pl.BlockSpec((tt, NSEL * SC_LANES), lambda i: (i, 0))
    return pl.pallas_call(
        _sc_coef_kernel, grid=(n // tt,),
        in_specs=[wide, pl.BlockSpec((tt, NSEL), lambda i: (i, 0)),
                  pl.BlockSpec((SUBLANES, D_MODEL), lambda i: (0, 0))], out_specs=wide,
        out_shape=jax.ShapeDtypeStruct((n, NSEL * SC_LANES), F32), name="sc_coef",
    )(a_part, g, after)


def _sc_residual_kernel(x1_ref, ff_ref, gate_ref, after_ref, o_ref):
    o_ref[...] = x1_ref[...] + gate_ref[0] * ff_ref[...]


def _sc_residual(x1, ff, gate, after, *, tok_offset, l_seq, tt=SC_SPLIT_BLOCK):
    n = x1.shape[0]
    row = pl.BlockSpec((tt, D_MODEL), lambda i: (i, 0))
    return pl.pallas_call(
        _sc_residual_kernel, grid=(n // tt,),
        in_specs=[row, row, pl.BlockSpec((1, 1, D_MODEL), lambda i: ((tok_offset + i * tt) // l_seq, 0, 0)),
                  pl.BlockSpec((SUBLANES, D_MODEL), lambda i: (0, 0))], out_specs=row,
        out_shape=jax.ShapeDtypeStruct((n, D_MODEL), F32), name="sc_residual",
    )(x1, ff, gate, after)


def _rope_tables(pos):
    half = DIFF_HD // 2
    inv = ROPE_THETA ** (-jnp.arange(half, dtype=F32) / half)
    ang = pos.astype(F32)[:, None] * inv[None, :]
    cos, sin = jnp.cos(ang), jnp.sin(ang)
    return jnp.tile(cos, (1, 4)), jnp.tile(jnp.concatenate([-sin, sin], axis=1), (1, 2))


def _stream(x, mods, mod_tiles, tile, cos, sin, l_seq, h0, wts, attend, lambda_init, sc_share):
    sh1, sc1, gt1, sh2, sc2, gate2 = mods
    q_g, k_g, v_g, r_g, la = _proj_gla(x, sh1, sc1, wts["n1"], wts["w_a"], wts["w_alpha"], wts["b_alpha"],
                                       tt=tile, mod_tiles=mod_tiles)
    q_bf, k_rows, k_bf, v_rows, v_bf = _proj_diff(x, sh1, sc1, wts["n1"], wts["w_b"], wts["q_gain"], wts["k_gain"],
                                                  cos, sin, wts["ones"], tt=tile, mod_tiles=mod_tiles)
    o_a, state = _gla(q_g, k_g, la, v_g, h0, l_seq=l_seq, tb=min(GLA_TILE, l_seq))
    o_b = attend(q_bf, k_bf, v_bf)
    post_tile = min(POST_TILE, tile)
    x1, h2 = _post_mixer(x, o_a, r_g, o_b, (sh1, sc1, gt1, sh2, sc2), wts["n1"], wts["n2"], wts["gla_gain"],
                         wts["diff_gain"], wts["w_gla_out"], wts["w_diff_out"], wts["w_g"], wts["w_out"],
                         tt=post_tile, mod_tiles=mod_tiles * (tile // post_tile), diff_out_scale=1.0 - lambda_init)
    idx_t, g_t = _peer_select(h2, wts["w_pq"], wts["k1"], wts["k2"])
    idx, g = idx_t.T, g_t.T
    n_seq = x.shape[0] // l_seq
    if not sc_share:
        y = _peer_gather(idx, g, h2, x1, gates=gate2, table=wts["table"], tt=PEER_TOK_TILE, l_seq=l_seq)
        return y, k_rows, v_rows, state

    nblk = x.shape[0] // SC_SPLIT_BLOCK
    n_sc = round(nblk * SC_SHARE) * SC_SPLIT_BLOCK
    n_tc = x.shape[0] - n_sc
    n_a = round(nblk * SC_FIRST_SHARE) * SC_SPLIT_BLOCK
    gather = functools.partial(_peer_gather, gates=gate2, table=wts["table"], tt=PEER_TOK_TILE, l_seq=l_seq)
    a_part = _sc_udot(idx[n_tc:], h2[n_tc:], wts["u"])
    y_a = gather(idx[:n_a], g[:n_a], h2[:n_a], x1[:n_a], tok_offset=0)
    coefb = _sc_coef(a_part, g[n_tc:], y_a[:SUBLANES])
    ff = _sc_vsum(idx[n_tc:], coefb, wts["v"])
    y_b = gather(idx[n_a:n_tc], g[n_a:n_tc], h2[n_a:n_tc], x1[n_a:n_tc], tok_offset=n_a)
    y_c = _sc_residual(x1[n_tc:], ff, gate2, y_b[:SUBLANES], tok_offset=n_tc, l_seq=l_seq)
    return jnp.concatenate([y_a, y_b, y_c], axis=0), k_rows, v_rows, state


def kernel(x_prompt, x_sample, c_prompt, c_sample, cache_k, cache_v, state_gla, norm1_gain, norm2_gain, w_mod, b_mod, w_in, w_alpha, b_alpha, gla_gain, w_gla_out, q_gain, k_gain, lam_q1, lam_k1, lam_q2, lam_k2, diff_gain, w_diff_out, w_out, w_pq, sub_keys1, sub_keys2, peer_u, peer_v):
    bp, lp, _ = x_prompt.shape
    bs, ls, _ = x_sample.shape
    past = cache_k.shape[2]
    l = 0
    lambda_init = 0.8 - 0.6 * math.exp(-0.3 * l)
    nb = bp + bs
    c_all = jnp.concatenate([c_prompt, c_sample, jnp.zeros(((-nb) % 8, D_MODEL), F32)], axis=0)
    mod_all = _adaln(c_all, w_mod[l], b_mod[l])
    mods = [mod_all[:nb, i * D_MODEL:(i + 1) * D_MODEL] for i in range(6)]

    wi = w_in[l]
    o_glr = 2 * GLA_QK_W + 2 * GLA_V_W
    o_diff = o_glr + GLA_LOWRANK
    o_gate = o_diff + 2 * DIFF_QK_W + DIFF_V_W
    pad_lr = LANES - GLA_LOWRANK
    lane = jnp.arange(LANES)
    wts = dict(
        n1=norm1_gain[l][None], n2=norm2_gain[l][None],
        w_a=jnp.concatenate([wi[:, :o_diff], jnp.zeros((D_MODEL, pad_lr), F32)], axis=1).astype(BF16),
        w_alpha=jnp.concatenate([w_alpha[l], jnp.zeros((pad_lr, GLA_QK_W), F32)], axis=0).astype(BF16),
        b_alpha=b_alpha[l][None],
        w_b=wi[:, o_diff:o_gate].astype(BF16), w_g=wi[:, o_gate:].astype(BF16),
        q_gain=q_gain[l].reshape(1, LANES), k_gain=k_gain[l].reshape(1, LANES),
        ones=(lane[:, None] // DIFF_HD == lane[None, :] // DIFF_HD).astype(BF16),
        gla_gain=jnp.tile(gla_gain[l], GLA_HEADS)[None], diff_gain=jnp.tile(diff_gain[l], DIFF_HEADS)[None],
        w_gla_out=w_gla_out[l].astype(BF16), w_diff_out=w_diff_out[l].astype(BF16), w_out=w_out[l].astype(BF16),
        w_pq=w_pq[l].astype(BF16), k1=sub_keys1[l].astype(BF16), k2=sub_keys2[l].astype(BF16),
        table=jnp.concatenate([peer_u[l], peer_v[l]], axis=1).reshape(-1, 1, 2 * D_MODEL),
        u=peer_u[l], v=peer_v[l],
    )
    lam = (jnp.exp(jnp.sum(lam_q1[l] * lam_k1[l])) - jnp.exp(jnp.sum(lam_q2[l] * lam_k2[l])) + lambda_init).reshape(1, 1)

    cos_p, sin_p = _rope_tables(jnp.arange(lp))
    mods_p = [m[:bp, None, :] for m in mods]
    attend_p = lambda q, k, v: _attn_prompt(q.reshape(bp, lp, -1), k.reshape(bp, lp, -1), v.reshape(bp, lp, -1),
                                            lam).reshape(bp * lp, -1)
    h0_p = jnp.zeros((bp, GLA_HEADS, GLA_DK, GLA_DV), F32)
    yp, kp, vp, sp = _stream(x_prompt.reshape(bp * lp, D_MODEL), mods_p, lp // PROJ_TILE, PROJ_TILE, cos_p, sin_p,
                             lp, h0_p, wts, attend_p, lambda_init, sc_share=bp >= 4)

    seq_per_tile = SAMPLE_TILE // ls
    cos_s, sin_s = _rope_tables(jnp.tile(past + jnp.arange(ls), seq_per_tile))
    mods_s = [jnp.repeat(m[bp:nb], ls, axis=0).reshape(-1, SAMPLE_TILE, D_MODEL) for m in mods[:5]]
    mods_s.append(mods[5][bp:nb, None, :])
    attend_s = lambda q, k, v: _attn_sample(q, k, v, cache_k[l], cache_v[l], lam, l_seq=ls)
    ys, ks, vs, ss = _stream(x_sample.reshape(bs * ls, D_MODEL), mods_s, 1, SAMPLE_TILE, cos_s, sin_s,
                             ls, state_gla[l], wts, attend_s, lambda_init, sc_share=False)

    return (yp.reshape(bp, lp, D_MODEL), ys.reshape(bs, ls, D_MODEL),
            kp.reshape(1, bp, lp, DIFF_HEADS, 2 * DIFF_HD), vp.reshape(1, bp, lp, DIFF_HEADS, DIFF_VD), sp[None],
            ks.reshape(1, bs, ls, DIFF_HEADS, 2 * DIFF_HD), vs.reshape(1, bs, ls, DIFF_HEADS, DIFF_VD), ss[None])
```

```python
import math
import functools
import jax, jax.numpy as jnp
from jax import lax
from jax.experimental import pallas as pl
from jax.experimental.pallas import tpu as pltpu
from jax.experimental.pallas import tpu_sc as plsc

D_MODEL = 1024
CHUNK = 64
EPS = 1e-6
GLA_HEADS = 4
GLA_DK = 128
GLA_DV = 256
GLA_LOWRANK = 16
GLA_TAU = 16.0
GLA_BLOCK = 16
DIFF_HEADS = 8
DIFF_HD = 64
DIFF_VD = 128
ROPE_THETA = 10000.0
PEER_HEADS = 8
PEER_NKEYS = 128
PEER_DKEY = 256
PEER_TOPK = 16
GLA_QK_W = GLA_HEADS * GLA_DK
GLA_V_W = GLA_HEADS * GLA_DV
DIFF_QK_W = DIFF_HEADS * 2 * DIFF_HD
DIFF_V_W = DIFF_HEADS * DIFF_VD
F32 = jnp.float32
BF16 = jnp.bfloat16
NSEL = PEER_HEADS * PEER_TOPK
LANES = 128
SUBLANES = 8

PROJ_TILE = 512
POST_TILE = 256
GLA_TILE = 256
SAMPLE_TILE = 256
PEER_TOK_TILE = 8
ATTN_NEG = -1e30
ATTN_TILE = 1024
SELECT_TILE = 1024
SC_LANES = 16
SC_WORKERS = 32
SC_ROWS = 32
SC_STRIP = 256
SC_RGROUP = 8
SC_UGROUP = 1
SC_SPLIT_BLOCK = 512
SC_SHARE = 27 / 64
SC_FIRST_SHARE = 26 / 64


def _mod_kernel(c_ref, w_ref, b_ref, o_ref):
    c = c_ref[...]
    s = c * jax.nn.sigmoid(c)
    o_ref[...] = jnp.dot(s.astype(BF16), w_ref[...].astype(BF16), preferred_element_type=F32) + b_ref[...]


def _adaln(c, w_mod, b_mod):
    n = c.shape[0]
    tn = 1536
    return pl.pallas_call(
        _mod_kernel,
        grid=(6 * D_MODEL // tn,),
        in_specs=[pl.BlockSpec((n, D_MODEL), lambda j: (0, 0)),
                  pl.BlockSpec((D_MODEL, tn), lambda j: (0, j)),
                  pl.BlockSpec((1, tn), lambda j: (0, j))],
        out_specs=pl.BlockSpec((n, tn), lambda j: (0, j)),
        out_shape=jax.ShapeDtypeStruct((n, 6 * D_MODEL), F32),
        name="adaln",
    )(c, w_mod, b_mod.reshape(1, -1))


def _modulated_norm(x, gain, scale, shift):
    y = x * lax.rsqrt(jnp.mean(x * x, axis=-1, keepdims=True) + EPS)
    return (y * gain) * (1.0 + scale) + shift


def _proj_gla_kernel(x_ref, sh_ref, sc_ref, gain_ref, w_ref, wa_ref, ba_ref,
                     q_ref, k_ref, v_ref, r_ref, la_ref):
    h = _modulated_norm(x_ref[...], gain_ref[...], sc_ref[0], sh_ref[0]).astype(BF16)
    z = jnp.dot(h, w_ref[...], preferred_element_type=F32)
    q_ref[...] = z[:, 0:GLA_QK_W] * (GLA_DK ** -0.5)
    k_ref[...] = z[:, GLA_QK_W:2 * GLA_QK_W]
    v_ref[...] = z[:, 2 * GLA_QK_W:2 * GLA_QK_W + GLA_V_W]
    r_ref[...] = z[:, 2 * GLA_QK_W + GLA_V_W:2 * GLA_QK_W + 2 * GLA_V_W]
    glr = z[:, 2 * GLA_QK_W + 2 * GLA_V_W:]
    a = jnp.dot(glr.astype(BF16), wa_ref[...], preferred_element_type=F32) + ba_ref[...]
    la_ref[...] = jax.nn.log_sigmoid(a) * (1.0 / GLA_TAU)


def _group_mean_square(x, ones_ref):
    s = x * x
    hi = s.astype(BF16)
    lo = (s - hi.astype(F32)).astype(BF16)
    ss = (jnp.dot(hi, ones_ref[...], preferred_element_type=F32)
          + jnp.dot(lo, ones_ref[...], preferred_element_type=F32))
    return ss * (1.0 / DIFF_HD)


def _norm_rope(x, gain, cos, sin_signed, ones_ref):
    outs = []
    lane = lax.broadcasted_iota(jnp.int32, (x.shape[0], LANES), 1)
    first_half = (lane % DIFF_HD) < (DIFF_HD // 2)
    for b in range(DIFF_QK_W // LANES):
        xb = x[:, b * LANES:(b + 1) * LANES]
        y = xb * lax.rsqrt(_group_mean_square(xb, ones_ref) + EPS) * gain
        rot = jnp.where(first_half, pltpu.roll(y, LANES - DIFF_HD // 2, axis=1), pltpu.roll(y, DIFF_HD // 2, axis=1))
        outs.append(y * cos + rot * sin_signed)
    return jnp.concatenate(outs, axis=1)


def _proj_diff_kernel(x_ref, sh_ref, sc_ref, gain_ref, w_ref, qg_ref, kg_ref, cos_ref, sin_ref, ones_ref,
                      qb_ref, k_ref, kb_ref, v_ref, vb_ref, *, q_scale):
    h = _modulated_norm(x_ref[...], gain_ref[...], sc_ref[0], sh_ref[0]).astype(BF16)
    z = jnp.dot(h, w_ref[...], preferred_element_type=F32)
    cos, sin = cos_ref[...], sin_ref[...]
    q = _norm_rope(z[:, 0:DIFF_QK_W], qg_ref[...], cos, sin, ones_ref)
    qb_ref[...] = (q * q_scale).astype(BF16)
    k = _norm_rope(z[:, DIFF_QK_W:2 * DIFF_QK_W], kg_ref[...], cos, sin, ones_ref)
    k_ref[...] = k
    kb_ref[...] = k.astype(BF16)
    v = z[:, 2 * DIFF_QK_W:2 * DIFF_QK_W + DIFF_V_W]
    v_ref[...] = v
    vb_ref[...] = v.astype(BF16)


def _proj_gla(x, shift, scale, gain, w_a, w_alpha_pad, b_alpha, *, tt, mod_tiles):
    n = x.shape[0]
    assert n % tt == 0
    bmap = lambda i: (i // mod_tiles, 0, 0)
    mod = pl.BlockSpec((1,) + shift.shape[1:], bmap)
    row = lambda w: pl.BlockSpec((tt, w), lambda i: (i, 0))
    const = lambda a: pl.BlockSpec(a.shape, lambda i: (0,) * a.ndim)
    return pl.pallas_call(
        _proj_gla_kernel,
        grid=(n // tt,),
        in_specs=[row(D_MODEL), mod, mod,
                  const(gain), const(w_a), const(w_alpha_pad), const(b_alpha)],
        out_specs=[row(GLA_QK_W), row(GLA_QK_W), row(GLA_V_W), row(GLA_V_W), row(GLA_QK_W)],
        out_shape=[jax.ShapeDtypeStruct((n, GLA_QK_W), F32), jax.ShapeDtypeStruct((n, GLA_QK_W), F32),
                   jax.ShapeDtypeStruct((n, GLA_V_W), F32), jax.ShapeDtypeStruct((n, GLA_V_W), F32),
                   jax.ShapeDtypeStruct((n, GLA_QK_W), F32)],
        compiler_params=pltpu.CompilerParams(dimension_semantics=("arbitrary",), vmem_limit_bytes=48 * 1024 * 1024),
        name="proj_gla",
    )(x, shift, scale, gain, w_a, w_alpha_pad, b_alpha)


def _proj_diff(x, shift, scale, gain, w_b, q_gain128, k_gain128, cos128, sin128, ones128, *, tt, mod_tiles):
    n = x.shape[0]
    assert n % tt == 0 and cos128.shape[0] % tt == 0
    bmap = lambda i: (i // mod_tiles, 0, 0)
    mod = pl.BlockSpec((1,) + shift.shape[1:], bmap)
    pos_tiles = cos128.shape[0] // tt
    row = lambda w: pl.BlockSpec((tt, w), lambda i: (i, 0))
    const = lambda a: pl.BlockSpec(a.shape, lambda i: (0,) * a.ndim)
    pos = pl.BlockSpec((tt, LANES), lambda i: (i % pos_tiles, 0))
    q_scale = math.log2(math.e) * DIFF_HD ** -0.5
    return pl.pallas_call(
        functools.partial(_proj_diff_kernel, q_scale=q_scale),
        grid=(n // tt,),
        in_specs=[row(D_MODEL), mod, mod,
                  const(gain), const(w_b), const(q_gain128), const(k_gain128), pos, pos, const(ones128)],
        out_specs=[row(DIFF_QK_W)] * 5,
        out_shape=[jax.ShapeDtypeStruct((n, DIFF_QK_W), BF16), jax.ShapeDtypeStruct((n, DIFF_QK_W), F32),
                   jax.ShapeDtypeStruct((n, DIFF_QK_W), BF16), jax.ShapeDtypeStruct((n, DIFF_V_W), F32),
                   jax.ShapeDtypeStruct((n, DIFF_V_W), BF16)],
        compiler_params=pltpu.CompilerParams(dimension_semantics=("arbitrary",), vmem_limit_bytes=48 * 1024 * 1024),
        name="proj_diff",
    )(x, shift, scale, gain, w_b, q_gain128, k_gain128, cos128, sin128, ones128)


def _gla_kernel(q_ref, k_ref, la_ref, v_ref, h0_ref, o_ref, hT_ref, st_ref, *, tb):
    i = pl.program_id(2)
    nb = tb // GLA_BLOCK

    @pl.when(i == 0)
    def _init():
        st_ref[...] = h0_ref[0, 0].T

    row = lax.broadcasted_iota(jnp.int32, (tb, tb), 0)
    col = lax.broadcasted_iota(jnp.int32, (tb, tb), 1)
    same = (row // GLA_BLOCK) == (col // GLA_BLOCK)
    causal = same & (col <= row)
    la = la_ref[...]
    hp = lax.Precision.HIGHEST
    b = jnp.dot(causal.astype(F32), la, precision=hp, preferred_element_type=F32)
    b_last = jnp.dot(same.astype(F32), la, precision=hp, preferred_element_type=F32)
    q, k, v = q_ref[...], k_ref[...], v_ref[...]
    qe = (q * jnp.exp(b)).astype(BF16)
    ke = (k * jnp.exp(-b)).astype(BF16)
    kd = (k * jnp.exp(b_last - b)).astype(BF16)
    vb = v.astype(BF16)
    a = lax.dot_general(qe, ke, (((1,), (1,)), ((), ())), preferred_element_type=F32)
    a = jnp.where(causal, a, 0.0).astype(BF16)
    o_intra = jnp.dot(a, vb, preferred_element_type=F32)

    st = st_ref[...]
    outs = []
    for n in range(nb):
        rs = slice(n * GLA_BLOCK, (n + 1) * GLA_BLOCK)
        o_inter = lax.dot_general(qe[rs], st.astype(BF16), (((1,), (1,)), ((), ())), preferred_element_type=F32)
        outs.append(o_inter + o_intra[rs])
        u = lax.dot_general(vb[rs], kd[rs], (((0,), (0,)), ((), ())), preferred_element_type=F32)
        st = jnp.exp(b_last[n * GLA_BLOCK:n * GLA_BLOCK + 1, :]) * st + u
    o_ref[...] = jnp.concatenate(outs, axis=0)
    st_ref[...] = st

    @pl.when(i == pl.num_programs(2) - 1)
    def _fin():
        hT_ref[0, 0] = st.T


def _gla(q, k, la, v, h0, *, l_seq, tb):
    n = q.shape[0]
    bsz = n // l_seq
    nt = l_seq // tb
    assert l_seq % tb == 0 and tb % GLA_BLOCK == 0
    tok = lambda w: pl.BlockSpec((tb, w), lambda b, h, i: (b * nt + i, h))
    st = pl.BlockSpec((1, 1, GLA_DK, GLA_DV), lambda b, h, i: (b, h, 0, 0))
    return pl.pallas_call(
        functools.partial(_gla_kernel, tb=tb),
        grid=(bsz, GLA_HEADS, nt),
        in_specs=[tok(GLA_DK), tok(GLA_DK), tok(GLA_DK), tok(GLA_DV), st],
        out_specs=[tok(GLA_DV), st],
        out_shape=[jax.ShapeDtypeStruct((n, GLA_V_W), F32),
                   jax.ShapeDtypeStruct((bsz, GLA_HEADS, GLA_DK, GLA_DV), F32)],
        scratch_shapes=[pltpu.VMEM((GLA_DV, GLA_DK), F32)],
        compiler_params=pltpu.CompilerParams(dimension_semantics=("arbitrary", "arbitrary", "arbitrary")),
        name="gla",
    )(q, k, la, v, h0)


def _split_maps(q):
    lane = lax.broadcasted_iota(jnp.int32, q.shape, 1)
    zero = jnp.zeros_like(q)
    return jnp.concatenate([jnp.where(lane < DIFF_HD, q, zero), jnp.where(lane >= DIFF_HD, q, zero)], axis=0)


def _attn_prompt_kernel(lam_ref, q_ref, k_ref, v_ref, o_ref, vt_ref, *, tq, tk):
    i = pl.program_id(2)
    n_kt = v_ref.shape[1] // tk

    @pl.when(i == 0)
    def _transpose_v():
        def body(j, c):
            vt_ref[j] = v_ref[0, pl.ds(j * tk, tk), :].astype(F32).T.astype(BF16)
            return c
        lax.fori_loop(0, n_kt, body, 0)

    qp = _split_maps(q_ref[0])

    def step(j, carry, masked, koff=0):
        m, l, acc = carry
        kt = k_ref[0, pl.ds(j * tk, tk), :]
        st = lax.dot_general(kt, qp, (((1,), (1,)), ((), ())), preferred_element_type=F32)
        if masked:
            kc = (lax.broadcasted_iota(jnp.int32, st.shape, 0) + koff) // CHUNK
            qc = (lax.broadcasted_iota(jnp.int32, st.shape, 1) % tq) // CHUNK
            st = jnp.where(qc >= kc, st, ATTN_NEG)
        m_new = jnp.maximum(m, jnp.max(st, axis=0, keepdims=True))
        alpha = jnp.exp2(m - m_new)
        p = jnp.exp2(st - m_new)
        l = alpha * l + jnp.sum(p, axis=0, keepdims=True)
        acc = alpha * acc + jnp.dot(vt_ref[j], p.astype(BF16), preferred_element_type=F32)
        return m_new, l, acc

    carry = (jnp.full((1, 2 * tq), ATTN_NEG, F32), jnp.zeros((1, 2 * tq), F32), jnp.zeros((LANES, 2 * tq), F32))
    r = tq // tk
    carry = lax.fori_loop(0, i * r, functools.partial(step, masked=False), carry)
    for d in range(r):
        carry = step(i * r + d, carry, True, koff=d * tk)
    m, l, acc = carry
    o = acc / l
    ot = o[:, :tq] - lam_ref[0, 0] * o[:, tq:]
    o_ref[0] = ot.T


def _attn_prompt(q, k, v, lam, *, tq=ATTN_TILE, tk=ATTN_TILE):
    b, l, w = q.shape
    h = w // LANES
    assert l % tq == 0 and tq % tk == 0 and tk % CHUNK == 0
    return pl.pallas_call(
        functools.partial(_attn_prompt_kernel, tq=tq, tk=tk),
        grid=(b, h, l // tq),
        in_specs=[
            pl.BlockSpec(memory_space=pltpu.SMEM),
            pl.BlockSpec((1, tq, LANES), lambda bi, hi, i: (bi, i, hi)),
            pl.BlockSpec((1, l, LANES), lambda bi, hi, i: (bi, 0, hi)),
            pl.BlockSpec((1, l, LANES), lambda bi, hi, i: (bi, 0, hi)),
        ],
        out_specs=pl.BlockSpec((1, tq, LANES), lambda bi, hi, i: (bi, i, hi)),
        out_shape=jax.ShapeDtypeStruct((b, l, w), F32),
        scratch_shapes=[pltpu.VMEM((l // tk, LANES, tk), BF16)],
        compiler_params=pltpu.CompilerParams(dimension_semantics=("arbitrary", "arbitrary", "arbitrary"),
                                             vmem_limit_bytes=40 * 1024 * 1024),
        name="attn_prompt",
    )(lam, q, k, v)


def _attn_sample_kernel(lam_ref, q_ref, kn_ref, vn_ref, kc_ref, vc_ref, o_ref, m_ref, l_ref, acc_ref, *, heads):
    j = pl.program_id(1)
    lq = q_ref.shape[0]
    nt = (((1,), (1,)), ((), ()))

    @pl.when(j == 0)
    def _init():
        m_ref[...] = jnp.full(m_ref.shape, ATTN_NEG, F32)
        l_ref[...] = jnp.zeros(l_ref.shape, F32)
        acc_ref[...] = jnp.zeros(acc_ref.shape, F32)

    def update(h, k, v):
        qp = _split_maps(q_ref[:, h * LANES:(h + 1) * LANES])
        s = lax.dot_general(qp, k, nt, preferred_element_type=F32)
        m_old = m_ref[h]
        m_new = jnp.maximum(m_old, jnp.max(s, axis=-1, keepdims=True))
        alpha = jnp.exp2(m_old - m_new)
        p = jnp.exp2(s - m_new)
        l_ref[h] = alpha * l_ref[h] + jnp.sum(p, axis=-1, keepdims=True)
        acc_ref[h] = alpha * acc_ref[h] + jnp.dot(p.astype(BF16), v, preferred_element_type=F32)
        m_ref[h] = m_new

    for h in range(heads):
        update(h, kc_ref[:, h, :].astype(BF16), vc_ref[:, h, :].astype(BF16))

    @pl.when(j == pl.num_programs(1) - 1)
    def _fin():
        for h in range(heads):
            update(h, kn_ref[:, h * LANES:(h + 1) * LANES], vn_ref[:, h * LANES:(h + 1) * LANES])
            o = acc_ref[h] / l_ref[h]
            o_ref[:, h * LANES:(h + 1) * LANES] = o[:lq] - lam_ref[0, 0] * o[lq:]


def _attn_sample(q, k_new, v_new, cache_k, cache_v, lam, *, l_seq, tk=1024):
    n, w = q.shape
    bsz, past, heads, _ = cache_k.shape
    tk = min(tk, past)
    assert n == bsz * l_seq and w == heads * LANES and past % tk == 0
    tok = pl.BlockSpec((l_seq, w), lambda b, j: (b, 0))
    cache = pl.BlockSpec((None, tk, heads, LANES), lambda b, j: (b, j, 0, 0))
    return pl.pallas_call(
        functools.partial(_attn_sample_kernel, heads=heads),
        grid=(bsz, past // tk),
        in_specs=[pl.BlockSpec(memory_space=pltpu.SMEM), tok, tok, tok, cache, cache],
        out_specs=tok,
        out_shape=jax.ShapeDtypeStruct((n, w), F32),
        scratch_shapes=[pltpu.VMEM((heads, 2 * l_seq, 1), F32), pltpu.VMEM((heads, 2 * l_seq, 1), F32),
                        pltpu.VMEM((heads, 2 * l_seq, LANES), F32)],
        compiler_params=pltpu.CompilerParams(dimension_semantics=("arbitrary", "arbitrary"),
                                             vmem_limit_bytes=40 * 1024 * 1024),
        name="attn_sample",
    )(lam, q, k_new, v_new, cache_k, cache_v)


def _head_norm(x, width):
    outs = []
    for g in range(x.shape[1] // width):
        xg = x[:, g * width:(g + 1) * width]
        outs.append(xg * lax.rsqrt(jnp.mean(xg * xg, axis=-1, keepdims=True) + EPS))
    return jnp.concatenate(outs, axis=1)


def _post_kernel(x_ref, oa_ref, gr_ref, ob_ref, sh1_ref, sc1_ref, gt1_ref, sh2_ref, sc2_ref,
                 n1_ref, n2_ref, ga_ref, gb_ref, wa_ref, wb_ref, wg_ref, wo_ref, x1_ref, h2_ref, *, diff_out_scale):
    x = x_ref[...]
    h = _modulated_norm(x, n1_ref[...], sc1_ref[0], sh1_ref[0]).astype(BF16)
    gates = jnp.dot(h, wg_ref[...], preferred_element_type=F32)
    gr = gr_ref[...]
    oa = _head_norm(oa_ref[...], GLA_DV) * ga_ref[...] * (gr * jax.nn.sigmoid(gr))
    ya = jnp.dot(oa.astype(BF16), wa_ref[...], preferred_element_type=F32)
    ob = _head_norm(ob_ref[...], DIFF_VD) * gb_ref[...] * diff_out_scale
    yb = jnp.dot(ob.astype(BF16), wb_ref[...], preferred_element_type=F32)
    merged = jax.nn.sigmoid(gates[:, :D_MODEL]) * ya + jax.nn.sigmoid(gates[:, D_MODEL:]) * yb
    mix = jnp.dot(merged.astype(BF16), wo_ref[...], preferred_element_type=F32)
    x1 = x + gt1_ref[0] * mix
    x1_ref[...] = x1
    h2_ref[...] = _modulated_norm(x1, n2_ref[...], sc2_ref[0], sh2_ref[0])


def _post_mixer(x, oa, gr, ob, mods, n1, n2, ga, gb, wa, wb, wg, wo, *, tt, mod_tiles, diff_out_scale):
    n = x.shape[0]
    assert n % tt == 0
    bmap = lambda i: (i // mod_tiles, 0, 0)
    mod = pl.BlockSpec((1,) + mods[0].shape[1:], bmap)
    row = pl.BlockSpec((tt, D_MODEL), lambda i: (i, 0))
    const = lambda a: pl.BlockSpec(a.shape, lambda i: (0,) * a.ndim)
    return pl.pallas_call(
        functools.partial(_post_kernel, diff_out_scale=diff_out_scale),
        grid=(n // tt,),
        in_specs=[row, row, row, row] + [mod] * 5 + [const(a) for a in (n1, n2, ga, gb, wa, wb, wg, wo)],
        out_specs=[row, row],
        out_shape=[jax.ShapeDtypeStruct((n, D_MODEL), F32), jax.ShapeDtypeStruct((n, D_MODEL), F32)],
        compiler_params=pltpu.CompilerParams(dimension_semantics=("arbitrary",), vmem_limit_bytes=48 * 1024 * 1024),
        name="post_mixer",
    )(x, oa, gr, ob, *mods, n1, n2, ga, gb, wa, wb, wg, wo)


def _topk_rows(s, k):
    r = s.shape[0]
    row = lax.broadcasted_iota(jnp.int32, s.shape, 0)
    vals, idxs = [], []
    for _ in range(k):
        mx = jnp.max(s, axis=0, keepdims=True)
        am = jnp.min(jnp.where(s == mx, row, r), axis=0, keepdims=True)
        vals.append(mx)
        idxs.append(am)
        s = jnp.where(row == am, -jnp.inf, s)
    return jnp.concatenate(vals, axis=0), jnp.concatenate(idxs, axis=0)


def _take_rows(tab, sel):
    out = jnp.zeros(sel.shape, tab.dtype)
    for a in range(tab.shape[0]):
        out = jnp.where(sel == a, tab[a:a + 1, :], out)
    return out


def _staircase():
    return [(a, b) for a in range(PEER_TOPK) for b in range(PEER_TOPK) if (a + 1) * (b + 1) <= PEER_TOPK]


def _peer_select_kernel(h2_ref, w_ref, k1_ref, k2_ref, idx_ref, g_ref):
    q = jnp.dot(h2_ref[...].astype(BF16), w_ref[...], preferred_element_type=F32)
    nt = (((1,), (1,)), ((), ()))
    s1 = lax.dot_general(k1_ref[...], q[:, :PEER_NKEYS].astype(BF16), nt, preferred_element_type=F32)
    s2 = lax.dot_general(k2_ref[...], q[:, PEER_NKEYS:].astype(BF16), nt, preferred_element_type=F32)
    v1, i1 = _topk_rows(s1, PEER_TOPK)
    v2, i2 = _topk_rows(s2, PEER_TOPK)
    pairs = _staircase()
    neg = jnp.full(((-len(pairs)) % SUBLANES, v1.shape[1]), -jnp.inf, F32)
    cand = jnp.concatenate([v1[a:a + 1, :] + v2[b:b + 1, :] for a, b in pairs] + [neg], axis=0)
    sc, cr = _topk_rows(cand, PEER_TOPK)
    ra = jnp.zeros(cr.shape, jnp.int32)
    rb = jnp.zeros(cr.shape, jnp.int32)
    for r, (a, b) in enumerate(pairs):
        hit = cr == r
        ra = jnp.where(hit, a, ra)
        rb = jnp.where(hit, b, rb)
    e1 = _take_rows(i1, ra)
    e2 = _take_rows(i2, rb)
    idx_ref[...] = e1 * PEER_NKEYS + e2
    p = jnp.exp(sc - sc[0:1, :])
    g_ref[...] = p / jnp.sum(p, axis=0, keepdims=True)


def _peer_select(h2, w_pq_bf, k1_bf, k2_bf, *, tt=SELECT_TILE):
    n, d = h2.shape
    assert n % tt == 0
    return pl.pallas_call(
        _peer_select_kernel,
        grid=(n // tt, PEER_HEADS),
        in_specs=[
            pl.BlockSpec((tt, d), lambda i, h: (i, 0)),
            pl.BlockSpec((d, PEER_DKEY), lambda i, h: (0, h)),
            pl.BlockSpec((PEER_NKEYS, PEER_DKEY // 2), lambda i, h: (0, 0)),
            pl.BlockSpec((PEER_NKEYS, PEER_DKEY // 2), lambda i, h: (0, 0)),
        ],
        out_specs=[pl.BlockSpec((PEER_TOPK, tt), lambda i, h: (h, i)),
                   pl.BlockSpec((PEER_TOPK, tt), lambda i, h: (h, i))],
        out_shape=[jax.ShapeDtypeStruct((NSEL, n), jnp.int32),
                   jax.ShapeDtypeStruct((NSEL, n), F32)],
        compiler_params=pltpu.CompilerParams(dimension_semantics=("arbitrary", "arbitrary")),
        name="peer_select",
    )(h2, w_pq_bf, k1_bf, k2_bf)


def _peer_gather_kernel(idx_ref, h2_ref, g_ref, x1_ref, gate_ref, tab_ref, out_ref, buf, sem, *, tt):
    s = pl.program_id(0)
    n = pl.num_programs(0) - 1

    @pl.when(s < n)
    def _issue():
        slot = s % 2

        def issue_tok(t, carry):
            for j in range(NSEL):
                e = idx_ref[t, j]
                pltpu.make_async_copy(tab_ref.at[e], buf.at[slot, t, pl.ds(j, 1), :], sem.at[slot, t]).start()
            return carry

        lax.fori_loop(0, tt, issue_tok, 0)

    @pl.when(s >= 1)
    def _compute():
        slot = (s + 1) % 2
        eye = (lax.broadcasted_iota(jnp.int32, (NSEL, NSEL), 0)
               == lax.broadcasted_iota(jnp.int32, (NSEL, NSEL), 1))
        gate = gate_ref[0]

        def tok(t, carry):
            for j in range(NSEL):
                pltpu.make_async_copy(tab_ref.at[0], buf.at[slot, t, pl.ds(j, 1), :], sem.at[slot, t]).wait()
            x = h2_ref[pl.ds(t, 1), :]
            p = buf[slot, t, :, 0:LANES] * x[:, 0:LANES]
            for c in range(1, D_MODEL // LANES):
                p = p + buf[slot, t, :, c * LANES:(c + 1) * LANES] * x[:, c * LANES:(c + 1) * LANES]
            a = jnp.sum(p, axis=-1, keepdims=True)
            act = 0.5 * a * (1.0 + lax.erf(a * (0.5 ** 0.5)))
            grow = g_ref[pl.ds(t, 1), :]
            gcol = jnp.sum(jnp.where(eye, grow, 0.0), axis=-1, keepdims=True)
            coef = gcol * act
            y = jnp.sum(coef * buf[slot, t, :, D_MODEL:2 * D_MODEL], axis=0, keepdims=True)
            out_ref[pl.ds(t, 1), :] = x1_ref[pl.ds(t, 1), :] + gate * y
            return carry

        lax.fori_loop(0, tt, tok, 0)


def _peer_gather(idx, g, h2, x1, *, gates, table, tt, l_seq, tok_offset=0):
    n_tok = idx.shape[0]
    n_tiles = n_tok // tt
    assert n_tiles * tt == n_tok and l_seq % tt == 0

    def cur(s):
        return jnp.maximum(s - 1, 0)

    return pl.pallas_call(
        functools.partial(_peer_gather_kernel, tt=tt),
        grid=(n_tiles + 1,),
        in_specs=[
            pl.BlockSpec((tt, NSEL), lambda s: (jnp.minimum(s, n_tiles - 1), 0), memory_space=pltpu.SMEM),
            pl.BlockSpec((tt, D_MODEL), lambda s: (cur(s), 0)),
            pl.BlockSpec((tt, NSEL), lambda s: (cur(s), 0)),
            pl.BlockSpec((tt, D_MODEL), lambda s: (cur(s), 0)),
            pl.BlockSpec((1, 1, D_MODEL), lambda s: ((tok_offset + cur(s) * tt) // l_seq, 0, 0)),
            pl.BlockSpec(memory_space=pl.ANY),
        ],
        out_specs=pl.BlockSpec((tt, D_MODEL), lambda s: (cur(s), 0)),
        out_shape=jax.ShapeDtypeStruct((n_tok, D_MODEL), F32),
        scratch_shapes=[pltpu.VMEM((2, tt, NSEL, 2 * D_MODEL), F32), pltpu.SemaphoreType.DMA((2, tt))],
        compiler_params=pltpu.CompilerParams(dimension_semantics=("arbitrary",),
                                             vmem_limit_bytes=48 * 1024 * 1024),
        name="peer_gather",
    )(idx, h2, g, x1, gates, table)


def _sc_chunk_copy(tab_hbm, idx_v, rows_v, sem, c):
    return pltpu.make_async_copy(tab_hbm.at[idx_v.at[pl.ds(c * SC_ROWS, SC_ROWS)]], rows_v.at[c % 2], sem.at[c % 2])


def _sc_udot(idx, x, table):
    ns = idx.shape[0]
    per_w = ns // SC_WORKERS
    assert per_w * SC_WORKERS == ns
    nchunk, nstrip, nk = NSEL // SC_ROWS, D_MODEL // SC_STRIP, SC_STRIP // SC_LANES

    @functools.partial(
        pl.kernel, mesh=plsc.VectorSubcoreMesh(core_axis_name="c", subcore_axis_name="s"),
        out_type=jax.ShapeDtypeStruct((ns, NSEL * SC_LANES), F32),
        scratch_types=[pltpu.VMEM((NSEL,), jnp.int32), pltpu.VMEM((D_MODEL,), F32),
                       pltpu.VMEM((2, SC_ROWS, D_MODEL), F32), pltpu.VMEM((NSEL * SC_LANES,), F32),
                       pltpu.SemaphoreType.DMA((2,))],
    )
    def k(idx_hbm, x_hbm, tab_hbm, out_hbm, idx_v, x_v, rows_v, acc_v, sem):
        base = (lax.axis_index("s") * 2 + lax.axis_index("c")) * per_w

        def token(i, carry):
            tok = base + i
            pltpu.sync_copy(idx_hbm.at[tok], idx_v)
            pltpu.sync_copy(x_hbm.at[tok], x_v)
            _sc_chunk_copy(tab_hbm, idx_v, rows_v, sem, 0).start()
            for c in range(nchunk):
                if c + 1 < nchunk:
                    _sc_chunk_copy(tab_hbm, idx_v, rows_v, sem, c + 1).start()
                _sc_chunk_copy(tab_hbm, idx_v, rows_v, sem, c).wait()
                for kc in range(nstrip):
                    xr = [x_v[pl.ds(kc * SC_STRIP + kk * SC_LANES, SC_LANES)] for kk in range(nk)]

                    def rows8(rg, cr, c=c, kc=kc, xr=xr):
                        for rr in range(SC_UGROUP):
                            r = rg * SC_UGROUP + rr
                            s = rows_v[c % 2, r, pl.ds(kc * SC_STRIP, SC_LANES)] * xr[0]
                            for kk in range(1, nk):
                                s = s + rows_v[c % 2, r, pl.ds(kc * SC_STRIP + kk * SC_LANES, SC_LANES)] * xr[kk]
                            dst = pl.ds((c * SC_ROWS + r) * SC_LANES, SC_LANES)
                            if kc == 0:
                                acc_v[dst] = s
                            else:
                                acc_v[dst] = acc_v[dst] + s
                        return cr

                    lax.fori_loop(0, SC_ROWS // SC_UGROUP, rows8, 0)
            pltpu.sync_copy(acc_v, out_hbm.at[tok])
            return carry

        lax.fori_loop(0, per_w, token, 0)

    return k(idx, x, table)


def _sc_vsum(idx, coefb, table):
    ns = idx.shape[0]
    per_w = ns // SC_WORKERS
    assert per_w * SC_WORKERS == ns
    nchunk, nstrip, nk = NSEL // SC_ROWS, D_MODEL // SC_STRIP, SC_STRIP // SC_LANES

    @functools.partial(
        pl.kernel, mesh=plsc.VectorSubcoreMesh(core_axis_name="c", subcore_axis_name="s"),
        out_type=jax.ShapeDtypeStruct((ns, D_MODEL), F32),
        scratch_types=[pltpu.VMEM((NSEL,), jnp.int32), pltpu.VMEM((NSEL * SC_LANES,), F32),
                       pltpu.VMEM((2, SC_ROWS, D_MODEL), F32), pltpu.VMEM((D_MODEL,), F32),
                       pltpu.SemaphoreType.DMA((2,))],
    )
    def k(idx_hbm, coef_hbm, tab_hbm, out_hbm, idx_v, coef_v, rows_v, y_v, sem):
        base = (lax.axis_index("s") * 2 + lax.axis_index("c")) * per_w

        def token(i, carry):
            tok = base + i
            pltpu.sync_copy(idx_hbm.at[tok], idx_v)
            pltpu.sync_copy(coef_hbm.at[tok], coef_v)
            _sc_chunk_copy(tab_hbm, idx_v, rows_v, sem, 0).start()
            for kk in range(D_MODEL // SC_LANES):
                y_v[pl.ds(kk * SC_LANES, SC_LANES)] = jnp.zeros((SC_LANES,), F32)
            for c in range(nchunk):
                if c + 1 < nchunk:
                    _sc_chunk_copy(tab_hbm, idx_v, rows_v, sem, c + 1).start()
                _sc_chunk_copy(tab_hbm, idx_v, rows_v, sem, c).wait()
                for kc in range(nstrip):
                    def rows8(rg, cr, c=c, kc=kc):
                        acc = [y_v[pl.ds(kc * SC_STRIP + kk * SC_LANES, SC_LANES)] for kk in range(nk)]
                        for rr in range(SC_RGROUP):
                            r = rg * SC_RGROUP + rr
                            cv = coef_v[pl.ds((c * SC_ROWS + r) * SC_LANES, SC_LANES)]
                            for kk in range(nk):
                                acc[kk] = acc[kk] + cv * rows_v[c % 2, r, pl.ds(kc * SC_STRIP + kk * SC_LANES, SC_LANES)]
                        for kk in range(nk):
                            y_v[pl.ds(kc * SC_STRIP + kk * SC_LANES, SC_LANES)] = acc[kk]
                        return cr

                    lax.fori_loop(0, SC_ROWS // SC_RGROUP, rows8, 0)
            pltpu.sync_copy(y_v, out_hbm.at[tok])
            return carry

        lax.fori_loop(0, per_w, token, 0)

    return k(idx, coefb, table)


def _sc_coef_kernel(a_ref, g_ref, after_ref, o_ref):
    hp = lax.Precision.HIGHEST
    grp = lax.broadcasted_iota(jnp.int32, (NSEL * SC_LANES, NSEL), 0) // SC_LANES
    fold = (grp == lax.broadcasted_iota(jnp.int32, (NSEL * SC_LANES, NSEL), 1)).astype(F32)
    a = jnp.dot(a_ref[...], fold, precision=hp, preferred_element_type=F32)
    coef = g_ref[...] * (0.5 * a * (1.0 + lax.erf(a * (0.5 ** 0.5))))
    o_ref[...] = lax.dot_general(coef, fold, (((1,), (1,)), ((), ())), precision=hp, preferred_element_type=F32)


def _sc_coef(a_part, g, after, *, tt=SC_SPLIT_BLOCK):
    n = a_part.shape[0]
    wide = pl.BlockSpec((tt, NSEL * SC_LANES), lambda i: (i, 0))
    return pl.pallas_call(
        _sc_coef_kernel, grid=(n // tt,),
        in_specs=[wide, pl.BlockSpec((tt, NSEL), lambda i: (i, 0)),
                  pl.BlockSpec((SUBLANES, D_MODEL), lambda i: (0, 0))], out_specs=wide,
        out_shape=jax.ShapeDtypeStruct((n, NSEL * SC_LANES), F32), name="sc_coef",
    )(a_part, g, after)


def _sc_residual_kernel(x1_ref, ff_ref, gate_ref, after_ref, o_ref):
    o_ref[...] = x1_ref[...] + gate_ref[0] * ff_ref[...]


def _sc_residual(x1, ff, gate, after, *, tok_offset, l_seq, tt=SC_SPLIT_BLOCK):
    n = x1.shape[0]
    row = pl.BlockSpec((tt, D_MODEL), lambda i: (i, 0))
    return pl.pallas_call(
        _sc_residual_kernel, grid=(n // tt,),
        in_specs=[row, row, pl.BlockSpec((1, 1, D_MODEL), lambda i: ((tok_offset + i * tt) // l_seq, 0, 0)),
                  pl.BlockSpec((SUBLANES, D_MODEL), lambda i: (0, 0))], out_specs=row,
        out_shape=jax.ShapeDtypeStruct((n, D_MODEL), F32), name="sc_residual",
    )(x1, ff, gate, after)


def _rope_tables(pos):
    half = DIFF_HD // 2
    inv = ROPE_THETA ** (-jnp.arange(half, dtype=F32) / half)
    ang = pos.astype(F32)[:, None] * inv[None, :]
    cos, sin = jnp.cos(ang), jnp.sin(ang)
    return jnp.tile(cos, (1, 4)), jnp.tile(jnp.concatenate([-sin, sin], axis=1), (1, 2))


def _stream(x, mods, mod_tiles, tile, cos, sin, l_seq, h0, wts, attend, lambda_init, sc_share):
    sh1, sc1, gt1, sh2, sc2, gate2 = mods
    q_g, k_g, v_g, r_g, la = _proj_gla(x, sh1, sc1, wts["n1"], wts["w_a"], wts["w_alpha"], wts["b_alpha"],
                                       tt=tile, mod_tiles=mod_tiles)
    q_bf, k_rows, k_bf, v_rows, v_bf = _proj_diff(x, sh1, sc1, wts["n1"], wts["w_b"], wts["q_gain"], wts["k_gain"],
                                                  cos, sin, wts["ones"], tt=tile, mod_tiles=mod_tiles)
    o_a, state = _gla(q_g, k_g, la, v_g, h0, l_seq=l_seq, tb=min(GLA_TILE, l_seq))
    o_b = attend(q_bf, k_bf, v_bf)
    post_tile = min(POST_TILE, tile)
    x1, h2 = _post_mixer(x, o_a, r_g, o_b, (sh1, sc1, gt1, sh2, sc2), wts["n1"], wts["n2"], wts["gla_gain"],
                         wts["diff_gain"], wts["w_gla_out"], wts["w_diff_out"], wts["w_g"], wts["w_out"],
                         tt=post_tile, mod_tiles=mod_tiles * (tile // post_tile), diff_out_scale=1.0 - lambda_init)
    idx_t, g_t = _peer_select(h2, wts["w_pq"], wts["k1"], wts["k2"])
    idx, g = idx_t.T, g_t.T
    n_seq = x.shape[0] // l_seq
    if not sc_share:
        y = _peer_gather(idx, g, h2, x1, gates=gate2, table=wts["table"], tt=PEER_TOK_TILE, l_seq=l_seq)
        return y, k_rows, v_rows, state

    nblk = x.shape[0] // SC_SPLIT_BLOCK
    n_sc = round(nblk * SC_SHARE) * SC_SPLIT_BLOCK
    n_tc = x.shape[0] - n_sc
    n_a = round(nblk * SC_FIRST_SHARE) * SC_SPLIT_BLOCK
    gather = functools.partial(_peer_gather, gates=gate2, table=wts["table"], tt=PEER_TOK_TILE, l_seq=l_seq)
    a_part = _sc_udot(idx[n_tc:], h2[n_tc:], wts["u"])
    y_a = gather(idx[:n_a], g[:n_a], h2[:n_a], x1[:n_a], tok_offset=0)
    coefb = _sc_coef(a_part, g[n_tc:], y_a[:SUBLANES])
    ff = _sc_vsum(idx[n_tc:], coefb, wts["v"])
    y_b = gather(idx[n_a:n_tc], g[n_a:n_tc], h2[n_a:n_tc], x1[n_a:n_tc], tok_offset=n_a)
    y_c = _sc_residual(x1[n_tc:], ff, gate2, y_b[:SUBLANES], tok_offset=n_tc, l_seq=l_seq)
    return jnp.concatenate([y_a, y_b, y_c], axis=0), k_rows, v_rows, state


def kernel(x_prompt, x_sample, c_prompt, c_sample, cache_k, cache_v, state_gla, norm1_gain, norm2_gain, w_mod, b_mod, w_in, w_alpha, b_alpha, gla_gain, w_gla_out, q_gain, k_gain, lam_q1, lam_k1, lam_q2, lam_k2, diff_gain, w_diff_out, w_out, w_pq, sub_keys1, sub_keys2, peer_u, peer_v):
    bp, lp, _ = x_prompt.shape
    bs, ls, _ = x_sample.shape
    past = cache_k.shape[2]
    l = 0
    lambda_init = 0.8 - 0.6 * math.exp(-0.3 * l)
    nb = bp + bs
    c_all = jnp.concatenate([c_prompt, c_sample, jnp.zeros(((-nb) % 8, D_MODEL), F32)], axis=0)
    mod_all = _adaln(c_all, w_mod[l], b_mod[l])
    mods = [mod_all[:nb, i * D_MODEL:(i + 1) * D_MODEL] for i in range(6)]

    wi = w_in[l]
    o_glr = 2 * GLA_QK_W + 2 * GLA_V_W
    o_diff = o_glr + GLA_LOWRANK
    o_gate = o_diff + 2 * DIFF_QK_W + DIFF_V_W
    pad_lr = LANES - GLA_LOWRANK
    lane = jnp.arange(LANES)
    wts = dict(
        n1=norm1_gain[l][None], n2=norm2_gain[l][None],
        w_a=jnp.concatenate([wi[:, :o_diff], jnp.zeros((D_MODEL, pad_lr), F32)], axis=1).astype(BF16),
        w_alpha=jnp.concatenate([w_alpha[l], jnp.zeros((pad_lr, GLA_QK_W), F32)], axis=0).astype(BF16),
        b_alpha=b_alpha[l][None],
        w_b=wi[:, o_diff:o_gate].astype(BF16), w_g=wi[:, o_gate:].astype(BF16),
        q_gain=q_gain[l].reshape(1, LANES), k_gain=k_gain[l].reshape(1, LANES),
        ones=(lane[:, None] // DIFF_HD == lane[None, :] // DIFF_HD).astype(BF16),
        gla_gain=jnp.tile(gla_gain[l], GLA_HEADS)[None], diff_gain=jnp.tile(diff_gain[l], DIFF_HEADS)[None],
        w_gla_out=w_gla_out[l].astype(BF16), w_diff_out=w_diff_out[l].astype(BF16), w_out=w_out[l].astype(BF16),
        w_pq=w_pq[l].astype(BF16), k1=sub_keys1[l].astype(BF16), k2=sub_keys2[l].astype(BF16),
        table=jnp.concatenate([peer_u[l], peer_v[l]], axis=1).reshape(-1, 1, 2 * D_MODEL),
        u=peer_u[l], v=peer_v[l],
    )
    lam = (jnp.exp(jnp.sum(lam_q1[l] * lam_k1[l])) - jnp.exp(jnp.sum(lam_q2[l] * lam_k2[l])) + lambda_init).reshape(1, 1)

    cos_p, sin_p = _rope_tables(jnp.arange(lp))
    mods_p = [m[:bp, None, :] for m in mods]
    attend_p = lambda q, k, v: _attn_prompt(q.reshape(bp, lp, -1), k.reshape(bp, lp, -1), v.reshape(bp, lp, -1),
                                            lam).reshape(bp * lp, -1)
    h0_p = jnp.zeros((bp, GLA_HEADS, GLA_DK, GLA_DV), F32)
    yp, kp, vp, sp = _stream(x_prompt.reshape(bp * lp, D_MODEL), mods_p, lp // PROJ_TILE, PROJ_TILE, cos_p, sin_p,
                             lp, h0_p, wts, attend_p, lambda_init, sc_share=bp >= 4)

    seq_per_tile = SAMPLE_TILE // ls
    cos_s, sin_s = _rope_tables(jnp.tile(past + jnp.arange(ls), seq_per_tile))
    mods_s = [jnp.repeat(m[bp:nb], ls, axis=0).reshape(-1, SAMPLE_TILE, D_MODEL) for m in mods[:5]]
    mods_s.append(mods[5][bp:nb, None, :])
    attend_s = lambda q, k, v: _attn_sample(q, k, v, cache_k[l], cache_v[l], lam, l_seq=ls)
    ys, ks, vs, ss = _stream(x_sample.reshape(bs * ls, D_MODEL), mods_s, 1, SAMPLE_TILE, cos_s, sin_s,
                             ls, state_gla[l], wts, attend_s, lambda_init, sc_share=False)

    return (yp.reshape(bp, lp, D_MODEL), ys.reshape(bs, ls, D_MODEL),
            kp.reshape(1, bp, lp, DIFF_HEADS, 2 * DIFF_HD), vp.reshape(1, bp, lp, DIFF_HEADS, DIFF_VD), sp[None],
            ks.reshape(1, bs, ls, DIFF_HEADS, 2 * DIFF_HD), vs.reshape(1, bs, ls, DIFF_HEADS, DIFF_VD), ss[None])
```

```python
import math
import functools
import jax, jax.numpy as jnp
from jax import lax
from jax.experimental import pallas as pl
from jax.experimental.pallas import tpu as pltpu
from jax.experimental.pallas import tpu_sc as plsc

D_MODEL = 1024
CHUNK = 64
EPS = 1e-6
GLA_HEADS = 4
GLA_DK = 128
GLA_DV = 256
GLA_LOWRANK = 16
GLA_TAU = 16.0
GLA_BLOCK = 16
DIFF_HEADS = 8
DIFF_HD = 64
DIFF_VD = 128
ROPE_THETA = 10000.0
PEER_HEADS = 8
PEER_NKEYS = 128
PEER_DKEY = 256
PEER_TOPK = 16
GLA_QK_W = GLA_HEADS * GLA_DK
GLA_V_W = GLA_HEADS * GLA_DV
DIFF_QK_W = DIFF_HEADS * 2 * DIFF_HD
DIFF_V_W = DIFF_HEADS * DIFF_VD
F32 = jnp.float32
BF16 = jnp.bfloat16
NSEL = PEER_HEADS * PEER_TOPK
LANES = 128
SUBLANES = 8

PROJ_TILE = 512
POST_TILE = 256
GLA_TILE = 256
SAMPLE_TILE = 256
PEER_TOK_TILE = 8
ATTN_NEG = -1e30
ATTN_TILE = 1024
SELECT_TILE = 1024
SC_LANES = 16
SC_WORKERS = 32
SC_ROWS = 32
SC_STRIP = 256
SC_RGROUP = 8
SC_UGROUP = 1
SC_SPLIT_BLOCK = 512
TC_FIRST_SHARE = 17 / 32


def _mod_kernel(c_ref, w_ref, b_ref, o_ref):
    c = c_ref[...]
    s = c * jax.nn.sigmoid(c)
    o_ref[...] = jnp.dot(s.astype(BF16), w_ref[...].astype(BF16), preferred_element_type=F32) + b_ref[...]


def _adaln(c, w_mod, b_mod):
    n = c.shape[0]
    tn = 1536
    return pl.pallas_call(
        _mod_kernel,
        grid=(6 * D_MODEL // tn,),
        in_specs=[pl.BlockSpec((n, D_MODEL), lambda j: (0, 0)),
                  pl.BlockSpec((D_MODEL, tn), lambda j: (0, j)),
                  pl.BlockSpec((1, tn), lambda j: (0, j))],
        out_specs=pl.BlockSpec((n, tn), lambda j: (0, j)),
        out_shape=jax.ShapeDtypeStruct((n, 6 * D_MODEL), F32),
        name="adaln",
    )(c, w_mod, b_mod.reshape(1, -1))


def _modulated_norm(x, gain, scale, shift):
    y = x * lax.rsqrt(jnp.mean(x * x, axis=-1, keepdims=True) + EPS)
    return (y * gain) * (1.0 + scale) + shift


def _proj_gla_kernel(x_ref, sh_ref, sc_ref, gain_ref, w_ref, wa_ref, ba_ref, after_ref,
                     q_ref, k_ref, v_ref, r_ref, la_ref):
    h = _modulated_norm(x_ref[...], gain_ref[...], sc_ref[0], sh_ref[0]).astype(BF16)
    z = jnp.dot(h, w_ref[...], preferred_element_type=F32)
    q_ref[...] = z[:, 0:GLA_QK_W] * (GLA_DK ** -0.5)
    k_ref[...] = z[:, GLA_QK_W:2 * GLA_QK_W]
    v_ref[...] = z[:, 2 * GLA_QK_W:2 * GLA_QK_W + GLA_V_W]
    r_ref[...] = z[:, 2 * GLA_QK_W + GLA_V_W:2 * GLA_QK_W + 2 * GLA_V_W]
    glr = z[:, 2 * GLA_QK_W + 2 * GLA_V_W:]
    a = jnp.dot(glr.astype(BF16), wa_ref[...], preferred_element_type=F32) + ba_ref[...]
    la_ref[...] = jax.nn.log_sigmoid(a) * (1.0 / GLA_TAU)


def _group_mean_square(x, ones_ref):
    s = x * x
    hi = s.astype(BF16)
    lo = (s - hi.astype(F32)).astype(BF16)
    ss = (jnp.dot(hi, ones_ref[...], preferred_element_type=F32)
          + jnp.dot(lo, ones_ref[...], preferred_element_type=F32))
    return ss * (1.0 / DIFF_HD)


def _norm_rope(x, gain, cos, sin_signed, ones_ref):
    outs = []
    lane = lax.broadcasted_iota(jnp.int32, (x.shape[0], LANES), 1)
    first_half = (lane % DIFF_HD) < (DIFF_HD // 2)
    for b in range(DIFF_QK_W // LANES):
        xb = x[:, b * LANES:(b + 1) * LANES]
        y = xb * lax.rsqrt(_group_mean_square(xb, ones_ref) + EPS) * gain
        rot = jnp.where(first_half, pltpu.roll(y, LANES - DIFF_HD // 2, axis=1), pltpu.roll(y, DIFF_HD // 2, axis=1))
        outs.append(y * cos + rot * sin_signed)
    return jnp.concatenate(outs, axis=1)


def _proj_diff_kernel(x_ref, sh_ref, sc_ref, gain_ref, w_ref, qg_ref, kg_ref, cos_ref, sin_ref, ones_ref, after_ref,
                      qb_ref, k_ref, kb_ref, v_ref, vb_ref, *, q_scale):
    h = _modulated_norm(x_ref[...], gain_ref[...], sc_ref[0], sh_ref[0]).astype(BF16)
    z = jnp.dot(h, w_ref[...], preferred_element_type=F32)
    cos, sin = cos_ref[...], sin_ref[...]
    q = _norm_rope(z[:, 0:DIFF_QK_W], qg_ref[...], cos, sin, ones_ref)
    qb_ref[...] = (q * q_scale).astype(BF16)
    k = _norm_rope(z[:, DIFF_QK_W:2 * DIFF_QK_W], kg_ref[...], cos, sin, ones_ref)
    k_ref[...] = k
    kb_ref[...] = k.astype(BF16)
    v = z[:, 2 * DIFF_QK_W:2 * DIFF_QK_W + DIFF_V_W]
    v_ref[...] = v
    vb_ref[...] = v.astype(BF16)


def _order_spec():
    return pl.BlockSpec((SUBLANES, LANES), lambda i: (0, 0))


def _proj_gla(x, shift, scale, gain, w_a, w_alpha_pad, b_alpha, after, *, tt, mod_tiles):
    n = x.shape[0]
    assert n % tt == 0
    bmap = lambda i: (i // mod_tiles, 0, 0)
    mod = pl.BlockSpec((1,) + shift.shape[1:], bmap)
    row = lambda w: pl.BlockSpec((tt, w), lambda i: (i, 0))
    const = lambda a: pl.BlockSpec(a.shape, lambda i: (0,) * a.ndim)
    return pl.pallas_call(
        _proj_gla_kernel,
        grid=(n // tt,),
        in_specs=[row(D_MODEL), mod, mod,
                  const(gain), const(w_a), const(w_alpha_pad), const(b_alpha), _order_spec()],
        out_specs=[row(GLA_QK_W), row(GLA_QK_W), row(GLA_V_W), row(GLA_V_W), row(GLA_QK_W)],
        out_shape=[jax.ShapeDtypeStruct((n, GLA_QK_W), F32), jax.ShapeDtypeStruct((n, GLA_QK_W), F32),
                   jax.ShapeDtypeStruct((n, GLA_V_W), F32), jax.ShapeDtypeStruct((n, GLA_V_W), F32),
                   jax.ShapeDtypeStruct((n, GLA_QK_W), F32)],
        compiler_params=pltpu.CompilerParams(dimension_semantics=("arbitrary",), vmem_limit_bytes=48 * 1024 * 1024),
        name="proj_gla",
    )(x, shift, scale, gain, w_a, w_alpha_pad, b_alpha, after)


def _proj_diff(x, shift, scale, gain, w_b, q_gain128, k_gain128, cos128, sin128, ones128, after, *, tt, mod_tiles):
    n = x.shape[0]
    assert n % tt == 0 and cos128.shape[0] % tt == 0
    bmap = lambda i: (i // mod_tiles, 0, 0)
    mod = pl.BlockSpec((1,) + shift.shape[1:], bmap)
    pos_tiles = cos128.shape[0] // tt
    row = lambda w: pl.BlockSpec((tt, w), lambda i: (i, 0))
    const = lambda a: pl.BlockSpec(a.shape, lambda i: (0,) * a.ndim)
    pos = pl.BlockSpec((tt, LANES), lambda i: (i % pos_tiles, 0))
    q_scale = math.log2(math.e) * DIFF_HD ** -0.5
    return pl.pallas_call(
        functools.partial(_proj_diff_kernel, q_scale=q_scale),
        grid=(n // tt,),
        in_specs=[row(D_MODEL), mod, mod,
                  const(gain), const(w_b), const(q_gain128), const(k_gain128), pos, pos, const(ones128),
                  _order_spec()],
        out_specs=[row(DIFF_QK_W)] * 5,
        out_shape=[jax.ShapeDtypeStruct((n, DIFF_QK_W), BF16), jax.ShapeDtypeStruct((n, DIFF_QK_W), F32),
                   jax.ShapeDtypeStruct((n, DIFF_QK_W), BF16), jax.ShapeDtypeStruct((n, DIFF_V_W), F32),
                   jax.ShapeDtypeStruct((n, DIFF_V_W), BF16)],
        compiler_params=pltpu.CompilerParams(dimension_semantics=("arbitrary",), vmem_limit_bytes=48 * 1024 * 1024),
        name="proj_diff",
    )(x, shift, scale, gain, w_b, q_gain128, k_gain128, cos128, sin128, ones128, after)


def _gla_kernel(q_ref, k_ref, la_ref, v_ref, h0_ref, o_ref, hT_ref, st_ref, *, tb):
    i = pl.program_id(2)
    nb = tb // GLA_BLOCK

    @pl.when(i == 0)
    def _init():
        st_ref[...] = h0_ref[0, 0].T

    row = lax.broadcasted_iota(jnp.int32, (tb, tb), 0)
    col = lax.broadcasted_iota(jnp.int32, (tb, tb), 1)
    same = (row // GLA_BLOCK) == (col // GLA_BLOCK)
    causal = same & (col <= row)
    la = la_ref[...]
    hp = lax.Precision.HIGHEST
    b = jnp.dot(causal.astype(F32), la, precision=hp, preferred_element_type=F32)
    b_last = jnp.dot(same.astype(F32), la, precision=hp, preferred_element_type=F32)
    q, k, v = q_ref[...], k_ref[...], v_ref[...]
    qe = (q * jnp.exp(b)).astype(BF16)
    ke = (k * jnp.exp(-b)).astype(BF16)
    kd = (k * jnp.exp(b_last - b)).astype(BF16)
    vb = v.astype(BF16)
    a = lax.dot_general(qe, ke, (((1,), (1,)), ((), ())), preferred_element_type=F32)
    a = jnp.where(causal, a, 0.0).astype(BF16)
    o_intra = jnp.dot(a, vb, preferred_element_type=F32)

    st = st_ref[...]
    outs = []
    for n in range(nb):
        rs = slice(n * GLA_BLOCK, (n + 1) * GLA_BLOCK)
        o_inter = lax.dot_general(qe[rs], st.astype(BF16), (((1,), (1,)), ((), ())), preferred_element_type=F32)
        outs.append(o_inter + o_intra[rs])
        u = lax.dot_general(vb[rs], kd[rs], (((0,), (0,)), ((), ())), preferred_element_type=F32)
        st = jnp.exp(b_last[n * GLA_BLOCK:n * GLA_BLOCK + 1, :]) * st + u
    o_ref[...] = jnp.concatenate(outs, axis=0)
    st_ref[...] = st

    @pl.when(i == pl.num_programs(2) - 1)
    def _fin():
        hT_ref[0, 0] = st.T


def _gla(q, k, la, v, h0, *, l_seq, tb):
    n = q.shape[0]
    bsz = n // l_seq
    nt = l_seq // tb
    assert l_seq % tb == 0 and tb % GLA_BLOCK == 0
    tok = lambda w: pl.BlockSpec((tb, w), lambda b, h, i: (b * nt + i, h))
    st = pl.BlockSpec((1, 1, GLA_DK, GLA_DV), lambda b, h, i: (b, h, 0, 0))
    return pl.pallas_call(
        functools.partial(_gla_kernel, tb=tb),
        grid=(bsz, GLA_HEADS, nt),
        in_specs=[tok(GLA_DK), tok(GLA_DK), tok(GLA_DK), tok(GLA_DV), st],
        out_specs=[tok(GLA_DV), st],
        out_shape=[jax.ShapeDtypeStruct((n, GLA_V_W), F32),
                   jax.ShapeDtypeStruct((bsz, GLA_HEADS, GLA_DK, GLA_DV), F32)],
        scratch_shapes=[pltpu.VMEM((GLA_DV, GLA_DK), F32)],
        compiler_params=pltpu.CompilerParams(dimension_semantics=("arbitrary", "arbitrary", "arbitrary")),
        name="gla",
    )(q, k, la, v, h0)


def _split_maps(q):
    lane = lax.broadcasted_iota(jnp.int32, q.shape, 1)
    zero = jnp.zeros_like(q)
    return jnp.concatenate([jnp.where(lane < DIFF_HD, q, zero), jnp.where(lane >= DIFF_HD, q, zero)], axis=0)


def _attn_prompt_kernel(lam_ref, q_ref, k_ref, v_ref, o_ref, vt_ref, *, tq, tk):
    i = pl.program_id(2)
    n_kt = v_ref.shape[1] // tk

    @pl.when(i == 0)
    def _transpose_v():
        def body(j, c):
            vt_ref[j] = v_ref[0, pl.ds(j * tk, tk), :].astype(F32).T.astype(BF16)
            return c
        lax.fori_loop(0, n_kt, body, 0)

    qp = _split_maps(q_ref[0])

    def step(j, carry, masked, koff=0):
        m, l, acc = carry
        kt = k_ref[0, pl.ds(j * tk, tk), :]
        st = lax.dot_general(kt, qp, (((1,), (1,)), ((), ())), preferred_element_type=F32)
        if masked:
            kc = (lax.broadcasted_iota(jnp.int32, st.shape, 0) + koff) // CHUNK
            qc = (lax.broadcasted_iota(jnp.int32, st.shape, 1) % tq) // CHUNK
            st = jnp.where(qc >= kc, st, ATTN_NEG)
        m_new = jnp.maximum(m, jnp.max(st, axis=0, keepdims=True))
        alpha = jnp.exp2(m - m_new)
        p = jnp.exp2(st - m_new)
        l = alpha * l + jnp.sum(p, axis=0, keepdims=True)
        acc = alpha * acc + jnp.dot(vt_ref[j], p.astype(BF16), preferred_element_type=F32)
        return m_new, l, acc

    carry = (jnp.full((1, 2 * tq), ATTN_NEG, F32), jnp.zeros((1, 2 * tq), F32), jnp.zeros((LANES, 2 * tq), F32))
    r = tq // tk
    carry = lax.fori_loop(0, i * r, functools.partial(step, masked=False), carry)
    for d in range(r):
        carry = step(i * r + d, carry, True, koff=d * tk)
    m, l, acc = carry
    o = acc / l
    ot = o[:, :tq] - lam_ref[0, 0] * o[:, tq:]
    o_ref[0] = ot.T


def _attn_prompt(q, k, v, lam, *, tq=ATTN_TILE, tk=ATTN_TILE):
    b, l, w = q.shape
    h = w // LANES
    assert l % tq == 0 and tq % tk == 0 and tk % CHUNK == 0
    return pl.pallas_call(
        functools.partial(_attn_prompt_kernel, tq=tq, tk=tk),
        grid=(b, h, l // tq),
        in_specs=[
            pl.BlockSpec(memory_space=pltpu.SMEM),
            pl.BlockSpec((1, tq, LANES), lambda bi, hi, i: (bi, i, hi)),
            pl.BlockSpec((1, l, LANES), lambda bi, hi, i: (bi, 0, hi)),
            pl.BlockSpec((1, l, LANES), lambda bi, hi, i: (bi, 0, hi)),
        ],
        out_specs=pl.BlockSpec((1, tq, LANES), lambda bi, hi, i: (bi, i, hi)),
        out_shape=jax.ShapeDtypeStruct((b, l, w), F32),
        scratch_shapes=[pltpu.VMEM((l // tk, LANES, tk), BF16)],
        compiler_params=pltpu.CompilerParams(dimension_semantics=("arbitrary", "arbitrary", "arbitrary"),
                                             vmem_limit_bytes=40 * 1024 * 1024),
        name="attn_prompt",
    )(lam, q, k, v)


def _attn_sample_kernel(lam_ref, q_ref, kn_ref, vn_ref, kc_ref, vc_ref, o_ref, m_ref, l_ref, acc_ref, *, heads):
    j = pl.program_id(1)
    lq = q_ref.shape[0]
    nt = (((1,), (1,)), ((), ()))

    @pl.when(j == 0)
    def _init():
        m_ref[...] = jnp.full(m_ref.shape, ATTN_NEG, F32)
        l_ref[...] = jnp.zeros(l_ref.shape, F32)
        acc_ref[...] = jnp.zeros(acc_ref.shape, F32)

    def update(h, k, v):
        qp = _split_maps(q_ref[:, h * LANES:(h + 1) * LANES])
        s = lax.dot_general(qp, k, nt, preferred_element_type=F32)
        m_old = m_ref[h]
        m_new = jnp.maximum(m_old, jnp.max(s, axis=-1, keepdims=True))
        alpha = jnp.exp2(m_old - m_new)
        p = jnp.exp2(s - m_new)
        l_ref[h] = alpha * l_ref[h] + jnp.sum(p, axis=-1, keepdims=True)
        acc_ref[h] = alpha * acc_ref[h] + jnp.dot(p.astype(BF16), v, preferred_element_type=F32)
        m_ref[h] = m_new

    for h in range(heads):
        update(h, kc_ref[:, h, :].astype(BF16), vc_ref[:, h, :].astype(BF16))

    @pl.when(j == pl.num_programs(1) - 1)
    def _fin():
        for h in range(heads):
            update(h, kn_ref[:, h * LANES:(h + 1) * LANES], vn_ref[:, h * LANES:(h + 1) * LANES])
            o = acc_ref[h] / l_ref[h]
            o_ref[:, h * LANES:(h + 1) * LANES] = o[:lq] - lam_ref[0, 0] * o[lq:]


def _attn_sample(q, k_new, v_new, cache_k, cache_v, lam, *, l_seq, tk=1024):
    n, w = q.shape
    bsz, past, heads, _ = cache_k.shape
    tk = min(tk, past)
    assert n == bsz * l_seq and w == heads * LANES and past % tk == 0
    tok = pl.BlockSpec((l_seq, w), lambda b, j: (b, 0))
    cache = pl.BlockSpec((None, tk, heads, LANES), lambda b, j: (b, j, 0, 0))
    return pl.pallas_call(
        functools.partial(_attn_sample_kernel, heads=heads),
        grid=(bsz, past // tk),
        in_specs=[pl.BlockSpec(memory_space=pltpu.SMEM), tok, tok, tok, cache, cache],
        out_specs=tok,
        out_shape=jax.ShapeDtypeStruct((n, w), F32),
        scratch_shapes=[pltpu.VMEM((heads, 2 * l_seq, 1), F32), pltpu.VMEM((heads, 2 * l_seq, 1), F32),
                        pltpu.VMEM((heads, 2 * l_seq, LANES), F32)],
        compiler_params=pltpu.CompilerParams(dimension_semantics=("arbitrary", "arbitrary"),
                                             vmem_limit_bytes=40 * 1024 * 1024),
        name="attn_sample",
    )(lam, q, k_new, v_new, cache_k, cache_v)


def _head_norm(x, width):
    outs = []
    for g in range(x.shape[1] // width):
        xg = x[:, g * width:(g + 1) * width]
        outs.append(xg * lax.rsqrt(jnp.mean(xg * xg, axis=-1, keepdims=True) + EPS))
    return jnp.concatenate(outs, axis=1)


def _post_kernel(x_ref, oa_ref, gr_ref, ob_ref, sh1_ref, sc1_ref, gt1_ref, sh2_ref, sc2_ref,
                 n1_ref, n2_ref, ga_ref, gb_ref, wa_ref, wb_ref, wg_ref, wo_ref, x1_ref, h2_ref, *, diff_out_scale):
    x = x_ref[...]
    h = _modulated_norm(x, n1_ref[...], sc1_ref[0], sh1_ref[0]).astype(BF16)
    gates = jnp.dot(h, wg_ref[...], preferred_element_type=F32)
    gr = gr_ref[...]
    oa = _head_norm(oa_ref[...], GLA_DV) * ga_ref[...] * (gr * jax.nn.sigmoid(gr))
    ya = jnp.dot(oa.astype(BF16), wa_ref[...], preferred_element_type=F32)
    ob = _head_norm(ob_ref[...], DIFF_VD) * gb_ref[...] * diff_out_scale
    yb = jnp.dot(ob.astype(BF16), wb_ref[...], preferred_element_type=F32)
    merged = jax.nn.sigmoid(gates[:, :D_MODEL]) * ya + jax.nn.sigmoid(gates[:, D_MODEL:]) * yb
    mix = jnp.dot(merged.astype(BF16), wo_ref[...], preferred_element_type=F32)
    x1 = x + gt1_ref[0] * mix
    x1_ref[...] = x1
    h2_ref[...] = _modulated_norm(x1, n2_ref[...], sc2_ref[0], sh2_ref[0])


def _post_mixer(x, oa, gr, ob, mods, n1, n2, ga, gb, wa, wb, wg, wo, *, tt, mod_tiles, diff_out_scale):
    n = x.shape[0]
    assert n % tt == 0
    bmap = lambda i: (i // mod_tiles, 0, 0)
    mod = pl.BlockSpec((1,) + mods[0].shape[1:], bmap)
    row = pl.BlockSpec((tt, D_MODEL), lambda i: (i, 0))
    const = lambda a: pl.BlockSpec(a.shape, lambda i: (0,) * a.ndim)
    return pl.pallas_call(
        functools.partial(_post_kernel, diff_out_scale=diff_out_scale),
        grid=(n // tt,),
        in_specs=[row, row, row, row] + [mod] * 5 + [const(a) for a in (n1, n2, ga, gb, wa, wb, wg, wo)],
        out_specs=[row, row],
        out_shape=[jax.ShapeDtypeStruct((n, D_MODEL), F32), jax.ShapeDtypeStruct((n, D_MODEL), F32)],
        compiler_params=pltpu.CompilerParams(dimension_semantics=("arbitrary",), vmem_limit_bytes=48 * 1024 * 1024),
        name="post_mixer",
    )(x, oa, gr, ob, *mods, n1, n2, ga, gb, wa, wb, wg, wo)


def _topk_rows(s, k):
    r = s.shape[0]
    row = lax.broadcasted_iota(jnp.int32, s.shape, 0)
    vals, idxs = [], []
    for _ in range(k):
        mx = jnp.max(s, axis=0, keepdims=True)
        am = jnp.min(jnp.where(s == mx, row, r), axis=0, keepdims=True)
        vals.append(mx)
        idxs.append(am)
        s = jnp.where(row == am, -jnp.inf, s)
    return jnp.concatenate(vals, axis=0), jnp.concatenate(idxs, axis=0)


def _take_rows(tab, sel):
    out = jnp.zeros(sel.shape, tab.dtype)
    for a in range(tab.shape[0]):
        out = jnp.where(sel == a, tab[a:a + 1, :], out)
    return out


def _staircase():
    return [(a, b) for a in range(PEER_TOPK) for b in range(PEER_TOPK) if (a + 1) * (b + 1) <= PEER_TOPK]


def _peer_select_kernel(h2_ref, w_ref, k1_ref, k2_ref, idx_ref, g_ref):
    q = jnp.dot(h2_ref[...].astype(BF16), w_ref[...], preferred_element_type=F32)
    nt = (((1,), (1,)), ((), ()))
    s1 = lax.dot_general(k1_ref[...], q[:, :PEER_NKEYS].astype(BF16), nt, preferred_element_type=F32)
    s2 = lax.dot_general(k2_ref[...], q[:, PEER_NKEYS:].astype(BF16), nt, preferred_element_type=F32)
    v1, i1 = _topk_rows(s1, PEER_TOPK)
    v2, i2 = _topk_rows(s2, PEER_TOPK)
    pairs = _staircase()
    neg = jnp.full(((-len(pairs)) % SUBLANES, v1.shape[1]), -jnp.inf, F32)
    cand = jnp.concatenate([v1[a:a + 1, :] + v2[b:b + 1, :] for a, b in pairs] + [neg], axis=0)
    sc, cr = _topk_rows(cand, PEER_TOPK)
    ra = jnp.zeros(cr.shape, jnp.int32)
    rb = jnp.zeros(cr.shape, jnp.int32)
    for r, (a, b) in enumerate(pairs):
        hit = cr == r
        ra = jnp.where(hit, a, ra)
        rb = jnp.where(hit, b, rb)
    e1 = _take_rows(i1, ra)
    e2 = _take_rows(i2, rb)
    idx_ref[...] = e1 * PEER_NKEYS + e2
    p = jnp.exp(sc - sc[0:1, :])
    g_ref[...] = p / jnp.sum(p, axis=0, keepdims=True)


def _peer_select(h2, w_pq_bf, k1_bf, k2_bf, *, tt=SELECT_TILE):
    n, d = h2.shape
    assert n % tt == 0
    return pl.pallas_call(
        _peer_select_kernel,
        grid=(n // tt, PEER_HEADS),
        in_specs=[
            pl.BlockSpec((tt, d), lambda i, h: (i, 0)),
            pl.BlockSpec((d, PEER_DKEY), lambda i, h: (0, h)),
            pl.BlockSpec((PEER_NKEYS, PEER_DKEY // 2), lambda i, h: (0, 0)),
            pl.BlockSpec((PEER_NKEYS, PEER_DKEY // 2), lambda i, h: (0, 0)),
        ],
        out_specs=[pl.BlockSpec((PEER_TOPK, tt), lambda i, h: (h, i)),
                   pl.BlockSpec((PEER_TOPK, tt), lambda i, h: (h, i))],
        out_shape=[jax.ShapeDtypeStruct((NSEL, n), jnp.int32),
                   jax.ShapeDtypeStruct((NSEL, n), F32)],
        compiler_params=pltpu.CompilerParams(dimension_semantics=("arbitrary", "arbitrary")),
        name="peer_select",
    )(h2, w_pq_bf, k1_bf, k2_bf)


def _peer_gather_kernel(idx_ref, h2_ref, g_ref, x1_ref, gate_ref, tab_ref, out_ref, buf, sem, *, tt):
    s = pl.program_id(0)
    n = pl.num_programs(0) - 1

    @pl.when(s < n)
    def _issue():
        slot = s % 2

        def issue_tok(t, carry):
            for j in range(NSEL):
                e = idx_ref[t, j]
                pltpu.make_async_copy(tab_ref.at[e], buf.at[slot, t, pl.ds(j, 1), :], sem.at[slot, t]).start()
            return carry

        lax.fori_loop(0, tt, issue_tok, 0)

    @pl.when(s >= 1)
    def _compute():
        slot = (s + 1) % 2
        eye = (lax.broadcasted_iota(jnp.int32, (NSEL, NSEL), 0)
               == lax.broadcasted_iota(jnp.int32, (NSEL, NSEL), 1))
        gate = gate_ref[0]

        def tok(t, carry):
            for j in range(NSEL):
                pltpu.make_async_copy(tab_ref.at[0], buf.at[slot, t, pl.ds(j, 1), :], sem.at[slot, t]).wait()
            x = h2_ref[pl.ds(t, 1), :]
            p = buf[slot, t, :, 0:LANES] * x[:, 0:LANES]
            for c in range(1, D_MODEL // LANES):
                p = p + buf[slot, t, :, c * LANES:(c + 1) * LANES] * x[:, c * LANES:(c + 1) * LANES]
            a = jnp.sum(p, axis=-1, keepdims=True)
            act = 0.5 * a * (1.0 + lax.erf(a * (0.5 ** 0.5)))
            grow = g_ref[pl.ds(t, 1), :]
            gcol = jnp.sum(jnp.where(eye, grow, 0.0), axis=-1, keepdims=True)
            coef = gcol * act
            y = jnp.sum(coef * buf[slot, t, :, D_MODEL:2 * D_MODEL], axis=0, keepdims=True)
            out_ref[pl.ds(t, 1), :] = x1_ref[pl.ds(t, 1), :] + gate * y
            return carry

        lax.fori_loop(0, tt, tok, 0)


def _peer_gather(idx, g, h2, x1, *, gates, table, tt, l_seq, tok_offset=0):
    n_tok = idx.shape[0]
    n_tiles = n_tok // tt
    assert n_tiles * tt == n_tok and l_seq % tt == 0

    def cur(s):
        return jnp.maximum(s - 1, 0)

    return pl.pallas_call(
        functools.partial(_peer_gather_kernel, tt=tt),
        grid=(n_tiles + 1,),
        in_specs=[
            pl.BlockSpec((tt, NSEL), lambda s: (jnp.minimum(s, n_tiles - 1), 0), memory_space=pltpu.SMEM),
            pl.BlockSpec((tt, D_MODEL), lambda s: (cur(s), 0)),
            pl.BlockSpec((tt, NSEL), lambda s: (cur(s), 0)),
            pl.BlockSpec((tt, D_MODEL), lambda s: (cur(s), 0)),
            pl.BlockSpec((1, 1, D_MODEL), lambda s: ((tok_offset + cur(s) * tt) // l_seq, 0, 0)),
            pl.BlockSpec(memory_space=pl.ANY),
        ],
        out_specs=pl.BlockSpec((tt, D_MODEL), lambda s: (cur(s), 0)),
        out_shape=jax.ShapeDtypeStruct((n_tok, D_MODEL), F32),
        scratch_shapes=[pltpu.VMEM((2, tt, NSEL, 2 * D_MODEL), F32), pltpu.SemaphoreType.DMA((2, tt))],
        compiler_params=pltpu.CompilerParams(dimension_semantics=("arbitrary",),
                                             vmem_limit_bytes=48 * 1024 * 1024),
        name="peer_gather",
    )(idx, h2, g, x1, gates, table)


def _sc_chunk_copy(tab_hbm, idx_v, rows_v, sem, c):
    return pltpu.make_async_copy(tab_hbm.at[idx_v.at[pl.ds(c * SC_ROWS, SC_ROWS)]], rows_v.at[c % 2], sem.at[c % 2])


def _sc_udot(idx, x, table):
    ns = idx.shape[0]
    per_w = ns // SC_WORKERS
    assert per_w * SC_WORKERS == ns
    nchunk, nstrip, nk = NSEL // SC_ROWS, D_MODEL // SC_STRIP, SC_STRIP // SC_LANES

    @functools.partial(
        pl.kernel, mesh=plsc.VectorSubcoreMesh(core_axis_name="c", subcore_axis_name="s"),
        out_type=jax.ShapeDtypeStruct((ns, NSEL * SC_LANES), F32),
        scratch_types=[pltpu.VMEM((NSEL,), jnp.int32), pltpu.VMEM((D_MODEL,), F32),
                       pltpu.VMEM((2, SC_ROWS, D_MODEL), F32), pltpu.VMEM((NSEL * SC_LANES,), F32),
                       pltpu.SemaphoreType.DMA((2,))],
    )
    def k(idx_hbm, x_hbm, tab_hbm, out_hbm, idx_v, x_v, rows_v, acc_v, sem):
        base = (lax.axis_index("s") * 2 + lax.axis_index("c")) * per_w

        def token(i, carry):
            tok = base + i
            pltpu.sync_copy(idx_hbm.at[tok], idx_v)
            pltpu.sync_copy(x_hbm.at[tok], x_v)
            _sc_chunk_copy(tab_hbm, idx_v, rows_v, sem, 0).start()
            for c in range(nchunk):
                if c + 1 < nchunk:
                    _sc_chunk_copy(tab_hbm, idx_v, rows_v, sem, c + 1).start()
                _sc_chunk_copy(tab_hbm, idx_v, rows_v, sem, c).wait()
                for kc in range(nstrip):
                    xr = [x_v[pl.ds(kc * SC_STRIP + kk * SC_LANES, SC_LANES)] for kk in range(nk)]

                    def rows8(rg, cr, c=c, kc=kc, xr=xr):
                        for rr in range(SC_UGROUP):
                            r = rg * SC_UGROUP + rr
                            s = rows_v[c % 2, r, pl.ds(kc * SC_STRIP, SC_LANES)] * xr[0]
                            for kk in range(1, nk):
                                s = s + rows_v[c % 2, r, pl.ds(kc * SC_STRIP + kk * SC_LANES, SC_LANES)] * xr[kk]
                            dst = pl.ds((c * SC_ROWS + r) * SC_LANES, SC_LANES)
                            if kc == 0:
                                acc_v[dst] = s
                            else:
                                acc_v[dst] = acc_v[dst] + s
                        return cr

                    lax.fori_loop(0, SC_ROWS // SC_UGROUP, rows8, 0)
            pltpu.sync_copy(acc_v, out_hbm.at[tok])
            return carry

        lax.fori_loop(0, per_w, token, 0)

    return k(idx, x, table)


def _sc_vsum(idx, coefb, table):
    ns = idx.shape[0]
    per_w = ns // SC_WORKERS
    assert per_w * SC_WORKERS == ns
    nchunk, nstrip, nk = NSEL // SC_ROWS, D_MODEL // SC_STRIP, SC_STRIP // SC_LANES

    @functools.partial(
        pl.kernel, mesh=plsc.VectorSubcoreMesh(core_axis_name="c", subcore_axis_name="s"),
        out_type=jax.ShapeDtypeStruct((ns, D_MODEL), F32),
        scratch_types=[pltpu.VMEM((NSEL,), jnp.int32), pltpu.VMEM((NSEL * SC_LANES,), F32),
                       pltpu.VMEM((2, SC_ROWS, D_MODEL), F32), pltpu.VMEM((D_MODEL,), F32),
                       pltpu.SemaphoreType.DMA((2,))],
    )
    def k(idx_hbm, coef_hbm, tab_hbm, out_hbm, idx_v, coef_v, rows_v, y_v, sem):
        base = (lax.axis_index("s") * 2 + lax.axis_index("c")) * per_w

        def token(i, carry):
            tok = base + i
            pltpu.sync_copy(idx_hbm.at[tok], idx_v)
            pltpu.sync_copy(coef_hbm.at[tok], coef_v)
            _sc_chunk_copy(tab_hbm, idx_v, rows_v, sem, 0).start()
            for kk in range(D_MODEL // SC_LANES):
                y_v[pl.ds(kk * SC_LANES, SC_LANES)] = jnp.zeros((SC_LANES,), F32)
            for c in range(nchunk):
                if c + 1 < nchunk:
                    _sc_chunk_copy(tab_hbm, idx_v, rows_v, sem, c + 1).start()
                _sc_chunk_copy(tab_hbm, idx_v, rows_v, sem, c).wait()
                for kc in range(nstrip):
                    def rows8(rg, cr, c=c, kc=kc):
                        acc = [y_v[pl.ds(kc * SC_STRIP + kk * SC_LANES, SC_LANES)] for kk in range(nk)]
                        for rr in range(SC_RGROUP):
                            r = rg * SC_RGROUP + rr
                            cv = coef_v[pl.ds((c * SC_ROWS + r) * SC_LANES, SC_LANES)]
                            for kk in range(nk):
                                acc[kk] = acc[kk] + cv * rows_v[c % 2, r, pl.ds(kc * SC_STRIP + kk * SC_LANES, SC_LANES)]
                        for kk in range(nk):
                            y_v[pl.ds(kc * SC_STRIP + kk * SC_LANES, SC_LANES)] = acc[kk]
                        return cr

                    lax.fori_loop(0, SC_ROWS // SC_RGROUP, rows8, 0)
            pltpu.sync_copy(y_v, out_hbm.at[tok])
            return carry

        lax.fori_loop(0, per_w, token, 0)

    return k(idx, coefb, table)


def _sc_coef_kernel(a_ref, g_ref, after_ref, o_ref):
    hp = lax.Precision.HIGHEST
    grp = lax.broadcasted_iota(jnp.int32, (NSEL * SC_LANES, NSEL), 0) // SC_LANES
    fold = (grp == lax.broadcasted_iota(jnp.int32, (NSEL * SC_LANES, NSEL), 1)).astype(F32)
    a = jnp.dot(a_ref[...], fold, precision=hp, preferred_element_type=F32)
    coef = g_ref[...] * (0.5 * a * (1.0 + lax.erf(a * (0.5 ** 0.5))))
    o_ref[...] = lax.dot_general(coef, fold, (((1,), (1,)), ((), ())), precision=hp, preferred_element_type=F32)


def _sc_coef(a_part, g, after, *, tt=SC_SPLIT_BLOCK):
    n = a_part.shape[0]
    wide = pl.BlockSpec((tt, NSEL * SC_LANES), lambda i: (i, 0))
    return pl.pallas_call(
        _sc_coef_kernel, grid=(n // tt,),
        in_specs=[wide, pl.BlockSpec((tt, NSEL), lambda i: (i, 0)),
                  pl.BlockSpec((SUBLANES, D_MODEL), lambda i: (0, 0))], out_specs=wide,
        out_shape=jax.ShapeDtypeStruct((n, NSEL * SC_LANES), F32), name="sc_coef",
    )(a_part, g, after)


def _sc_residual_kernel(x1_ref, ff_ref, gate_ref, after_ref, o_ref):
    o_ref[...] = x1_ref[...] + gate_ref[0] * ff_ref[...]


def _sc_residual(x1, ff, gate, after, *, tok_offset, l_seq, tt=SC_SPLIT_BLOCK):
    n = x1.shape[0]
    row = pl.BlockSpec((tt, D_MODEL), lambda i: (i, 0))
    return pl.pallas_call(
        _sc_residual_kernel, grid=(n // tt,),
        in_specs=[row, row, pl.BlockSpec((1, 1, D_MODEL), lambda i: ((tok_offset + i * tt) // l_seq, 0, 0)),
                  pl.BlockSpec((SUBLANES, D_MODEL), lambda i: (0, 0))], out_specs=row,
        out_shape=jax.ShapeDtypeStruct((n, D_MODEL), F32), name="sc_residual",
    )(x1, ff, gate, after)


def _rope_tables(pos):
    half = DIFF_HD // 2
    inv = ROPE_THETA ** (-jnp.arange(half, dtype=F32) / half)
    ang = pos.astype(F32)[:, None] * inv[None, :]
    cos, sin = jnp.cos(ang), jnp.sin(ang)
    return jnp.tile(cos, (1, 4)), jnp.tile(jnp.concatenate([-sin, sin], axis=1), (1, 2))


def _mixers(x, mods, mod_tiles, tile, cos, sin, l_seq, h0, wts, attend, lambda_init, after):
    sh1, sc1, gt1, sh2, sc2 = mods
    q_g, k_g, v_g, r_g, la = _proj_gla(x, sh1, sc1, wts["n1"], wts["w_a"], wts["w_alpha"], wts["b_alpha"], after,
                                       tt=tile, mod_tiles=mod_tiles)
    q_bf, k_rows, k_bf, v_rows, v_bf = _proj_diff(x, sh1, sc1, wts["n1"], wts["w_b"], wts["q_gain"], wts["k_gain"],
                                                  cos, sin, wts["ones"], after, tt=tile, mod_tiles=mod_tiles)
    o_a, state = _gla(q_g, k_g, la, v_g, h0, l_seq=l_seq, tb=min(GLA_TILE, l_seq))
    o_b = attend(q_bf, k_bf, v_bf)
    post_tile = min(POST_TILE, tile)
    x1, h2 = _post_mixer(x, o_a, r_g, o_b, (sh1, sc1, gt1, sh2, sc2), wts["n1"], wts["n2"], wts["gla_gain"],
                         wts["diff_gain"], wts["w_gla_out"], wts["w_diff_out"], wts["w_g"], wts["w_out"],
                         tt=post_tile, mod_tiles=mod_tiles * (tile // post_tile), diff_out_scale=1.0 - lambda_init)
    idx_t, g_t = _peer_select(h2, wts["w_pq"], wts["k1"], wts["k2"])
    return x1, h2, idx_t.T, g_t.T, k_rows, v_rows, state


def kernel(x_prompt, x_sample, c_prompt, c_sample, cache_k, cache_v, state_gla, norm1_gain, norm2_gain, w_mod, b_mod, w_in, w_alpha, b_alpha, gla_gain, w_gla_out, q_gain, k_gain, lam_q1, lam_k1, lam_q2, lam_k2, diff_gain, w_diff_out, w_out, w_pq, sub_keys1, sub_keys2, peer_u, peer_v):
    bp, lp, _ = x_prompt.shape
    bs, ls, _ = x_sample.shape
    past = cache_k.shape[2]
    l = 0
    lambda_init = 0.8 - 0.6 * math.exp(-0.3 * l)
    nb = bp + bs
    c_all = jnp.concatenate([c_prompt, c_sample, jnp.zeros(((-nb) % 8, D_MODEL), F32)], axis=0)
    mod_all = _adaln(c_all, w_mod[l], b_mod[l])
    mods = [mod_all[:nb, i * D_MODEL:(i + 1) * D_MODEL] for i in range(6)]

    wi = w_in[l]
    o_glr = 2 * GLA_QK_W + 2 * GLA_V_W
    o_diff = o_glr + GLA_LOWRANK
    o_gate = o_diff + 2 * DIFF_QK_W + DIFF_V_W
    pad_lr = LANES - GLA_LOWRANK
    lane = jnp.arange(LANES)
    wts = dict(
        n1=norm1_gain[l][None], n2=norm2_gain[l][None],
        w_a=jnp.concatenate([wi[:, :o_diff], jnp.zeros((D_MODEL, pad_lr), F32)], axis=1).astype(BF16),
        w_alpha=jnp.concatenate([w_alpha[l], jnp.zeros((pad_lr, GLA_QK_W), F32)], axis=0).astype(BF16),
        b_alpha=b_alpha[l][None],
        w_b=wi[:, o_diff:o_gate].astype(BF16), w_g=wi[:, o_gate:].astype(BF16),
        q_gain=q_gain[l].reshape(1, LANES), k_gain=k_gain[l].reshape(1, LANES),
        ones=(lane[:, None] // DIFF_HD == lane[None, :] // DIFF_HD).astype(BF16),
        gla_gain=jnp.tile(gla_gain[l], GLA_HEADS)[None], diff_gain=jnp.tile(diff_gain[l], DIFF_HEADS)[None],
        w_gla_out=w_gla_out[l].astype(BF16), w_diff_out=w_diff_out[l].astype(BF16), w_out=w_out[l].astype(BF16),
        w_pq=w_pq[l].astype(BF16), k1=sub_keys1[l].astype(BF16), k2=sub_keys2[l].astype(BF16),
        table=jnp.concatenate([peer_u[l], peer_v[l]], axis=1).reshape(-1, 1, 2 * D_MODEL),
        u=peer_u[l], v=peer_v[l],
    )
    lam = (jnp.exp(jnp.sum(lam_q1[l] * lam_k1[l])) - jnp.exp(jnp.sum(lam_q2[l] * lam_k2[l])) + lambda_init).reshape(1, 1)

    no_order = jnp.zeros((SUBLANES, LANES), jnp.int32)
    gather = functools.partial(_peer_gather, table=wts["table"], tt=PEER_TOK_TILE)

    cos_p, sin_p = _rope_tables(jnp.arange(lp))

    def prompt_group(lo, hi, after):
        nsq = hi - lo
        attend = lambda q, k, v: _attn_prompt(q.reshape(nsq, lp, -1), k.reshape(nsq, lp, -1), v.reshape(nsq, lp, -1),
                                              lam).reshape(nsq * lp, -1)
        return _mixers(x_prompt[lo:hi].reshape(nsq * lp, D_MODEL), [m[lo:hi, None, :] for m in mods[:5]],
                       lp // PROJ_TILE, PROJ_TILE, cos_p, sin_p, lp, jnp.zeros((nsq, GLA_HEADS, GLA_DK, GLA_DV), F32),
                       wts, attend, lambda_init, after)

    n_sc_seq = bp // 2 if bp >= 4 else 0
    if n_sc_seq:
        x1a, h2a, idxa, ga, ka, va, sa = prompt_group(0, n_sc_seq, no_order)
        a_part = _sc_udot(idxa, h2a, wts["u"])
        x1b, h2b, idxb, gb, kb, vb, sb = prompt_group(n_sc_seq, bp, idxa[:SUBLANES])
        gates_b = mods[5][n_sc_seq:bp, None, :]
        n_first = round(x1b.shape[0] // SC_SPLIT_BLOCK * TC_FIRST_SHARE) * SC_SPLIT_BLOCK
        y_b1 = gather(idxb[:n_first], gb[:n_first], h2b[:n_first], x1b[:n_first], gates=gates_b, l_seq=lp)
        coefb = _sc_coef(a_part, ga, y_b1[:SUBLANES])
        ff = _sc_vsum(idxa, coefb, wts["v"])
        y_b2 = gather(idxb[n_first:], gb[n_first:], h2b[n_first:], x1b[n_first:], gates=gates_b, l_seq=lp,
                      tok_offset=n_first)
        y_a = _sc_residual(x1a, ff, mods[5][:n_sc_seq, None, :], y_b2[:SUBLANES], tok_offset=0, l_seq=lp)
        yp = jnp.concatenate([y_a, y_b1, y_b2], axis=0)
        kp, vp, sp = (jnp.concatenate(p, axis=0) for p in ((ka, kb), (va, vb), (sa, sb)))
    else:
        x1p, h2p, idxp, gp, kp, vp, sp = prompt_group(0, bp, no_order)
        yp = gather(idxp, gp, h2p, x1p, gates=mods[5][:bp, None, :], l_seq=lp)

    seq_per_tile = SAMPLE_TILE // ls
    cos_s, sin_s = _rope_tables(jnp.tile(past + jnp.arange(ls), seq_per_tile))
    mods_s = [jnp.repeat(m[bp:nb], ls, axis=0).reshape(-1, SAMPLE_TILE, D_MODEL) for m in mods[:5]]
    attend_s = lambda q, k, v: _attn_sample(q, k, v, cache_k[l], cache_v[l], lam, l_seq=ls)
    x1s, h2s, idxs, gs, ks, vs, ss = _mixers(x_sample.reshape(bs * ls, D_MODEL), mods_s, 1, SAMPLE_TILE, cos_s, sin_s,
                                             ls, state_gla[l], wts, attend_s, lambda_init, no_order)
    ys = gather(idxs, gs, h2s, x1s, gates=mods[5][bp:nb, None, :], l_seq=ls)

    return (yp.reshape(bp, lp, D_MODEL), ys.reshape(bs, ls, D_MODEL),
            kp.reshape(1, bp, lp, DIFF_HEADS, 2 * DIFF_HD), vp.reshape(1, bp, lp, DIFF_HEADS, DIFF_VD), sp[None],
            ks.reshape(1, bs, ls, DIFF_HEADS, 2 * DIFF_HD), vs.reshape(1, bs, ls, DIFF_HEADS, DIFF_VD), ss[None])
```

```python
import math
import functools
import jax, jax.numpy as jnp
from jax import lax
from jax.experimental import pallas as pl
from jax.experimental.pallas import tpu as pltpu
from jax.experimental.pallas import tpu_sc as plsc

D_MODEL = 1024
CHUNK = 64
EPS = 1e-6
GLA_HEADS = 4
GLA_DK = 128
GLA_DV = 256
GLA_LOWRANK = 16
GLA_TAU = 16.0
GLA_BLOCK = 16
DIFF_HEADS = 8
DIFF_HD = 64
DIFF_VD = 128
ROPE_THETA = 10000.0
PEER_HEADS = 8
PEER_NKEYS = 128
PEER_DKEY = 256
PEER_TOPK = 16
GLA_QK_W = GLA_HEADS * GLA_DK
GLA_V_W = GLA_HEADS * GLA_DV
DIFF_QK_W = DIFF_HEADS * 2 * DIFF_HD
DIFF_V_W = DIFF_HEADS * DIFF_VD
F32 = jnp.float32
BF16 = jnp.bfloat16
NSEL = PEER_HEADS * PEER_TOPK
LANES = 128
SUBLANES = 8

PROJ_TILE = 512
POST_TILE = 256
GLA_TILE = 256
SAMPLE_TILE = 256
PEER_TOK_TILE = 8
ATTN_NEG = -1e30
ATTN_TILE = 1024
SELECT_TILE = 1024
SC_LANES = 16
SC_WORKERS = 32
SC_ROWS = 32
SC_STRIP = 256
SC_RGROUP = 8
SC_UGROUP = 1
SC_SPLIT_BLOCK = 512
TC_FIRST_SHARE = 17 / 32


def _mod_kernel(c_ref, w_ref, b_ref, o_ref):
    c = c_ref[...]
    s = c * jax.nn.sigmoid(c)
    o_ref[...] = jnp.dot(s.astype(BF16), w_ref[...].astype(BF16), preferred_element_type=F32) + b_ref[...]


def _adaln(c, w_mod, b_mod):
    n = c.shape[0]
    tn = 1536
    return pl.pallas_call(
        _mod_kernel,
        grid=(6 * D_MODEL // tn,),
        in_specs=[pl.BlockSpec((n, D_MODEL), lambda j: (0, 0)),
                  pl.BlockSpec((D_MODEL, tn), lambda j: (0, j)),
                  pl.BlockSpec((1, tn), lambda j: (0, j))],
        out_specs=pl.BlockSpec((n, tn), lambda j: (0, j)),
        out_shape=jax.ShapeDtypeStruct((n, 6 * D_MODEL), F32),
        name="adaln",
    )(c, w_mod, b_mod.reshape(1, -1))


def _modulated_norm(x, gain, scale, shift):
    y = x * lax.rsqrt(jnp.mean(x * x, axis=-1, keepdims=True) + EPS)
    return (y * gain) * (1.0 + scale) + shift


def _proj_gla_kernel(x_ref, sh_ref, sc_ref, gain_ref, w_ref, wa_ref, ba_ref, after_ref,
                     q_ref, k_ref, v_ref, r_ref, la_ref):
    h = _modulated_norm(x_ref[...], gain_ref[...], sc_ref[0], sh_ref[0]).astype(BF16)
    z = jnp.dot(h, w_ref[...], preferred_element_type=F32)
    q_ref[...] = z[:, 0:GLA_QK_W] * (GLA_DK ** -0.5)
    k_ref[...] = z[:, GLA_QK_W:2 * GLA_QK_W]
    v_ref[...] = z[:, 2 * GLA_QK_W:2 * GLA_QK_W + GLA_V_W]
    r_ref[...] = z[:, 2 * GLA_QK_W + GLA_V_W:2 * GLA_QK_W + 2 * GLA_V_W]
    glr = z[:, 2 * GLA_QK_W + 2 * GLA_V_W:]
    a = jnp.dot(glr.astype(BF16), wa_ref[...], preferred_element_type=F32) + ba_ref[...]
    la_ref[...] = jax.nn.log_sigmoid(a) * (1.0 / GLA_TAU)


def _group_mean_square(x, ones_ref):
    s = x * x
    hi = s.astype(BF16)
    lo = (s - hi.astype(F32)).astype(BF16)
    ss = (jnp.dot(hi, ones_ref[...], preferred_element_type=F32)
          + jnp.dot(lo, ones_ref[...], preferred_element_type=F32))
    return ss * (1.0 / DIFF_HD)


def _norm_rope(x, gain, cos, sin_signed, ones_ref):
    outs = []
    lane = lax.broadcasted_iota(jnp.int32, (x.shape[0], LANES), 1)
    first_half = (lane % DIFF_HD) < (DIFF_HD // 2)
    for b in range(DIFF_QK_W // LANES):
        xb = x[:, b * LANES:(b + 1) * LANES]
        y = xb * lax.rsqrt(_group_mean_square(xb, ones_ref) + EPS) * gain
        rot = jnp.where(first_half, pltpu.roll(y, LANES - DIFF_HD // 2, axis=1), pltpu.roll(y, DIFF_HD // 2, axis=1))
        outs.append(y * cos + rot * sin_signed)
    return jnp.concatenate(outs, axis=1)


def _proj_diff_kernel(x_ref, sh_ref, sc_ref, gain_ref, w_ref, qg_ref, kg_ref, cos_ref, sin_ref, ones_ref, after_ref,
                      qb_ref, k_ref, kb_ref, v_ref, vb_ref, *, q_scale):
    h = _modulated_norm(x_ref[...], gain_ref[...], sc_ref[0], sh_ref[0]).astype(BF16)
    z = jnp.dot(h, w_ref[...], preferred_element_type=F32)
    cos, sin = cos_ref[...], sin_ref[...]
    q = _norm_rope(z[:, 0:DIFF_QK_W], qg_ref[...], cos, sin, ones_ref)
    qb_ref[...] = (q * q_scale).astype(BF16)
    k = _norm_rope(z[:, DIFF_QK_W:2 * DIFF_QK_W], kg_ref[...], cos, sin, ones_ref)
    k_ref[...] = k
    kb_ref[...] = k.astype(BF16)
    v = z[:, 2 * DIFF_QK_W:2 * DIFF_QK_W + DIFF_V_W]
    v_ref[...] = v
    vb_ref[...] = v.astype(BF16)


def _order_spec():
    return pl.BlockSpec((SUBLANES, LANES), lambda i: (0, 0))


def _proj_gla(x, shift, scale, gain, w_a, w_alpha_pad, b_alpha, after, *, tt, mod_tiles):
    n = x.shape[0]
    assert n % tt == 0
    bmap = lambda i: (i // mod_tiles, 0, 0)
    mod = pl.BlockSpec((1,) + shift.shape[1:], bmap)
    row = lambda w: pl.BlockSpec((tt, w), lambda i: (i, 0))
    const = lambda a: pl.BlockSpec(a.shape, lambda i: (0,) * a.ndim)
    return pl.pallas_call(
        _proj_gla_kernel,
        grid=(n // tt,),
        in_specs=[row(D_MODEL), mod, mod,
                  const(gain), const(w_a), const(w_alpha_pad), const(b_alpha), _order_spec()],
        out_specs=[row(GLA_QK_W), row(GLA_QK_W), row(GLA_V_W), row(GLA_V_W), row(GLA_QK_W)],
        out_shape=[jax.ShapeDtypeStruct((n, GLA_QK_W), F32), jax.ShapeDtypeStruct((n, GLA_QK_W), F32),
                   jax.ShapeDtypeStruct((n, GLA_V_W), F32), jax.ShapeDtypeStruct((n, GLA_V_W), F32),
                   jax.ShapeDtypeStruct((n, GLA_QK_W), F32)],
        compiler_params=pltpu.CompilerParams(dimension_semantics=("arbitrary",), vmem_limit_bytes=48 * 1024 * 1024),
        name="proj_gla",
    )(x, shift, scale, gain, w_a, w_alpha_pad, b_alpha, after)


def _proj_diff(x, shift, scale, gain, w_b, q_gain128, k_gain128, cos128, sin128, ones128, after, *, tt, mod_tiles):
    n = x.shape[0]
    assert n % tt == 0 and cos128.shape[0] % tt == 0
    bmap = lambda i: (i // mod_tiles, 0, 0)
    mod = pl.BlockSpec((1,) + shift.shape[1:], bmap)
    pos_tiles = cos128.shape[0] // tt
    row = lambda w: pl.BlockSpec((tt, w), lambda i: (i, 0))
    const = lambda a: pl.BlockSpec(a.shape, lambda i: (0,) * a.ndim)
    pos = pl.BlockSpec((tt, LANES), lambda i: (i % pos_tiles, 0))
    q_scale = math.log2(math.e) * DIFF_HD ** -0.5
    return pl.pallas_call(
        functools.partial(_proj_diff_kernel, q_scale=q_scale),
        grid=(n // tt,),
        in_specs=[row(D_MODEL), mod, mod,
                  const(gain), const(w_b), const(q_gain128), const(k_gain128), pos, pos, const(ones128),
                  _order_spec()],
        out_specs=[row(DIFF_QK_W)] * 5,
        out_shape=[jax.ShapeDtypeStruct((n, DIFF_QK_W), BF16), jax.ShapeDtypeStruct((n, DIFF_QK_W), F32),
                   jax.ShapeDtypeStruct((n, DIFF_QK_W), BF16), jax.ShapeDtypeStruct((n, DIFF_V_W), F32),
                   jax.ShapeDtypeStruct((n, DIFF_V_W), BF16)],
        compiler_params=pltpu.CompilerParams(dimension_semantics=("arbitrary",), vmem_limit_bytes=48 * 1024 * 1024),
        name="proj_diff",
    )(x, shift, scale, gain, w_b, q_gain128, k_gain128, cos128, sin128, ones128, after)


def _gla_kernel(q_ref, k_ref, la_ref, v_ref, h0_ref, o_ref, hT_ref, st_ref, *, tb):
    i = pl.program_id(2)
    nb = tb // GLA_BLOCK

    @pl.when(i == 0)
    def _init():
        st_ref[...] = h0_ref[0, 0].T

    row = lax.broadcasted_iota(jnp.int32, (tb, tb), 0)
    col = lax.broadcasted_iota(jnp.int32, (tb, tb), 1)
    same = (row // GLA_BLOCK) == (col // GLA_BLOCK)
    causal = same & (col <= row)
    la = la_ref[...]
    hp = lax.Precision.HIGHEST
    b = jnp.dot(causal.astype(F32), la, precision=hp, preferred_element_type=F32)
    b_last = jnp.dot(same.astype(F32), la, precision=hp, preferred_element_type=F32)
    q, k, v = q_ref[...], k_ref[...], v_ref[...]
    qe = (q * jnp.exp(b)).astype(BF16)
    ke = (k * jnp.exp(-b)).astype(BF16)
    kd = (k * jnp.exp(b_last - b)).astype(BF16)
    vb = v.astype(BF16)
    a = lax.dot_general(qe, ke, (((1,), (1,)), ((), ())), preferred_element_type=F32)
    a = jnp.where(causal, a, 0.0).astype(BF16)
    o_intra = jnp.dot(a, vb, preferred_element_type=F32)

    st = st_ref[...]
    outs = []
    for n in range(nb):
        rs = slice(n * GLA_BLOCK, (n + 1) * GLA_BLOCK)
        o_inter = lax.dot_general(qe[rs], st.astype(BF16), (((1,), (1,)), ((), ())), preferred_element_type=F32)
        outs.append(o_inter + o_intra[rs])
        u = lax.dot_general(vb[rs], kd[rs], (((0,), (0,)), ((), ())), preferred_element_type=F32)
        st = jnp.exp(b_last[n * GLA_BLOCK:n * GLA_BLOCK + 1, :]) * st + u
    o_ref[...] = jnp.concatenate(outs, axis=0)
    st_ref[...] = st

    @pl.when(i == pl.num_programs(2) - 1)
    def _fin():
        hT_ref[0, 0] = st.T


def _gla(q, k, la, v, h0, *, l_seq, tb):
    n = q.shape[0]
    bsz = n // l_seq
    nt = l_seq // tb
    assert l_seq % tb == 0 and tb % GLA_BLOCK == 0
    tok = lambda w: pl.BlockSpec((tb, w), lambda b, h, i: (b * nt + i, h))
    st = pl.BlockSpec((1, 1, GLA_DK, GLA_DV), lambda b, h, i: (b, h, 0, 0))
    return pl.pallas_call(
        functools.partial(_gla_kernel, tb=tb),
        grid=(bsz, GLA_HEADS, nt),
        in_specs=[tok(GLA_DK), tok(GLA_DK), tok(GLA_DK), tok(GLA_DV), st],
        out_specs=[tok(GLA_DV), st],
        out_shape=[jax.ShapeDtypeStruct((n, GLA_V_W), F32),
                   jax.ShapeDtypeStruct((bsz, GLA_HEADS, GLA_DK, GLA_DV), F32)],
        scratch_shapes=[pltpu.VMEM((GLA_DV, GLA_DK), F32)],
        compiler_params=pltpu.CompilerParams(dimension_semantics=("arbitrary", "arbitrary", "arbitrary")),
        name="gla",
    )(q, k, la, v, h0)


def _split_maps(q):
    lane = lax.broadcasted_iota(jnp.int32, q.shape, 1)
    zero = jnp.zeros_like(q)
    return jnp.concatenate([jnp.where(lane < DIFF_HD, q, zero), jnp.where(lane >= DIFF_HD, q, zero)], axis=0)


def _attn_prompt_kernel(lam_ref, q_ref, k_ref, v_ref, o_ref, vt_ref, *, tq, tk):
    i = pl.program_id(2)
    n_kt = v_ref.shape[1] // tk

    @pl.when(i == 0)
    def _transpose_v():
        def body(j, c):
            vt_ref[j] = v_ref[0, pl.ds(j * tk, tk), :].astype(F32).T.astype(BF16)
            return c
        lax.fori_loop(0, n_kt, body, 0)

    qp = _split_maps(q_ref[0])

    def step(j, carry, masked, koff=0):
        m, l, acc = carry
        kt = k_ref[0, pl.ds(j * tk, tk), :]
        st = lax.dot_general(kt, qp, (((1,), (1,)), ((), ())), preferred_element_type=F32)
        if masked:
            kc = (lax.broadcasted_iota(jnp.int32, st.shape, 0) + koff) // CHUNK
            qc = (lax.broadcasted_iota(jnp.int32, st.shape, 1) % tq) // CHUNK
            st = jnp.where(qc >= kc, st, ATTN_NEG)
        m_new = jnp.maximum(m, jnp.max(st, axis=0, keepdims=True))
        alpha = jnp.exp2(m - m_new)
        p = jnp.exp2(st - m_new)
        l = alpha * l + jnp.sum(p, axis=0, keepdims=True)
        acc = alpha * acc + jnp.dot(vt_ref[j], p.astype(BF16), preferred_element_type=F32)
        return m_new, l, acc

    carry = (jnp.full((1, 2 * tq), ATTN_NEG, F32), jnp.zeros((1, 2 * tq), F32), jnp.zeros((LANES, 2 * tq), F32))
    r = tq // tk
    carry = lax.fori_loop(0, i * r, functools.partial(step, masked=False), carry)
    for d in range(r):
        carry = step(i * r + d, carry, True, koff=d * tk)
    m, l, acc = carry
    o = acc / l
    ot = o[:, :tq] - lam_ref[0, 0] * o[:, tq:]
    o_ref[0] = ot.T


def _attn_prompt(q, k, v, lam, *, tq=ATTN_TILE, tk=ATTN_TILE):
    b, l, w = q.shape
    h = w // LANES
    assert l % tq == 0 and tq % tk == 0 and tk % CHUNK == 0
    return pl.pallas_call(
        functools.partial(_attn_prompt_kernel, tq=tq, tk=tk),
        grid=(b, h, l // tq),
        in_specs=[
            pl.BlockSpec(memory_space=pltpu.SMEM),
            pl.BlockSpec((1, tq, LANES), lambda bi, hi, i: (bi, i, hi)),
            pl.BlockSpec((1, l, LANES), lambda bi, hi, i: (bi, 0, hi)),
            pl.BlockSpec((1, l, LANES), lambda bi, hi, i: (bi, 0, hi)),
        ],
        out_specs=pl.BlockSpec((1, tq, LANES), lambda bi, hi, i: (bi, i, hi)),
        out_shape=jax.ShapeDtypeStruct((b, l, w), F32),
        scratch_shapes=[pltpu.VMEM((l // tk, LANES, tk), BF16)],
        compiler_params=pltpu.CompilerParams(dimension_semantics=("arbitrary", "arbitrary", "arbitrary"),
                                             vmem_limit_bytes=40 * 1024 * 1024),
        name="attn_prompt",
    )(lam, q, k, v)


def _attn_sample_kernel(lam_ref, q_ref, kn_ref, vn_ref, kc_ref, vc_ref, o_ref, m_ref, l_ref, acc_ref, *, heads):
    j = pl.program_id(1)
    lq = q_ref.shape[0]
    nt = (((1,), (1,)), ((), ()))

    @pl.when(j == 0)
    def _init():
        m_ref[...] = jnp.full(m_ref.shape, ATTN_NEG, F32)
        l_ref[...] = jnp.zeros(l_ref.shape, F32)
        acc_ref[...] = jnp.zeros(acc_ref.shape, F32)

    def update(h, k, v):
        qp = _split_maps(q_ref[:, h * LANES:(h + 1) * LANES])
        s = lax.dot_general(qp, k, nt, preferred_element_type=F32)
        m_old = m_ref[h]
        m_new = jnp.maximum(m_old, jnp.max(s, axis=-1, keepdims=True))
        alpha = jnp.exp2(m_old - m_new)
        p = jnp.exp2(s - m_new)
        l_ref[h] = alpha * l_ref[h] + jnp.sum(p, axis=-1, keepdims=True)
        acc_ref[h] = alpha * acc_ref[h] + jnp.dot(p.astype(BF16), v, preferred_element_type=F32)
        m_ref[h] = m_new

    for h in range(heads):
        update(h, kc_ref[:, h, :].astype(BF16), vc_ref[:, h, :].astype(BF16))

    @pl.when(j == pl.num_programs(1) - 1)
    def _fin():
        for h in range(heads):
            update(h, kn_ref[:, h * LANES:(h + 1) * LANES], vn_ref[:, h * LANES:(h + 1) * LANES])
            o = acc_ref[h] / l_ref[h]
            o_ref[:, h * LANES:(h + 1) * LANES] = o[:lq] - lam_ref[0, 0] * o[lq:]


def _attn_sample(q, k_new, v_new, cache_k, cache_v, lam, *, l_seq, tk=1024):
    n, w = q.shape
    bsz, past, heads, _ = cache_k.shape
    tk = min(tk, past)
    assert n == bsz * l_seq and w == heads * LANES and past % tk == 0
    tok = pl.BlockSpec((l_seq, w), lambda b, j: (b, 0))
    cache = pl.BlockSpec((None, tk, heads, LANES), lambda b, j: (b, j, 0, 0))
    return pl.pallas_call(
        functools.partial(_attn_sample_kernel, heads=heads),
        grid=(bsz, past // tk),
        in_specs=[pl.BlockSpec(memory_space=pltpu.SMEM), tok, tok, tok, cache, cache],
        out_specs=tok,
        out_shape=jax.ShapeDtypeStruct((n, w), F32),
        scratch_shapes=[pltpu.VMEM((heads, 2 * l_seq, 1), F32), pltpu.VMEM((heads, 2 * l_seq, 1), F32),
                        pltpu.VMEM((heads, 2 * l_seq, LANES), F32)],
        compiler_params=pltpu.CompilerParams(dimension_semantics=("arbitrary", "arbitrary"),
                                             vmem_limit_bytes=40 * 1024 * 1024),
        name="attn_sample",
    )(lam, q, k_new, v_new, cache_k, cache_v)


def _head_norm(x, width):
    outs = []
    for g in range(x.shape[1] // width):
        xg = x[:, g * width:(g + 1) * width]
        outs.append(xg * lax.rsqrt(jnp.mean(xg * xg, axis=-1, keepdims=True) + EPS))
    return jnp.concatenate(outs, axis=1)


def _post_kernel(x_ref, oa_ref, gr_ref, ob_ref, sh1_ref, sc1_ref, gt1_ref, sh2_ref, sc2_ref,
                 n1_ref, n2_ref, ga_ref, gb_ref, wa_ref, wb_ref, wg_ref, wo_ref, x1_ref, h2_ref, *, diff_out_scale):
    x = x_ref[...]
    h = _modulated_norm(x, n1_ref[...], sc1_ref[0], sh1_ref[0]).astype(BF16)
    gates = jnp.dot(h, wg_ref[...], preferred_element_type=F32)
    gr = gr_ref[...]
    oa = _head_norm(oa_ref[...], GLA_DV) * ga_ref[...] * (gr * jax.nn.sigmoid(gr))
    ya = jnp.dot(oa.astype(BF16), wa_ref[...], preferred_element_type=F32)
    ob = _head_norm(ob_ref[...], DIFF_VD) * gb_ref[...] * diff_out_scale
    yb = jnp.dot(ob.astype(BF16), wb_ref[...], preferred_element_type=F32)
    merged = jax.nn.sigmoid(gates[:, :D_MODEL]) * ya + jax.nn.sigmoid(gates[:, D_MODEL:]) * yb
    mix = jnp.dot(merged.astype(BF16), wo_ref[...], preferred_element_type=F32)
    x1 = x + gt1_ref[0] * mix
    x1_ref[...] = x1
    h2_ref[...] = _modulated_norm(x1, n2_ref[...], sc2_ref[0], sh2_ref[0])


def _post_mixer(x, oa, gr, ob, mods, n1, n2, ga, gb, wa, wb, wg, wo, *, tt, mod_tiles, diff_out_scale):
    n = x.shape[0]
    assert n % tt == 0
    bmap = lambda i: (i // mod_tiles, 0, 0)
    mod = pl.BlockSpec((1,) + mods[0].shape[1:], bmap)
    row = pl.BlockSpec((tt, D_MODEL), lambda i: (i, 0))
    const = lambda a: pl.BlockSpec(a.shape, lambda i: (0,) * a.ndim)
    return pl.pallas_call(
        functools.partial(_post_kernel, diff_out_scale=diff_out_scale),
        grid=(n // tt,),
        in_specs=[row, row, row, row] + [mod] * 5 + [const(a) for a in (n1, n2, ga, gb, wa, wb, wg, wo)],
        out_specs=[row, row],
        out_shape=[jax.ShapeDtypeStruct((n, D_MODEL), F32), jax.ShapeDtypeStruct((n, D_MODEL), F32)],
        compiler_params=pltpu.CompilerParams(dimension_semantics=("arbitrary",), vmem_limit_bytes=48 * 1024 * 1024),
        name="post_mixer",
    )(x, oa, gr, ob, *mods, n1, n2, ga, gb, wa, wb, wg, wo)


def _topk_rows(s, k):
    r = s.shape[0]
    row = lax.broadcasted_iota(jnp.int32, s.shape, 0)
    vals, idxs = [], []
    for _ in range(k):
        mx = jnp.max(s, axis=0, keepdims=True)
        am = jnp.min(jnp.where(s == mx, row, r), axis=0, keepdims=True)
        vals.append(mx)
        idxs.append(am)
        s = jnp.where(row == am, -jnp.inf, s)
    return jnp.concatenate(vals, axis=0), jnp.concatenate(idxs, axis=0)


def _take_rows(tab, sel):
    out = jnp.zeros(sel.shape, tab.dtype)
    for a in range(tab.shape[0]):
        out = jnp.where(sel == a, tab[a:a + 1, :], out)
    return out


def _staircase():
    return [(a, b) for a in range(PEER_TOPK) for b in range(PEER_TOPK) if (a + 1) * (b + 1) <= PEER_TOPK]


def _peer_select_kernel(h2_ref, w_ref, k1_ref, k2_ref, idx_ref, g_ref, idx_acc, g_acc):
    q = jnp.dot(h2_ref[...].astype(BF16), w_ref[...], preferred_element_type=F32)
    nt = (((1,), (1,)), ((), ()))
    s1 = lax.dot_general(k1_ref[...], q[:, :PEER_NKEYS].astype(BF16), nt, preferred_element_type=F32)
    s2 = lax.dot_general(k2_ref[...], q[:, PEER_NKEYS:].astype(BF16), nt, preferred_element_type=F32)
    v1, i1 = _topk_rows(s1, PEER_TOPK)
    v2, i2 = _topk_rows(s2, PEER_TOPK)
    pairs = _staircase()
    neg = jnp.full(((-len(pairs)) % SUBLANES, v1.shape[1]), -jnp.inf, F32)
    cand = jnp.concatenate([v1[a:a + 1, :] + v2[b:b + 1, :] for a, b in pairs] + [neg], axis=0)
    sc, cr = _topk_rows(cand, PEER_TOPK)
    ra = jnp.zeros(cr.shape, jnp.int32)
    rb = jnp.zeros(cr.shape, jnp.int32)
    for r, (a, b) in enumerate(pairs):
        hit = cr == r
        ra = jnp.where(hit, a, ra)
        rb = jnp.where(hit, b, rb)
    e1 = _take_rows(i1, ra)
    e2 = _take_rows(i2, rb)
    p = jnp.exp(sc - sc[0:1, :])
    h = pl.program_id(1)
    rows = pl.ds(pl.multiple_of(h * PEER_TOPK, PEER_TOPK), PEER_TOPK)
    idx_acc[rows, :] = (e1 * PEER_NKEYS + e2).astype(F32)
    g_acc[rows, :] = p / jnp.sum(p, axis=0, keepdims=True)

    @pl.when(h == pl.num_programs(1) - 1)
    def _emit():
        idx_ref[...] = idx_acc[...].T.astype(jnp.int32)
        g_ref[...] = g_acc[...].T


def _peer_select(h2, w_pq_bf, k1_bf, k2_bf, *, tt=SELECT_TILE):
    n, d = h2.shape
    assert n % tt == 0
    return pl.pallas_call(
        _peer_select_kernel,
        grid=(n // tt, PEER_HEADS),
        in_specs=[
            pl.BlockSpec((tt, d), lambda i, h: (i, 0)),
            pl.BlockSpec((d, PEER_DKEY), lambda i, h: (0, h)),
            pl.BlockSpec((PEER_NKEYS, PEER_DKEY // 2), lambda i, h: (0, 0)),
            pl.BlockSpec((PEER_NKEYS, PEER_DKEY // 2), lambda i, h: (0, 0)),
        ],
        out_specs=[pl.BlockSpec((tt, NSEL), lambda i, h: (i, 0)),
                   pl.BlockSpec((tt, NSEL), lambda i, h: (i, 0))],
        out_shape=[jax.ShapeDtypeStruct((n, NSEL), jnp.int32),
                   jax.ShapeDtypeStruct((n, NSEL), F32)],
        scratch_shapes=[pltpu.VMEM((NSEL, tt), F32), pltpu.VMEM((NSEL, tt), F32)],
        compiler_params=pltpu.CompilerParams(dimension_semantics=("arbitrary", "arbitrary")),
        name="peer_select",
    )(h2, w_pq_bf, k1_bf, k2_bf)


def _peer_gather_kernel(idx_ref, h2_ref, g_ref, x1_ref, gate_ref, tab_ref, out_ref, buf, sem, *, tt):
    s = pl.program_id(0)
    n = pl.num_programs(0) - 1

    @pl.when(s < n)
    def _issue():
        slot = s % 2

        def issue_tok(t, carry):
            for j in range(NSEL):
                e = idx_ref[t, j]
                pltpu.make_async_copy(tab_ref.at[e], buf.at[slot, t, pl.ds(j, 1), :], sem.at[slot, t]).start()
            return carry

        lax.fori_loop(0, tt, issue_tok, 0)

    @pl.when(s >= 1)
    def _compute():
        slot = (s + 1) % 2
        eye = (lax.broadcasted_iota(jnp.int32, (NSEL, NSEL), 0)
               == lax.broadcasted_iota(jnp.int32, (NSEL, NSEL), 1))
        gate = gate_ref[0]

        def tok(t, carry):
            for j in range(NSEL):
                pltpu.make_async_copy(tab_ref.at[0], buf.at[slot, t, pl.ds(j, 1), :], sem.at[slot, t]).wait()
            x = h2_ref[pl.ds(t, 1), :]
            p = buf[slot, t, :, 0:LANES] * x[:, 0:LANES]
            for c in range(1, D_MODEL // LANES):
                p = p + buf[slot, t, :, c * LANES:(c + 1) * LANES] * x[:, c * LANES:(c + 1) * LANES]
            a = jnp.sum(p, axis=-1, keepdims=True)
            act = 0.5 * a * (1.0 + lax.erf(a * (0.5 ** 0.5)))
            grow = g_ref[pl.ds(t, 1), :]
            gcol = jnp.sum(jnp.where(eye, grow, 0.0), axis=-1, keepdims=True)
            coef = gcol * act
            y = jnp.sum(coef * buf[slot, t, :, D_MODEL:2 * D_MODEL], axis=0, keepdims=True)
            out_ref[pl.ds(t, 1), :] = x1_ref[pl.ds(t, 1), :] + gate * y
            return carry

        lax.fori_loop(0, tt, tok, 0)


def _peer_gather(idx, g, h2, x1, *, gates, table, tt, l_seq, row0=0, n_tok=None):
    n_tok = idx.shape[0] if n_tok is None else n_tok
    n_tiles = n_tok // tt
    base = row0 // tt
    assert n_tiles * tt == n_tok and base * tt == row0 and l_seq % tt == 0

    def cur(s):
        return jnp.maximum(s - 1, 0)

    return pl.pallas_call(
        functools.partial(_peer_gather_kernel, tt=tt),
        grid=(n_tiles + 1,),
        in_specs=[
            pl.BlockSpec((tt, NSEL), lambda s: (base + jnp.minimum(s, n_tiles - 1), 0), memory_space=pltpu.SMEM),
            pl.BlockSpec((tt, D_MODEL), lambda s: (base + cur(s), 0)),
            pl.BlockSpec((tt, NSEL), lambda s: (base + cur(s), 0)),
            pl.BlockSpec((tt, D_MODEL), lambda s: (base + cur(s), 0)),
            pl.BlockSpec((1, 1, D_MODEL), lambda s: ((row0 + cur(s) * tt) // l_seq, 0, 0)),
            pl.BlockSpec(memory_space=pl.ANY),
        ],
        out_specs=pl.BlockSpec((tt, D_MODEL), lambda s: (cur(s), 0)),
        out_shape=jax.ShapeDtypeStruct((n_tok, D_MODEL), F32),
        scratch_shapes=[pltpu.VMEM((2, tt, NSEL, 2 * D_MODEL), F32), pltpu.SemaphoreType.DMA((2, tt))],
        compiler_params=pltpu.CompilerParams(dimension_semantics=("arbitrary",),
                                             vmem_limit_bytes=48 * 1024 * 1024),
        name="peer_gather",
    )(idx, h2, g, x1, gates, table)


def _sc_chunk_copy(tab_hbm, idx_v, rows_v, sem, c):
    return pltpu.make_async_copy(tab_hbm.at[idx_v.at[pl.ds(c * SC_ROWS, SC_ROWS)]], rows_v.at[c % 2], sem.at[c % 2])


def _sc_udot(idx, x, table):
    ns = idx.shape[0]
    per_w = ns // SC_WORKERS
    assert per_w * SC_WORKERS == ns
    nchunk, nstrip, nk = NSEL // SC_ROWS, D_MODEL // SC_STRIP, SC_STRIP // SC_LANES

    @functools.partial(
        pl.kernel, mesh=plsc.VectorSubcoreMesh(core_axis_name="c", subcore_axis_name="s"),
        out_type=jax.ShapeDtypeStruct((ns, NSEL * SC_LANES), F32),
        scratch_types=[pltpu.VMEM((NSEL,), jnp.int32), pltpu.VMEM((D_MODEL,), F32),
                       pltpu.VMEM((2, SC_ROWS, D_MODEL), F32), pltpu.VMEM((NSEL * SC_LANES,), F32),
                       pltpu.SemaphoreType.DMA((2,))],
    )
    def k(idx_hbm, x_hbm, tab_hbm, out_hbm, idx_v, x_v, rows_v, acc_v, sem):
        base = (lax.axis_index("s") * 2 + lax.axis_index("c")) * per_w

        def token(i, carry):
            tok = base + i
            pltpu.sync_copy(idx_hbm.at[tok], idx_v)
            pltpu.sync_copy(x_hbm.at[tok], x_v)
            _sc_chunk_copy(tab_hbm, idx_v, rows_v, sem, 0).start()
            for c in range(nchunk):
                if c + 1 < nchunk:
                    _sc_chunk_copy(tab_hbm, idx_v, rows_v, sem, c + 1).start()
                _sc_chunk_copy(tab_hbm, idx_v, rows_v, sem, c).wait()
                for kc in range(nstrip):
                    xr = [x_v[pl.ds(kc * SC_STRIP + kk * SC_LANES, SC_LANES)] for kk in range(nk)]

                    def rows8(rg, cr, c=c, kc=kc, xr=xr):
                        for rr in range(SC_UGROUP):
                            r = rg * SC_UGROUP + rr
                            s = rows_v[c % 2, r, pl.ds(kc * SC_STRIP, SC_LANES)] * xr[0]
                            for kk in range(1, nk):
                                s = s + rows_v[c % 2, r, pl.ds(kc * SC_STRIP + kk * SC_LANES, SC_LANES)] * xr[kk]
                            dst = pl.ds((c * SC_ROWS + r) * SC_LANES, SC_LANES)
                            if kc == 0:
                                acc_v[dst] = s
                            else:
                                acc_v[dst] = acc_v[dst] + s
                        return cr

                    lax.fori_loop(0, SC_ROWS // SC_UGROUP, rows8, 0)
            pltpu.sync_copy(acc_v, out_hbm.at[tok])
            return carry

        lax.fori_loop(0, per_w, token, 0)

    return k(idx, x, table)


def _sc_vsum(idx, coefb, table):
    ns = idx.shape[0]
    per_w = ns // SC_WORKERS
    assert per_w * SC_WORKERS == ns
    nchunk, nstrip, nk = NSEL // SC_ROWS, D_MODEL // SC_STRIP, SC_STRIP // SC_LANES

    @functools.partial(
        pl.kernel, mesh=plsc.VectorSubcoreMesh(core_axis_name="c", subcore_axis_name="s"),
        out_type=jax.ShapeDtypeStruct((ns, D_MODEL), F32),
        scratch_types=[pltpu.VMEM((NSEL,), jnp.int32), pltpu.VMEM((NSEL * SC_LANES,), F32),
                       pltpu.VMEM((2, SC_ROWS, D_MODEL), F32), pltpu.VMEM((D_MODEL,), F32),
                       pltpu.SemaphoreType.DMA((2,))],
    )
    def k(idx_hbm, coef_hbm, tab_hbm, out_hbm, idx_v, coef_v, rows_v, y_v, sem):
        base = (lax.axis_index("s") * 2 + lax.axis_index("c")) * per_w

        def token(i, carry):
            tok = base + i
            pltpu.sync_copy(idx_hbm.at[tok], idx_v)
            pltpu.sync_copy(coef_hbm.at[tok], coef_v)
            _sc_chunk_copy(tab_hbm, idx_v, rows_v, sem, 0).start()
            for kk in range(D_MODEL // SC_LANES):
                y_v[pl.ds(kk * SC_LANES, SC_LANES)] = jnp.zeros((SC_LANES,), F32)
            for c in range(nchunk):
                if c + 1 < nchunk:
                    _sc_chunk_copy(tab_hbm, idx_v, rows_v, sem, c + 1).start()
                _sc_chunk_copy(tab_hbm, idx_v, rows_v, sem, c).wait()
                for kc in range(nstrip):
                    def rows8(rg, cr, c=c, kc=kc):
                        acc = [y_v[pl.ds(kc * SC_STRIP + kk * SC_LANES, SC_LANES)] for kk in range(nk)]
                        for rr in range(SC_RGROUP):
                            r = rg * SC_RGROUP + rr
                            cv = coef_v[pl.ds((c * SC_ROWS + r) * SC_LANES, SC_LANES)]
                            for kk in range(nk):
                                acc[kk] = acc[kk] + cv * rows_v[c % 2, r, pl.ds(kc * SC_STRIP + kk * SC_LANES, SC_LANES)]
                        for kk in range(nk):
                            y_v[pl.ds(kc * SC_STRIP + kk * SC_LANES, SC_LANES)] = acc[kk]
                        return cr

                    lax.fori_loop(0, SC_ROWS // SC_RGROUP, rows8, 0)
            pltpu.sync_copy(y_v, out_hbm.at[tok])
            return carry

        lax.fori_loop(0, per_w, token, 0)

    return k(idx, coefb, table)


def _sc_coef_kernel(a_ref, g_ref, after_ref, o_ref):
    hp = lax.Precision.HIGHEST
    grp = lax.broadcasted_iota(jnp.int32, (NSEL * SC_LANES, NSEL), 0) // SC_LANES
    fold = (grp == lax.broadcasted_iota(jnp.int32, (NSEL * SC_LANES, NSEL), 1)).astype(F32)
    a = jnp.dot(a_ref[...], fold, precision=hp, preferred_element_type=F32)
    coef = g_ref[...] * (0.5 * a * (1.0 + lax.erf(a * (0.5 ** 0.5))))
    o_ref[...] = lax.dot_general(coef, fold, (((1,), (1,)), ((), ())), precision=hp, preferred_element_type=F32)


def _sc_coef(a_part, g, after, *, tt=SC_SPLIT_BLOCK):
    n = a_part.shape[0]
    wide = pl.BlockSpec((tt, NSEL * SC_LANES), lambda i: (i, 0))
    return pl.pallas_call(
        _sc_coef_kernel, grid=(n // tt,),
        in_specs=[wide, pl.BlockSpec((tt, NSEL), lambda i: (i, 0)),
                  pl.BlockSpec((SUBLANES, D_MODEL), lambda i: (0, 0))], out_specs=wide,
        out_shape=jax.ShapeDtypeStruct((n, NSEL * SC_LANES), F32), name="sc_coef",
    )(a_part, g, after)


def _sc_residual_kernel(x1_ref, ff_ref, gate_ref, after_ref, o_ref):
    o_ref[...] = x1_ref[...] + gate_ref[0] * ff_ref[...]


def _sc_residual(x1, ff, gate, after, *, tok_offset, l_seq, tt=SC_SPLIT_BLOCK):
    n = x1.shape[0]
    row = pl.BlockSpec((tt, D_MODEL), lambda i: (i, 0))
    return pl.pallas_call(
        _sc_residual_kernel, grid=(n // tt,),
        in_specs=[row, row, pl.BlockSpec((1, 1, D_MODEL), lambda i: ((tok_offset + i * tt) // l_seq, 0, 0)),
                  pl.BlockSpec((SUBLANES, D_MODEL), lambda i: (0, 0))], out_specs=row,
        out_shape=jax.ShapeDtypeStruct((n, D_MODEL), F32), name="sc_residual",
    )(x1, ff, gate, after)


def _rope_tables(pos):
    half = DIFF_HD // 2
    inv = ROPE_THETA ** (-jnp.arange(half, dtype=F32) / half)
    ang = pos.astype(F32)[:, None] * inv[None, :]
    cos, sin = jnp.cos(ang), jnp.sin(ang)
    return jnp.tile(cos, (1, 4)), jnp.tile(jnp.concatenate([-sin, sin], axis=1), (1, 2))


def _mixers(x, mods, mod_tiles, tile, cos, sin, l_seq, h0, wts, attend, lambda_init, after):
    sh1, sc1, gt1, sh2, sc2 = mods
    q_g, k_g, v_g, r_g, la = _proj_gla(x, sh1, sc1, wts["n1"], wts["w_a"], wts["w_alpha"], wts["b_alpha"], after,
                                       tt=tile, mod_tiles=mod_tiles)
    q_bf, k_rows, k_bf, v_rows, v_bf = _proj_diff(x, sh1, sc1, wts["n1"], wts["w_b"], wts["q_gain"], wts["k_gain"],
                                                  cos, sin, wts["ones"], after, tt=tile, mod_tiles=mod_tiles)
    o_a, state = _gla(q_g, k_g, la, v_g, h0, l_seq=l_seq, tb=min(GLA_TILE, l_seq))
    o_b = attend(q_bf, k_bf, v_bf)
    post_tile = min(POST_TILE, tile)
    x1, h2 = _post_mixer(x, o_a, r_g, o_b, (sh1, sc1, gt1, sh2, sc2), wts["n1"], wts["n2"], wts["gla_gain"],
                         wts["diff_gain"], wts["w_gla_out"], wts["w_diff_out"], wts["w_g"], wts["w_out"],
                         tt=post_tile, mod_tiles=mod_tiles * (tile // post_tile), diff_out_scale=1.0 - lambda_init)
    idx, g = _peer_select(h2, wts["w_pq"], wts["k1"], wts["k2"])
    return x1, h2, idx, g, k_rows, v_rows, state


def kernel(x_prompt, x_sample, c_prompt, c_sample, cache_k, cache_v, state_gla, norm1_gain, norm2_gain, w_mod, b_mod, w_in, w_alpha, b_alpha, gla_gain, w_gla_out, q_gain, k_gain, lam_q1, lam_k1, lam_q2, lam_k2, diff_gain, w_diff_out, w_out, w_pq, sub_keys1, sub_keys2, peer_u, peer_v):
    bp, lp, _ = x_prompt.shape
    bs, ls, _ = x_sample.shape
    past = cache_k.shape[2]
    l = 0
    lambda_init = 0.8 - 0.6 * math.exp(-0.3 * l)
    nb = bp + bs
    c_all = jnp.concatenate([c_prompt, c_sample, jnp.zeros(((-nb) % 8, D_MODEL), F32)], axis=0)
    mod_all = _adaln(c_all, w_mod[l], b_mod[l])
    mods = [mod_all[:nb, i * D_MODEL:(i + 1) * D_MODEL] for i in range(6)]

    wi = w_in[l]
    o_glr = 2 * GLA_QK_W + 2 * GLA_V_W
    o_diff = o_glr + GLA_LOWRANK
    o_gate = o_diff + 2 * DIFF_QK_W + DIFF_V_W
    pad_lr = LANES - GLA_LOWRANK
    lane = jnp.arange(LANES)
    wts = dict(
        n1=norm1_gain[l][None], n2=norm2_gain[l][None],
        w_a=jnp.concatenate([wi[:, :o_diff], jnp.zeros((D_MODEL, pad_lr), F32)], axis=1).astype(BF16),
        w_alpha=jnp.concatenate([w_alpha[l], jnp.zeros((pad_lr, GLA_QK_W), F32)], axis=0).astype(BF16),
        b_alpha=b_alpha[l][None],
        w_b=wi[:, o_diff:o_gate].astype(BF16), w_g=wi[:, o_gate:].astype(BF16),
        q_gain=q_gain[l].reshape(1, LANES), k_gain=k_gain[l].reshape(1, LANES),
        ones=(lane[:, None] // DIFF_HD == lane[None, :] // DIFF_HD).astype(BF16),
        gla_gain=jnp.tile(gla_gain[l], GLA_HEADS)[None], diff_gain=jnp.tile(diff_gain[l], DIFF_HEADS)[None],
        w_gla_out=w_gla_out[l].astype(BF16), w_diff_out=w_diff_out[l].astype(BF16), w_out=w_out[l].astype(BF16),
        w_pq=w_pq[l].astype(BF16), k1=sub_keys1[l].astype(BF16), k2=sub_keys2[l].astype(BF16),
        table=jnp.concatenate([peer_u[l], peer_v[l]], axis=1).reshape(-1, 1, 2 * D_MODEL),
        u=peer_u[l], v=peer_v[l],
    )
    lam = (jnp.exp(jnp.sum(lam_q1[l] * lam_k1[l])) - jnp.exp(jnp.sum(lam_q2[l] * lam_k2[l])) + lambda_init).reshape(1, 1)

    no_order = jnp.zeros((SUBLANES, LANES), jnp.int32)
    gather = functools.partial(_peer_gather, table=wts["table"], tt=PEER_TOK_TILE)

    cos_p, sin_p = _rope_tables(jnp.arange(lp))

    def prompt_group(lo, hi, after):
        nsq = hi - lo
        attend = lambda q, k, v: _attn_prompt(q.reshape(nsq, lp, -1), k.reshape(nsq, lp, -1), v.reshape(nsq, lp, -1),
                                              lam).reshape(nsq * lp, -1)
        return _mixers(x_prompt[lo:hi].reshape(nsq * lp, D_MODEL), [m[lo:hi, None, :] for m in mods[:5]],
                       lp // PROJ_TILE, PROJ_TILE, cos_p, sin_p, lp, jnp.zeros((nsq, GLA_HEADS, GLA_DK, GLA_DV), F32),
                       wts, attend, lambda_init, after)

    n_sc_seq = bp // 2 if bp >= 4 else 0
    if n_sc_seq:
        x1a, h2a, idxa, ga, ka, va, sa = prompt_group(0, n_sc_seq, no_order)
        a_part = _sc_udot(idxa, h2a, wts["u"])
        x1b, h2b, idxb, gb, kb, vb, sb = prompt_group(n_sc_seq, bp, idxa[:SUBLANES])
        gates_b = mods[5][n_sc_seq:bp, None, :]
        n_first = round(x1b.shape[0] // SC_SPLIT_BLOCK * TC_FIRST_SHARE) * SC_SPLIT_BLOCK
        y_b1 = gather(idxb, gb, h2b, x1b, gates=gates_b, l_seq=lp, row0=0, n_tok=n_first)
        coefb = _sc_coef(a_part, ga, y_b1[:SUBLANES])
        ff = _sc_vsum(idxa, coefb, wts["v"])
        y_b2 = gather(idxb, gb, h2b, x1b, gates=gates_b, l_seq=lp, row0=n_first, n_tok=x1b.shape[0] - n_first)
        y_a = _sc_residual(x1a, ff, mods[5][:n_sc_seq, None, :], y_b2[:SUBLANES], tok_offset=0, l_seq=lp)
        yp = jnp.concatenate([y_a, y_b1, y_b2], axis=0)
        kp, vp, sp = (jnp.concatenate(p, axis=0) for p in ((ka, kb), (va, vb), (sa, sb)))
    else:
        x1p, h2p, idxp, gp, kp, vp, sp = prompt_group(0, bp, no_order)
        yp = gather(idxp, gp, h2p, x1p, gates=mods[5][:bp, None, :], l_seq=lp)

    seq_per_tile = SAMPLE_TILE // ls
    cos_s, sin_s = _rope_tables(jnp.tile(past + jnp.arange(ls), seq_per_tile))
    mods_s = [jnp.repeat(m[bp:nb], ls, axis=0).reshape(-1, SAMPLE_TILE, D_MODEL) for m in mods[:5]]
    attend_s = lambda q, k, v: _attn_sample(q, k, v, cache_k[l], cache_v[l], lam, l_seq=ls)
    x1s, h2s, idxs, gs, ks, vs, ss = _mixers(x_sample.reshape(bs * ls, D_MODEL), mods_s, 1, SAMPLE_TILE, cos_s, sin_s,
                                             ls, state_gla[l], wts, attend_s, lambda_init, no_order)
    ys = gather(idxs, gs, h2s, x1s, gates=mods[5][bp:nb, None, :], l_seq=ls)

    return (yp.reshape(bp, lp, D_MODEL), ys.reshape(bs, ls, D_MODEL),
            kp.reshape(1, bp, lp, DIFF_HEADS, 2 * DIFF_HD), vp.reshape(1, bp, lp, DIFF_HEADS, DIFF_VD), sp[None],
            ks.reshape(1, bs, ls, DIFF_HEADS, 2 * DIFF_HD), vs.reshape(1, bs, ls, DIFF_HEADS, DIFF_VD), ss[None])
```

```python
import math
import functools
import jax, jax.numpy as jnp
from jax import lax
from jax.experimental import pallas as pl
from jax.experimental.pallas import tpu as pltpu
from jax.experimental.pallas import tpu_sc as plsc

D_MODEL = 1024
CHUNK = 64
EPS = 1e-6
GLA_HEADS = 4
GLA_DK = 128
GLA_DV = 256
GLA_LOWRANK = 16
GLA_TAU = 16.0
GLA_BLOCK = 16
DIFF_HEADS = 8
DIFF_HD = 64
DIFF_VD = 128
ROPE_THETA = 10000.0
PEER_HEADS = 8
PEER_NKEYS = 128
PEER_DKEY = 256
PEER_TOPK = 16
GLA_QK_W = GLA_HEADS * GLA_DK
GLA_V_W = GLA_HEADS * GLA_DV
DIFF_QK_W = DIFF_HEADS * 2 * DIFF_HD
DIFF_V_W = DIFF_HEADS * DIFF_VD
F32 = jnp.float32
BF16 = jnp.bfloat16
NSEL = PEER_HEADS * PEER_TOPK
LANES = 128
SUBLANES = 8

PROJ_TILE = 512
POST_TILE = 256
GLA_TILE = 256
SAMPLE_TILE = 256
PEER_TOK_TILE = 8
ATTN_NEG = -1e30
ATTN_TILE = 1024
SELECT_TILE = 1024
SC_LANES = 16
SC_WORKERS = 32
SC_ROWS = 32
SC_STRIP = 256
SC_RGROUP = 8
SC_UGROUP = 1
SC_SPLIT_BLOCK = 512
TC_FIRST_SHARE = 16 / 32


def _mod_kernel(c_ref, w_ref, b_ref, o_ref):
    c = c_ref[...]
    s = c * jax.nn.sigmoid(c)
    o_ref[...] = jnp.dot(s.astype(BF16), w_ref[...].astype(BF16), preferred_element_type=F32) + b_ref[...]


def _adaln(c, w_mod, b_mod):
    n = c.shape[0]
    tn = 1536
    return pl.pallas_call(
        _mod_kernel,
        grid=(6 * D_MODEL // tn,),
        in_specs=[pl.BlockSpec((n, D_MODEL), lambda j: (0, 0)),
                  pl.BlockSpec((D_MODEL, tn), lambda j: (0, j)),
                  pl.BlockSpec((1, tn), lambda j: (0, j))],
        out_specs=pl.BlockSpec((n, tn), lambda j: (0, j)),
        out_shape=jax.ShapeDtypeStruct((n, 6 * D_MODEL), F32),
        name="adaln",
    )(c, w_mod, b_mod.reshape(1, -1))


def _modulated_norm(x, gain, scale, shift):
    y = x * lax.rsqrt(jnp.mean(x * x, axis=-1, keepdims=True) + EPS)
    return (y * gain) * (1.0 + scale) + shift


def _proj_gla_kernel(x_ref, sh_ref, sc_ref, gain_ref, w_ref, wa_ref, ba_ref, after_ref,
                     q_ref, k_ref, v_ref, r_ref, la_ref):
    h = _modulated_norm(x_ref[...], gain_ref[...], sc_ref[0], sh_ref[0]).astype(BF16)
    z = jnp.dot(h, w_ref[...], preferred_element_type=F32)
    q_ref[...] = z[:, 0:GLA_QK_W] * (GLA_DK ** -0.5)
    k_ref[...] = z[:, GLA_QK_W:2 * GLA_QK_W]
    v_ref[...] = z[:, 2 * GLA_QK_W:2 * GLA_QK_W + GLA_V_W]
    r_ref[...] = z[:, 2 * GLA_QK_W + GLA_V_W:2 * GLA_QK_W + 2 * GLA_V_W]
    glr = z[:, 2 * GLA_QK_W + 2 * GLA_V_W:]
    a = jnp.dot(glr.astype(BF16), wa_ref[...], preferred_element_type=F32) + ba_ref[...]
    la_ref[...] = jax.nn.log_sigmoid(a) * (1.0 / GLA_TAU)


def _group_mean_square(x, ones_ref):
    s = x * x
    hi = s.astype(BF16)
    lo = (s - hi.astype(F32)).astype(BF16)
    ss = (jnp.dot(hi, ones_ref[...], preferred_element_type=F32)
          + jnp.dot(lo, ones_ref[...], preferred_element_type=F32))
    return ss * (1.0 / DIFF_HD)


def _norm_rope(x, gain, cos, sin_signed, ones_ref):
    outs = []
    lane = lax.broadcasted_iota(jnp.int32, (x.shape[0], LANES), 1)
    first_half = (lane % DIFF_HD) < (DIFF_HD // 2)
    for b in range(DIFF_QK_W // LANES):
        xb = x[:, b * LANES:(b + 1) * LANES]
        y = xb * lax.rsqrt(_group_mean_square(xb, ones_ref) + EPS) * gain
        rot = jnp.where(first_half, pltpu.roll(y, LANES - DIFF_HD // 2, axis=1), pltpu.roll(y, DIFF_HD // 2, axis=1))
        outs.append(y * cos + rot * sin_signed)
    return jnp.concatenate(outs, axis=1)


def _proj_diff_kernel(x_ref, sh_ref, sc_ref, gain_ref, w_ref, qg_ref, kg_ref, cos_ref, sin_ref, ones_ref, after_ref,
                      qb_ref, k_ref, kb_ref, v_ref, vb_ref, *, q_scale):
    h = _modulated_norm(x_ref[...], gain_ref[...], sc_ref[0], sh_ref[0]).astype(BF16)
    z = jnp.dot(h, w_ref[...], preferred_element_type=F32)
    cos, sin = cos_ref[...], sin_ref[...]
    q = _norm_rope(z[:, 0:DIFF_QK_W], qg_ref[...], cos, sin, ones_ref)
    qb_ref[...] = (q * q_scale).astype(BF16)
    k = _norm_rope(z[:, DIFF_QK_W:2 * DIFF_QK_W], kg_ref[...], cos, sin, ones_ref)
    k_ref[...] = k
    kb_ref[...] = k.astype(BF16)
    v = z[:, 2 * DIFF_QK_W:2 * DIFF_QK_W + DIFF_V_W]
    v_ref[...] = v
    vb_ref[...] = v.astype(BF16)


def _order_spec():
    return pl.BlockSpec((SUBLANES, LANES), lambda i: (0, 0))


def _proj_gla(x, shift, scale, gain, w_a, w_alpha_pad, b_alpha, after, *, tt, mod_tiles):
    n = x.shape[0]
    assert n % tt == 0
    bmap = lambda i: (i // mod_tiles, 0, 0)
    mod = pl.BlockSpec((1,) + shift.shape[1:], bmap)
    row = lambda w: pl.BlockSpec((tt, w), lambda i: (i, 0))
    const = lambda a: pl.BlockSpec(a.shape, lambda i: (0,) * a.ndim)
    return pl.pallas_call(
        _proj_gla_kernel,
        grid=(n // tt,),
        in_specs=[row(D_MODEL), mod, mod,
                  const(gain), const(w_a), const(w_alpha_pad), const(b_alpha), _order_spec()],
        out_specs=[row(GLA_QK_W), row(GLA_QK_W), row(GLA_V_W), row(GLA_V_W), row(GLA_QK_W)],
        out_shape=[jax.ShapeDtypeStruct((n, GLA_QK_W), F32), jax.ShapeDtypeStruct((n, GLA_QK_W), F32),
                   jax.ShapeDtypeStruct((n, GLA_V_W), F32), jax.ShapeDtypeStruct((n, GLA_V_W), F32),
                   jax.ShapeDtypeStruct((n, GLA_QK_W), F32)],
        compiler_params=pltpu.CompilerParams(dimension_semantics=("arbitrary",), vmem_limit_bytes=48 * 1024 * 1024),
        name="proj_gla",
    )(x, shift, scale, gain, w_a, w_alpha_pad, b_alpha, after)


def _proj_diff(x, shift, scale, gain, w_b, q_gain128, k_gain128, cos128, sin128, ones128, after, *, tt, mod_tiles):
    n = x.shape[0]
    assert n % tt == 0 and cos128.shape[0] % tt == 0
    bmap = lambda i: (i // mod_tiles, 0, 0)
    mod = pl.BlockSpec((1,) + shift.shape[1:], bmap)
    pos_tiles = cos128.shape[0] // tt
    row = lambda w: pl.BlockSpec((tt, w), lambda i: (i, 0))
    const = lambda a: pl.BlockSpec(a.shape, lambda i: (0,) * a.ndim)
    pos = pl.BlockSpec((tt, LANES), lambda i: (i % pos_tiles, 0))
    q_scale = math.log2(math.e) * DIFF_HD ** -0.5
    return pl.pallas_call(
        functools.partial(_proj_diff_kernel, q_scale=q_scale),
        grid=(n // tt,),
        in_specs=[row(D_MODEL), mod, mod,
                  const(gain), const(w_b), const(q_gain128), const(k_gain128), pos, pos, const(ones128),
                  _order_spec()],
        out_specs=[row(DIFF_QK_W)] * 5,
        out_shape=[jax.ShapeDtypeStruct((n, DIFF_QK_W), BF16), jax.ShapeDtypeStruct((n, DIFF_QK_W), F32),
                   jax.ShapeDtypeStruct((n, DIFF_QK_W), BF16), jax.ShapeDtypeStruct((n, DIFF_V_W), F32),
                   jax.ShapeDtypeStruct((n, DIFF_V_W), BF16)],
        compiler_params=pltpu.CompilerParams(dimension_semantics=("arbitrary",), vmem_limit_bytes=48 * 1024 * 1024),
        name="proj_diff",
    )(x, shift, scale, gain, w_b, q_gain128, k_gain128, cos128, sin128, ones128, after)


def _gla_kernel(q_ref, k_ref, la_ref, v_ref, h0_ref, o_ref, hT_ref, st_ref, *, tb):
    i = pl.program_id(2)
    nb = tb // GLA_BLOCK

    @pl.when(i == 0)
    def _init():
        st_ref[...] = h0_ref[0, 0].T

    row = lax.broadcasted_iota(jnp.int32, (tb, tb), 0)
    col = lax.broadcasted_iota(jnp.int32, (tb, tb), 1)
    same = (row // GLA_BLOCK) == (col // GLA_BLOCK)
    causal = same & (col <= row)
    la = la_ref[...]
    hp = lax.Precision.HIGHEST
    b = jnp.dot(causal.astype(F32), la, precision=hp, preferred_element_type=F32)
    b_last = jnp.dot(same.astype(F32), la, precision=hp, preferred_element_type=F32)
    q, k, v = q_ref[...], k_ref[...], v_ref[...]
    qe = (q * jnp.exp(b)).astype(BF16)
    ke = (k * jnp.exp(-b)).astype(BF16)
    kd = (k * jnp.exp(b_last - b)).astype(BF16)
    vb = v.astype(BF16)
    a = lax.dot_general(qe, ke, (((1,), (1,)), ((), ())), preferred_element_type=F32)
    a = jnp.where(causal, a, 0.0).astype(BF16)
    o_intra = jnp.dot(a, vb, preferred_element_type=F32)

    st = st_ref[...]
    outs = []
    for n in range(nb):
        rs = slice(n * GLA_BLOCK, (n + 1) * GLA_BLOCK)
        o_inter = lax.dot_general(qe[rs], st.astype(BF16), (((1,), (1,)), ((), ())), preferred_element_type=F32)
        outs.append(o_inter + o_intra[rs])
        u = lax.dot_general(vb[rs], kd[rs], (((0,), (0,)), ((), ())), preferred_element_type=F32)
        st = jnp.exp(b_last[n * GLA_BLOCK:n * GLA_BLOCK + 1, :]) * st + u
    o_ref[...] = jnp.concatenate(outs, axis=0)
    st_ref[...] = st

    @pl.when(i == pl.num_programs(2) - 1)
    def _fin():
        hT_ref[0, 0] = st.T


def _gla(q, k, la, v, h0, *, l_seq, tb):
    n = q.shape[0]
    bsz = n // l_seq
    nt = l_seq // tb
    assert l_seq % tb == 0 and tb % GLA_BLOCK == 0
    tok = lambda w: pl.BlockSpec((tb, w), lambda b, h, i: (b * nt + i, h))
    st = pl.BlockSpec((1, 1, GLA_DK, GLA_DV), lambda b, h, i: (b, h, 0, 0))
    return pl.pallas_call(
        functools.partial(_gla_kernel, tb=tb),
        grid=(bsz, GLA_HEADS, nt),
        in_specs=[tok(GLA_DK), tok(GLA_DK), tok(GLA_DK), tok(GLA_DV), st],
        out_specs=[tok(GLA_DV), st],
        out_shape=[jax.ShapeDtypeStruct((n, GLA_V_W), F32),
                   jax.ShapeDtypeStruct((bsz, GLA_HEADS, GLA_DK, GLA_DV), F32)],
        scratch_shapes=[pltpu.VMEM((GLA_DV, GLA_DK), F32)],
        compiler_params=pltpu.CompilerParams(dimension_semantics=("arbitrary", "arbitrary", "arbitrary")),
        name="gla",
    )(q, k, la, v, h0)


def _split_maps(q):
    lane = lax.broadcasted_iota(jnp.int32, q.shape, 1)
    zero = jnp.zeros_like(q)
    return jnp.concatenate([jnp.where(lane < DIFF_HD, q, zero), jnp.where(lane >= DIFF_HD, q, zero)], axis=0)


def _attn_prompt_kernel(lam_ref, q_ref, k_ref, v_ref, o_ref, vt_ref, *, tq, tk):
    i = pl.program_id(2)
    n_kt = v_ref.shape[1] // tk

    @pl.when(i == 0)
    def _transpose_v():
        def body(j, c):
            vt_ref[j] = v_ref[0, pl.ds(j * tk, tk), :].astype(F32).T.astype(BF16)
            return c
        lax.fori_loop(0, n_kt, body, 0)

    qp = _split_maps(q_ref[0])

    def step(j, carry, masked, koff=0):
        m, l, acc = carry
        kt = k_ref[0, pl.ds(j * tk, tk), :]
        st = lax.dot_general(kt, qp, (((1,), (1,)), ((), ())), preferred_element_type=F32)
        if masked:
            kc = (lax.broadcasted_iota(jnp.int32, st.shape, 0) + koff) // CHUNK
            qc = (lax.broadcasted_iota(jnp.int32, st.shape, 1) % tq) // CHUNK
            st = jnp.where(qc >= kc, st, ATTN_NEG)
        m_new = jnp.maximum(m, jnp.max(st, axis=0, keepdims=True))
        alpha = jnp.exp2(m - m_new)
        p = jnp.exp2(st - m_new)
        l = alpha * l + jnp.sum(p, axis=0, keepdims=True)
        acc = alpha * acc + jnp.dot(vt_ref[j], p.astype(BF16), preferred_element_type=F32)
        return m_new, l, acc

    carry = (jnp.full((1, 2 * tq), ATTN_NEG, F32), jnp.zeros((1, 2 * tq), F32), jnp.zeros((LANES, 2 * tq), F32))
    r = tq // tk
    carry = lax.fori_loop(0, i * r, functools.partial(step, masked=False), carry)
    for d in range(r):
        carry = step(i * r + d, carry, True, koff=d * tk)
    m, l, acc = carry
    o = acc / l
    ot = o[:, :tq] - lam_ref[0, 0] * o[:, tq:]
    o_ref[0] = ot.T


def _attn_prompt(q, k, v, lam, *, tq=ATTN_TILE, tk=ATTN_TILE):
    b, l, w = q.shape
    h = w // LANES
    assert l % tq == 0 and tq % tk == 0 and tk % CHUNK == 0
    return pl.pallas_call(
        functools.partial(_attn_prompt_kernel, tq=tq, tk=tk),
        grid=(b, h, l // tq),
        in_specs=[
            pl.BlockSpec(memory_space=pltpu.SMEM),
            pl.BlockSpec((1, tq, LANES), lambda bi, hi, i: (bi, i, hi)),
            pl.BlockSpec((1, l, LANES), lambda bi, hi, i: (bi, 0, hi)),
            pl.BlockSpec((1, l, LANES), lambda bi, hi, i: (bi, 0, hi)),
        ],
        out_specs=pl.BlockSpec((1, tq, LANES), lambda bi, hi, i: (bi, i, hi)),
        out_shape=jax.ShapeDtypeStruct((b, l, w), F32),
        scratch_shapes=[pltpu.VMEM((l // tk, LANES, tk), BF16)],
        compiler_params=pltpu.CompilerParams(dimension_semantics=("arbitrary", "arbitrary", "arbitrary"),
                                             vmem_limit_bytes=40 * 1024 * 1024),
        name="attn_prompt",
    )(lam, q, k, v)


def _attn_sample_kernel(lam_ref, q_ref, kn_ref, vn_ref, kc_ref, vc_ref, o_ref, m_ref, l_ref, acc_ref, *, heads):
    j = pl.program_id(1)
    lq = q_ref.shape[0]
    nt = (((1,), (1,)), ((), ()))

    @pl.when(j == 0)
    def _init():
        m_ref[...] = jnp.full(m_ref.shape, ATTN_NEG, F32)
        l_ref[...] = jnp.zeros(l_ref.shape, F32)
        acc_ref[...] = jnp.zeros(acc_ref.shape, F32)

    def update(h, k, v):
        qp = _split_maps(q_ref[:, h * LANES:(h + 1) * LANES])
        s = lax.dot_general(qp, k, nt, preferred_element_type=F32)
        m_old = m_ref[h]
        m_new = jnp.maximum(m_old, jnp.max(s, axis=-1, keepdims=True))
        alpha = jnp.exp2(m_old - m_new)
        p = jnp.exp2(s - m_new)
        l_ref[h] = alpha * l_ref[h] + jnp.sum(p, axis=-1, keepdims=True)
        acc_ref[h] = alpha * acc_ref[h] + jnp.dot(p.astype(BF16), v, preferred_element_type=F32)
        m_ref[h] = m_new

    for h in range(heads):
        update(h, kc_ref[:, h, :].astype(BF16), vc_ref[:, h, :].astype(BF16))

    @pl.when(j == pl.num_programs(1) - 1)
    def _fin():
        for h in range(heads):
            update(h, kn_ref[:, h * LANES:(h + 1) * LANES], vn_ref[:, h * LANES:(h + 1) * LANES])
            o = acc_ref[h] / l_ref[h]
            o_ref[:, h * LANES:(h + 1) * LANES] = o[:lq] - lam_ref[0, 0] * o[lq:]


def _attn_sample(q, k_new, v_new, cache_k, cache_v, lam, *, l_seq, tk=1024):
    n, w = q.shape
    bsz, past, heads, _ = cache_k.shape
    tk = min(tk, past)
    assert n == bsz * l_seq and w == heads * LANES and past % tk == 0
    tok = pl.BlockSpec((l_seq, w), lambda b, j: (b, 0))
    cache = pl.BlockSpec((None, tk, heads, LANES), lambda b, j: (b, j, 0, 0))
    return pl.pallas_call(
        functools.partial(_attn_sample_kernel, heads=heads),
        grid=(bsz, past // tk),
        in_specs=[pl.BlockSpec(memory_space=pltpu.SMEM), tok, tok, tok, cache, cache],
        out_specs=tok,
        out_shape=jax.ShapeDtypeStruct((n, w), F32),
        scratch_shapes=[pltpu.VMEM((heads, 2 * l_seq, 1), F32), pltpu.VMEM((heads, 2 * l_seq, 1), F32),
                        pltpu.VMEM((heads, 2 * l_seq, LANES), F32)],
        compiler_params=pltpu.CompilerParams(dimension_semantics=("arbitrary", "arbitrary"),
                                             vmem_limit_bytes=40 * 1024 * 1024),
        name="attn_sample",
    )(lam, q, k_new, v_new, cache_k, cache_v)


def _head_norm(x, width):
    outs = []
    for g in range(x.shape[1] // width):
        xg = x[:, g * width:(g + 1) * width]
        outs.append(xg * lax.rsqrt(jnp.mean(xg * xg, axis=-1, keepdims=True) + EPS))
    return jnp.concatenate(outs, axis=1)


def _post_kernel(x_ref, oa_ref, gr_ref, ob_ref, sh1_ref, sc1_ref, gt1_ref, sh2_ref, sc2_ref,
                 n1_ref, n2_ref, ga_ref, gb_ref, wa_ref, wb_ref, wg_ref, wo_ref, x1_ref, h2_ref, *, diff_out_scale):
    x = x_ref[...]
    h = _modulated_norm(x, n1_ref[...], sc1_ref[0], sh1_ref[0]).astype(BF16)
    gates = jnp.dot(h, wg_ref[...], preferred_element_type=F32)
    gr = gr_ref[...]
    oa = _head_norm(oa_ref[...], GLA_DV) * ga_ref[...] * (gr * jax.nn.sigmoid(gr))
    ya = jnp.dot(oa.astype(BF16), wa_ref[...], preferred_element_type=F32)
    ob = _head_norm(ob_ref[...], DIFF_VD) * gb_ref[...] * diff_out_scale
    yb = jnp.dot(ob.astype(BF16), wb_ref[...], preferred_element_type=F32)
    merged = jax.nn.sigmoid(gates[:, :D_MODEL]) * ya + jax.nn.sigmoid(gates[:, D_MODEL:]) * yb
    mix = jnp.dot(merged.astype(BF16), wo_ref[...], preferred_element_type=F32)
    x1 = x + gt1_ref[0] * mix
    x1_ref[...] = x1
    h2_ref[...] = _modulated_norm(x1, n2_ref[...], sc2_ref[0], sh2_ref[0])


def _post_mixer(x, oa, gr, ob, mods, n1, n2, ga, gb, wa, wb, wg, wo, *, tt, mod_tiles, diff_out_scale):
    n = x.shape[0]
    assert n % tt == 0
    bmap = lambda i: (i // mod_tiles, 0, 0)
    mod = pl.BlockSpec((1,) + mods[0].shape[1:], bmap)
    row = pl.BlockSpec((tt, D_MODEL), lambda i: (i, 0))
    const = lambda a: pl.BlockSpec(a.shape, lambda i: (0,) * a.ndim)
    return pl.pallas_call(
        functools.partial(_post_kernel, diff_out_scale=diff_out_scale),
        grid=(n // tt,),
        in_specs=[row, row, row, row] + [mod] * 5 + [const(a) for a in (n1, n2, ga, gb, wa, wb, wg, wo)],
        out_specs=[row, row],
        out_shape=[jax.ShapeDtypeStruct((n, D_MODEL), F32), jax.ShapeDtypeStruct((n, D_MODEL), F32)],
        compiler_params=pltpu.CompilerParams(dimension_semantics=("arbitrary",), vmem_limit_bytes=48 * 1024 * 1024),
        name="post_mixer",
    )(x, oa, gr, ob, *mods, n1, n2, ga, gb, wa, wb, wg, wo)


def _topk_rows(s, k):
    r = s.shape[0]
    row = lax.broadcasted_iota(jnp.int32, s.shape, 0)
    vals, idxs = [], []
    for _ in range(k):
        mx = jnp.max(s, axis=0, keepdims=True)
        am = jnp.min(jnp.where(s == mx, row, r), axis=0, keepdims=True)
        vals.append(mx)
        idxs.append(am)
        s = jnp.where(row == am, -jnp.inf, s)
    return jnp.concatenate(vals, axis=0), jnp.concatenate(idxs, axis=0)


def _take_rows(tab, sel):
    out = jnp.zeros(sel.shape, tab.dtype)
    for a in range(tab.shape[0]):
        out = jnp.where(sel == a, tab[a:a + 1, :], out)
    return out


def _staircase():
    return [(a, b) for a in range(PEER_TOPK) for b in range(PEER_TOPK) if (a + 1) * (b + 1) <= PEER_TOPK]


def _peer_select_kernel(h2_ref, w_ref, k1_ref, k2_ref, idx_ref, g_ref, idx_acc, g_acc):
    q = jnp.dot(h2_ref[...].astype(BF16), w_ref[...], preferred_element_type=F32)
    nt = (((1,), (1,)), ((), ()))
    s1 = lax.dot_general(k1_ref[...], q[:, :PEER_NKEYS].astype(BF16), nt, preferred_element_type=F32)
    s2 = lax.dot_general(k2_ref[...], q[:, PEER_NKEYS:].astype(BF16), nt, preferred_element_type=F32)
    v1, i1 = _topk_rows(s1, PEER_TOPK)
    v2, i2 = _topk_rows(s2, PEER_TOPK)
    pairs = _staircase()
    neg = jnp.full(((-len(pairs)) % SUBLANES, v1.shape[1]), -jnp.inf, F32)
    cand = jnp.concatenate([v1[a:a + 1, :] + v2[b:b + 1, :] for a, b in pairs] + [neg], axis=0)
    sc, cr = _topk_rows(cand, PEER_TOPK)
    ra = jnp.zeros(cr.shape, jnp.int32)
    rb = jnp.zeros(cr.shape, jnp.int32)
    for r, (a, b) in enumerate(pairs):
        hit = cr == r
        ra = jnp.where(hit, a, ra)
        rb = jnp.where(hit, b, rb)
    e1 = _take_rows(i1, ra)
    e2 = _take_rows(i2, rb)
    p = jnp.exp(sc - sc[0:1, :])
    h = pl.program_id(1)
    rows = pl.ds(pl.multiple_of(h * PEER_TOPK, PEER_TOPK), PEER_TOPK)
    idx_acc[rows, :] = (e1 * PEER_NKEYS + e2).astype(F32)
    g_acc[rows, :] = p / jnp.sum(p, axis=0, keepdims=True)

    @pl.when(h == pl.num_programs(1) - 1)
    def _emit():
        idx_ref[...] = idx_acc[...].T.astype(jnp.int32)
        g_ref[...] = g_acc[...].T


def _peer_select(h2, w_pq_bf, k1_bf, k2_bf, *, tt=SELECT_TILE):
    n, d = h2.shape
    assert n % tt == 0
    return pl.pallas_call(
        _peer_select_kernel,
        grid=(n // tt, PEER_HEADS),
        in_specs=[
            pl.BlockSpec((tt, d), lambda i, h: (i, 0)),
            pl.BlockSpec((d, PEER_DKEY), lambda i, h: (0, h)),
            pl.BlockSpec((PEER_NKEYS, PEER_DKEY // 2), lambda i, h: (0, 0)),
            pl.BlockSpec((PEER_NKEYS, PEER_DKEY // 2), lambda i, h: (0, 0)),
        ],
        out_specs=[pl.BlockSpec((tt, NSEL), lambda i, h: (i, 0)),
                   pl.BlockSpec((tt, NSEL), lambda i, h: (i, 0))],
        out_shape=[jax.ShapeDtypeStruct((n, NSEL), jnp.int32),
                   jax.ShapeDtypeStruct((n, NSEL), F32)],
        scratch_shapes=[pltpu.VMEM((NSEL, tt), F32), pltpu.VMEM((NSEL, tt), F32)],
        compiler_params=pltpu.CompilerParams(dimension_semantics=("arbitrary", "arbitrary")),
        name="peer_select",
    )(h2, w_pq_bf, k1_bf, k2_bf)


def _peer_gather_kernel(idx_ref, h2_ref, g_ref, x1_ref, gate_ref, tab_ref, out_ref, buf, sem, *, tt):
    s = pl.program_id(0)
    n = pl.num_programs(0) - 1

    @pl.when(s < n)
    def _issue():
        slot = s % 2

        def issue_tok(t, carry):
            for j in range(NSEL):
                e = idx_ref[t, j]
                pltpu.make_async_copy(tab_ref.at[e], buf.at[slot, t, pl.ds(j, 1), :], sem.at[slot, t]).start()
            return carry

        lax.fori_loop(0, tt, issue_tok, 0)

    @pl.when(s >= 1)
    def _compute():
        slot = (s + 1) % 2
        eye = (lax.broadcasted_iota(jnp.int32, (NSEL, NSEL), 0)
               == lax.broadcasted_iota(jnp.int32, (NSEL, NSEL), 1))
        gate = gate_ref[0]

        def tok(t, carry):
            for j in range(NSEL):
                pltpu.make_async_copy(tab_ref.at[0], buf.at[slot, t, pl.ds(j, 1), :], sem.at[slot, t]).wait()
            x = h2_ref[pl.ds(t, 1), :]
            p = buf[slot, t, :, 0:LANES] * x[:, 0:LANES]
            for c in range(1, D_MODEL // LANES):
                p = p + buf[slot, t, :, c * LANES:(c + 1) * LANES] * x[:, c * LANES:(c + 1) * LANES]
            a = jnp.sum(p, axis=-1, keepdims=True)
            act = 0.5 * a * (1.0 + lax.erf(a * (0.5 ** 0.5)))
            grow = g_ref[pl.ds(t, 1), :]
            gcol = jnp.sum(jnp.where(eye, grow, 0.0), axis=-1, keepdims=True)
            coef = gcol * act
            y = jnp.sum(coef * buf[slot, t, :, D_MODEL:2 * D_MODEL], axis=0, keepdims=True)
            out_ref[pl.ds(t, 1), :] = x1_ref[pl.ds(t, 1), :] + gate * y
            return carry

        lax.fori_loop(0, tt, tok, 0)


def _peer_gather(idx, g, h2, x1, *, gates, table, tt, l_seq, row0=0, n_tok=None):
    n_tok = idx.shape[0] if n_tok is None else n_tok
    n_tiles = n_tok // tt
    base = row0 // tt
    assert n_tiles * tt == n_tok and base * tt == row0 and l_seq % tt == 0

    def cur(s):
        return jnp.maximum(s - 1, 0)

    return pl.pallas_call(
        functools.partial(_peer_gather_kernel, tt=tt),
        grid=(n_tiles + 1,),
        in_specs=[
            pl.BlockSpec((tt, NSEL), lambda s: (base + jnp.minimum(s, n_tiles - 1), 0), memory_space=pltpu.SMEM),
            pl.BlockSpec((tt, D_MODEL), lambda s: (base + cur(s), 0)),
            pl.BlockSpec((tt, NSEL), lambda s: (base + cur(s), 0)),
            pl.BlockSpec((tt, D_MODEL), lambda s: (base + cur(s), 0)),
            pl.BlockSpec((1, 1, D_MODEL), lambda s: ((row0 + cur(s) * tt) // l_seq, 0, 0)),
            pl.BlockSpec(memory_space=pl.ANY),
        ],
        out_specs=pl.BlockSpec((tt, D_MODEL), lambda s: (cur(s), 0)),
        out_shape=jax.ShapeDtypeStruct((n_tok, D_MODEL), F32),
        scratch_shapes=[pltpu.VMEM((2, tt, NSEL, 2 * D_MODEL), F32), pltpu.SemaphoreType.DMA((2, tt))],
        compiler_params=pltpu.CompilerParams(dimension_semantics=("arbitrary",),
                                             vmem_limit_bytes=48 * 1024 * 1024),
        name="peer_gather",
    )(idx, h2, g, x1, gates, table)


def _sc_chunk_copy(tab_hbm, idx_v, rows_v, sem, c):
    return pltpu.make_async_copy(tab_hbm.at[idx_v.at[pl.ds(c * SC_ROWS, SC_ROWS)]], rows_v.at[c % 2], sem.at[c % 2])


def _sc_udot(idx, x, table):
    ns = idx.shape[0]
    per_w = ns // SC_WORKERS
    assert per_w * SC_WORKERS == ns
    nchunk, nstrip, nk = NSEL // SC_ROWS, D_MODEL // SC_STRIP, SC_STRIP // SC_LANES

    @functools.partial(
        pl.kernel, mesh=plsc.VectorSubcoreMesh(core_axis_name="c", subcore_axis_name="s"),
        out_type=jax.ShapeDtypeStruct((ns, NSEL * SC_LANES), F32),
        scratch_types=[pltpu.VMEM((NSEL,), jnp.int32), pltpu.VMEM((D_MODEL,), F32),
                       pltpu.VMEM((2, SC_ROWS, D_MODEL), F32), pltpu.VMEM((NSEL * SC_LANES,), F32),
                       pltpu.SemaphoreType.DMA((2,))],
    )
    def k(idx_hbm, x_hbm, tab_hbm, out_hbm, idx_v, x_v, rows_v, acc_v, sem):
        base = (lax.axis_index("s") * 2 + lax.axis_index("c")) * per_w

        def token(i, carry):
            tok = base + i
            pltpu.sync_copy(idx_hbm.at[tok], idx_v)
            pltpu.sync_copy(x_hbm.at[tok], x_v)
            _sc_chunk_copy(tab_hbm, idx_v, rows_v, sem, 0).start()
            for c in range(nchunk):
                if c + 1 < nchunk:
                    _sc_chunk_copy(tab_hbm, idx_v, rows_v, sem, c + 1).start()
                _sc_chunk_copy(tab_hbm, idx_v, rows_v, sem, c).wait()
                for kc in range(nstrip):
                    xr = [x_v[pl.ds(kc * SC_STRIP + kk * SC_LANES, SC_LANES)] for kk in range(nk)]

                    def rows8(rg, cr, c=c, kc=kc, xr=xr):
                        for rr in range(SC_UGROUP):
                            r = rg * SC_UGROUP + rr
                            s = rows_v[c % 2, r, pl.ds(kc * SC_STRIP, SC_LANES)] * xr[0]
                            for kk in range(1, nk):
                                s = s + rows_v[c % 2, r, pl.ds(kc * SC_STRIP + kk * SC_LANES, SC_LANES)] * xr[kk]
                            dst = pl.ds((c * SC_ROWS + r) * SC_LANES, SC_LANES)
                            if kc == 0:
                                acc_v[dst] = s
                            else:
                                acc_v[dst] = acc_v[dst] + s
                        return cr

                    lax.fori_loop(0, SC_ROWS // SC_UGROUP, rows8, 0)
            pltpu.sync_copy(acc_v, out_hbm.at[tok])
            return carry

        lax.fori_loop(0, per_w, token, 0)

    return k(idx, x, table)


def _sc_vsum(idx, coefb, table):
    ns = idx.shape[0]
    per_w = ns // SC_WORKERS
    assert per_w * SC_WORKERS == ns
    nchunk, nstrip, nk = NSEL // SC_ROWS, D_MODEL // SC_STRIP, SC_STRIP // SC_LANES

    @functools.partial(
        pl.kernel, mesh=plsc.VectorSubcoreMesh(core_axis_name="c", subcore_axis_name="s"),
        out_type=jax.ShapeDtypeStruct((ns, D_MODEL), F32),
        scratch_types=[pltpu.VMEM((NSEL,), jnp.int32), pltpu.VMEM((NSEL * SC_LANES,), F32),
                       pltpu.VMEM((2, SC_ROWS, D_MODEL), F32), pltpu.VMEM((D_MODEL,), F32),
                       pltpu.SemaphoreType.DMA((2,))],
    )
    def k(idx_hbm, coef_hbm, tab_hbm, out_hbm, idx_v, coef_v, rows_v, y_v, sem):
        base = (lax.axis_index("s") * 2 + lax.axis_index("c")) * per_w

        def token(i, carry):
            tok = base + i
            pltpu.sync_copy(idx_hbm.at[tok], idx_v)
            pltpu.sync_copy(coef_hbm.at[tok], coef_v)
            _sc_chunk_copy(tab_hbm, idx_v, rows_v, sem, 0).start()
            for kk in range(D_MODEL // SC_LANES):
                y_v[pl.ds(kk * SC_LANES, SC_LANES)] = jnp.zeros((SC_LANES,), F32)
            for c in range(nchunk):
                if c + 1 < nchunk:
                    _sc_chunk_copy(tab_hbm, idx_v, rows_v, sem, c + 1).start()
                _sc_chunk_copy(tab_hbm, idx_v, rows_v, sem, c).wait()
                for kc in range(nstrip):
                    def rows8(rg, cr, c=c, kc=kc):
                        acc = [y_v[pl.ds(kc * SC_STRIP + kk * SC_LANES, SC_LANES)] for kk in range(nk)]
                        for rr in range(SC_RGROUP):
                            r = rg * SC_RGROUP + rr
                            cv = coef_v[pl.ds((c * SC_ROWS + r) * SC_LANES, SC_LANES)]
                            for kk in range(nk):
                                acc[kk] = acc[kk] + cv * rows_v[c % 2, r, pl.ds(kc * SC_STRIP + kk * SC_LANES, SC_LANES)]
                        for kk in range(nk):
                            y_v[pl.ds(kc * SC_STRIP + kk * SC_LANES, SC_LANES)] = acc[kk]
                        return cr

                    lax.fori_loop(0, SC_ROWS // SC_RGROUP, rows8, 0)
            pltpu.sync_copy(y_v, out_hbm.at[tok])
            return carry

        lax.fori_loop(0, per_w, token, 0)

    return k(idx, coefb, table)


def _sc_coef_kernel(a_ref, g_ref, after_ref, o_ref):
    grp = lax.broadcasted_iota(jnp.int32, (NSEL * SC_LANES, NSEL), 0) // SC_LANES
    fold = (grp == lax.broadcasted_iota(jnp.int32, (NSEL * SC_LANES, NSEL), 1)).astype(BF16)

    def pieces(x):
        hi = x.astype(BF16)
        r1 = x - hi.astype(F32)
        mid = r1.astype(BF16)
        return hi, mid, (r1 - mid.astype(F32)).astype(BF16)

    a = sum(jnp.dot(p, fold, preferred_element_type=F32) for p in pieces(a_ref[...]))
    coef = g_ref[...] * (0.5 * a * (1.0 + lax.erf(a * (0.5 ** 0.5))))
    nt = (((1,), (1,)), ((), ()))
    o_ref[...] = sum(lax.dot_general(p, fold, nt, preferred_element_type=F32) for p in pieces(coef))


def _sc_coef(a_part, g, after, *, tt=SC_SPLIT_BLOCK):
    n = a_part.shape[0]
    wide = pl.BlockSpec((tt, NSEL * SC_LANES), lambda i: (i, 0))
    return pl.pallas_call(
        _sc_coef_kernel, grid=(n // tt,),
        in_specs=[wide, pl.BlockSpec((tt, NSEL), lambda i: (i, 0)),
                  pl.BlockSpec((SUBLANES, D_MODEL), lambda i: (0, 0))], out_specs=wide,
        out_shape=jax.ShapeDtypeStruct((n, NSEL * SC_LANES), F32), name="sc_coef",
    )(a_part, g, after)


def _sc_residual_kernel(x1_ref, ff_ref, gate_ref, after_ref, o_ref):
    o_ref[...] = x1_ref[...] + gate_ref[0] * ff_ref[...]


def _sc_residual(x1, ff, gate, after, *, tok_offset, l_seq, tt=SC_SPLIT_BLOCK):
    n = x1.shape[0]
    row = pl.BlockSpec((tt, D_MODEL), lambda i: (i, 0))
    return pl.pallas_call(
        _sc_residual_kernel, grid=(n // tt,),
        in_specs=[row, row, pl.BlockSpec((1, 1, D_MODEL), lambda i: ((tok_offset + i * tt) // l_seq, 0, 0)),
                  pl.BlockSpec((SUBLANES, D_MODEL), lambda i: (0, 0))], out_specs=row,
        out_shape=jax.ShapeDtypeStruct((n, D_MODEL), F32), name="sc_residual",
    )(x1, ff, gate, after)


def _rope_tables(pos):
    half = DIFF_HD // 2
    inv = ROPE_THETA ** (-jnp.arange(half, dtype=F32) / half)
    ang = pos.astype(F32)[:, None] * inv[None, :]
    cos, sin = jnp.cos(ang), jnp.sin(ang)
    return jnp.tile(cos, (1, 4)), jnp.tile(jnp.concatenate([-sin, sin], axis=1), (1, 2))


def _mixers(x, mods, mod_tiles, tile, cos, sin, l_seq, h0, wts, attend, lambda_init, after):
    sh1, sc1, gt1, sh2, sc2 = mods
    q_g, k_g, v_g, r_g, la = _proj_gla(x, sh1, sc1, wts["n1"], wts["w_a"], wts["w_alpha"], wts["b_alpha"], after,
                                       tt=tile, mod_tiles=mod_tiles)
    q_bf, k_rows, k_bf, v_rows, v_bf = _proj_diff(x, sh1, sc1, wts["n1"], wts["w_b"], wts["q_gain"], wts["k_gain"],
                                                  cos, sin, wts["ones"], after, tt=tile, mod_tiles=mod_tiles)
    o_a, state = _gla(q_g, k_g, la, v_g, h0, l_seq=l_seq, tb=min(GLA_TILE, l_seq))
    o_b = attend(q_bf, k_bf, v_bf)
    post_tile = min(POST_TILE, tile)
    x1, h2 = _post_mixer(x, o_a, r_g, o_b, (sh1, sc1, gt1, sh2, sc2), wts["n1"], wts["n2"], wts["gla_gain"],
                         wts["diff_gain"], wts["w_gla_out"], wts["w_diff_out"], wts["w_g"], wts["w_out"],
                         tt=post_tile, mod_tiles=mod_tiles * (tile // post_tile), diff_out_scale=1.0 - lambda_init)
    idx, g = _peer_select(h2, wts["w_pq"], wts["k1"], wts["k2"])
    return x1, h2, idx, g, k_rows, v_rows, state


def kernel(x_prompt, x_sample, c_prompt, c_sample, cache_k, cache_v, state_gla, norm1_gain, norm2_gain, w_mod, b_mod, w_in, w_alpha, b_alpha, gla_gain, w_gla_out, q_gain, k_gain, lam_q1, lam_k1, lam_q2, lam_k2, diff_gain, w_diff_out, w_out, w_pq, sub_keys1, sub_keys2, peer_u, peer_v):
    bp, lp, _ = x_prompt.shape
    bs, ls, _ = x_sample.shape
    past = cache_k.shape[2]
    l = 0
    lambda_init = 0.8 - 0.6 * math.exp(-0.3 * l)
    nb = bp + bs
    c_all = jnp.concatenate([c_prompt, c_sample, jnp.zeros(((-nb) % 8, D_MODEL), F32)], axis=0)
    mod_all = _adaln(c_all, w_mod[l], b_mod[l])
    mods = [mod_all[:nb, i * D_MODEL:(i + 1) * D_MODEL] for i in range(6)]

    wi = w_in[l]
    o_glr = 2 * GLA_QK_W + 2 * GLA_V_W
    o_diff = o_glr + GLA_LOWRANK
    o_gate = o_diff + 2 * DIFF_QK_W + DIFF_V_W
    pad_lr = LANES - GLA_LOWRANK
    lane = jnp.arange(LANES)
    wts = dict(
        n1=norm1_gain[l][None], n2=norm2_gain[l][None],
        w_a=jnp.concatenate([wi[:, :o_diff], jnp.zeros((D_MODEL, pad_lr), F32)], axis=1).astype(BF16),
        w_alpha=jnp.concatenate([w_alpha[l], jnp.zeros((pad_lr, GLA_QK_W), F32)], axis=0).astype(BF16),
        b_alpha=b_alpha[l][None],
        w_b=wi[:, o_diff:o_gate].astype(BF16), w_g=wi[:, o_gate:].astype(BF16),
        q_gain=q_gain[l].reshape(1, LANES), k_gain=k_gain[l].reshape(1, LANES),
        ones=(lane[:, None] // DIFF_HD == lane[None, :] // DIFF_HD).astype(BF16),
        gla_gain=jnp.tile(gla_gain[l], GLA_HEADS)[None], diff_gain=jnp.tile(diff_gain[l], DIFF_HEADS)[None],
        w_gla_out=w_gla_out[l].astype(BF16), w_diff_out=w_diff_out[l].astype(BF16), w_out=w_out[l].astype(BF16),
        w_pq=w_pq[l].astype(BF16), k1=sub_keys1[l].astype(BF16), k2=sub_keys2[l].astype(BF16),
        table=jnp.concatenate([peer_u[l], peer_v[l]], axis=1).reshape(-1, 1, 2 * D_MODEL),
        u=peer_u[l], v=peer_v[l],
    )
    lam = (jnp.exp(jnp.sum(lam_q1[l] * lam_k1[l])) - jnp.exp(jnp.sum(lam_q2[l] * lam_k2[l])) + lambda_init).reshape(1, 1)

    no_order = jnp.zeros((SUBLANES, LANES), jnp.int32)
    gather = functools.partial(_peer_gather, table=wts["table"], tt=PEER_TOK_TILE)

    cos_p, sin_p = _rope_tables(jnp.arange(lp))

    def prompt_group(lo, hi, after):
        nsq = hi - lo
        attend = lambda q, k, v: _attn_prompt(q.reshape(nsq, lp, -1), k.reshape(nsq, lp, -1), v.reshape(nsq, lp, -1),
                                              lam).reshape(nsq * lp, -1)
        return _mixers(x_prompt[lo:hi].reshape(nsq * lp, D_MODEL), [m[lo:hi, None, :] for m in mods[:5]],
                       lp // PROJ_TILE, PROJ_TILE, cos_p, sin_p, lp, jnp.zeros((nsq, GLA_HEADS, GLA_DK, GLA_DV), F32),
                       wts, attend, lambda_init, after)

    n_sc_seq = bp // 2 if bp >= 4 else 0
    if n_sc_seq:
        x1a, h2a, idxa, ga, ka, va, sa = prompt_group(0, n_sc_seq, no_order)
        a_part = _sc_udot(idxa, h2a, wts["u"])
        x1b, h2b, idxb, gb, kb, vb, sb = prompt_group(n_sc_seq, bp, idxa[:SUBLANES])
        gates_b = mods[5][n_sc_seq:bp, None, :]
        n_first = round(x1b.shape[0] // SC_SPLIT_BLOCK * TC_FIRST_SHARE) * SC_SPLIT_BLOCK
        y_b1 = gather(idxb, gb, h2b, x1b, gates=gates_b, l_seq=lp, row0=0, n_tok=n_first)
        coefb = _sc_coef(a_part, ga, y_b1[:SUBLANES])
        ff = _sc_vsum(idxa, coefb, wts["v"])
        y_b2 = gather(idxb, gb, h2b, x1b, gates=gates_b, l_seq=lp, row0=n_first, n_tok=x1b.shape[0] - n_first)
        y_a = _sc_residual(x1a, ff, mods[5][:n_sc_seq, None, :], y_b2[:SUBLANES], tok_offset=0, l_seq=lp)
        yp = jnp.concatenate([y_a, y_b1, y_b2], axis=0)
        kp, vp, sp = (jnp.concatenate(p, axis=0) for p in ((ka, kb), (va, vb), (sa, sb)))
    else:
        x1p, h2p, idxp, gp, kp, vp, sp = prompt_group(0, bp, no_order)
        yp = gather(idxp, gp, h2p, x1p, gates=mods[5][:bp, None, :], l_seq=lp)

    seq_per_tile = SAMPLE_TILE // ls
    cos_s, sin_s = _rope_tables(jnp.tile(past + jnp.arange(ls), seq_per_tile))
    mods_s = [jnp.repeat(m[bp:nb], ls, axis=0).reshape(-1, SAMPLE_TILE, D_MODEL) for m in mods[:5]]
    attend_s = lambda q, k, v: _attn_sample(q, k, v, cache_k[l], cache_v[l], lam, l_seq=ls)
    x1s, h2s, idxs, gs, ks, vs, ss = _mixers(x_sample.reshape(bs * ls, D_MODEL), mods_s, 1, SAMPLE_TILE, cos_s, sin_s,
                                             ls, state_gla[l], wts, attend_s, lambda_init, no_order)
    ys = gather(idxs, gs, h2s, x1s, gates=mods[5][bp:nb, None, :], l_seq=ls)

    return (yp.reshape(bp, lp, D_MODEL), ys.reshape(bs, ls, D_MODEL),
            kp.reshape(1, bp, lp, DIFF_HEADS, 2 * DIFF_HD), vp.reshape(1, bp, lp, DIFF_HEADS, DIFF_VD), sp[None],
            ks.reshape(1, bs, ls, DIFF_HEADS, 2 * DIFF_HD), vs.reshape(1, bs, ls, DIFF_HEADS, DIFF_VD), ss[None])
```

```python
import math
import functools
import jax, jax.numpy as jnp
from jax import lax
from jax.experimental import pallas as pl
from jax.experimental.pallas import tpu as pltpu
from jax.experimental.pallas import tpu_sc as plsc

D_MODEL = 1024
CHUNK = 64
EPS = 1e-6
GLA_HEADS = 4
GLA_DK = 128
GLA_DV = 256
GLA_LOWRANK = 16
GLA_TAU = 16.0
GLA_BLOCK = 16
DIFF_HEADS = 8
DIFF_HD = 64
DIFF_VD = 128
ROPE_THETA = 10000.0
PEER_HEADS = 8
PEER_NKEYS = 128
PEER_DKEY = 256
PEER_TOPK = 16
GLA_QK_W = GLA_HEADS * GLA_DK
GLA_V_W = GLA_HEADS * GLA_DV
DIFF_QK_W = DIFF_HEADS * 2 * DIFF_HD
DIFF_V_W = DIFF_HEADS * DIFF_VD
F32 = jnp.float32
BF16 = jnp.bfloat16
NSEL = PEER_HEADS * PEER_TOPK
LANES = 128
SUBLANES = 8

PROJ_TILE = 512
POST_TILE = 256
GLA_TILE = 256
SAMPLE_TILE = 256
PEER_TOK_TILE = 8
ATTN_NEG = -1e30
ATTN_TILE = 1024
SELECT_TILE = 1024
SC_LANES = 16
SC_WORKERS = 32
SC_ROWS = 32
SC_STRIP = 256
SC_RGROUP = 8
SC_UGROUP = 1
SC_SPLIT_BLOCK = 512
TC_FIRST_SHARE = 16 / 32


def _mod_kernel(c_ref, w_ref, b_ref, o_ref):
    c = c_ref[...]
    s = c * jax.nn.sigmoid(c)
    o_ref[...] = jnp.dot(s.astype(BF16), w_ref[...].astype(BF16), preferred_element_type=F32) + b_ref[...]


def _adaln(c, w_mod, b_mod):
    n = c.shape[0]
    tn = 1536
    return pl.pallas_call(
        _mod_kernel,
        grid=(6 * D_MODEL // tn,),
        in_specs=[pl.BlockSpec((n, D_MODEL), lambda j: (0, 0)),
                  pl.BlockSpec((D_MODEL, tn), lambda j: (0, j)),
                  pl.BlockSpec((1, tn), lambda j: (0, j))],
        out_specs=pl.BlockSpec((n, tn), lambda j: (0, j)),
        out_shape=jax.ShapeDtypeStruct((n, 6 * D_MODEL), F32),
        name="adaln",
    )(c, w_mod, b_mod.reshape(1, -1))


def _modulated_norm(x, gain, scale, shift):
    y = x * lax.rsqrt(jnp.mean(x * x, axis=-1, keepdims=True) + EPS)
    return (y * gain) * (1.0 + scale) + shift


def _proj_gla_kernel(x_ref, sh_ref, sc_ref, gain_ref, w_ref, wa_ref, ba_ref, after_ref,
                     q_ref, k_ref, v_ref, r_ref, la_ref):
    h = _modulated_norm(x_ref[...], gain_ref[...], sc_ref[0], sh_ref[0]).astype(BF16)
    z = jnp.dot(h, w_ref[...], preferred_element_type=F32)
    q_ref[...] = z[:, 0:GLA_QK_W] * (GLA_DK ** -0.5)
    k_ref[...] = z[:, GLA_QK_W:2 * GLA_QK_W]
    v_ref[...] = z[:, 2 * GLA_QK_W:2 * GLA_QK_W + GLA_V_W]
    r_ref[...] = z[:, 2 * GLA_QK_W + GLA_V_W:2 * GLA_QK_W + 2 * GLA_V_W]
    glr = z[:, 2 * GLA_QK_W + 2 * GLA_V_W:]
    a = jnp.dot(glr.astype(BF16), wa_ref[...], preferred_element_type=F32) + ba_ref[...]
    la_ref[...] = jax.nn.log_sigmoid(a) * (1.0 / GLA_TAU)


def _group_mean_square(x, ones_ref):
    s = x * x
    hi = s.astype(BF16)
    lo = (s - hi.astype(F32)).astype(BF16)
    ss = (jnp.dot(hi, ones_ref[...], preferred_element_type=F32)
          + jnp.dot(lo, ones_ref[...], preferred_element_type=F32))
    return ss * (1.0 / DIFF_HD)


def _norm_rope(x, gain, cos, sin_signed, ones_ref):
    outs = []
    lane = lax.broadcasted_iota(jnp.int32, (x.shape[0], LANES), 1)
    first_half = (lane % DIFF_HD) < (DIFF_HD // 2)
    for b in range(DIFF_QK_W // LANES):
        xb = x[:, b * LANES:(b + 1) * LANES]
        y = xb * lax.rsqrt(_group_mean_square(xb, ones_ref) + EPS) * gain
        rot = jnp.where(first_half, pltpu.roll(y, LANES - DIFF_HD // 2, axis=1), pltpu.roll(y, DIFF_HD // 2, axis=1))
        outs.append(y * cos + rot * sin_signed)
    return jnp.concatenate(outs, axis=1)


def _proj_diff_kernel(x_ref, sh_ref, sc_ref, gain_ref, w_ref, qg_ref, kg_ref, cos_ref, sin_ref, ones_ref, after_ref,
                      qb_ref, k_ref, kb_ref, v_ref, vb_ref, *, q_scale):
    h = _modulated_norm(x_ref[...], gain_ref[...], sc_ref[0], sh_ref[0]).astype(BF16)
    z = jnp.dot(h, w_ref[...], preferred_element_type=F32)
    cos, sin = cos_ref[...], sin_ref[...]
    q = _norm_rope(z[:, 0:DIFF_QK_W], qg_ref[...], cos, sin, ones_ref)
    qb_ref[...] = (q * q_scale).astype(BF16)
    k = _norm_rope(z[:, DIFF_QK_W:2 * DIFF_QK_W], kg_ref[...], cos, sin, ones_ref)
    k_ref[...] = k
    kb_ref[...] = k.astype(BF16)
    v = z[:, 2 * DIFF_QK_W:2 * DIFF_QK_W + DIFF_V_W]
    v_ref[...] = v
    vb_ref[...] = v.astype(BF16)


def _order_spec():
    return pl.BlockSpec((SUBLANES, LANES), lambda i: (0, 0))


def _proj_gla(x, shift, scale, gain, w_a, w_alpha_pad, b_alpha, after, *, tt, mod_tiles):
    n = x.shape[0]
    assert n % tt == 0
    bmap = lambda i: (i // mod_tiles, 0, 0)
    mod = pl.BlockSpec((1,) + shift.shape[1:], bmap)
    row = lambda w: pl.BlockSpec((tt, w), lambda i: (i, 0))
    const = lambda a: pl.BlockSpec(a.shape, lambda i: (0,) * a.ndim)
    return pl.pallas_call(
        _proj_gla_kernel,
        grid=(n // tt,),
        in_specs=[row(D_MODEL), mod, mod,
                  const(gain), const(w_a), const(w_alpha_pad), const(b_alpha), _order_spec()],
        out_specs=[row(GLA_QK_W), row(GLA_QK_W), row(GLA_V_W), row(GLA_V_W), row(GLA_QK_W)],
        out_shape=[jax.ShapeDtypeStruct((n, GLA_QK_W), F32), jax.ShapeDtypeStruct((n, GLA_QK_W), F32),
                   jax.ShapeDtypeStruct((n, GLA_V_W), F32), jax.ShapeDtypeStruct((n, GLA_V_W), F32),
                   jax.ShapeDtypeStruct((n, GLA_QK_W), F32)],
        compiler_params=pltpu.CompilerParams(dimension_semantics=("arbitrary",), vmem_limit_bytes=48 * 1024 * 1024),
        name="proj_gla",
    )(x, shift, scale, gain, w_a, w_alpha_pad, b_alpha, after)


def _proj_diff(x, shift, scale, gain, w_b, q_gain128, k_gain128, cos128, sin128, ones128, after, *, tt, mod_tiles):
    n = x.shape[0]
    assert n % tt == 0 and cos128.shape[0] % tt == 0
    bmap = lambda i: (i // mod_tiles, 0, 0)
    mod = pl.BlockSpec((1,) + shift.shape[1:], bmap)
    pos_tiles = cos128.shape[0] // tt
    row = lambda w: pl.BlockSpec((tt, w), lambda i: (i, 0))
    const = lambda a: pl.BlockSpec(a.shape, lambda i: (0,) * a.ndim)
    pos = pl.BlockSpec((tt, LANES), lambda i: (i % pos_tiles, 0))
    q_scale = math.log2(math.e) * DIFF_HD ** -0.5
    return pl.pallas_call(
        functools.partial(_proj_diff_kernel, q_scale=q_scale),
        grid=(n // tt,),
        in_specs=[row(D_MODEL), mod, mod,
                  const(gain), const(w_b), const(q_gain128), const(k_gain128), pos, pos, const(ones128),
                  _order_spec()],
        out_specs=[row(DIFF_QK_W)] * 5,
        out_shape=[jax.ShapeDtypeStruct((n, DIFF_QK_W), BF16), jax.ShapeDtypeStruct((n, DIFF_QK_W), F32),
                   jax.ShapeDtypeStruct((n, DIFF_QK_W), BF16), jax.ShapeDtypeStruct((n, DIFF_V_W), F32),
                   jax.ShapeDtypeStruct((n, DIFF_V_W), BF16)],
        compiler_params=pltpu.CompilerParams(dimension_semantics=("arbitrary",), vmem_limit_bytes=48 * 1024 * 1024),
        name="proj_diff",
    )(x, shift, scale, gain, w_b, q_gain128, k_gain128, cos128, sin128, ones128, after)


def _gla_kernel(q_ref, k_ref, la_ref, v_ref, h0_ref, o_ref, hT_ref, st_ref, *, tb):
    i = pl.program_id(2)
    nb = tb // GLA_BLOCK

    @pl.when(i == 0)
    def _init():
        st_ref[...] = h0_ref[0, 0].T

    row = lax.broadcasted_iota(jnp.int32, (tb, tb), 0)
    col = lax.broadcasted_iota(jnp.int32, (tb, tb), 1)
    same = (row // GLA_BLOCK) == (col // GLA_BLOCK)
    causal = same & (col <= row)
    la = la_ref[...]
    hp = lax.Precision.HIGHEST
    b = jnp.dot(causal.astype(F32), la, precision=hp, preferred_element_type=F32)
    b_last = jnp.dot(same.astype(F32), la, precision=hp, preferred_element_type=F32)
    q, k, v = q_ref[...], k_ref[...], v_ref[...]
    qe = (q * jnp.exp(b)).astype(BF16)
    ke = (k * jnp.exp(-b)).astype(BF16)
    kd = (k * jnp.exp(b_last - b)).astype(BF16)
    vb = v.astype(BF16)
    a = lax.dot_general(qe, ke, (((1,), (1,)), ((), ())), preferred_element_type=F32)
    a = jnp.where(causal, a, 0.0).astype(BF16)
    o_intra = jnp.dot(a, vb, preferred_element_type=F32)

    st = st_ref[...]
    outs = []
    for n in range(nb):
        rs = slice(n * GLA_BLOCK, (n + 1) * GLA_BLOCK)
        o_inter = lax.dot_general(qe[rs], st.astype(BF16), (((1,), (1,)), ((), ())), preferred_element_type=F32)
        outs.append(o_inter + o_intra[rs])
        u = lax.dot_general(vb[rs], kd[rs], (((0,), (0,)), ((), ())), preferred_element_type=F32)
        st = jnp.exp(b_last[n * GLA_BLOCK:n * GLA_BLOCK + 1, :]) * st + u
    o_ref[...] = jnp.concatenate(outs, axis=0)
    st_ref[...] = st

    @pl.when(i == pl.num_programs(2) - 1)
    def _fin():
        hT_ref[0, 0] = st.T


def _gla(q, k, la, v, h0, *, l_seq, tb):
    n = q.shape[0]
    bsz = n // l_seq
    nt = l_seq // tb
    assert l_seq % tb == 0 and tb % GLA_BLOCK == 0
    tok = lambda w: pl.BlockSpec((tb, w), lambda b, h, i: (b * nt + i, h))
    st = pl.BlockSpec((1, 1, GLA_DK, GLA_DV), lambda b, h, i: (b, h, 0, 0))
    return pl.pallas_call(
        functools.partial(_gla_kernel, tb=tb),
        grid=(bsz, GLA_HEADS, nt),
        in_specs=[tok(GLA_DK), tok(GLA_DK), tok(GLA_DK), tok(GLA_DV), st],
        out_specs=[tok(GLA_DV), st],
        out_shape=[jax.ShapeDtypeStruct((n, GLA_V_W), F32),
                   jax.ShapeDtypeStruct((bsz, GLA_HEADS, GLA_DK, GLA_DV), F32)],
        scratch_shapes=[pltpu.VMEM((GLA_DV, GLA_DK), F32)],
        compiler_params=pltpu.CompilerParams(dimension_semantics=("arbitrary", "arbitrary", "arbitrary")),
        name="gla",
    )(q, k, la, v, h0)


def _split_maps(q):
    lane = lax.broadcasted_iota(jnp.int32, q.shape, 1)
    zero = jnp.zeros_like(q)
    return jnp.concatenate([jnp.where(lane < DIFF_HD, q, zero), jnp.where(lane >= DIFF_HD, q, zero)], axis=0)


def _attn_prompt_kernel(lam_ref, q_ref, k_ref, v_ref, after_ref, o_ref, vt_ref, *, tq, tk):
    i = pl.program_id(2)
    n_kt = v_ref.shape[1] // tk

    @pl.when(i == 0)
    def _transpose_v():
        def body(j, c):
            vt_ref[j] = v_ref[0, pl.ds(j * tk, tk), :].astype(F32).T.astype(BF16)
            return c
        lax.fori_loop(0, n_kt, body, 0)

    qp = _split_maps(q_ref[0])

    def step(j, carry, masked, koff=0):
        m, l, acc = carry
        kt = k_ref[0, pl.ds(j * tk, tk), :]
        st = lax.dot_general(kt, qp, (((1,), (1,)), ((), ())), preferred_element_type=F32)
        if masked:
            kc = (lax.broadcasted_iota(jnp.int32, st.shape, 0) + koff) // CHUNK
            qc = (lax.broadcasted_iota(jnp.int32, st.shape, 1) % tq) // CHUNK
            st = jnp.where(qc >= kc, st, ATTN_NEG)
        m_new = jnp.maximum(m, jnp.max(st, axis=0, keepdims=True))
        alpha = jnp.exp2(m - m_new)
        p = jnp.exp2(st - m_new)
        l = alpha * l + jnp.sum(p, axis=0, keepdims=True)
        acc = alpha * acc + jnp.dot(vt_ref[j], p.astype(BF16), preferred_element_type=F32)
        return m_new, l, acc

    carry = (jnp.full((1, 2 * tq), ATTN_NEG, F32), jnp.zeros((1, 2 * tq), F32), jnp.zeros((LANES, 2 * tq), F32))
    r = tq // tk
    carry = lax.fori_loop(0, i * r, functools.partial(step, masked=False), carry)
    for d in range(r):
        carry = step(i * r + d, carry, True, koff=d * tk)
    m, l, acc = carry
    o = acc / l
    ot = o[:, :tq] - lam_ref[0, 0] * o[:, tq:]
    o_ref[0] = ot.T


def _attn_prompt(q, k, v, lam, after, *, seq0, nseq, tq=ATTN_TILE, tk=ATTN_TILE):
    _, l, w = q.shape
    h = w // LANES
    assert l % tq == 0 and tq % tk == 0 and tk % CHUNK == 0
    return pl.pallas_call(
        functools.partial(_attn_prompt_kernel, tq=tq, tk=tk),
        grid=(nseq, h, l // tq),
        in_specs=[
            pl.BlockSpec(memory_space=pltpu.SMEM),
            pl.BlockSpec((1, tq, LANES), lambda bi, hi, i: (seq0 + bi, i, hi)),
            pl.BlockSpec((1, l, LANES), lambda bi, hi, i: (seq0 + bi, 0, hi)),
            pl.BlockSpec((1, l, LANES), lambda bi, hi, i: (seq0 + bi, 0, hi)),
            pl.BlockSpec((SUBLANES, LANES), lambda bi, hi, i: (0, 0)),
        ],
        out_specs=pl.BlockSpec((1, tq, LANES), lambda bi, hi, i: (bi, i, hi)),
        out_shape=jax.ShapeDtypeStruct((nseq, l, w), F32),
        scratch_shapes=[pltpu.VMEM((l // tk, LANES, tk), BF16)],
        compiler_params=pltpu.CompilerParams(dimension_semantics=("arbitrary", "arbitrary", "arbitrary"),
                                             vmem_limit_bytes=40 * 1024 * 1024),
        name="attn_prompt",
    )(lam, q, k, v, after)


def _attn_sample_kernel(lam_ref, q_ref, kn_ref, vn_ref, kc_ref, vc_ref, o_ref, m_ref, l_ref, acc_ref, *, heads):
    j = pl.program_id(1)
    lq = q_ref.shape[0]
    nt = (((1,), (1,)), ((), ()))

    @pl.when(j == 0)
    def _init():
        m_ref[...] = jnp.full(m_ref.shape, ATTN_NEG, F32)
        l_ref[...] = jnp.zeros(l_ref.shape, F32)
        acc_ref[...] = jnp.zeros(acc_ref.shape, F32)

    def update(h, k, v):
        qp = _split_maps(q_ref[:, h * LANES:(h + 1) * LANES])
        s = lax.dot_general(qp, k, nt, preferred_element_type=F32)
        m_old = m_ref[h]
        m_new = jnp.maximum(m_old, jnp.max(s, axis=-1, keepdims=True))
        alpha = jnp.exp2(m_old - m_new)
        p = jnp.exp2(s - m_new)
        l_ref[h] = alpha * l_ref[h] + jnp.sum(p, axis=-1, keepdims=True)
        acc_ref[h] = alpha * acc_ref[h] + jnp.dot(p.astype(BF16), v, preferred_element_type=F32)
        m_ref[h] = m_new

    for h in range(heads):
        update(h, kc_ref[:, h, :].astype(BF16), vc_ref[:, h, :].astype(BF16))

    @pl.when(j == pl.num_programs(1) - 1)
    def _fin():
        for h in range(heads):
            update(h, kn_ref[:, h * LANES:(h + 1) * LANES], vn_ref[:, h * LANES:(h + 1) * LANES])
            o = acc_ref[h] / l_ref[h]
            o_ref[:, h * LANES:(h + 1) * LANES] = o[:lq] - lam_ref[0, 0] * o[lq:]


def _attn_sample(q, k_new, v_new, cache_k, cache_v, lam, *, l_seq, tk=1024):
    n, w = q.shape
    bsz, past, heads, _ = cache_k.shape
    tk = min(tk, past)
    assert n == bsz * l_seq and w == heads * LANES and past % tk == 0
    tok = pl.BlockSpec((l_seq, w), lambda b, j: (b, 0))
    cache = pl.BlockSpec((None, tk, heads, LANES), lambda b, j: (b, j, 0, 0))
    return pl.pallas_call(
        functools.partial(_attn_sample_kernel, heads=heads),
        grid=(bsz, past // tk),
        in_specs=[pl.BlockSpec(memory_space=pltpu.SMEM), tok, tok, tok, cache, cache],
        out_specs=tok,
        out_shape=jax.ShapeDtypeStruct((n, w), F32),
        scratch_shapes=[pltpu.VMEM((heads, 2 * l_seq, 1), F32), pltpu.VMEM((heads, 2 * l_seq, 1), F32),
                        pltpu.VMEM((heads, 2 * l_seq, LANES), F32)],
        compiler_params=pltpu.CompilerParams(dimension_semantics=("arbitrary", "arbitrary"),
                                             vmem_limit_bytes=40 * 1024 * 1024),
        name="attn_sample",
    )(lam, q, k_new, v_new, cache_k, cache_v)


def _head_norm(x, width):
    outs = []
    for g in range(x.shape[1] // width):
        xg = x[:, g * width:(g + 1) * width]
        outs.append(xg * lax.rsqrt(jnp.mean(xg * xg, axis=-1, keepdims=True) + EPS))
    return jnp.concatenate(outs, axis=1)


def _post_kernel(x_ref, oa_ref, gr_ref, ob_ref, sh1_ref, sc1_ref, gt1_ref, sh2_ref, sc2_ref,
                 n1_ref, n2_ref, ga_ref, gb_ref, wa_ref, wb_ref, wg_ref, wo_ref, x1_ref, h2_ref, *, diff_out_scale):
    x = x_ref[...]
    h = _modulated_norm(x, n1_ref[...], sc1_ref[0], sh1_ref[0]).astype(BF16)
    gates = jnp.dot(h, wg_ref[...], preferred_element_type=F32)
    gr = gr_ref[...]
    oa = _head_norm(oa_ref[...], GLA_DV) * ga_ref[...] * (gr * jax.nn.sigmoid(gr))
    ya = jnp.dot(oa.astype(BF16), wa_ref[...], preferred_element_type=F32)
    ob = _head_norm(ob_ref[...], DIFF_VD) * gb_ref[...] * diff_out_scale
    yb = jnp.dot(ob.astype(BF16), wb_ref[...], preferred_element_type=F32)
    merged = jax.nn.sigmoid(gates[:, :D_MODEL]) * ya + jax.nn.sigmoid(gates[:, D_MODEL:]) * yb
    mix = jnp.dot(merged.astype(BF16), wo_ref[...], preferred_element_type=F32)
    x1 = x + gt1_ref[0] * mix
    x1_ref[...] = x1
    h2_ref[...] = _modulated_norm(x1, n2_ref[...], sc2_ref[0], sh2_ref[0])


def _post_mixer(x, oa, gr, ob, mods, n1, n2, ga, gb, wa, wb, wg, wo, *, tt, mod_tiles, diff_out_scale):
    n = x.shape[0]
    assert n % tt == 0
    bmap = lambda i: (i // mod_tiles, 0, 0)
    mod = pl.BlockSpec((1,) + mods[0].shape[1:], bmap)
    row = pl.BlockSpec((tt, D_MODEL), lambda i: (i, 0))
    const = lambda a: pl.BlockSpec(a.shape, lambda i: (0,) * a.ndim)
    return pl.pallas_call(
        functools.partial(_post_kernel, diff_out_scale=diff_out_scale),
        grid=(n // tt,),
        in_specs=[row, row, row, row] + [mod] * 5 + [const(a) for a in (n1, n2, ga, gb, wa, wb, wg, wo)],
        out_specs=[row, row],
        out_shape=[jax.ShapeDtypeStruct((n, D_MODEL), F32), jax.ShapeDtypeStruct((n, D_MODEL), F32)],
        compiler_params=pltpu.CompilerParams(dimension_semantics=("arbitrary",), vmem_limit_bytes=48 * 1024 * 1024),
        name="post_mixer",
    )(x, oa, gr, ob, *mods, n1, n2, ga, gb, wa, wb, wg, wo)


def _topk_rows(s, k):
    r = s.shape[0]
    row = lax.broadcasted_iota(jnp.int32, s.shape, 0)
    vals, idxs = [], []
    for _ in range(k):
        mx = jnp.max(s, axis=0, keepdims=True)
        am = jnp.min(jnp.where(s == mx, row, r), axis=0, keepdims=True)
        vals.append(mx)
        idxs.append(am)
        s = jnp.where(row == am, -jnp.inf, s)
    return jnp.concatenate(vals, axis=0), jnp.concatenate(idxs, axis=0)


def _take_rows(tab, sel):
    out = jnp.zeros(sel.shape, tab.dtype)
    for a in range(tab.shape[0]):
        out = jnp.where(sel == a, tab[a:a + 1, :], out)
    return out


def _staircase():
    return [(a, b) for a in range(PEER_TOPK) for b in range(PEER_TOPK) if (a + 1) * (b + 1) <= PEER_TOPK]


def _peer_select_kernel(h2_ref, w_ref, k1_ref, k2_ref, idx_ref, g_ref, idx_acc, g_acc):
    q = jnp.dot(h2_ref[...].astype(BF16), w_ref[...], preferred_element_type=F32)
    nt = (((1,), (1,)), ((), ()))
    s1 = lax.dot_general(k1_ref[...], q[:, :PEER_NKEYS].astype(BF16), nt, preferred_element_type=F32)
    s2 = lax.dot_general(k2_ref[...], q[:, PEER_NKEYS:].astype(BF16), nt, preferred_element_type=F32)
    v1, i1 = _topk_rows(s1, PEER_TOPK)
    v2, i2 = _topk_rows(s2, PEER_TOPK)
    pairs = _staircase()
    neg = jnp.full(((-len(pairs)) % SUBLANES, v1.shape[1]), -jnp.inf, F32)
    cand = jnp.concatenate([v1[a:a + 1, :] + v2[b:b + 1, :] for a, b in pairs] + [neg], axis=0)
    sc, cr = _topk_rows(cand, PEER_TOPK)
    ra = jnp.zeros(cr.shape, jnp.int32)
    rb = jnp.zeros(cr.shape, jnp.int32)
    for r, (a, b) in enumerate(pairs):
        hit = cr == r
        ra = jnp.where(hit, a, ra)
        rb = jnp.where(hit, b, rb)
    e1 = _take_rows(i1, ra)
    e2 = _take_rows(i2, rb)
    p = jnp.exp(sc - sc[0:1, :])
    h = pl.program_id(1)
    rows = pl.ds(pl.multiple_of(h * PEER_TOPK, PEER_TOPK), PEER_TOPK)
    idx_acc[rows, :] = (e1 * PEER_NKEYS + e2).astype(F32)
    g_acc[rows, :] = p / jnp.sum(p, axis=0, keepdims=True)

    @pl.when(h == pl.num_programs(1) - 1)
    def _emit():
        idx_ref[...] = idx_acc[...].T.astype(jnp.int32)
        g_ref[...] = g_acc[...].T


def _peer_select(h2, w_pq_bf, k1_bf, k2_bf, *, tt=SELECT_TILE):
    n, d = h2.shape
    assert n % tt == 0
    return pl.pallas_call(
        _peer_select_kernel,
        grid=(n // tt, PEER_HEADS),
        in_specs=[
            pl.BlockSpec((tt, d), lambda i, h: (i, 0)),
            pl.BlockSpec((d, PEER_DKEY), lambda i, h: (0, h)),
            pl.BlockSpec((PEER_NKEYS, PEER_DKEY // 2), lambda i, h: (0, 0)),
            pl.BlockSpec((PEER_NKEYS, PEER_DKEY // 2), lambda i, h: (0, 0)),
        ],
        out_specs=[pl.BlockSpec((tt, NSEL), lambda i, h: (i, 0)),
                   pl.BlockSpec((tt, NSEL), lambda i, h: (i, 0))],
        out_shape=[jax.ShapeDtypeStruct((n, NSEL), jnp.int32),
                   jax.ShapeDtypeStruct((n, NSEL), F32)],
        scratch_shapes=[pltpu.VMEM((NSEL, tt), F32), pltpu.VMEM((NSEL, tt), F32)],
        compiler_params=pltpu.CompilerParams(dimension_semantics=("arbitrary", "arbitrary")),
        name="peer_select",
    )(h2, w_pq_bf, k1_bf, k2_bf)


def _peer_gather_kernel(idx_ref, h2_ref, g_ref, x1_ref, gate_ref, tab_ref, out_ref, buf, sem, *, tt):
    s = pl.program_id(0)
    n = pl.num_programs(0) - 1

    @pl.when(s < n)
    def _issue():
        slot = s % 2

        def issue_tok(t, carry):
            for j in range(NSEL):
                e = idx_ref[t, j]
                pltpu.make_async_copy(tab_ref.at[e], buf.at[slot, t, pl.ds(j, 1), :], sem.at[slot, t]).start()
            return carry

        lax.fori_loop(0, tt, issue_tok, 0)

    @pl.when(s >= 1)
    def _compute():
        slot = (s + 1) % 2
        eye = (lax.broadcasted_iota(jnp.int32, (NSEL, NSEL), 0)
               == lax.broadcasted_iota(jnp.int32, (NSEL, NSEL), 1))
        gate = gate_ref[0]

        def tok(t, carry):
            for j in range(NSEL):
                pltpu.make_async_copy(tab_ref.at[0], buf.at[slot, t, pl.ds(j, 1), :], sem.at[slot, t]).wait()
            x = h2_ref[pl.ds(t, 1), :]
            p = buf[slot, t, :, 0:LANES] * x[:, 0:LANES]
            for c in range(1, D_MODEL // LANES):
                p = p + buf[slot, t, :, c * LANES:(c + 1) * LANES] * x[:, c * LANES:(c + 1) * LANES]
            a = jnp.sum(p, axis=-1, keepdims=True)
            act = 0.5 * a * (1.0 + lax.erf(a * (0.5 ** 0.5)))
            grow = g_ref[pl.ds(t, 1), :]
            gcol = jnp.sum(jnp.where(eye, grow, 0.0), axis=-1, keepdims=True)
            coef = gcol * act
            y = jnp.sum(coef * buf[slot, t, :, D_MODEL:2 * D_MODEL], axis=0, keepdims=True)
            out_ref[pl.ds(t, 1), :] = x1_ref[pl.ds(t, 1), :] + gate * y
            return carry

        lax.fori_loop(0, tt, tok, 0)


def _peer_gather(idx, g, h2, x1, *, gates, table, tt, l_seq, row0=0, n_tok=None):
    n_tok = idx.shape[0] if n_tok is None else n_tok
    n_tiles = n_tok // tt
    base = row0 // tt
    assert n_tiles * tt == n_tok and base * tt == row0 and l_seq % tt == 0

    def cur(s):
        return jnp.maximum(s - 1, 0)

    return pl.pallas_call(
        functools.partial(_peer_gather_kernel, tt=tt),
        grid=(n_tiles + 1,),
        in_specs=[
            pl.BlockSpec((tt, NSEL), lambda s: (base + jnp.minimum(s, n_tiles - 1), 0), memory_space=pltpu.SMEM),
            pl.BlockSpec((tt, D_MODEL), lambda s: (base + cur(s), 0)),
            pl.BlockSpec((tt, NSEL), lambda s: (base + cur(s), 0)),
            pl.BlockSpec((tt, D_MODEL), lambda s: (base + cur(s), 0)),
            pl.BlockSpec((1, 1, D_MODEL), lambda s: ((row0 + cur(s) * tt) // l_seq, 0, 0)),
            pl.BlockSpec(memory_space=pl.ANY),
        ],
        out_specs=pl.BlockSpec((tt, D_MODEL), lambda s: (cur(s), 0)),
        out_shape=jax.ShapeDtypeStruct((n_tok, D_MODEL), F32),
        scratch_shapes=[pltpu.VMEM((2, tt, NSEL, 2 * D_MODEL), F32), pltpu.SemaphoreType.DMA((2, tt))],
        compiler_params=pltpu.CompilerParams(dimension_semantics=("arbitrary",),
                                             vmem_limit_bytes=48 * 1024 * 1024),
        name="peer_gather",
    )(idx, h2, g, x1, gates, table)


def _sc_chunk_copy(tab_hbm, idx_v, rows_v, sem, c):
    return pltpu.make_async_copy(tab_hbm.at[idx_v.at[pl.ds(c * SC_ROWS, SC_ROWS)]], rows_v.at[c % 2], sem.at[c % 2])


def _sc_udot(idx, x, table):
    ns = idx.shape[0]
    per_w = ns // SC_WORKERS
    assert per_w * SC_WORKERS == ns
    nchunk, nstrip, nk = NSEL // SC_ROWS, D_MODEL // SC_STRIP, SC_STRIP // SC_LANES

    @functools.partial(
        pl.kernel, mesh=plsc.VectorSubcoreMesh(core_axis_name="c", subcore_axis_name="s"),
        out_type=jax.ShapeDtypeStruct((ns, NSEL * SC_LANES), F32),
        scratch_types=[pltpu.VMEM((NSEL,), jnp.int32), pltpu.VMEM((D_MODEL,), F32),
                       pltpu.VMEM((2, SC_ROWS, D_MODEL), F32), pltpu.VMEM((NSEL * SC_LANES,), F32),
                       pltpu.SemaphoreType.DMA((2,))],
    )
    def k(idx_hbm, x_hbm, tab_hbm, out_hbm, idx_v, x_v, rows_v, acc_v, sem):
        base = (lax.axis_index("s") * 2 + lax.axis_index("c")) * per_w

        def token(i, carry):
            tok = base + i
            pltpu.sync_copy(idx_hbm.at[tok], idx_v)
            pltpu.sync_copy(x_hbm.at[tok], x_v)
            _sc_chunk_copy(tab_hbm, idx_v, rows_v, sem, 0).start()
            for c in range(nchunk):
                if c + 1 < nchunk:
                    _sc_chunk_copy(tab_hbm, idx_v, rows_v, sem, c + 1).start()
                _sc_chunk_copy(tab_hbm, idx_v, rows_v, sem, c).wait()
                for kc in range(nstrip):
                    xr = [x_v[pl.ds(kc * SC_STRIP + kk * SC_LANES, SC_LANES)] for kk in range(nk)]

                    def rows8(rg, cr, c=c, kc=kc, xr=xr):
                        for rr in range(SC_UGROUP):
                            r = rg * SC_UGROUP + rr
                            s = rows_v[c % 2, r, pl.ds(kc * SC_STRIP, SC_LANES)] * xr[0]
                            for kk in range(1, nk):
                                s = s + rows_v[c % 2, r, pl.ds(kc * SC_STRIP + kk * SC_LANES, SC_LANES)] * xr[kk]
                            dst = pl.ds((c * SC_ROWS + r) * SC_LANES, SC_LANES)
                            if kc == 0:
                                acc_v[dst] = s
                            else:
                                acc_v[dst] = acc_v[dst] + s
                        return cr

                    lax.fori_loop(0, SC_ROWS // SC_UGROUP, rows8, 0)
            pltpu.sync_copy(acc_v, out_hbm.at[tok])
            return carry

        lax.fori_loop(0, per_w, token, 0)

    return k(idx, x, table)


def _sc_vsum(idx, coefb, table):
    ns = idx.shape[0]
    per_w = ns // SC_WORKERS
    assert per_w * SC_WORKERS == ns
    nchunk, nstrip, nk = NSEL // SC_ROWS, D_MODEL // SC_STRIP, SC_STRIP // SC_LANES

    @functools.partial(
        pl.kernel, mesh=plsc.VectorSubcoreMesh(core_axis_name="c", subcore_axis_name="s"),
        out_type=jax.ShapeDtypeStruct((ns, D_MODEL), F32),
        scratch_types=[pltpu.VMEM((NSEL,), jnp.int32), pltpu.VMEM((NSEL * SC_LANES,), F32),
                       pltpu.VMEM((2, SC_ROWS, D_MODEL), F32), pltpu.VMEM((D_MODEL,), F32),
                       pltpu.SemaphoreType.DMA((2,))],
    )
    def k(idx_hbm, coef_hbm, tab_hbm, out_hbm, idx_v, coef_v, rows_v, y_v, sem):
        base = (lax.axis_index("s") * 2 + lax.axis_index("c")) * per_w

        def token(i, carry):
            tok = base + i
            pltpu.sync_copy(idx_hbm.at[tok], idx_v)
            pltpu.sync_copy(coef_hbm.at[tok], coef_v)
            _sc_chunk_copy(tab_hbm, idx_v, rows_v, sem, 0).start()
            for kk in range(D_MODEL // SC_LANES):
                y_v[pl.ds(kk * SC_LANES, SC_LANES)] = jnp.zeros((SC_LANES,), F32)
            for c in range(nchunk):
                if c + 1 < nchunk:
                    _sc_chunk_copy(tab_hbm, idx_v, rows_v, sem, c + 1).start()
                _sc_chunk_copy(tab_hbm, idx_v, rows_v, sem, c).wait()
                for kc in range(nstrip):
                    def rows8(rg, cr, c=c, kc=kc):
                        acc = [y_v[pl.ds(kc * SC_STRIP + kk * SC_LANES, SC_LANES)] for kk in range(nk)]
                        for rr in range(SC_RGROUP):
                            r = rg * SC_RGROUP + rr
                            cv = coef_v[pl.ds((c * SC_ROWS + r) * SC_LANES, SC_LANES)]
                            for kk in range(nk):
                                acc[kk] = acc[kk] + cv * rows_v[c % 2, r, pl.ds(kc * SC_STRIP + kk * SC_LANES, SC_LANES)]
                        for kk in range(nk):
                            y_v[pl.ds(kc * SC_STRIP + kk * SC_LANES, SC_LANES)] = acc[kk]
                        return cr

                    lax.fori_loop(0, SC_ROWS // SC_RGROUP, rows8, 0)
            pltpu.sync_copy(y_v, out_hbm.at[tok])
            return carry

        lax.fori_loop(0, per_w, token, 0)

    return k(idx, coefb, table)


def _sc_coef_kernel(a_ref, g_ref, after_ref, o_ref):
    grp = lax.broadcasted_iota(jnp.int32, (NSEL * SC_LANES, NSEL), 0) // SC_LANES
    fold = (grp == lax.broadcasted_iota(jnp.int32, (NSEL * SC_LANES, NSEL), 1)).astype(BF16)

    def pieces(x):
        hi = x.astype(BF16)
        r1 = x - hi.astype(F32)
        mid = r1.astype(BF16)
        return hi, mid, (r1 - mid.astype(F32)).astype(BF16)

    a = sum(jnp.dot(p, fold, preferred_element_type=F32) for p in pieces(a_ref[...]))
    coef = g_ref[...] * (0.5 * a * (1.0 + lax.erf(a * (0.5 ** 0.5))))
    nt = (((1,), (1,)), ((), ()))
    o_ref[...] = sum(lax.dot_general(p, fold, nt, preferred_element_type=F32) for p in pieces(coef))


def _sc_coef(a_part, g, after, *, tt=SC_SPLIT_BLOCK):
    n = a_part.shape[0]
    wide = pl.BlockSpec((tt, NSEL * SC_LANES), lambda i: (i, 0))
    return pl.pallas_call(
        _sc_coef_kernel, grid=(n // tt,),
        in_specs=[wide, pl.BlockSpec((tt, NSEL), lambda i: (i, 0)),
                  pl.BlockSpec((SUBLANES, D_MODEL), lambda i: (0, 0))], out_specs=wide,
        out_shape=jax.ShapeDtypeStruct((n, NSEL * SC_LANES), F32), name="sc_coef",
    )(a_part, g, after)


def _sc_residual_kernel(x1_ref, ff_ref, gate_ref, after_ref, o_ref):
    o_ref[...] = x1_ref[...] + gate_ref[0] * ff_ref[...]


def _sc_residual(x1, ff, gate, after, *, tok_offset, l_seq, tt=SC_SPLIT_BLOCK):
    n = x1.shape[0]
    row = pl.BlockSpec((tt, D_MODEL), lambda i: (i, 0))
    return pl.pallas_call(
        _sc_residual_kernel, grid=(n // tt,),
        in_specs=[row, row, pl.BlockSpec((1, 1, D_MODEL), lambda i: ((tok_offset + i * tt) // l_seq, 0, 0)),
                  pl.BlockSpec((SUBLANES, D_MODEL), lambda i: (0, 0))], out_specs=row,
        out_shape=jax.ShapeDtypeStruct((n, D_MODEL), F32), name="sc_residual",
    )(x1, ff, gate, after)


def _rope_tables(pos):
    half = DIFF_HD // 2
    inv = ROPE_THETA ** (-jnp.arange(half, dtype=F32) / half)
    ang = pos.astype(F32)[:, None] * inv[None, :]
    cos, sin = jnp.cos(ang), jnp.sin(ang)
    return jnp.tile(cos, (1, 4)), jnp.tile(jnp.concatenate([-sin, sin], axis=1), (1, 2))


def _mixers(x, mods, mod_tiles, tile, l_seq, h0, wts, attend, lambda_init, after):
    sh1, sc1, gt1, sh2, sc2 = mods
    q_g, k_g, v_g, r_g, la = _proj_gla(x, sh1, sc1, wts["n1"], wts["w_a"], wts["w_alpha"], wts["b_alpha"], after,
                                       tt=tile, mod_tiles=mod_tiles)
    o_a, state = _gla(q_g, k_g, la, v_g, h0, l_seq=l_seq, tb=min(GLA_TILE, l_seq))
    o_b = attend()
    post_tile = min(POST_TILE, tile)
    x1, h2 = _post_mixer(x, o_a, r_g, o_b, (sh1, sc1, gt1, sh2, sc2), wts["n1"], wts["n2"], wts["gla_gain"],
                         wts["diff_gain"], wts["w_gla_out"], wts["w_diff_out"], wts["w_g"], wts["w_out"],
                         tt=post_tile, mod_tiles=mod_tiles * (tile // post_tile), diff_out_scale=1.0 - lambda_init)
    idx, g = _peer_select(h2, wts["w_pq"], wts["k1"], wts["k2"])
    return x1, h2, idx, g, state


def kernel(x_prompt, x_sample, c_prompt, c_sample, cache_k, cache_v, state_gla, norm1_gain, norm2_gain, w_mod, b_mod, w_in, w_alpha, b_alpha, gla_gain, w_gla_out, q_gain, k_gain, lam_q1, lam_k1, lam_q2, lam_k2, diff_gain, w_diff_out, w_out, w_pq, sub_keys1, sub_keys2, peer_u, peer_v):
    bp, lp, _ = x_prompt.shape
    bs, ls, _ = x_sample.shape
    past = cache_k.shape[2]
    l = 0
    lambda_init = 0.8 - 0.6 * math.exp(-0.3 * l)
    nb = bp + bs
    c_all = jnp.concatenate([c_prompt, c_sample, jnp.zeros(((-nb) % 8, D_MODEL), F32)], axis=0)
    mod_all = _adaln(c_all, w_mod[l], b_mod[l])
    mods = [mod_all[:nb, i * D_MODEL:(i + 1) * D_MODEL] for i in range(6)]

    wi = w_in[l]
    o_glr = 2 * GLA_QK_W + 2 * GLA_V_W
    o_diff = o_glr + GLA_LOWRANK
    o_gate = o_diff + 2 * DIFF_QK_W + DIFF_V_W
    pad_lr = LANES - GLA_LOWRANK
    lane = jnp.arange(LANES)
    wts = dict(
        n1=norm1_gain[l][None], n2=norm2_gain[l][None],
        w_a=jnp.concatenate([wi[:, :o_diff], jnp.zeros((D_MODEL, pad_lr), F32)], axis=1).astype(BF16),
        w_alpha=jnp.concatenate([w_alpha[l], jnp.zeros((pad_lr, GLA_QK_W), F32)], axis=0).astype(BF16),
        b_alpha=b_alpha[l][None],
        w_b=wi[:, o_diff:o_gate].astype(BF16), w_g=wi[:, o_gate:].astype(BF16),
        q_gain=q_gain[l].reshape(1, LANES), k_gain=k_gain[l].reshape(1, LANES),
        ones=(lane[:, None] // DIFF_HD == lane[None, :] // DIFF_HD).astype(BF16),
        gla_gain=jnp.tile(gla_gain[l], GLA_HEADS)[None], diff_gain=jnp.tile(diff_gain[l], DIFF_HEADS)[None],
        w_gla_out=w_gla_out[l].astype(BF16), w_diff_out=w_diff_out[l].astype(BF16), w_out=w_out[l].astype(BF16),
        w_pq=w_pq[l].astype(BF16), k1=sub_keys1[l].astype(BF16), k2=sub_keys2[l].astype(BF16),
        table=jnp.concatenate([peer_u[l], peer_v[l]], axis=1).reshape(-1, 1, 2 * D_MODEL),
        u=peer_u[l], v=peer_v[l],
    )
    lam = (jnp.exp(jnp.sum(lam_q1[l] * lam_k1[l])) - jnp.exp(jnp.sum(lam_q2[l] * lam_k2[l])) + lambda_init).reshape(1, 1)

    no_order = jnp.zeros((SUBLANES, LANES), jnp.int32)
    gather = functools.partial(_peer_gather, table=wts["table"], tt=PEER_TOK_TILE)

    def diff_proj(x, mods_g, cos, sin, tile, mod_tiles):
        return _proj_diff(x, mods_g[0], mods_g[1], wts["n1"], wts["w_b"], wts["q_gain"], wts["k_gain"], cos, sin,
                          wts["ones"], no_order, tt=tile, mod_tiles=mod_tiles)

    xp = x_prompt.reshape(bp * lp, D_MODEL)
    mods_p = [m[:bp, None, :] for m in mods[:5]]
    q_p, kp, k_p, vp, v_p = diff_proj(xp, mods_p, *_rope_tables(jnp.arange(lp)), PROJ_TILE, lp // PROJ_TILE)
    q_p, k_p, v_p = (a.reshape(bp, lp, -1) for a in (q_p, k_p, v_p))

    def prompt_group(lo, hi, after):
        nsq = hi - lo
        attend = lambda: _attn_prompt(q_p, k_p, v_p, lam, after, seq0=lo, nseq=nsq).reshape(nsq * lp, -1)
        return _mixers(xp[lo * lp:hi * lp], [m[lo:hi] for m in mods_p], lp // PROJ_TILE, PROJ_TILE, lp,
                       jnp.zeros((nsq, GLA_HEADS, GLA_DK, GLA_DV), F32), wts, attend, lambda_init, after)

    n_sc_seq = bp // 2 if bp >= 4 else 0
    if n_sc_seq:
        x1a, h2a, idxa, ga, sa = prompt_group(0, n_sc_seq, no_order)
        a_part = _sc_udot(idxa, h2a, wts["u"])
        x1b, h2b, idxb, gb, sb = prompt_group(n_sc_seq, bp, idxa[:SUBLANES])
        gates_b = mods[5][n_sc_seq:bp, None, :]
        n_first = round(x1b.shape[0] // SC_SPLIT_BLOCK * TC_FIRST_SHARE) * SC_SPLIT_BLOCK
        y_b1 = gather(idxb, gb, h2b, x1b, gates=gates_b, l_seq=lp, row0=0, n_tok=n_first)
        coefb = _sc_coef(a_part, ga, y_b1[:SUBLANES])
        ff = _sc_vsum(idxa, coefb, wts["v"])
        y_b2 = gather(idxb, gb, h2b, x1b, gates=gates_b, l_seq=lp, row0=n_first, n_tok=x1b.shape[0] - n_first)
        y_a = _sc_residual(x1a, ff, mods[5][:n_sc_seq, None, :], y_b2[:SUBLANES], tok_offset=0, l_seq=lp)
        yp = jnp.concatenate([y_a, y_b1, y_b2], axis=0)
        sp = jnp.concatenate([sa, sb], axis=0)
    else:
        x1p, h2p, idxp, gp, sp = prompt_group(0, bp, no_order)
        yp = gather(idxp, gp, h2p, x1p, gates=mods[5][:bp, None, :], l_seq=lp)

    seq_per_tile = SAMPLE_TILE // ls
    xs = x_sample.reshape(bs * ls, D_MODEL)
    mods_s = [jnp.repeat(m[bp:nb], ls, axis=0).reshape(-1, SAMPLE_TILE, D_MODEL) for m in mods[:5]]
    q_s, ks, k_s, vs, v_s = diff_proj(xs, mods_s, *_rope_tables(jnp.tile(past + jnp.arange(ls), seq_per_tile)),
                                      SAMPLE_TILE, 1)
    attend_s = lambda: _attn_sample(q_s, k_s, v_s, cache_k[l], cache_v[l], lam, l_seq=ls)
    x1s, h2s, idxs, gs, ss = _mixers(xs, mods_s, 1, SAMPLE_TILE, ls, state_gla[l], wts, attend_s, lambda_init, no_order)
    ys = gather(idxs, gs, h2s, x1s, gates=mods[5][bp:nb, None, :], l_seq=ls)

    return (yp.reshape(bp, lp, D_MODEL), ys.reshape(bs, ls, D_MODEL),
            kp.reshape(1, bp, lp, DIFF_HEADS, 2 * DIFF_HD), vp.reshape(1, bp, lp, DIFF_HEADS, DIFF_VD), sp[None],
            ks.reshape(1, bs, ls, DIFF_HEADS, 2 * DIFF_HD), vs.reshape(1, bs, ls, DIFF_HEADS, DIFF_VD), ss[None])
```

```python
import math
import functools
import jax, jax.numpy as jnp
from jax import lax
from jax.experimental import pallas as pl
from jax.experimental.pallas import tpu as pltpu
from jax.experimental.pallas import tpu_sc as plsc

D_MODEL = 1024
CHUNK = 64
EPS = 1e-6
GLA_HEADS = 4
GLA_DK = 128
GLA_DV = 256
GLA_LOWRANK = 16
GLA_TAU = 16.0
GLA_BLOCK = 16
DIFF_HEADS = 8
DIFF_HD = 64
DIFF_VD = 128
ROPE_THETA = 10000.0
PEER_HEADS = 8
PEER_NKEYS = 128
PEER_DKEY = 256
PEER_TOPK = 16
GLA_QK_W = GLA_HEADS * GLA_DK
GLA_V_W = GLA_HEADS * GLA_DV
DIFF_QK_W = DIFF_HEADS * 2 * DIFF_HD
DIFF_V_W = DIFF_HEADS * DIFF_VD
F32 = jnp.float32
BF16 = jnp.bfloat16
NSEL = PEER_HEADS * PEER_TOPK
LANES = 128
SUBLANES = 8

PROJ_TILE = 512
POST_TILE = 256
GLA_TILE = 256
SAMPLE_TILE = 256
PEER_TOK_TILE = 8
ATTN_NEG = -1e30
ATTN_TILE = 1024
SELECT_TILE = 1024
SC_LANES = 16
SC_WORKERS = 32
SC_ROWS = 32
SC_NBUF = 3
SC_STRIP = 256
SC_RGROUP = 8
SC_UGROUP = 1
SC_SPLIT_BLOCK = 512
TC_FIRST_SHARE = 16 / 32


def _mod_kernel(c_ref, w_ref, b_ref, o_ref):
    c = c_ref[...]
    s = c * jax.nn.sigmoid(c)
    o_ref[...] = jnp.dot(s.astype(BF16), w_ref[...].astype(BF16), preferred_element_type=F32) + b_ref[...]


def _adaln(c, w_mod, b_mod):
    n = c.shape[0]
    tn = 1536
    return pl.pallas_call(
        _mod_kernel,
        grid=(6 * D_MODEL // tn,),
        in_specs=[pl.BlockSpec((n, D_MODEL), lambda j: (0, 0)),
                  pl.BlockSpec((D_MODEL, tn), lambda j: (0, j)),
                  pl.BlockSpec((1, tn), lambda j: (0, j))],
        out_specs=pl.BlockSpec((n, tn), lambda j: (0, j)),
        out_shape=jax.ShapeDtypeStruct((n, 6 * D_MODEL), F32),
        name="adaln",
    )(c, w_mod, b_mod.reshape(1, -1))


def _modulated_norm(x, gain, scale, shift):
    y = x * lax.rsqrt(jnp.mean(x * x, axis=-1, keepdims=True) + EPS)
    return (y * gain) * (1.0 + scale) + shift


def _proj_gla_kernel(x_ref, sh_ref, sc_ref, gain_ref, w_ref, wa_ref, ba_ref, after_ref,
                     q_ref, k_ref, v_ref, r_ref, la_ref):
    h = _modulated_norm(x_ref[...], gain_ref[...], sc_ref[0], sh_ref[0]).astype(BF16)
    z = jnp.dot(h, w_ref[...], preferred_element_type=F32)
    q_ref[...] = z[:, 0:GLA_QK_W] * (GLA_DK ** -0.5)
    k_ref[...] = z[:, GLA_QK_W:2 * GLA_QK_W]
    v_ref[...] = z[:, 2 * GLA_QK_W:2 * GLA_QK_W + GLA_V_W]
    r_ref[...] = z[:, 2 * GLA_QK_W + GLA_V_W:2 * GLA_QK_W + 2 * GLA_V_W]
    glr = z[:, 2 * GLA_QK_W + 2 * GLA_V_W:]
    a = jnp.dot(glr.astype(BF16), wa_ref[...], preferred_element_type=F32) + ba_ref[...]
    la_ref[...] = jax.nn.log_sigmoid(a) * (1.0 / GLA_TAU)


def _group_mean_square(x, ones_ref):
    s = x * x
    hi = s.astype(BF16)
    lo = (s - hi.astype(F32)).astype(BF16)
    ss = (jnp.dot(hi, ones_ref[...], preferred_element_type=F32)
          + jnp.dot(lo, ones_ref[...], preferred_element_type=F32))
    return ss * (1.0 / DIFF_HD)


def _norm_rope(x, gain, cos, sin_signed, ones_ref):
    outs = []
    lane = lax.broadcasted_iota(jnp.int32, (x.shape[0], LANES), 1)
    first_half = (lane % DIFF_HD) < (DIFF_HD // 2)
    for b in range(DIFF_QK_W // LANES):
        xb = x[:, b * LANES:(b + 1) * LANES]
        y = xb * lax.rsqrt(_group_mean_square(xb, ones_ref) + EPS) * gain
        rot = jnp.where(first_half, pltpu.roll(y, LANES - DIFF_HD // 2, axis=1), pltpu.roll(y, DIFF_HD // 2, axis=1))
        outs.append(y * cos + rot * sin_signed)
    return jnp.concatenate(outs, axis=1)


def _proj_diff_kernel(x_ref, sh_ref, sc_ref, gain_ref, w_ref, qg_ref, kg_ref, cos_ref, sin_ref, ones_ref, after_ref,
                      qb_ref, k_ref, kb_ref, v_ref, vb_ref, *, q_scale):
    h = _modulated_norm(x_ref[...], gain_ref[...], sc_ref[0], sh_ref[0]).astype(BF16)
    z = jnp.dot(h, w_ref[...], preferred_element_type=F32)
    cos, sin = cos_ref[...], sin_ref[...]
    q = _norm_rope(z[:, 0:DIFF_QK_W], qg_ref[...], cos, sin, ones_ref)
    qb_ref[...] = (q * q_scale).astype(BF16)
    k = _norm_rope(z[:, DIFF_QK_W:2 * DIFF_QK_W], kg_ref[...], cos, sin, ones_ref)
    k_ref[...] = k
    kb_ref[...] = k.astype(BF16)
    v = z[:, 2 * DIFF_QK_W:2 * DIFF_QK_W + DIFF_V_W]
    v_ref[...] = v
    vb_ref[...] = v.astype(BF16)


def _order_spec():
    return pl.BlockSpec((SUBLANES, LANES), lambda i: (0, 0))


def _proj_gla(x, shift, scale, gain, w_a, w_alpha_pad, b_alpha, after, *, tt, mod_tiles):
    n = x.shape[0]
    assert n % tt == 0
    bmap = lambda i: (i // mod_tiles, 0, 0)
    mod = pl.BlockSpec((1,) + shift.shape[1:], bmap)
    row = lambda w: pl.BlockSpec((tt, w), lambda i: (i, 0))
    const = lambda a: pl.BlockSpec(a.shape, lambda i: (0,) * a.ndim)
    return pl.pallas_call(
        _proj_gla_kernel,
        grid=(n // tt,),
        in_specs=[row(D_MODEL), mod, mod,
                  const(gain), const(w_a), const(w_alpha_pad), const(b_alpha), _order_spec()],
        out_specs=[row(GLA_QK_W), row(GLA_QK_W), row(GLA_V_W), row(GLA_V_W), row(GLA_QK_W)],
        out_shape=[jax.ShapeDtypeStruct((n, GLA_QK_W), F32), jax.ShapeDtypeStruct((n, GLA_QK_W), F32),
                   jax.ShapeDtypeStruct((n, GLA_V_W), F32), jax.ShapeDtypeStruct((n, GLA_V_W), F32),
                   jax.ShapeDtypeStruct((n, GLA_QK_W), F32)],
        compiler_params=pltpu.CompilerParams(dimension_semantics=("arbitrary",), vmem_limit_bytes=48 * 1024 * 1024),
        name="proj_gla",
    )(x, shift, scale, gain, w_a, w_alpha_pad, b_alpha, after)


def _proj_diff(x, shift, scale, gain, w_b, q_gain128, k_gain128, cos128, sin128, ones128, after, *, tt, mod_tiles):
    n = x.shape[0]
    assert n % tt == 0 and cos128.shape[0] % tt == 0
    bmap = lambda i: (i // mod_tiles, 0, 0)
    mod = pl.BlockSpec((1,) + shift.shape[1:], bmap)
    pos_tiles = cos128.shape[0] // tt
    row = lambda w: pl.BlockSpec((tt, w), lambda i: (i, 0))
    const = lambda a: pl.BlockSpec(a.shape, lambda i: (0,) * a.ndim)
    pos = pl.BlockSpec((tt, LANES), lambda i: (i % pos_tiles, 0))
    q_scale = math.log2(math.e) * DIFF_HD ** -0.5
    return pl.pallas_call(
        functools.partial(_proj_diff_kernel, q_scale=q_scale),
        grid=(n // tt,),
        in_specs=[row(D_MODEL), mod, mod,
                  const(gain), const(w_b), const(q_gain128), const(k_gain128), pos, pos, const(ones128),
                  _order_spec()],
        out_specs=[row(DIFF_QK_W)] * 5,
        out_shape=[jax.ShapeDtypeStruct((n, DIFF_QK_W), BF16), jax.ShapeDtypeStruct((n, DIFF_QK_W), F32),
                   jax.ShapeDtypeStruct((n, DIFF_QK_W), BF16), jax.ShapeDtypeStruct((n, DIFF_V_W), F32),
                   jax.ShapeDtypeStruct((n, DIFF_V_W), BF16)],
        compiler_params=pltpu.CompilerParams(dimension_semantics=("arbitrary",), vmem_limit_bytes=48 * 1024 * 1024),
        name="proj_diff",
    )(x, shift, scale, gain, w_b, q_gain128, k_gain128, cos128, sin128, ones128, after)


def _gla_kernel(q_ref, k_ref, la_ref, v_ref, h0_ref, o_ref, hT_ref, st_ref, *, tb):
    i = pl.program_id(2)
    nb = tb // GLA_BLOCK

    @pl.when(i == 0)
    def _init():
        st_ref[...] = h0_ref[0, 0].T

    row = lax.broadcasted_iota(jnp.int32, (tb, tb), 0)
    col = lax.broadcasted_iota(jnp.int32, (tb, tb), 1)
    same = (row // GLA_BLOCK) == (col // GLA_BLOCK)
    causal = same & (col <= row)
    la = la_ref[...]
    hp = lax.Precision.HIGHEST
    b = jnp.dot(causal.astype(F32), la, precision=hp, preferred_element_type=F32)
    b_last = jnp.dot(same.astype(F32), la, precision=hp, preferred_element_type=F32)
    q, k, v = q_ref[...], k_ref[...], v_ref[...]
    qe = (q * jnp.exp(b)).astype(BF16)
    ke = (k * jnp.exp(-b)).astype(BF16)
    kd = (k * jnp.exp(b_last - b)).astype(BF16)
    vb = v.astype(BF16)
    a = lax.dot_general(qe, ke, (((1,), (1,)), ((), ())), preferred_element_type=F32)
    a = jnp.where(causal, a, 0.0).astype(BF16)
    o_intra = jnp.dot(a, vb, preferred_element_type=F32)

    st = st_ref[...]
    outs = []
    for n in range(nb):
        rs = slice(n * GLA_BLOCK, (n + 1) * GLA_BLOCK)
        o_inter = lax.dot_general(qe[rs], st.astype(BF16), (((1,), (1,)), ((), ())), preferred_element_type=F32)
        outs.append(o_inter + o_intra[rs])
        u = lax.dot_general(vb[rs], kd[rs], (((0,), (0,)), ((), ())), preferred_element_type=F32)
        st = jnp.exp(b_last[n * GLA_BLOCK:n * GLA_BLOCK + 1, :]) * st + u
    o_ref[...] = jnp.concatenate(outs, axis=0)
    st_ref[...] = st

    @pl.when(i == pl.num_programs(2) - 1)
    def _fin():
        hT_ref[0, 0] = st.T


def _gla(q, k, la, v, h0, *, l_seq, tb):
    n = q.shape[0]
    bsz = n // l_seq
    nt = l_seq // tb
    assert l_seq % tb == 0 and tb % GLA_BLOCK == 0
    tok = lambda w: pl.BlockSpec((tb, w), lambda b, h, i: (b * nt + i, h))
    st = pl.BlockSpec((1, 1, GLA_DK, GLA_DV), lambda b, h, i: (b, h, 0, 0))
    return pl.pallas_call(
        functools.partial(_gla_kernel, tb=tb),
        grid=(bsz, GLA_HEADS, nt),
        in_specs=[tok(GLA_DK), tok(GLA_DK), tok(GLA_DK), tok(GLA_DV), st],
        out_specs=[tok(GLA_DV), st],
        out_shape=[jax.ShapeDtypeStruct((n, GLA_V_W), F32),
                   jax.ShapeDtypeStruct((bsz, GLA_HEADS, GLA_DK, GLA_DV), F32)],
        scratch_shapes=[pltpu.VMEM((GLA_DV, GLA_DK), F32)],
        compiler_params=pltpu.CompilerParams(dimension_semantics=("arbitrary", "arbitrary", "arbitrary")),
        name="gla",
    )(q, k, la, v, h0)


def _split_maps(q):
    lane = lax.broadcasted_iota(jnp.int32, q.shape, 1)
    zero = jnp.zeros_like(q)
    return jnp.concatenate([jnp.where(lane < DIFF_HD, q, zero), jnp.where(lane >= DIFF_HD, q, zero)], axis=0)


def _attn_prompt_kernel(lam_ref, q_ref, k_ref, v_ref, o_ref, vt_ref, *, tq, tk):
    i = pl.program_id(2)
    n_kt = v_ref.shape[1] // tk

    @pl.when(i == 0)
    def _transpose_v():
        def body(j, c):
            vt_ref[j] = v_ref[0, pl.ds(j * tk, tk), :].astype(F32).T.astype(BF16)
            return c
        lax.fori_loop(0, n_kt, body, 0)

    qp = _split_maps(q_ref[0])

    def step(j, carry, masked, koff=0):
        m, l, acc = carry
        kt = k_ref[0, pl.ds(j * tk, tk), :]
        st = lax.dot_general(kt, qp, (((1,), (1,)), ((), ())), preferred_element_type=F32)
        if masked:
            kc = (lax.broadcasted_iota(jnp.int32, st.shape, 0) + koff) // CHUNK
            qc = (lax.broadcasted_iota(jnp.int32, st.shape, 1) % tq) // CHUNK
            st = jnp.where(qc >= kc, st, ATTN_NEG)
        m_new = jnp.maximum(m, jnp.max(st, axis=0, keepdims=True))
        alpha = jnp.exp2(m - m_new)
        p = jnp.exp2(st - m_new)
        l = alpha * l + jnp.sum(p, axis=0, keepdims=True)
        acc = alpha * acc + jnp.dot(vt_ref[j], p.astype(BF16), preferred_element_type=F32)
        return m_new, l, acc

    carry = (jnp.full((1, 2 * tq), ATTN_NEG, F32), jnp.zeros((1, 2 * tq), F32), jnp.zeros((LANES, 2 * tq), F32))
    r = tq // tk
    carry = lax.fori_loop(0, i * r, functools.partial(step, masked=False), carry)
    for d in range(r):
        carry = step(i * r + d, carry, True, koff=d * tk)
    m, l, acc = carry
    o = acc / l
    ot = o[:, :tq] - lam_ref[0, 0] * o[:, tq:]
    o_ref[0] = ot.T


def _attn_prompt(q, k, v, lam, *, tq=ATTN_TILE, tk=ATTN_TILE):
    b, l, w = q.shape
    h = w // LANES
    assert l % tq == 0 and tq % tk == 0 and tk % CHUNK == 0
    return pl.pallas_call(
        functools.partial(_attn_prompt_kernel, tq=tq, tk=tk),
        grid=(b, h, l // tq),
        in_specs=[
            pl.BlockSpec(memory_space=pltpu.SMEM),
            pl.BlockSpec((1, tq, LANES), lambda bi, hi, i: (bi, i, hi)),
            pl.BlockSpec((1, l, LANES), lambda bi, hi, i: (bi, 0, hi)),
            pl.BlockSpec((1, l, LANES), lambda bi, hi, i: (bi, 0, hi)),
        ],
        out_specs=pl.BlockSpec((1, tq, LANES), lambda bi, hi, i: (bi, i, hi)),
        out_shape=jax.ShapeDtypeStruct((b, l, w), F32),
        scratch_shapes=[pltpu.VMEM((l // tk, LANES, tk), BF16)],
        compiler_params=pltpu.CompilerParams(dimension_semantics=("arbitrary", "arbitrary", "arbitrary"),
                                             vmem_limit_bytes=40 * 1024 * 1024),
        name="attn_prompt",
    )(lam, q, k, v)


def _attn_sample_kernel(lam_ref, q_ref, kn_ref, vn_ref, kc_ref, vc_ref, o_ref, m_ref, l_ref, acc_ref, *, heads):
    j = pl.program_id(1)
    lq = q_ref.shape[0]
    nt = (((1,), (1,)), ((), ()))

    @pl.when(j == 0)
    def _init():
        m_ref[...] = jnp.full(m_ref.shape, ATTN_NEG, F32)
        l_ref[...] = jnp.zeros(l_ref.shape, F32)
        acc_ref[...] = jnp.zeros(acc_ref.shape, F32)

    def update(h, k, v):
        qp = _split_maps(q_ref[:, h * LANES:(h + 1) * LANES])
        s = lax.dot_general(qp, k, nt, preferred_element_type=F32)
        m_old = m_ref[h]
        m_new = jnp.maximum(m_old, jnp.max(s, axis=-1, keepdims=True))
        alpha = jnp.exp2(m_old - m_new)
        p = jnp.exp2(s - m_new)
        l_ref[h] = alpha * l_ref[h] + jnp.sum(p, axis=-1, keepdims=True)
        acc_ref[h] = alpha * acc_ref[h] + jnp.dot(p.astype(BF16), v, preferred_element_type=F32)
        m_ref[h] = m_new

    for h in range(heads):
        update(h, kc_ref[:, h, :].astype(BF16), vc_ref[:, h, :].astype(BF16))

    @pl.when(j == pl.num_programs(1) - 1)
    def _fin():
        for h in range(heads):
            update(h, kn_ref[:, h * LANES:(h + 1) * LANES], vn_ref[:, h * LANES:(h + 1) * LANES])
            o = acc_ref[h] / l_ref[h]
            o_ref[:, h * LANES:(h + 1) * LANES] = o[:lq] - lam_ref[0, 0] * o[lq:]


def _attn_sample(q, k_new, v_new, cache_k, cache_v, lam, *, l_seq, tk=1024):
    n, w = q.shape
    bsz, past, heads, _ = cache_k.shape
    tk = min(tk, past)
    assert n == bsz * l_seq and w == heads * LANES and past % tk == 0
    tok = pl.BlockSpec((l_seq, w), lambda b, j: (b, 0))
    cache = pl.BlockSpec((None, tk, heads, LANES), lambda b, j: (b, j, 0, 0))
    return pl.pallas_call(
        functools.partial(_attn_sample_kernel, heads=heads),
        grid=(bsz, past // tk),
        in_specs=[pl.BlockSpec(memory_space=pltpu.SMEM), tok, tok, tok, cache, cache],
        out_specs=tok,
        out_shape=jax.ShapeDtypeStruct((n, w), F32),
        scratch_shapes=[pltpu.VMEM((heads, 2 * l_seq, 1), F32), pltpu.VMEM((heads, 2 * l_seq, 1), F32),
                        pltpu.VMEM((heads, 2 * l_seq, LANES), F32)],
        compiler_params=pltpu.CompilerParams(dimension_semantics=("arbitrary", "arbitrary"),
                                             vmem_limit_bytes=40 * 1024 * 1024),
        name="attn_sample",
    )(lam, q, k_new, v_new, cache_k, cache_v)


def _head_norm(x, width):
    outs = []
    for g in range(x.shape[1] // width):
        xg = x[:, g * width:(g + 1) * width]
        outs.append(xg * lax.rsqrt(jnp.mean(xg * xg, axis=-1, keepdims=True) + EPS))
    return jnp.concatenate(outs, axis=1)


def _post_kernel(x_ref, oa_ref, gr_ref, ob_ref, sh1_ref, sc1_ref, gt1_ref, sh2_ref, sc2_ref,
                 n1_ref, n2_ref, ga_ref, gb_ref, wa_ref, wb_ref, wg_ref, wo_ref, x1_ref, h2_ref, *, diff_out_scale):
    x = x_ref[...]
    h = _modulated_norm(x, n1_ref[...], sc1_ref[0], sh1_ref[0]).astype(BF16)
    gates = jnp.dot(h, wg_ref[...], preferred_element_type=F32)
    gr = gr_ref[...]
    oa = _head_norm(oa_ref[...], GLA_DV) * ga_ref[...] * (gr * jax.nn.sigmoid(gr))
    ya = jnp.dot(oa.astype(BF16), wa_ref[...], preferred_element_type=F32)
    ob = _head_norm(ob_ref[...], DIFF_VD) * gb_ref[...] * diff_out_scale
    yb = jnp.dot(ob.astype(BF16), wb_ref[...], preferred_element_type=F32)
    merged = jax.nn.sigmoid(gates[:, :D_MODEL]) * ya + jax.nn.sigmoid(gates[:, D_MODEL:]) * yb
    mix = jnp.dot(merged.astype(BF16), wo_ref[...], preferred_element_type=F32)
    x1 = x + gt1_ref[0] * mix
    x1_ref[...] = x1
    h2_ref[...] = _modulated_norm(x1, n2_ref[...], sc2_ref[0], sh2_ref[0])


def _post_mixer(x, oa, gr, ob, mods, n1, n2, ga, gb, wa, wb, wg, wo, *, tt, mod_tiles, diff_out_scale):
    n = x.shape[0]
    assert n % tt == 0
    bmap = lambda i: (i // mod_tiles, 0, 0)
    mod = pl.BlockSpec((1,) + mods[0].shape[1:], bmap)
    row = pl.BlockSpec((tt, D_MODEL), lambda i: (i, 0))
    const = lambda a: pl.BlockSpec(a.shape, lambda i: (0,) * a.ndim)
    return pl.pallas_call(
        functools.partial(_post_kernel, diff_out_scale=diff_out_scale),
        grid=(n // tt,),
        in_specs=[row, row, row, row] + [mod] * 5 + [const(a) for a in (n1, n2, ga, gb, wa, wb, wg, wo)],
        out_specs=[row, row],
        out_shape=[jax.ShapeDtypeStruct((n, D_MODEL), F32), jax.ShapeDtypeStruct((n, D_MODEL), F32)],
        compiler_params=pltpu.CompilerParams(dimension_semantics=("arbitrary",), vmem_limit_bytes=48 * 1024 * 1024),
        name="post_mixer",
    )(x, oa, gr, ob, *mods, n1, n2, ga, gb, wa, wb, wg, wo)


def _topk_rows(s, k):
    r = s.shape[0]
    row = lax.broadcasted_iota(jnp.int32, s.shape, 0)
    vals, idxs = [], []
    for _ in range(k):
        mx = jnp.max(s, axis=0, keepdims=True)
        am = jnp.min(jnp.where(s == mx, row, r), axis=0, keepdims=True)
        vals.append(mx)
        idxs.append(am)
        s = jnp.where(row == am, -jnp.inf, s)
    return jnp.concatenate(vals, axis=0), jnp.concatenate(idxs, axis=0)


def _take_rows(tab, sel):
    out = jnp.zeros(sel.shape, tab.dtype)
    for a in range(tab.shape[0]):
        out = jnp.where(sel == a, tab[a:a + 1, :], out)
    return out


def _staircase():
    return [(a, b) for a in range(PEER_TOPK) for b in range(PEER_TOPK) if (a + 1) * (b + 1) <= PEER_TOPK]


def _peer_select_kernel(h2_ref, w_ref, k1_ref, k2_ref, idx_ref, g_ref, idx_acc, g_acc):
    q = jnp.dot(h2_ref[...].astype(BF16), w_ref[...], preferred_element_type=F32)
    nt = (((1,), (1,)), ((), ()))
    s1 = lax.dot_general(k1_ref[...], q[:, :PEER_NKEYS].astype(BF16), nt, preferred_element_type=F32)
    s2 = lax.dot_general(k2_ref[...], q[:, PEER_NKEYS:].astype(BF16), nt, preferred_element_type=F32)
    v1, i1 = _topk_rows(s1, PEER_TOPK)
    v2, i2 = _topk_rows(s2, PEER_TOPK)
    pairs = _staircase()
    neg = jnp.full(((-len(pairs)) % SUBLANES, v1.shape[1]), -jnp.inf, F32)
    cand = jnp.concatenate([v1[a:a + 1, :] + v2[b:b + 1, :] for a, b in pairs] + [neg], axis=0)
    sc, cr = _topk_rows(cand, PEER_TOPK)
    ra = jnp.zeros(cr.shape, jnp.int32)
    rb = jnp.zeros(cr.shape, jnp.int32)
    for r, (a, b) in enumerate(pairs):
        hit = cr == r
        ra = jnp.where(hit, a, ra)
        rb = jnp.where(hit, b, rb)
    e1 = _take_rows(i1, ra)
    e2 = _take_rows(i2, rb)
    p = jnp.exp(sc - sc[0:1, :])
    h = pl.program_id(1)
    rows = pl.ds(pl.multiple_of(h * PEER_TOPK, PEER_TOPK), PEER_TOPK)
    idx_acc[rows, :] = (e1 * PEER_NKEYS + e2).astype(F32)
    g_acc[rows, :] = p / jnp.sum(p, axis=0, keepdims=True)

    @pl.when(h == pl.num_programs(1) - 1)
    def _emit():
        idx_ref[...] = idx_acc[...].T.astype(jnp.int32)
        g_ref[...] = g_acc[...].T


def _peer_select(h2, w_pq_bf, k1_bf, k2_bf, *, tt=SELECT_TILE):
    n, d = h2.shape
    assert n % tt == 0
    return pl.pallas_call(
        _peer_select_kernel,
        grid=(n // tt, PEER_HEADS),
        in_specs=[
            pl.BlockSpec((tt, d), lambda i, h: (i, 0)),
            pl.BlockSpec((d, PEER_DKEY), lambda i, h: (0, h)),
            pl.BlockSpec((PEER_NKEYS, PEER_DKEY // 2), lambda i, h: (0, 0)),
            pl.BlockSpec((PEER_NKEYS, PEER_DKEY // 2), lambda i, h: (0, 0)),
        ],
        out_specs=[pl.BlockSpec((tt, NSEL), lambda i, h: (i, 0)),
                   pl.BlockSpec((tt, NSEL), lambda i, h: (i, 0))],
        out_shape=[jax.ShapeDtypeStruct((n, NSEL), jnp.int32),
                   jax.ShapeDtypeStruct((n, NSEL), F32)],
        scratch_shapes=[pltpu.VMEM((NSEL, tt), F32), pltpu.VMEM((NSEL, tt), F32)],
        compiler_params=pltpu.CompilerParams(dimension_semantics=("arbitrary", "arbitrary")),
        name="peer_select",
    )(h2, w_pq_bf, k1_bf, k2_bf)


def _peer_gather_kernel(idx_ref, h2_ref, g_ref, x1_ref, gate_ref, tab_ref, out_ref, buf, sem, *, tt):
    s = pl.program_id(0)
    n = pl.num_programs(0) - 1

    @pl.when(s < n)
    def _issue():
        slot = s % 2

        def issue_tok(t, carry):
            for j in range(NSEL):
                e = idx_ref[t, j]
                pltpu.make_async_copy(tab_ref.at[e], buf.at[slot, t, pl.ds(j, 1), :], sem.at[slot, t]).start()
            return carry

        lax.fori_loop(0, tt, issue_tok, 0)

    @pl.when(s >= 1)
    def _compute():
        slot = (s + 1) % 2
        eye = (lax.broadcasted_iota(jnp.int32, (NSEL, NSEL), 0)
               == lax.broadcasted_iota(jnp.int32, (NSEL, NSEL), 1))
        gate = gate_ref[0]

        def tok(t, carry):
            for j in range(NSEL):
                pltpu.make_async_copy(tab_ref.at[0], buf.at[slot, t, pl.ds(j, 1), :], sem.at[slot, t]).wait()
            x = h2_ref[pl.ds(t, 1), :]
            p = buf[slot, t, :, 0:LANES] * x[:, 0:LANES]
            for c in range(1, D_MODEL // LANES):
                p = p + buf[slot, t, :, c * LANES:(c + 1) * LANES] * x[:, c * LANES:(c + 1) * LANES]
            a = jnp.sum(p, axis=-1, keepdims=True)
            act = 0.5 * a * (1.0 + lax.erf(a * (0.5 ** 0.5)))
            grow = g_ref[pl.ds(t, 1), :]
            gcol = jnp.sum(jnp.where(eye, grow, 0.0), axis=-1, keepdims=True)
            coef = gcol * act
            y = jnp.sum(coef * buf[slot, t, :, D_MODEL:2 * D_MODEL], axis=0, keepdims=True)
            out_ref[pl.ds(t, 1), :] = x1_ref[pl.ds(t, 1), :] + gate * y
            return carry

        lax.fori_loop(0, tt, tok, 0)


def _peer_gather(idx, g, h2, x1, *, gates, table, tt, l_seq, row0=0, n_tok=None):
    n_tok = idx.shape[0] if n_tok is None else n_tok
    n_tiles = n_tok // tt
    base = row0 // tt
    assert n_tiles * tt == n_tok and base * tt == row0 and l_seq % tt == 0

    def cur(s):
        return jnp.maximum(s - 1, 0)

    return pl.pallas_call(
        functools.partial(_peer_gather_kernel, tt=tt),
        grid=(n_tiles + 1,),
        in_specs=[
            pl.BlockSpec((tt, NSEL), lambda s: (base + jnp.minimum(s, n_tiles - 1), 0), memory_space=pltpu.SMEM),
            pl.BlockSpec((tt, D_MODEL), lambda s: (base + cur(s), 0)),
            pl.BlockSpec((tt, NSEL), lambda s: (base + cur(s), 0)),
            pl.BlockSpec((tt, D_MODEL), lambda s: (base + cur(s), 0)),
            pl.BlockSpec((1, 1, D_MODEL), lambda s: ((row0 + cur(s) * tt) // l_seq, 0, 0)),
            pl.BlockSpec(memory_space=pl.ANY),
        ],
        out_specs=pl.BlockSpec((tt, D_MODEL), lambda s: (cur(s), 0)),
        out_shape=jax.ShapeDtypeStruct((n_tok, D_MODEL), F32),
        scratch_shapes=[pltpu.VMEM((2, tt, NSEL, 2 * D_MODEL), F32), pltpu.SemaphoreType.DMA((2, tt))],
        compiler_params=pltpu.CompilerParams(dimension_semantics=("arbitrary",),
                                             vmem_limit_bytes=48 * 1024 * 1024),
        name="peer_gather",
    )(idx, h2, g, x1, gates, table)


def _sc_chunk_copy(tab_hbm, idx_v, rows_v, sem, c):
    return pltpu.make_async_copy(tab_hbm.at[idx_v.at[pl.ds(c * SC_ROWS, SC_ROWS)]], rows_v.at[c % SC_NBUF], sem.at[c % SC_NBUF])


def _sc_udot(idx, x, table):
    ns = idx.shape[0]
    per_w = ns // SC_WORKERS
    assert per_w * SC_WORKERS == ns
    nchunk, nstrip, nk = NSEL // SC_ROWS, D_MODEL // SC_STRIP, SC_STRIP // SC_LANES

    @functools.partial(
        pl.kernel, mesh=plsc.VectorSubcoreMesh(core_axis_name="c", subcore_axis_name="s"),
        out_type=jax.ShapeDtypeStruct((ns, NSEL * SC_LANES), F32),
        scratch_types=[pltpu.VMEM((NSEL,), jnp.int32), pltpu.VMEM((D_MODEL,), F32),
                       pltpu.VMEM((SC_NBUF, SC_ROWS, D_MODEL), F32), pltpu.VMEM((NSEL * SC_LANES,), F32),
                       pltpu.SemaphoreType.DMA((SC_NBUF,))],
    )
    def k(idx_hbm, x_hbm, tab_hbm, out_hbm, idx_v, x_v, rows_v, acc_v, sem):
        base = (lax.axis_index("s") * 2 + lax.axis_index("c")) * per_w

        def token(i, carry):
            tok = base + i
            pltpu.sync_copy(idx_hbm.at[tok], idx_v)
            pltpu.sync_copy(x_hbm.at[tok], x_v)
            for c in range(SC_NBUF - 1):
                _sc_chunk_copy(tab_hbm, idx_v, rows_v, sem, c).start()
            for c in range(nchunk):
                if c + SC_NBUF - 1 < nchunk:
                    _sc_chunk_copy(tab_hbm, idx_v, rows_v, sem, c + SC_NBUF - 1).start()
                _sc_chunk_copy(tab_hbm, idx_v, rows_v, sem, c).wait()
                for kc in range(nstrip):
                    xr = [x_v[pl.ds(kc * SC_STRIP + kk * SC_LANES, SC_LANES)] for kk in range(nk)]

                    def rows8(rg, cr, c=c, kc=kc, xr=xr):
                        for rr in range(SC_UGROUP):
                            r = rg * SC_UGROUP + rr
                            s = rows_v[c % SC_NBUF, r, pl.ds(kc * SC_STRIP, SC_LANES)] * xr[0]
                            for kk in range(1, nk):
                                s = s + rows_v[c % SC_NBUF, r, pl.ds(kc * SC_STRIP + kk * SC_LANES, SC_LANES)] * xr[kk]
                            dst = pl.ds((c * SC_ROWS + r) * SC_LANES, SC_LANES)
                            if kc == 0:
                                acc_v[dst] = s
                            else:
                                acc_v[dst] = acc_v[dst] + s
                        return cr

                    lax.fori_loop(0, SC_ROWS // SC_UGROUP, rows8, 0)
            pltpu.sync_copy(acc_v, out_hbm.at[tok])
            return carry

        lax.fori_loop(0, per_w, token, 0)

    return k(idx, x, table)


def _sc_vsum(idx, coefb, table):
    ns = idx.shape[0]
    per_w = ns // SC_WORKERS
    assert per_w * SC_WORKERS == ns
    nchunk, nstrip, nk = NSEL // SC_ROWS, D_MODEL // SC_STRIP, SC_STRIP // SC_LANES

    @functools.partial(
        pl.kernel, mesh=plsc.VectorSubcoreMesh(core_axis_name="c", subcore_axis_name="s"),
        out_type=jax.ShapeDtypeStruct((ns, D_MODEL), F32),
        scratch_types=[pltpu.VMEM((NSEL,), jnp.int32), pltpu.VMEM((NSEL * SC_LANES,), F32),
                       pltpu.VMEM((SC_NBUF, SC_ROWS, D_MODEL), F32), pltpu.VMEM((D_MODEL,), F32),
                       pltpu.SemaphoreType.DMA((SC_NBUF,))],
    )
    def k(idx_hbm, coef_hbm, tab_hbm, out_hbm, idx_v, coef_v, rows_v, y_v, sem):
        base = (lax.axis_index("s") * 2 + lax.axis_index("c")) * per_w

        def token(i, carry):
            tok = base + i
            pltpu.sync_copy(idx_hbm.at[tok], idx_v)
            pltpu.sync_copy(coef_hbm.at[tok], coef_v)
            for c in range(SC_NBUF - 1):
                _sc_chunk_copy(tab_hbm, idx_v, rows_v, sem, c).start()
            for kk in range(D_MODEL // SC_LANES):
                y_v[pl.ds(kk * SC_LANES, SC_LANES)] = jnp.zeros((SC_LANES,), F32)
            for c in range(nchunk):
                if c + SC_NBUF - 1 < nchunk:
                    _sc_chunk_copy(tab_hbm, idx_v, rows_v, sem, c + SC_NBUF - 1).start()
                _sc_chunk_copy(tab_hbm, idx_v, rows_v, sem, c).wait()
                for kc in range(nstrip):
                    def rows8(rg, cr, c=c, kc=kc):
                        acc = [y_v[pl.ds(kc * SC_STRIP + kk * SC_LANES, SC_LANES)] for kk in range(nk)]
                        for rr in range(SC_RGROUP):
                            r = rg * SC_RGROUP + rr
                            cv = coef_v[pl.ds((c * SC_ROWS + r) * SC_LANES, SC_LANES)]
                            for kk in range(nk):
                                acc[kk] = acc[kk] + cv * rows_v[c % SC_NBUF, r, pl.ds(kc * SC_STRIP + kk * SC_LANES, SC_LANES)]
                        for kk in range(nk):
                            y_v[pl.ds(kc * SC_STRIP + kk * SC_LANES, SC_LANES)] = acc[kk]
                        return cr

                    lax.fori_loop(0, SC_ROWS // SC_RGROUP, rows8, 0)
            pltpu.sync_copy(y_v, out_hbm.at[tok])
            return carry

        lax.fori_loop(0, per_w, token, 0)

    return k(idx, coefb, table)


def _sc_coef_kernel(a_ref, g_ref, after_ref, o_ref):
    grp = lax.broadcasted_iota(jnp.int32, (NSEL * SC_LANES, NSEL), 0) // SC_LANES
    fold = (grp == lax.broadcasted_iota(jnp.int32, (NSEL * SC_LANES, NSEL), 1)).astype(BF16)

    def pieces(x):
        hi = x.astype(BF16)
        r1 = x - hi.astype(F32)
        mid = r1.astype(BF16)
        return hi, mid, (r1 - mid.astype(F32)).astype(BF16)

    a = sum(jnp.dot(p, fold, preferred_element_type=F32) for p in pieces(a_ref[...]))
    coef = g_ref[...] * (0.5 * a * (1.0 + lax.erf(a * (0.5 ** 0.5))))
    nt = (((1,), (1,)), ((), ()))
    o_ref[...] = sum(lax.dot_general(p, fold, nt, preferred_element_type=F32) for p in pieces(coef))


def _sc_coef(a_part, g, after, *, tt=SC_SPLIT_BLOCK):
    n = a_part.shape[0]
    wide = pl.BlockSpec((tt, NSEL * SC_LANES), lambda i: (i, 0))
    return pl.pallas_call(
        _sc_coef_kernel, grid=(n // tt,),
        in_specs=[wide, pl.BlockSpec((tt, NSEL), lambda i: (i, 0)),
                  pl.BlockSpec((SUBLANES, D_MODEL), lambda i: (0, 0))], out_specs=wide,
        out_shape=jax.ShapeDtypeStruct((n, NSEL * SC_LANES), F32), name="sc_coef",
    )(a_part, g, after)


def _sc_residual_kernel(x1_ref, ff_ref, gate_ref, after_ref, o_ref):
    o_ref[...] = x1_ref[...] + gate_ref[0] * ff_ref[...]


def _sc_residual(x1, ff, gate, after, *, tok_offset, l_seq, tt=SC_SPLIT_BLOCK):
    n = x1.shape[0]
    row = pl.BlockSpec((tt, D_MODEL), lambda i: (i, 0))
    return pl.pallas_call(
        _sc_residual_kernel, grid=(n // tt,),
        in_specs=[row, row, pl.BlockSpec((1, 1, D_MODEL), lambda i: ((tok_offset + i * tt) // l_seq, 0, 0)),
                  pl.BlockSpec((SUBLANES, D_MODEL), lambda i: (0, 0))], out_specs=row,
        out_shape=jax.ShapeDtypeStruct((n, D_MODEL), F32), name="sc_residual",
    )(x1, ff, gate, after)


def _rope_tables(pos):
    half = DIFF_HD // 2
    inv = ROPE_THETA ** (-jnp.arange(half, dtype=F32) / half)
    ang = pos.astype(F32)[:, None] * inv[None, :]
    cos, sin = jnp.cos(ang), jnp.sin(ang)
    return jnp.tile(cos, (1, 4)), jnp.tile(jnp.concatenate([-sin, sin], axis=1), (1, 2))


def _mixers(x, mods, mod_tiles, tile, cos, sin, l_seq, h0, wts, attend, lambda_init, after):
    sh1, sc1, gt1, sh2, sc2 = mods
    q_g, k_g, v_g, r_g, la = _proj_gla(x, sh1, sc1, wts["n1"], wts["w_a"], wts["w_alpha"], wts["b_alpha"], after,
                                       tt=tile, mod_tiles=mod_tiles)
    q_bf, k_rows, k_bf, v_rows, v_bf = _proj_diff(x, sh1, sc1, wts["n1"], wts["w_b"], wts["q_gain"], wts["k_gain"],
                                                  cos, sin, wts["ones"], after, tt=tile, mod_tiles=mod_tiles)
    o_a, state = _gla(q_g, k_g, la, v_g, h0, l_seq=l_seq, tb=min(GLA_TILE, l_seq))
    o_b = attend(q_bf, k_bf, v_bf)
    post_tile = min(POST_TILE, tile)
    x1, h2 = _post_mixer(x, o_a, r_g, o_b, (sh1, sc1, gt1, sh2, sc2), wts["n1"], wts["n2"], wts["gla_gain"],
                         wts["diff_gain"], wts["w_gla_out"], wts["w_diff_out"], wts["w_g"], wts["w_out"],
                         tt=post_tile, mod_tiles=mod_tiles * (tile // post_tile), diff_out_scale=1.0 - lambda_init)
    idx, g = _peer_select(h2, wts["w_pq"], wts["k1"], wts["k2"])
    return x1, h2, idx, g, k_rows, v_rows, state


def kernel(x_prompt, x_sample, c_prompt, c_sample, cache_k, cache_v, state_gla, norm1_gain, norm2_gain, w_mod, b_mod, w_in, w_alpha, b_alpha, gla_gain, w_gla_out, q_gain, k_gain, lam_q1, lam_k1, lam_q2, lam_k2, diff_gain, w_diff_out, w_out, w_pq, sub_keys1, sub_keys2, peer_u, peer_v):
    bp, lp, _ = x_prompt.shape
    bs, ls, _ = x_sample.shape
    past = cache_k.shape[2]
    l = 0
    lambda_init = 0.8 - 0.6 * math.exp(-0.3 * l)
    nb = bp + bs
    c_all = jnp.concatenate([c_prompt, c_sample, jnp.zeros(((-nb) % 8, D_MODEL), F32)], axis=0)
    mod_all = _adaln(c_all, w_mod[l], b_mod[l])
    mods = [mod_all[:nb, i * D_MODEL:(i + 1) * D_MODEL] for i in range(6)]

    wi = w_in[l]
    o_glr = 2 * GLA_QK_W + 2 * GLA_V_W
    o_diff = o_glr + GLA_LOWRANK
    o_gate = o_diff + 2 * DIFF_QK_W + DIFF_V_W
    pad_lr = LANES - GLA_LOWRANK
    lane = jnp.arange(LANES)
    wts = dict(
        n1=norm1_gain[l][None], n2=norm2_gain[l][None],
        w_a=jnp.concatenate([wi[:, :o_diff], jnp.zeros((D_MODEL, pad_lr), F32)], axis=1).astype(BF16),
        w_alpha=jnp.concatenate([w_alpha[l], jnp.zeros((pad_lr, GLA_QK_W), F32)], axis=0).astype(BF16),
        b_alpha=b_alpha[l][None],
        w_b=wi[:, o_diff:o_gate].astype(BF16), w_g=wi[:, o_gate:].astype(BF16),
        q_gain=q_gain[l].reshape(1, LANES), k_gain=k_gain[l].reshape(1, LANES),
        ones=(lane[:, None] // DIFF_HD == lane[None, :] // DIFF_HD).astype(BF16),
        gla_gain=jnp.tile(gla_gain[l], GLA_HEADS)[None], diff_gain=jnp.tile(diff_gain[l], DIFF_HEADS)[None],
        w_gla_out=w_gla_out[l].astype(BF16), w_diff_out=w_diff_out[l].astype(BF16), w_out=w_out[l].astype(BF16),
        w_pq=w_pq[l].astype(BF16), k1=sub_keys1[l].astype(BF16), k2=sub_keys2[l].astype(BF16),
        table=jnp.concatenate([peer_u[l], peer_v[l]], axis=1).reshape(-1, 1, 2 * D_MODEL),
        u=peer_u[l], v=peer_v[l],
    )
    lam = (jnp.exp(jnp.sum(lam_q1[l] * lam_k1[l])) - jnp.exp(jnp.sum(lam_q2[l] * lam_k2[l])) + lambda_init).reshape(1, 1)

    no_order = jnp.zeros((SUBLANES, LANES), jnp.int32)
    gather = functools.partial(_peer_gather, table=wts["table"], tt=PEER_TOK_TILE)

    cos_p, sin_p = _rope_tables(jnp.arange(lp))

    def prompt_group(lo, hi, after):
        nsq = hi - lo
        attend = lambda q, k, v: _attn_prompt(q.reshape(nsq, lp, -1), k.reshape(nsq, lp, -1), v.reshape(nsq, lp, -1),
                                              lam).reshape(nsq * lp, -1)
        return _mixers(x_prompt[lo:hi].reshape(nsq * lp, D_MODEL), [m[lo:hi, None, :] for m in mods[:5]],
                       lp // PROJ_TILE, PROJ_TILE, cos_p, sin_p, lp, jnp.zeros((nsq, GLA_HEADS, GLA_DK, GLA_DV), F32),
                       wts, attend, lambda_init, after)

    n_sc_seq = bp // 2 if bp >= 4 else 0
    if n_sc_seq:
        x1a, h2a, idxa, ga, ka, va, sa = prompt_group(0, n_sc_seq, no_order)
        a_part = _sc_udot(idxa, h2a, wts["u"])
        x1b, h2b, idxb, gb, kb, vb, sb = prompt_group(n_sc_seq, bp, idxa[:SUBLANES])
        gates_b = mods[5][n_sc_seq:bp, None, :]
        n_first = round(x1b.shape[0] // SC_SPLIT_BLOCK * TC_FIRST_SHARE) * SC_SPLIT_BLOCK
        y_b1 = gather(idxb, gb, h2b, x1b, gates=gates_b, l_seq=lp, row0=0, n_tok=n_first)
        coefb = _sc_coef(a_part, ga, y_b1[:SUBLANES])
        ff = _sc_vsum(idxa, coefb, wts["v"])
        y_b2 = gather(idxb, gb, h2b, x1b, gates=gates_b, l_seq=lp, row0=n_first, n_tok=x1b.shape[0] - n_first)
        y_a = _sc_residual(x1a, ff, mods[5][:n_sc_seq, None, :], y_b2[:SUBLANES], tok_offset=0, l_seq=lp)
        yp = jnp.concatenate([y_a, y_b1, y_b2], axis=0)
        kp, vp, sp = (jnp.concatenate(p, axis=0) for p in ((ka, kb), (va, vb), (sa, sb)))
    else:
        x1p, h2p, idxp, gp, kp, vp, sp = prompt_group(0, bp, no_order)
        yp = gather(idxp, gp, h2p, x1p, gates=mods[5][:bp, None, :], l_seq=lp)

    seq_per_tile = SAMPLE_TILE // ls
    cos_s, sin_s = _rope_tables(jnp.tile(past + jnp.arange(ls), seq_per_tile))
    mods_s = [jnp.repeat(m[bp:nb], ls, axis=0).reshape(-1, SAMPLE_TILE, D_MODEL) for m in mods[:5]]
    attend_s = lambda q, k, v: _attn_sample(q, k, v, cache_k[l], cache_v[l], lam, l_seq=ls)
    x1s, h2s, idxs, gs, ks, vs, ss = _mixers(x_sample.reshape(bs * ls, D_MODEL), mods_s, 1, SAMPLE_TILE, cos_s, sin_s,
                                             ls, state_gla[l], wts, attend_s, lambda_init, no_order)
    ys = gather(idxs, gs, h2s, x1s, gates=mods[5][bp:nb, None, :], l_seq=ls)

    return (yp.reshape(bp, lp, D_MODEL), ys.reshape(bs, ls, D_MODEL),
            kp.reshape(1, bp, lp, DIFF_HEADS, 2 * DIFF_HD), vp.reshape(1, bp, lp, DIFF_HEADS, DIFF_VD), sp[None],
            ks.reshape(1, bs, ls, DIFF_HEADS, 2 * DIFF_HD), vs.reshape(1, bs, ls, DIFF_HEADS, DIFF_VD), ss[None])
```
